```python
import math
import jax
import jax.numpy as jnp
from jax import lax
import numpy as np

D_MODEL = 1024
BATCH = 4
SEQ = 4096
DEPTH = 1

MLA_HEADS = 8
MLA_NOPE = 128
MLA_ROPE = 64
MLA_V = 128
Q_LORA = 384
KV_LORA = 256
ROPE_BASE = 10000.0
ATTN_BLOCK = 128
RET_HEADS = 8
RET_DK = 128
RET_DV = 128
RET_CHUNK = 128
N_EXPERTS = 32
TOP_K = 4
D_EXPERT = 1024
SWIGLU_LIMIT = 7.0
SWIGLU_ALPHA = 1.702
EXPERT_BLOCK = 128
MAX_POS_OFFSET = 2048
DN_ALPHA = (2.0 * DEPTH) ** 0.25
DN_BETA = (8.0 * DEPTH) ** -0.25
LN_EPS = 1e-5
RMS_EPS = 1e-6
GN_EPS = 1e-6
NEG_INF = -1e30
IN_SPLITS = (Q_LORA, KV_LORA, MLA_ROPE,
             RET_HEADS * RET_DK, RET_HEADS * RET_DK,
             RET_HEADS * RET_DV, RET_HEADS * RET_DV,
             D_MODEL, D_MODEL)
D_IN = Q_LORA + KV_LORA + MLA_ROPE + 2 * RET_HEADS * RET_DK + 2 * RET_HEADS * RET_DV + 2 * D_MODEL

kernel_name = 'hybrid_mla_retention_moe_deepnorm'


def _layernorm(x, g, b):
    xf = x.astype(jnp.float32)
    mu = jnp.mean(xf, axis=-1, keepdims=True)
    var = jnp.mean(jnp.square(xf - mu), axis=-1, keepdims=True)
    return ((xf - mu) * lax.rsqrt(var + LN_EPS) * g + b).astype(x.dtype)


def _rmsnorm(x, g):
    xf = x.astype(jnp.float32)
    return (xf * lax.rsqrt(jnp.mean(jnp.square(xf), axis=-1, keepdims=True) + RMS_EPS) * g).astype(x.dtype)


def _rotary(x, positions, inv_freq):
    ang = positions.astype(jnp.float32)[:, :, None] * inv_freq[None, None, :]
    cos = jnp.cos(ang)[:, :, None, :]
    sin = jnp.sin(ang)[:, :, None, :]
    xf = x.astype(jnp.float32)
    half = x.shape[-1] // 2
    x1, x2 = xf[..., :half], xf[..., half:]
    return jnp.concatenate([x1 * cos - x2 * sin, x2 * cos + x1 * sin], axis=-1).astype(x.dtype)


def _causal_block_attention(q, k, v):
    B, S, H, Dq = q.shape
    nq = S // ATTN_BLOCK
    scale = Dq ** -0.5
    qb = q.reshape(B, nq, ATTN_BLOCK, H, Dq).transpose(1, 0, 2, 3, 4)
    starts = jnp.arange(nq, dtype=jnp.int32) * ATTN_BLOCK
    key_pos = jnp.arange(S, dtype=jnp.int32)

    def one_block(args):
        q_blk, start = args
        s = jnp.einsum('bqhd,bkhd->bhqk', q_blk, k).astype(jnp.float32) * scale
        q_pos = start + jnp.arange(ATTN_BLOCK, dtype=jnp.int32)
        causal = (q_pos[:, None] >= key_pos[None, :])[None, None]
        p = jax.nn.softmax(jnp.where(causal, s, NEG_INF), axis=-1).astype(v.dtype)
        return jnp.einsum('bhqk,bkhd->bqhd', p, v)

    o = lax.map(one_block, (qb, starts))
    return o.transpose(1, 0, 2, 3, 4).reshape(B, S, H, v.shape[-1])


def _retention(q, k, v, positions):
    B, S, H, Dk = q.shape
    Dv = v.shape[-1]
    C = RET_CHUNK
    N = S // C
    inv_freq = 1.0 / (10000.0 ** jnp.linspace(0.0, 1.0, Dk // 2, dtype=jnp.float32))
    q = _rotary(q, positions, inv_freq)
    k = _rotary(k, positions, inv_freq) * (Dk ** -0.5)
    log_gamma = jnp.log(1.0 - 2.0 ** (-5.0 - jnp.arange(H, dtype=jnp.float32)))
    idx = jnp.arange(C, dtype=jnp.float32)
    rel = idx[:, None] - idx[None, :]
    decay = jnp.where(rel[None] >= 0,
                      jnp.exp(jnp.maximum(rel, 0.0)[None] * log_gamma[:, None, None]), 0.0)
    qc = q.reshape(B, N, C, H, Dk)
    kc = k.reshape(B, N, C, H, Dk)
    vc = v.reshape(B, N, C, H, Dv)
    scores = jnp.einsum('bnihd,bnjhd->bnhij', qc, kc).astype(jnp.float32) * decay[None, None]
    o_inner = jnp.einsum('bnhij,bnjhe->bnihe', scores, vc.astype(jnp.float32))
    k_w = jnp.exp((C - 1.0 - idx)[:, None] * log_gamma[None, :])
    q_w = jnp.exp((idx + 1.0)[:, None] * log_gamma[None, :])
    U = jnp.einsum('bnjhd,bnjhe->nbhde', kc.astype(jnp.float32) * k_w[:, :, None],
                   vc.astype(jnp.float32))
    chunk_decay = jnp.exp(C * log_gamma)[None, :, None, None]

    def step(R, U_n):
        return chunk_decay * R + U_n, R

    _, R_prev = lax.scan(step, jnp.zeros((B, H, Dk, Dv), jnp.float32), U)
    o_cross = jnp.einsum('bnihd,nbhde->bnihe', qc.astype(jnp.float32) * q_w[:, :, None], R_prev)
    o = (o_inner + o_cross).reshape(B, S, H, Dv)
    mu = jnp.mean(o, axis=-1, keepdims=True)
    var = jnp.mean(jnp.square(o - mu), axis=-1, keepdims=True)
    return ((o - mu) * lax.rsqrt(var + GN_EPS)).astype(v.dtype)


def _hybrid_mixer(x, positions, w_in, q_norm_g, w_uq, kv_norm_g, w_ukv, w_o):
    B, S, _ = x.shape
    split_at = [int(i) for i in np.cumsum(IN_SPLITS)[:-1]]
    proj = x @ w_in
    c_q, c_kv, k_r, r_q, r_k, r_v, r_g, g_mla, g_ret = jnp.split(proj, split_at, axis=-1)
    inv_freq = 1.0 / (ROPE_BASE ** (jnp.arange(0, MLA_ROPE, 2, dtype=jnp.float32) / MLA_ROPE))
    q = (_rmsnorm(c_q, q_norm_g) @ w_uq).reshape(B, S, MLA_HEADS, MLA_NOPE + MLA_ROPE)
    q = jnp.concatenate([q[..., :MLA_NOPE], _rotary(q[..., MLA_NOPE:], positions, inv_freq)], axis=-1)
    kv = (_rmsnorm(c_kv, kv_norm_g) @ w_ukv).reshape(B, S, MLA_HEADS, MLA_NOPE + MLA_V)
    k_nope, v = kv[..., :MLA_NOPE], kv[..., MLA_NOPE:]
    k_rope = _rotary(k_r[:, :, None, :], positions, inv_freq)
    k = jnp.concatenate([k_nope, jnp.broadcast_to(k_rope, (B, S, MLA_HEADS, MLA_ROPE))], axis=-1)
    o_mla = _causal_block_attention(q, k, v).reshape(B, S, MLA_HEADS * MLA_V)
    o_ret = _retention(r_q.reshape(B, S, RET_HEADS, RET_DK), r_k.reshape(B, S, RET_HEADS, RET_DK),
                       r_v.reshape(B, S, RET_HEADS, RET_DV), positions)
    o_ret = o_ret.reshape(B, S, RET_HEADS * RET_DV) * jax.nn.silu(r_g)
    mixed = jax.nn.sigmoid(g_mla) * o_mla + jax.nn.sigmoid(g_ret) * o_ret
    return (mixed @ w_o).astype(x.dtype)


def _routed_experts(h, w_router, b_router, w_up, b_up, w_down, b_down):
    Bn, Sn, D = h.shape
    T = Bn * Sn
    xt = h.reshape(T, D)
    logits = (xt @ w_router + b_router).astype(jnp.float32)
    top_v, top_i = lax.top_k(logits, TOP_K)
    gate = jax.nn.softmax(top_v, axis=-1)
    n_assign = T * TOP_K
    flat_e = top_i.reshape(-1).astype(jnp.int32)
    flat_tok = jnp.arange(n_assign, dtype=jnp.int32) // TOP_K
    flat_w = gate.reshape(-1)
    order = jnp.argsort(flat_e)
    se = flat_e[order]
    counts = jnp.bincount(flat_e, length=N_EXPERTS).astype(jnp.int32)
    padded = ((counts + EXPERT_BLOCK - 1) // EXPERT_BLOCK) * EXPERT_BLOCK
    pad_end = jnp.cumsum(padded)
    pad_start = pad_end - padded
    grp_start = jnp.cumsum(counts) - counts
    rank = jnp.arange(n_assign, dtype=jnp.int32) - grp_start[se]
    dest = pad_start[se] + rank
    P = n_assign + N_EXPERTS * EXPERT_BLOCK
    n_blocks = P // EXPERT_BLOCK
    row_tok = jnp.zeros((P,), jnp.int32).at[dest].set(flat_tok[order])
    row_w = jnp.zeros((P,), jnp.float32).at[dest].set(flat_w[order])
    row_valid = jnp.zeros((P,), bool).at[dest].set(True)
    blk_e = jnp.minimum(jnp.searchsorted(pad_end, jnp.arange(n_blocks, dtype=jnp.int32) * EXPERT_BLOCK,
                                         side='right'), N_EXPERTS - 1).astype(jnp.int32)
    xs = xt[row_tok].reshape(n_blocks, EXPERT_BLOCK, D)

    def expert_block(args):
        xb, e = args
        hcat = xb @ w_up[e] + b_up[e]
        h_glu = jnp.minimum(hcat[:, :D_EXPERT], SWIGLU_LIMIT)
        h_lin = jnp.clip(hcat[:, D_EXPERT:], -SWIGLU_LIMIT, SWIGLU_LIMIT)
        a = h_glu * jax.nn.sigmoid(SWIGLU_ALPHA * h_glu) * (h_lin + 1.0)
        return a @ w_down[e] + b_down[e]

    ys = lax.map(expert_block, (xs, blk_e)).reshape(P, D)
    ys = jnp.where(row_valid[:, None], ys * row_w[:, None], 0.0)
    out = jax.ops.segment_sum(ys, row_tok, num_segments=T)
    return out.reshape(Bn, Sn, D).astype(h.dtype)


def setup_inputs(seed: int = 0) -> dict:
    key = jax.random.key(seed)
    ks = jax.random.split(key, 20)
    f32 = jnp.float32

    def nrm(k, shape, scale):
        return jax.random.normal(k, shape, f32) * scale

    x = nrm(ks[0], (BATCH, SEQ, D_MODEL), 1.0)
    offset = jax.random.randint(ks[1], (BATCH, 1), 0, MAX_POS_OFFSET, dtype=jnp.int32)
    positions = offset + jnp.arange(SEQ, dtype=jnp.int32)[None, :]
    return {
        'x': x,
        'positions': positions,
        'w_in': nrm(ks[2], (DEPTH, D_MODEL, D_IN), D_MODEL ** -0.5),
        'q_norm_g': 1.0 + nrm(ks[3], (DEPTH, Q_LORA), 0.02),
        'w_uq': nrm(ks[4], (DEPTH, Q_LORA, MLA_HEADS * (MLA_NOPE + MLA_ROPE)), Q_LORA ** -0.5),
        'kv_norm_g': 1.0 + nrm(ks[5], (DEPTH, KV_LORA), 0.02),
        'w_ukv': nrm(ks[6], (DEPTH, KV_LORA, MLA_HEADS * (MLA_NOPE + MLA_V)), KV_LORA ** -0.5),
        'w_o': nrm(ks[7], (DEPTH, D_MODEL, D_MODEL), D_MODEL ** -0.5 * DN_BETA),
        'ln1_g': 1.0 + nrm(ks[8], (DEPTH, D_MODEL), 0.02),
        'ln1_b': nrm(ks[9], (DEPTH, D_MODEL), 0.02),
        'w_router': nrm(ks[10], (DEPTH, D_MODEL, N_EXPERTS), D_MODEL ** -0.5),
        'b_router': nrm(ks[11], (DEPTH, N_EXPERTS), 0.01),
        'w_up': nrm(ks[12], (DEPTH, N_EXPERTS, D_MODEL, 2 * D_EXPERT), D_MODEL ** -0.5),
        'b_up': nrm(ks[13], (DEPTH, N_EXPERTS, 2 * D_EXPERT), 0.01),
        'w_down': nrm(ks[14], (DEPTH, N_EXPERTS, D_EXPERT, D_MODEL), D_EXPERT ** -0.5 * DN_BETA),
        'b_down': nrm(ks[15], (DEPTH, N_EXPERTS, D_MODEL), 0.01),
        'ln2_g': 1.0 + nrm(ks[16], (DEPTH, D_MODEL), 0.02),
        'ln2_b': nrm(ks[17], (DEPTH, D_MODEL), 0.02),
    }


def reference(x, positions, w_in, q_norm_g, w_uq, kv_norm_g, w_ukv, w_o, ln1_g, ln1_b,
              w_router, b_router, w_up, b_up, w_down, b_down, ln2_g, ln2_b):
    h = x
    for l in range(DEPTH):
        mix = _hybrid_mixer(h, positions, w_in[l], q_norm_g[l], w_uq[l], kv_norm_g[l], w_ukv[l], w_o[l])
        h = _layernorm(DN_ALPHA * h + mix, ln1_g[l], ln1_b[l])
        ffn = _routed_experts(h, w_router[l], b_router[l], w_up[l], b_up[l], w_down[l], b_down[l])
        h = _layernorm(DN_ALPHA * h + ffn, ln2_g[l], ln2_b[l])
    return h
```

```python
import functools

import numpy as np
import jax
import jax.numpy as jnp
from jax import lax
from jax.experimental import pallas as pl
from jax.experimental.pallas import tpu as pltpu

F32 = jnp.float32
BF16 = jnp.bfloat16

D_MODEL = 1024
DEPTH = 1
MLA_HEADS = 8
MLA_NOPE = 128
MLA_ROPE = 64
MLA_V = 128
Q_LORA = 384
KV_LORA = 256
ROPE_BASE = 10000.0
RET_HEADS = 8
RET_DK = 128
RET_DV = 128
RET_CHUNK = 128
N_EXPERTS = 32
TOP_K = 4
D_EXPERT = 1024
SWIGLU_LIMIT = 7.0
SWIGLU_ALPHA = 1.702
DN_ALPHA = (2.0 * DEPTH) ** 0.25
LN_EPS = 1e-5
RMS_EPS = 1e-6
GN_EPS = 1e-6
NEG_INF = -1e30

LANES = 128
MLA_QK = 2 * LANES
LATENT_W = 768
N_WIDE = 6 * D_MODEL
N_PROJ = N_WIDE + LATENT_W
VMEM_LIMIT = 56 * 1024 * 1024

ROW_TILE = 512
ATTN_TILE = 512
RET_GROUP = 4
EXPERT_TILE = 256


def _cparams(*sem):
    return pltpu.CompilerParams(dimension_semantics=sem, vmem_limit_bytes=VMEM_LIMIT)


def _const_spec(shape):
    nd = len(shape)
    return pl.BlockSpec(shape, lambda *_: (0,) * nd, pipeline_mode=pl.Buffered(1))


def _sigmoid(x):
    return 1.0 / (1.0 + jnp.exp(-x))


def _swap_halves(x):
    return pltpu.roll(x, LANES // 2, 1)


def _proj_kernel(x_ref, w_ref, o_ref):
    xb = x_ref[...].astype(BF16)
    for c in range(N_PROJ // LATENT_W):
        sl = slice(c * LATENT_W, (c + 1) * LATENT_W)
        o_ref[:, sl] = jnp.dot(xb, w_ref[:, sl], preferred_element_type=F32).astype(o_ref.dtype)


def _input_projection(x2, w_cat):
    T = x2.shape[0]
    return pl.pallas_call(
        _proj_kernel,
        out_shape=jax.ShapeDtypeStruct((T, N_PROJ), BF16),
        grid=(T // ROW_TILE,),
        in_specs=[pl.BlockSpec((ROW_TILE, D_MODEL), lambda i: (i, 0)),
                  _const_spec((D_MODEL, N_PROJ))],
        out_specs=pl.BlockSpec((ROW_TILE, N_PROJ), lambda i: (i, 0)),
        compiler_params=_cparams("parallel"),
        name="input_projection",
    )(x2, w_cat)


def _rope_table_kernel(pos_ref, freq_ref, rc_ref, rs_ref, mc_ref, ms_ref):
    ang = pos_ref[...].astype(F32) * freq_ref[...]
    c = jnp.cos(ang)
    s = jnp.sin(ang)
    lane = lax.broadcasted_iota(jnp.int32, ang.shape, 1)
    c64 = pltpu.roll(c, 64, 1)
    s64 = pltpu.roll(s, 64, 1)
    rc_ref[...] = jnp.where(lane < 64, c, c64)
    rs_ref[...] = jnp.where(lane < 64, -s, s64)
    c96 = pltpu.roll(c, 96, 1)
    s96 = pltpu.roll(s, 96, 1)
    mc_ref[...] = jnp.where(lane < 32, c64, jnp.where(lane < 64, c96, 0.0))
    ms_ref[...] = jnp.where(lane < 32, s64, jnp.where(lane < 64, s96, 0.0))


def _rope_tables(pos_col, freq_row):
    T = pos_col.shape[0]
    rows = min(2048, T)
    tab =jax.ShapeDtypeStruct((T, LANES), F32)
    spec = pl.BlockSpec((rows, LANES), lambda i: (i, 0))
    return pl.pallas_call(
        _rope_table_kernel,
        out_shape=(tab, tab, tab, tab),
        grid=(T // rows,),
        in_specs=[pl.BlockSpec((rows, 1), lambda i: (i, 0)), _const_spec((1, LANES))],
        out_specs=(spec, spec, spec, spec),
        compiler_params=_cparams("parallel"),
        name="rope_tables",
    )(pos_col, freq_row)


def _mla_prep_kernel(a_ref, mc_ref, ms_ref, wq_ref, wkv_ref, qg_ref, kvg_ref,
                     q_ref, kv_ref, kr_ref):
    a = a_ref[...].astype(F32)
    mc = mc_ref[...]
    ms = ms_ref[...]
    cq = a[:, :Q_LORA]
    qn = cq * lax.rsqrt(jnp.mean(cq * cq, axis=-1, keepdims=True) + RMS_EPS) * qg_ref[...]
    q = jnp.dot(qn.astype(BF16), wq_ref[...], preferred_element_type=F32)
    scale = (MLA_NOPE + MLA_ROPE) ** -0.5
    for h in range(MLA_HEADS):
        nope = q[:, h * MLA_QK:h * MLA_QK + LANES]
        blk = q[:, h * MLA_QK + LANES:(h + 1) * MLA_QK]
        rope = blk * mc + _swap_halves(blk) * ms
        q_ref[:, h * MLA_QK:h * MLA_QK + LANES] = (nope * scale).astype(BF16)
        q_ref[:, h * MLA_QK + LANES:(h + 1) * MLA_QK] = (rope * scale).astype(BF16)
    ckv = a[:, Q_LORA:Q_LORA + KV_LORA]
    kvn = ckv * lax.rsqrt(jnp.mean(ckv * ckv, axis=-1, keepdims=True) + RMS_EPS) * kvg_ref[...]
    kv_ref[...] = jnp.dot(kvn.astype(BF16), wkv_ref[...], preferred_element_type=F32).astype(BF16)
    krb = a[:, Q_LORA + KV_LORA:]
    kr_ref[...] = (krb * mc + _swap_halves(krb) * ms).astype(BF16)


def _mla_prep(proj, mc, ms, wq, wkv, qg, kvg):
    T = proj.shape[0]
    row = lambda w: pl.BlockSpec((ROW_TILE, w), lambda i: (i, 0))
    return pl.pallas_call(
        _mla_prep_kernel,
        out_shape=(jax.ShapeDtypeStruct((T, MLA_HEADS * MLA_QK), BF16),
                   jax.ShapeDtypeStruct((T, MLA_HEADS * MLA_QK), BF16),
                   jax.ShapeDtypeStruct((T, LANES), BF16)),
        grid=(T // ROW_TILE,),
        in_specs=[pl.BlockSpec((ROW_TILE, LATENT_W), lambda i: (i, N_WIDE // LATENT_W)),
                  row(LANES), row(LANES),
                  _const_spec(wq.shape), _const_spec(wkv.shape),
                  _const_spec(qg.shape), _const_spec(kvg.shape)],
        out_specs=(row(MLA_HEADS * MLA_QK), row(MLA_HEADS * MLA_QK), row(LANES)),
        compiler_params=_cparams("parallel"),
        name="mla_prep",
    )(proj, mc, ms, wq, wkv, qg, kvg)


def _attn_kernel(q_ref, kv_ref, kr_ref, o_ref, m_scr, l_scr, acc_scr):
    qi = pl.program_id(2)
    q = q_ref[...]
    m_scr[...] = jnp.full(m_scr.shape, NEG_INF, F32)
    l_scr[...] = jnp.zeros(l_scr.shape, F32)
    acc_scr[...] = jnp.zeros(acc_scr.shape, F32)

    def tile(j, masked):
        rows = pl.ds(pl.multiple_of(j * ATTN_TILE, ATTN_TILE), ATTN_TILE)
        k = jnp.concatenate([kv_ref[rows, :LANES], kr_ref[rows, :]], axis=-1)
        v = kv_ref[rows, LANES:]
        s = lax.dot_general(q, k, (((1,), (1,)), ((), ())), preferred_element_type=F32)
        if masked:
            qp = lax.broadcasted_iota(jnp.int32, s.shape, 0)
            kp = lax.broadcasted_iota(jnp.int32, s.shape, 1)
            s = jnp.where(qp >= kp, s, NEG_INF)
        m_prev = m_scr[...]
        m_new = jnp.maximum(m_prev, jnp.max(s, axis=-1, keepdims=True))
        alpha = jnp.exp(m_prev - m_new)
        p = jnp.exp(s - m_new)
        l_scr[...] = alpha * l_scr[...] + jnp.sum(p, axis=-1, keepdims=True)
        acc_scr[...] = alpha * acc_scr[...] + jnp.dot(p.astype(BF16), v, preferred_element_type=F32)
        m_scr[...] = m_new

    def body(j, carry):
        tile(j, False)
        return carry

    lax.fori_loop(0, qi, body, 0)
    tile(qi, True)
    o_ref[...] = (acc_scr[...] / l_scr[...]).astype(o_ref.dtype)


def _attention(q_cat, kv, kr, B, S):
    T = B * S
    nq = S // ATTN_TILE
    return pl.pallas_call(
        _attn_kernel,
        out_shape=jax.ShapeDtypeStruct((T, MLA_HEADS * MLA_V), BF16),
        grid=(B, MLA_HEADS, nq),
        in_specs=[pl.BlockSpec((ATTN_TILE, MLA_QK), lambda b, h, i: (b * nq + i, h)),
                  pl.BlockSpec((S, MLA_QK), lambda b, h, i: (b, h)),
                  pl.BlockSpec((S, LANES), lambda b, h, i: (b, 0))],
        out_specs=pl.BlockSpec((ATTN_TILE, MLA_V), lambda b, h, i: (b * nq + i, h)),
        scratch_shapes=[pltpu.VMEM((ATTN_TILE, 1), F32), pltpu.VMEM((ATTN_TILE, 1), F32),
                        pltpu.VMEM((ATTN_TILE, MLA_V), F32)],
        compiler_params=_cparams("parallel", "parallel", "arbitrary"),
        name="mla_attention",
    )(q_cat, kv, kr)


def _ret_kernel(q_ref, k_ref, v_ref, g_ref, rc_ref, rs_ref, dec_ref, qw_ref, kw_ref, cd_ref,
                o_ref, qb_scr, qwb_scr, kb_scr, kwb_scr):
    S = q_ref.shape[0]
    C = RET_CHUNK
    n_chunks = S // C
    rc = rc_ref[...]
    rs = rs_ref[...]
    q = q_ref[...].astype(F32)
    k = k_ref[...].astype(F32)
    q = q * rc + _swap_halves(q) * rs
    k = (k * rc + _swap_halves(k) * rs) * (RET_DK ** -0.5)
    qw = qw_ref[0]
    kw = kw_ref[0]
    q3 = q.reshape(n_chunks, C, RET_DK)
    k3 = k.reshape(n_chunks, C, RET_DK)
    qb_scr[...] = q.astype(BF16)
    kb_scr[...] = k.astype(BF16)
    qwb_scr[...] = (q3 * qw[None]).reshape(S, RET_DK).astype(BF16)
    kwb_scr[...] = (k3 * kw[None]).reshape(S, RET_DK)
    dec = dec_ref[0]
    cd = cd_ref[0]

    def group(gi, R):
        for u in range(RET_GROUP):
            rows = pl.ds(pl.multiple_of((gi * RET_GROUP + u) * C, C), C)
            qn = qb_scr[rows, :]
            kn = kb_scr[rows, :]
            vn = v_ref[rows, :]
            s = lax.dot_general(qn, kn, (((1,), (1,)), ((), ())), preferred_element_type=F32) * dec
            o = jnp.dot(s.astype(BF16), vn, preferred_element_type=F32)
            o = o + jnp.dot(qwb_scr[rows, :], R.astype(BF16), preferred_element_type=F32)
            kwt = kwb_scr[rows, :].T.astype(BF16)
            R = cd * R + jnp.dot(kwt, vn, preferred_element_type=F32)
            mu = jnp.mean(o, axis=-1, keepdims=True)
            d = o - mu
            var = jnp.mean(d * d, axis=-1, keepdims=True)
            gate = g_ref[rows, :].astype(F32)
            o_ref[rows, :] = (d * lax.rsqrt(var + GN_EPS) * (gate * _sigmoid(gate))).astype(o_ref.dtype)
        return R

    lax.fori_loop(0, n_chunks // RET_GROUP, group, jnp.zeros((RET_DK, RET_DV), F32))


def _retention(proj, rc, rs, dec, qw, kw, cd, B, S):
    T = B * S
    H = RET_HEADS
    col = lambda off: pl.BlockSpec((S, LANES), lambda b, h: (b, off + h))
    tab = pl.BlockSpec((S, LANES), lambda b, h: (b, 0))
    per_head = lambda r: pl.BlockSpec((1, r, LANES), lambda b, h: (h, 0, 0))
    return pl.pallas_call(
        _ret_kernel,
        out_shape=jax.ShapeDtypeStruct((T, H * RET_DV), BF16),
        grid=(B, H),
        in_specs=[col(0), col(H), col(2 * H), col(3 * H), tab, tab,
                  per_head(RET_CHUNK), per_head(RET_CHUNK), per_head(RET_CHUNK), per_head(1)],
        out_specs=pl.BlockSpec((S, RET_DV), lambda b, h: (b, h)),
        scratch_shapes=[pltpu.VMEM((S, RET_DK), BF16), pltpu.VMEM((S, RET_DK), BF16),
                        pltpu.VMEM((S, RET_DK), BF16), pltpu.VMEM((S, RET_DK), F32)],
        compiler_params=_cparams("parallel", "parallel"),
        name="retention",
    )(proj, proj, proj, proj, rc, rs, dec, qw, kw, cd)


def _retention_constants():
    C = RET_CHUNK
    h = np.arange(RET_HEADS, dtype=np.float64)
    log_gamma = np.log(1.0 - 2.0 ** (-5.0 - h))
    idx = np.arange(C, dtype=np.float64)
    rel = idx[:, None] - idx[None, :]
    dec = np.where(rel[None] >= 0, np.exp(np.maximum(rel, 0.0)[None] * log_gamma[:, None, None]), 0.0)
    qw = np.exp((idx + 1.0)[None, :] * log_gamma[:, None])
    kw = np.exp((C - 1.0 - idx)[None, :] * log_gamma[:, None])
    cd = np.exp(C * log_gamma)
    bc = lambda a: np.ascontiguousarray(np.broadcast_to(a[..., None], a.shape + (LANES,))).astype(np.float32)
    return dec.astype(np.float32), bc(qw), bc(kw), bc(cd[:, None])


def _layernorm(y, g, b):
    mu = jnp.mean(y, axis=-1, keepdims=True)
    d = y - mu
    var = jnp.mean(d * d, axis=-1, keepdims=True)
    return d * lax.rsqrt(var + LN_EPS) * g + b


def _merge_kernel(x_ref, om_ref, or_ref, gm_ref, gr_ref, wo_ref, g_ref, b_ref, h_ref, hb_ref):
    mixed = (_sigmoid(gm_ref[...].astype(F32)) * om_ref[...].astype(F32)
             + _sigmoid(gr_ref[...].astype(F32)) * or_ref[...].astype(F32))
    mix = jnp.dot(mixed.astype(BF16), wo_ref[...], preferred_element_type=F32)
    h = _layernorm(DN_ALPHA * x_ref[...] + mix, g_ref[...], b_ref[...])
    h_ref[...] = h
    hb_ref[...] = h.astype(BF16)


def _merge(x2, o_mla, o_ret, proj, wo, g, b):
    T = x2.shape[0]
    row = lambda c: pl.BlockSpec((ROW_TILE, D_MODEL), lambda i: (i, c))
    return pl.pallas_call(
        _merge_kernel,
        out_shape=(jax.ShapeDtypeStruct((T, D_MODEL), F32), jax.ShapeDtypeStruct((T, D_MODEL), BF16)),
        grid=(T // ROW_TILE,),
        in_specs=[row(0), row(0), row(0), row(4), row(5),
                  _const_spec(wo.shape), _const_spec(g.shape), _const_spec(b.shape)],
        out_specs=(row(0), row(0)),
        compiler_params=_cparams("parallel"),
        name="merge_out_proj_ln",
    )(x2, o_mla, o_ret, proj, proj, wo, g, b)


def _split3(a):
    a0 = a.astype(BF16)
    r1 = a - a0.astype(F32)
    a1 = r1.astype(BF16)
    a2 = (r1 - a1.astype(F32)).astype(BF16)
    return a0, a1, a2


def _router_kernel(h_ref, w0_ref, w1_ref, w2_ref, b_ref, idx_ref, gate_ref, cnt_ref, run_scr):
    i = pl.program_id(0)

    @pl.when(i == 0)
    def _():
        run_scr[...] = jnp.zeros(run_scr.shape, F32)

    h0, h1, h2 = _split3(h_ref[...])
    w0, w1, w2 = w0_ref[...], w1_ref[...], w2_ref[...]
    dot = lambda a, w: jnp.dot(a, w, preferred_element_type=F32)
    logits = (dot(h0, w0) + (dot(h0, w1) + dot(h1, w0))
              + (dot(h0, w2) + dot(h1, w1) + dot(h2, w0))) + b_ref[...]
    rows = logits.shape[0]
    lane = lax.broadcasted_iota(jnp.int32, (rows, LANES), 1)
    lane_f = lane.astype(F32)
    work = jnp.where(lane < N_EXPERTS, logits, -jnp.inf)
    vals, ids, sels = [], [], []
    for _ in range(TOP_K):
        m = jnp.max(work, axis=-1, keepdims=True)
        first = jnp.min(jnp.where(work == m, lane_f, float(LANES)), axis=-1, keepdims=True)
        sel = lane_f == first
        work = jnp.where(sel, -jnp.inf, work)
        vals.append(m)
        ids.append(first)
        sels.append(sel)
    exps = [jnp.exp(v - vals[0]) for v in vals]
    denom = exps[0] + exps[1] + exps[2] + exps[3]
    chosen = jnp.zeros((rows, LANES), F32)
    for sel in sels:
        chosen = jnp.where(sel, 1.0, chosen)
    r = lax.broadcasted_iota(jnp.int32, (rows, rows), 0)
    c = lax.broadcasted_iota(jnp.int32, (rows, rows), 1)
    lower = jnp.where(r > c, 1.0, 0.0).astype(BF16)
    before = jnp.dot(lower, chosen.astype(BF16), preferred_element_type=F32) + run_scr[...]
    idx_out = jnp.zeros((rows, LANES), jnp.int32)
    gate_out = jnp.zeros((rows, LANES), F32)
    for kk in range(TOP_K):
        rank = jnp.sum(jnp.where(sels[kk], before, 0.0), axis=-1, keepdims=True).astype(jnp.int32)
        idx_out = jnp.where(lane == kk, ids[kk].astype(jnp.int32), idx_out)
        idx_out = jnp.where(lane == TOP_K + kk, rank, idx_out)
        gate_out = jnp.where(lane == kk, exps[kk] / denom, gate_out)
    idx_ref[...] = idx_out
    gate_ref[...] = gate_out
    total = run_scr[...] + jnp.sum(chosen, axis=0, keepdims=True)
    run_scr[...] = total
    cnt_ref[...] = total.astype(jnp.int32)


def _router(h, w_router, b_router):
    T = h.shape[0]
    wpad = jnp.zeros((D_MODEL, LANES), F32).at[:, :N_EXPERTS].set(w_router)
    w0 = wpad.astype(BF16)
    r1 = wpad - w0.astype(F32)
    w1 = r1.astype(BF16)
    w2 = (r1 - w1.astype(F32)).astype(BF16)
    bpad = jnp.zeros((1, LANES), F32).at[0, :N_EXPERTS].set(b_router)
    row = pl.BlockSpec((ROW_TILE, LANES), lambda i: (i, 0))
    return pl.pallas_call(
        _router_kernel,
        out_shape=(jax.ShapeDtypeStruct((T, LANES), jnp.int32),
                   jax.ShapeDtypeStruct((T, LANES), F32),
                   jax.ShapeDtypeStruct((1, LANES), jnp.int32)),
        grid=(T // ROW_TILE,),
        in_specs=[pl.BlockSpec((ROW_TILE, D_MODEL), lambda i: (i, 0)),
                  _const_spec(w0.shape), _const_spec(w0.shape), _const_spec(w0.shape),
                  _const_spec(bpad.shape)],
        out_specs=(row, row, pl.BlockSpec((1, LANES), lambda i: (0, 0))),
        scratch_shapes=[pltpu.VMEM((1, LANES), F32)],
        compiler_params=_cparams("arbitrary"),
        name="router",
    )(h, w0, w1, w2, bpad)


def _expert_kernel(blk_e_ref, nact_ref, x_ref, wu_ref, bu_ref, wd_ref, bd_ref, y_ref,
                   wub_scr, wdb_scr, last_scr):
    i = pl.program_id(0)
    e = blk_e_ref[i]

    @pl.when(i == 0)
    def _():
        last_scr[0] = -1

    @pl.when(jnp.logical_and(i < nact_ref[0], e != last_scr[0]))
    def _():
        wub_scr[...] = wu_ref[0].astype(BF16)
        wdb_scr[...] = wd_ref[0].astype(BF16)
        last_scr[0] = e

    @pl.when(i < nact_ref[0])
    def _():
        x = x_ref[...]
        acc = jnp.zeros((EXPERT_TILE, D_MODEL), F32)
        step = 256
        for c in range(D_EXPERT // step):
            glu = jnp.dot(x, wub_scr[:, c * step:(c + 1) * step], preferred_element_type=F32)
            lin = jnp.dot(x, wub_scr[:, D_EXPERT + c * step:D_EXPERT + (c + 1) * step],
                          preferred_element_type=F32)
            glu = jnp.minimum(glu + bu_ref[0, :, c * step:(c + 1) * step], SWIGLU_LIMIT)
            lin = jnp.clip(lin + bu_ref[0, :, D_EXPERT + c * step:D_EXPERT + (c + 1) * step],
                           -SWIGLU_LIMIT, SWIGLU_LIMIT)
            a = glu * _sigmoid(SWIGLU_ALPHA * glu) * (lin + 1.0)
            acc = acc + jnp.dot(a.astype(BF16), wdb_scr[c * step:(c + 1) * step, :],
                                preferred_element_type=F32)
        y_ref[...] = (acc + bd_ref[0]).astype(y_ref.dtype)

    @pl.when(i >= nact_ref[0])
    def _():
        y_ref[...] = jnp.zeros(y_ref.shape, y_ref.dtype)


def _experts(blk_e, nact, xs, w_up, b_up, w_down, b_down):
    P = xs.shape[0]
    n_blocks = P // EXPERT_TILE
    act = lambda i, na: jnp.minimum(i, na[0] - 1)
    grid_spec = pltpu.PrefetchScalarGridSpec(
        num_scalar_prefetch=2,
        grid=(n_blocks,),
        in_specs=[pl.BlockSpec((EXPERT_TILE, D_MODEL), lambda i, be, na: (act(i, na), 0)),
                  pl.BlockSpec((1, D_MODEL, 2 * D_EXPERT), lambda i, be, na: (be[act(i, na)], 0, 0)),
                  pl.BlockSpec((1, 1, 2 * D_EXPERT), lambda i, be, na: (be[act(i, na)], 0, 0)),
                  pl.BlockSpec((1, D_EXPERT, D_MODEL), lambda i, be, na: (be[act(i, na)], 0, 0)),
                  pl.BlockSpec((1, 1, D_MODEL), lambda i, be, na: (be[act(i, na)], 0, 0))],
        out_specs=pl.BlockSpec((EXPERT_TILE, D_MODEL), lambda i, be, na: (i, 0)),
        scratch_shapes=[pltpu.VMEM((D_MODEL, 2 * D_EXPERT), BF16),
                        pltpu.VMEM((D_EXPERT, D_MODEL), BF16),
                        pltpu.SMEM((1,), jnp.int32)],
    )
    return pl.pallas_call(
        _expert_kernel,
        out_shape=jax.ShapeDtypeStruct((P, D_MODEL), BF16),
        grid_spec=grid_spec,
        compiler_params=_cparams("arbitrary"),
        name="routed_experts",
    )(blk_e, nact, xs, w_up, b_up.reshape(N_EXPERTS, 1, 2 * D_EXPERT),
      w_down, b_down.reshape(N_EXPERTS, 1, D_MODEL))


def _combine_kernel(h_ref, y_ref, gate_ref, g_ref, b_ref, o_ref):
    gate = gate_ref[...]
    ffn = jnp.zeros(h_ref.shape, F32)
    for kk in range(TOP_K):
        ffn = ffn + gate[:, kk:kk + 1] * y_ref[kk].astype(F32)
    o_ref[...] = _layernorm(DN_ALPHA * h_ref[...] + ffn, g_ref[...], b_ref[...])


def _combine(h, yg, gate, g, b):
    T = h.shape[0]
    return pl.pallas_call(
        _combine_kernel,
        out_shape=jax.ShapeDtypeStruct((T, D_MODEL), F32),
        grid=(T // ROW_TILE,),
        in_specs=[pl.BlockSpec((ROW_TILE, D_MODEL), lambda i: (i, 0)),
                  pl.BlockSpec((TOP_K, ROW_TILE, D_MODEL), lambda i: (0, i, 0)),
                  pl.BlockSpec((ROW_TILE, LANES), lambda i: (i, 0)),
                  _const_spec(g.shape), _const_spec(b.shape)],
        out_specs=pl.BlockSpec((ROW_TILE, D_MODEL), lambda i: (i, 0)),
        compiler_params=_cparams("parallel"),
        name="combine_ln",
    )(h, yg, gate, g, b)


def _rot_cols(w):
    half = w.shape[-1] // 2
    return jnp.concatenate([-w[:, half:], w[:, :half]], axis=-1)


def _pack_w_in(w_in):
    lat = Q_LORA + KV_LORA
    w_kr = w_in[:, lat:lat + MLA_ROPE]
    return jnp.concatenate([w_in[:, lat + MLA_ROPE:], w_in[:, :lat + MLA_ROPE], _rot_cols(w_kr)],
                           axis=-1).astype(BF16)


def _pack_w_uq(w_uq):
    w = w_uq.reshape(Q_LORA, MLA_HEADS, MLA_NOPE + MLA_ROPE)
    rope = w[:, :, MLA_NOPE:]
    rot = jnp.concatenate([-rope[:, :, MLA_ROPE // 2:], rope[:, :, :MLA_ROPE // 2]], axis=-1)
    return jnp.concatenate([w, rot], axis=-1).reshape(Q_LORA, MLA_HEADS * MLA_QK).astype(BF16)


def _rope_freqs():
    ret = 1.0 / (10000.0 ** jnp.linspace(0.0, 1.0, RET_DK // 2, dtype=F32))
    mla = 1.0 / (ROPE_BASE ** (jnp.arange(0, MLA_ROPE, 2, dtype=F32) / MLA_ROPE))
    return jnp.concatenate([ret, mla, jnp.zeros((LANES - ret.shape[0] - mla.shape[0],), F32)])[None, :]


def _routing_plan(idx_out, cnt):
    top_i = idx_out[:, :TOP_K]
    rank = idx_out[:, TOP_K:2 * TOP_K]
    counts = cnt[0, :N_EXPERTS]
    padded = ((counts + EXPERT_TILE - 1) // EXPERT_TILE) * EXPERT_TILE
    pad_end = jnp.cumsum(padded)
    pad_start = pad_end - padded
    dest = pad_start[top_i] + rank
    n_rows = top_i.shape[0] * TOP_K + N_EXPERTS * EXPERT_TILE
    n_blocks = n_rows // EXPERT_TILE
    blk_e = jnp.minimum(jnp.searchsorted(pad_end, jnp.arange(n_blocks, dtype=jnp.int32) * EXPERT_TILE,
                                         side='right'), N_EXPERTS - 1).astype(jnp.int32)
    nact = (pad_end[-1:] // EXPERT_TILE).astype(jnp.int32)
    return dest.astype(jnp.int32), blk_e, nact, n_rows


def kernel(x, positions, w_in, q_norm_g, w_uq, kv_norm_g, w_ukv, w_o, ln1_g, ln1_b,
           w_router, b_router, w_up, b_up, w_down, b_down, ln2_g, ln2_b):
    B, S, _ = x.shape
    T = B * S
    pos_col = positions.reshape(T, 1).astype(jnp.int32)
    rc, rs, mc, ms = _rope_tables(pos_col, _rope_freqs())
    dec, qw, kw, cd = _retention_constants()
    h = x.reshape(T, D_MODEL)
    for l in range(DEPTH):
        proj = _input_projection(h, _pack_w_in(w_in[l]))
        q_cat, kv, kr = _mla_prep(proj, mc, ms, _pack_w_uq(w_uq[l]), w_ukv[l].astype(BF16),
                                  q_norm_g[l][None, :], kv_norm_g[l][None, :])
        o_mla = _attention(q_cat, kv, kr, B, S)
        o_ret = _retention(proj, rc, rs, dec, qw, kw, cd, B, S)
        h1, h1b = _merge(h, o_mla, o_ret, proj, w_o[l].astype(BF16), ln1_g[l][None, :], ln1_b[l][None, :])
        idx_out, gate, cnt = _router(h1, w_router[l], b_router[l])
        dest, blk_e, nact, n_rows = _routing_plan(idx_out, cnt)
        row_tok = jnp.zeros((n_rows,), jnp.int32).at[dest.reshape(-1)].set(
            jnp.arange(T * TOP_K, dtype=jnp.int32) // TOP_K)
        xs = h1b[row_tok]
        ys = _experts(blk_e, nact, xs, w_up[l], b_up[l], w_down[l], b_down[l])
        yg = ys[dest.T]
        h = _combine(h1, yg, gate, ln2_g[l][None, :], ln2_b[l][None, :])
    return h.reshape(B, S, D_MODEL)
```

```python
import functools

import numpy as np
import jax
import jax.numpy as jnp
from jax import lax
from jax.experimental import pallas as pl
from jax.experimental.pallas import tpu as pltpu

F32 = jnp.float32
BF16 = jnp.bfloat16

D_MODEL = 1024
DEPTH = 1
MLA_HEADS = 8
MLA_NOPE = 128
MLA_ROPE = 64
MLA_V = 128
Q_LORA = 384
KV_LORA = 256
ROPE_BASE = 10000.0
RET_HEADS = 8
RET_DK = 128
RET_DV = 128
RET_CHUNK = 128
N_EXPERTS = 32
TOP_K = 4
D_EXPERT = 1024
SWIGLU_LIMIT = 7.0
SWIGLU_ALPHA = 1.702
DN_ALPHA = (2.0 * DEPTH) ** 0.25
LN_EPS = 1e-5
RMS_EPS = 1e-6
GN_EPS = 1e-6
NEG_INF = -1e30

LANES = 128
MLA_QK = 2 * LANES
LATENT_W = 768
N_WIDE = 6 * D_MODEL
N_PROJ = N_WIDE + LATENT_W
VMEM_LIMIT = 56 * 1024 * 1024

ROW_TILE = 512
ATTN_TILE = ROW_TILE
ATTN_Q_SPLIT = 2
RET_GROUP = 4
EXPERT_TILE = 256


def _cparams(*sem):
    return pltpu.CompilerParams(dimension_semantics=sem, vmem_limit_bytes=VMEM_LIMIT)


def _const_spec(shape):
    nd = len(shape)
    return pl.BlockSpec(shape, lambda *_: (0,) * nd, pipeline_mode=pl.Buffered(1))


def _sigmoid(x):
    return 1.0 / (1.0 + jnp.exp(-x))


def _swap_halves(x):
    return pltpu.roll(x, LANES // 2, 1)


def _proj_kernel(x_ref, w_ref, o_ref):
    xb = x_ref[...].astype(BF16)
    for c in range(N_PROJ // LATENT_W):
        sl = slice(c * LATENT_W, (c + 1) * LATENT_W)
        o_ref[:, sl] = jnp.dot(xb, w_ref[:, sl], preferred_element_type=F32).astype(o_ref.dtype)


def _input_projection(x2, w_cat):
    T = x2.shape[0]
    return pl.pallas_call(
        _proj_kernel,
        out_shape=jax.ShapeDtypeStruct((T, N_PROJ), BF16),
        grid=(T // ROW_TILE,),
        in_specs=[pl.BlockSpec((ROW_TILE, D_MODEL), lambda i: (i, 0)),
                  _const_spec((D_MODEL, N_PROJ))],
        out_specs=pl.BlockSpec((ROW_TILE, N_PROJ), lambda i: (i, 0)),
        compiler_params=_cparams("parallel"),
        name="input_projection",
    )(x2, w_cat)


def _rope_table_kernel(pos_ref, freq_ref, rc_ref, rs_ref, mc_ref, ms_ref):
    ang = pos_ref[...].astype(F32) * freq_ref[...]
    c = jnp.cos(ang)
    s = jnp.sin(ang)
    lane = lax.broadcasted_iota(jnp.int32, ang.shape, 1)
    c64 = pltpu.roll(c, 64, 1)
    s64 = pltpu.roll(s, 64, 1)
    rc_ref[...] = jnp.where(lane < 64, c, c64)
    rs_ref[...] = jnp.where(lane < 64, -s, s64)
    c96 = pltpu.roll(c, 96, 1)
    s96 = pltpu.roll(s, 96, 1)
    mc_ref[...] = jnp.where(lane < 32, c64, jnp.where(lane < 64, c96, 0.0))
    ms_ref[...] = jnp.where(lane < 32, s64, jnp.where(lane < 64, s96, 0.0))


def _rope_tables(pos_col, freq_row):
    T = pos_col.shape[0]
    rows = min(2048, T)
    tab =jax.ShapeDtypeStruct((T, LANES), F32)
    spec = pl.BlockSpec((rows, LANES), lambda i: (i, 0))
    return pl.pallas_call(
        _rope_table_kernel,
        out_shape=(tab, tab, tab, tab),
        grid=(T // rows,),
        in_specs=[pl.BlockSpec((rows, 1), lambda i: (i, 0)), _const_spec((1, LANES))],
        out_specs=(spec, spec, spec, spec),
        compiler_params=_cparams("parallel"),
        name="rope_tables",
    )(pos_col, freq_row)


def _mla_prep_kernel(a_ref, mc_ref, ms_ref, wq_ref, wk_ref, wvt_ref, qg_ref, kvg_ref,
                     q_ref, kn_ref, vt_ref, kr_ref):
    a = a_ref[...].astype(F32)
    mc = mc_ref[...]
    ms = ms_ref[...]
    cq = a[:, :Q_LORA]
    qn = cq * lax.rsqrt(jnp.mean(cq * cq, axis=-1, keepdims=True) + RMS_EPS) * qg_ref[...]
    q = jnp.dot(qn.astype(BF16), wq_ref[...], preferred_element_type=F32)
    scale = (MLA_NOPE + MLA_ROPE) ** -0.5
    for h in range(MLA_HEADS):
        nope = q[:, h * MLA_QK:h * MLA_QK + LANES]
        blk = q[:, h * MLA_QK + LANES:(h + 1) * MLA_QK]
        rope = blk * mc + _swap_halves(blk) * ms
        q_ref[:, h * MLA_QK:h * MLA_QK + LANES] = (nope * scale).astype(BF16)
        q_ref[:, h * MLA_QK + LANES:(h + 1) * MLA_QK] = (rope * scale).astype(BF16)
    ckv = a[:, Q_LORA:Q_LORA + KV_LORA]
    kvn = ckv * lax.rsqrt(jnp.mean(ckv * ckv, axis=-1, keepdims=True) + RMS_EPS) * kvg_ref[...]
    kvb = kvn.astype(BF16)
    kn_ref[...] = jnp.dot(kvb, wk_ref[...], preferred_element_type=F32).astype(BF16)
    vt_ref[0] = lax.dot_general(wvt_ref[...], kvb, (((1,), (1,)), ((), ())),
                                preferred_element_type=F32).astype(BF16)
    krb =a[:, Q_LORA + KV_LORA:]
    kr_ref[...] = (krb * mc + _swap_halves(krb) * ms).astype(BF16)


def _mla_prep(proj, mc, ms, wq, wk, wvt, qg, kvg):
    T = proj.shape[0]
    row = lambda w: pl.BlockSpec((ROW_TILE, w), lambda i: (i, 0))
    return pl.pallas_call(
        _mla_prep_kernel,
        out_shape=(jax.ShapeDtypeStruct((T, MLA_HEADS * MLA_QK), BF16),
                   jax.ShapeDtypeStruct((T, MLA_HEADS * MLA_NOPE), BF16),
                   jax.ShapeDtypeStruct((T // ROW_TILE, MLA_HEADS * MLA_V, ROW_TILE), BF16),
                   jax.ShapeDtypeStruct((T, LANES), BF16)),
        grid=(T // ROW_TILE,),
        in_specs=[pl.BlockSpec((ROW_TILE, LATENT_W), lambda i: (i, N_WIDE // LATENT_W)),
                  row(LANES), row(LANES),
                  _const_spec(wq.shape), _const_spec(wk.shape), _const_spec(wvt.shape),
                  _const_spec(qg.shape), _const_spec(kvg.shape)],
        out_specs=(row(MLA_HEADS * MLA_QK), row(MLA_HEADS * MLA_NOPE),
                   pl.BlockSpec((1, MLA_HEADS * MLA_V, ROW_TILE), lambda i: (i, 0, 0)), row(LANES)),
        compiler_params=_cparams("parallel"),
        name="mla_prep",
    )(proj, mc, ms, wq, wk, wvt, qg, kvg)


def _attn_kernel(q_ref, kn_ref, kr_ref, vt_ref, o_ref, acc_scr):
    qi = pl.program_id(2)
    W = ATTN_TILE // ATTN_Q_SPLIT
    qs = [q_ref[c * W:(c + 1) * W, :] for c in range(ATTN_Q_SPLIT)]
    acc_scr[...] = jnp.zeros(acc_scr.shape, F32)

    def scores(j):
        rows = pl.ds(pl.multiple_of(j * ATTN_TILE, ATTN_TILE), ATTN_TILE)
        k = jnp.concatenate([kn_ref[rows, :], kr_ref[rows, :]], axis=-1)
        return tuple(lax.dot_general(k, qs[c], (((1,), (1,)), ((), ())), preferred_element_type=F32)
                     for c in range(ATTN_Q_SPLIT))

    def softmax_pv(j, s_all, stats, masked):
        vt = vt_ref[j]
        out = []
        for c in range(ATTN_Q_SPLIT):
            m_prev, l_prev = stats[c]
            s = s_all[c]
            if masked:
                kp = lax.broadcasted_iota(jnp.int32, s.shape, 0)
                qp = lax.broadcasted_iota(jnp.int32, s.shape, 1) + c * W
                s = jnp.where(kp <= qp, s, NEG_INF)
            m_new = jnp.maximum(m_prev, jnp.max(s, axis=0, keepdims=True))
            alpha = jnp.exp(m_prev - m_new)
            p = jnp.exp(s - m_new)
            l_new = alpha * l_prev + jnp.sum(p, axis=0, keepdims=True)
            acc_scr[c] = alpha * acc_scr[c] + jnp.dot(vt, p.astype(BF16), preferred_element_type=F32)
            out.append((m_new, l_new))
        return tuple(out)

    def body(j, carry):
        s_cur, stats = carry
        s_next = scores(j + 1)
        return s_next, softmax_pv(j, s_cur, stats, False)

    init = tuple((jnp.full((1, W), NEG_INF, F32), jnp.zeros((1, W), F32)) for _ in range(ATTN_Q_SPLIT))
    s_last, stats = lax.fori_loop(0, qi, body, (scores(0), init))
    stats = softmax_pv(qi, s_last, stats, True)
    for c in range(ATTN_Q_SPLIT):
        o_ref[c * W:(c + 1) * W, :] = (acc_scr[c] / stats[c][1]).T.astype(o_ref.dtype)


def _attention(q_cat, kn, kr, vt, B, S):
    T = B * S
    nq = S // ATTN_TILE
    return pl.pallas_call(
        _attn_kernel,
        out_shape=jax.ShapeDtypeStruct((T, MLA_HEADS * MLA_V), BF16),
        grid=(B, MLA_HEADS, nq),
        in_specs=[pl.BlockSpec((ATTN_TILE, MLA_QK), lambda b, h, i: (b * nq + i, h)),
                  pl.BlockSpec((S, MLA_NOPE), lambda b, h, i: (b, h)),
                  pl.BlockSpec((S, LANES), lambda b, h, i: (b, 0)),
                  pl.BlockSpec((nq, MLA_V, ATTN_TILE), lambda b, h, i: (b, h, 0))],
        out_specs=pl.BlockSpec((ATTN_TILE, MLA_V), lambda b, h, i: (b * nq + i, h)),
        scratch_shapes=[pltpu.VMEM((ATTN_Q_SPLIT, MLA_V, ATTN_TILE // ATTN_Q_SPLIT), F32)],
        compiler_params=_cparams("parallel", "parallel", "arbitrary"),
        name="mla_attention",
    )(q_cat, kn, kr, vt)


def _ret_kernel(q_ref, k_ref, v_ref, g_ref, rc_ref, rs_ref, dec_ref, qw_ref, kw_ref, cd_ref,
                o_ref, qb_scr, qwb_scr, kb_scr, kwb_scr):
    S = q_ref.shape[0]
    C = RET_CHUNK
    n_chunks = S // C
    rc = rc_ref[...]
    rs = rs_ref[...]
    q = q_ref[...].astype(F32)
    k = k_ref[...].astype(F32)
    q = q * rc + _swap_halves(q) * rs
    k = (k * rc + _swap_halves(k) * rs) * (RET_DK ** -0.5)
    qw = qw_ref[0]
    kw = kw_ref[0]
    q3 = q.reshape(n_chunks, C, RET_DK)
    k3 = k.reshape(n_chunks, C, RET_DK)
    qb_scr[...] = q.astype(BF16)
    kb_scr[...] = k.astype(BF16)
    qwb_scr[...] = (q3 * qw[None]).reshape(S, RET_DK).astype(BF16)
    kwb_scr[...] = (k3 * kw[None]).reshape(S, RET_DK)
    dec = dec_ref[0]
    cd = cd_ref[0]

    def group(gi, R):
        for u in range(RET_GROUP):
            rows = pl.ds(pl.multiple_of((gi * RET_GROUP + u) * C, C), C)
            qn = qb_scr[rows, :]
            kn = kb_scr[rows, :]
            vn = v_ref[rows, :]
            s = lax.dot_general(qn, kn, (((1,), (1,)), ((), ())), preferred_element_type=F32) * dec
            o = jnp.dot(s.astype(BF16), vn, preferred_element_type=F32)
            o = o + jnp.dot(qwb_scr[rows, :], R.astype(BF16), preferred_element_type=F32)
            kwt = kwb_scr[rows, :].T.astype(BF16)
            R = cd * R + jnp.dot(kwt, vn, preferred_element_type=F32)
            mu = jnp.mean(o, axis=-1, keepdims=True)
            d = o - mu
            var = jnp.mean(d * d, axis=-1, keepdims=True)
            gate = g_ref[rows, :].astype(F32)
            o_ref[rows, :] = (d * lax.rsqrt(var + GN_EPS) * (gate * _sigmoid(gate))).astype(o_ref.dtype)
        return R

    lax.fori_loop(0, n_chunks // RET_GROUP, group, jnp.zeros((RET_DK, RET_DV), F32))


def _retention(proj, rc, rs, dec, qw, kw, cd, B, S):
    T = B * S
    H = RET_HEADS
    col = lambda off: pl.BlockSpec((S, LANES), lambda b, h: (b, off + h))
    tab = pl.BlockSpec((S, LANES), lambda b, h: (b, 0))
    per_head = lambda r: pl.BlockSpec((1, r, LANES), lambda b, h: (h, 0, 0))
    return pl.pallas_call(
        _ret_kernel,
        out_shape=jax.ShapeDtypeStruct((T, H * RET_DV), BF16),
        grid=(B, H),
        in_specs=[col(0), col(H), col(2 * H), col(3 * H), tab, tab,
                  per_head(RET_CHUNK), per_head(RET_CHUNK), per_head(RET_CHUNK), per_head(1)],
        out_specs=pl.BlockSpec((S, RET_DV), lambda b, h: (b, h)),
        scratch_shapes=[pltpu.VMEM((S, RET_DK), BF16), pltpu.VMEM((S, RET_DK), BF16),
                        pltpu.VMEM((S, RET_DK), BF16), pltpu.VMEM((S, RET_DK), F32)],
        compiler_params=_cparams("parallel", "parallel"),
        name="retention",
    )(proj, proj, proj, proj, rc, rs, dec, qw, kw, cd)


def _retention_constants():
    C = RET_CHUNK
    h = np.arange(RET_HEADS, dtype=np.float64)
    log_gamma = np.log(1.0 - 2.0 ** (-5.0 - h))
    idx = np.arange(C, dtype=np.float64)
    rel = idx[:, None] - idx[None, :]
    dec = np.where(rel[None] >= 0, np.exp(np.maximum(rel, 0.0)[None] * log_gamma[:, None, None]), 0.0)
    qw = np.exp((idx + 1.0)[None, :] * log_gamma[:, None])
    kw = np.exp((C - 1.0 - idx)[None, :] * log_gamma[:, None])
    cd = np.exp(C * log_gamma)
    bc = lambda a: np.ascontiguousarray(np.broadcast_to(a[..., None], a.shape + (LANES,))).astype(np.float32)
    return dec.astype(np.float32), bc(qw), bc(kw), bc(cd[:, None])


def _layernorm(y, g, b):
    mu = jnp.mean(y, axis=-1, keepdims=True)
    d = y - mu
    var = jnp.mean(d * d, axis=-1, keepdims=True)
    return d * lax.rsqrt(var + LN_EPS) * g + b


def _merge_kernel(x_ref, om_ref, or_ref, gm_ref, gr_ref, wo_ref, g_ref, b_ref, h_ref, hb_ref):
    mixed = (_sigmoid(gm_ref[...].astype(F32)) * om_ref[...].astype(F32)
             + _sigmoid(gr_ref[...].astype(F32)) * or_ref[...].astype(F32))
    mix = jnp.dot(mixed.astype(BF16), wo_ref[...], preferred_element_type=F32)
    h = _layernorm(DN_ALPHA * x_ref[...] + mix, g_ref[...], b_ref[...])
    h_ref[...] = h
    hb_ref[...] = h.astype(BF16)


def _merge(x2, o_mla, o_ret, proj, wo, g, b):
    T = x2.shape[0]
    row = lambda c: pl.BlockSpec((ROW_TILE, D_MODEL), lambda i: (i, c))
    return pl.pallas_call(
        _merge_kernel,
        out_shape=(jax.ShapeDtypeStruct((T, D_MODEL), F32), jax.ShapeDtypeStruct((T, D_MODEL), BF16)),
        grid=(T // ROW_TILE,),
        in_specs=[row(0), row(0), row(0), row(4), row(5),
                  _const_spec(wo.shape), _const_spec(g.shape), _const_spec(b.shape)],
        out_specs=(row(0), row(0)),
        compiler_params=_cparams("parallel"),
        name="merge_out_proj_ln",
    )(x2, o_mla, o_ret, proj, proj, wo, g, b)


def _split3(a):
    a0 = a.astype(BF16)
    r1 = a - a0.astype(F32)
    a1 = r1.astype(BF16)
    a2 = (r1 - a1.astype(F32)).astype(BF16)
    return a0, a1, a2


def _router_kernel(h_ref, w0_ref, w1_ref, w2_ref, b_ref, idx_ref, gate_ref, cnt_ref, run_scr):
    i = pl.program_id(0)

    @pl.when(i == 0)
    def _():
        run_scr[...] = jnp.zeros(run_scr.shape, F32)

    h0, h1, h2 = _split3(h_ref[...])
    w0, w1, w2 = w0_ref[...], w1_ref[...], w2_ref[...]
    dot = lambda a, w: jnp.dot(a, w, preferred_element_type=F32)
    logits = (dot(h0, w0) + (dot(h0, w1) + dot(h1, w0))
              + (dot(h0, w2) + dot(h1, w1) + dot(h2, w0))) + b_ref[...]
    rows = logits.shape[0]
    lane = lax.broadcasted_iota(jnp.int32, (rows, LANES), 1)
    lane_f = lane.astype(F32)
    work = jnp.where(lane < N_EXPERTS, logits, -jnp.inf)
    vals, ids, sels = [], [], []
    for _ in range(TOP_K):
        m = jnp.max(work, axis=-1, keepdims=True)
        first = jnp.min(jnp.where(work == m, lane_f, float(LANES)), axis=-1, keepdims=True)
        sel = lane_f == first
        work = jnp.where(sel, -jnp.inf, work)
        vals.append(m)
        ids.append(first)
        sels.append(sel)
    exps = [jnp.exp(v - vals[0]) for v in vals]
    denom = exps[0] + exps[1] + exps[2] + exps[3]
    chosen = jnp.zeros((rows, LANES), F32)
    for sel in sels:
        chosen = jnp.where(sel, 1.0, chosen)
    r = lax.broadcasted_iota(jnp.int32, (rows, rows), 0)
    c = lax.broadcasted_iota(jnp.int32, (rows, rows), 1)
    lower = jnp.where(r > c, 1.0, 0.0).astype(BF16)
    before = jnp.dot(lower, chosen.astype(BF16), preferred_element_type=F32) + run_scr[...]
    idx_out = jnp.zeros((rows, LANES), jnp.int32)
    gate_out = jnp.zeros((rows, LANES), F32)
    for kk in range(TOP_K):
        rank = jnp.sum(jnp.where(sels[kk], before, 0.0), axis=-1, keepdims=True).astype(jnp.int32)
        idx_out = jnp.where(lane == kk, ids[kk].astype(jnp.int32), idx_out)
        idx_out = jnp.where(lane == TOP_K + kk, rank, idx_out)
        gate_out = jnp.where(lane == kk, exps[kk] / denom, gate_out)
    idx_ref[...] = idx_out
    gate_ref[...] = gate_out
    total = run_scr[...] + jnp.sum(chosen, axis=0, keepdims=True)
    run_scr[...] = total
    cnt_ref[...] = total.astype(jnp.int32)


def _router(h, w_router, b_router):
    T = h.shape[0]
    wpad = jnp.zeros((D_MODEL, LANES), F32).at[:, :N_EXPERTS].set(w_router)
    w0 = wpad.astype(BF16)
    r1 = wpad - w0.astype(F32)
    w1 = r1.astype(BF16)
    w2 = (r1 - w1.astype(F32)).astype(BF16)
    bpad = jnp.zeros((1, LANES), F32).at[0, :N_EXPERTS].set(b_router)
    row = pl.BlockSpec((ROW_TILE, LANES), lambda i: (i, 0))
    return pl.pallas_call(
        _router_kernel,
        out_shape=(jax.ShapeDtypeStruct((T, LANES), jnp.int32),
                   jax.ShapeDtypeStruct((T, LANES), F32),
                   jax.ShapeDtypeStruct((1, LANES), jnp.int32)),
        grid=(T // ROW_TILE,),
        in_specs=[pl.BlockSpec((ROW_TILE, D_MODEL), lambda i: (i, 0)),
                  _const_spec(w0.shape), _const_spec(w0.shape), _const_spec(w0.shape),
                  _const_spec(bpad.shape)],
        out_specs=(row, row, pl.BlockSpec((1, LANES), lambda i: (0, 0))),
        scratch_shapes=[pltpu.VMEM((1, LANES), F32)],
        compiler_params=_cparams("arbitrary"),
        name="router",
    )(h, w0, w1, w2, bpad)


def _expert_kernel(blk_e_ref, nact_ref, x_ref, wu_ref, bu_ref, wd_ref, bd_ref, y_ref,
                   wub_scr, wdb_scr, last_scr):
    i = pl.program_id(0)
    e = blk_e_ref[i]

    @pl.when(i == 0)
    def _():
        last_scr[0] = -1

    @pl.when(jnp.logical_and(i < nact_ref[0], e != last_scr[0]))
    def _():
        wub_scr[...] = wu_ref[0].astype(BF16)
        wdb_scr[...] = wd_ref[0].astype(BF16)
        last_scr[0] = e

    @pl.when(i < nact_ref[0])
    def _():
        x = x_ref[...]
        acc = jnp.zeros((EXPERT_TILE, D_MODEL), F32)
        step = 256
        for c in range(D_EXPERT // step):
            glu = jnp.dot(x, wub_scr[:, c * step:(c + 1) * step], preferred_element_type=F32)
            lin = jnp.dot(x, wub_scr[:, D_EXPERT + c * step:D_EXPERT + (c + 1) * step],
                          preferred_element_type=F32)
            glu = jnp.minimum(glu + bu_ref[0, :, c * step:(c + 1) * step], SWIGLU_LIMIT)
            lin = jnp.clip(lin + bu_ref[0, :, D_EXPERT + c * step:D_EXPERT + (c + 1) * step],
                           -SWIGLU_LIMIT, SWIGLU_LIMIT)
            a = glu * _sigmoid(SWIGLU_ALPHA * glu) * (lin + 1.0)
            acc = acc + jnp.dot(a.astype(BF16), wdb_scr[c * step:(c + 1) * step, :],
                                preferred_element_type=F32)
        y_ref[...] = (acc + bd_ref[0]).astype(y_ref.dtype)

    @pl.when(i >= nact_ref[0])
    def _():
        y_ref[...] = jnp.zeros(y_ref.shape, y_ref.dtype)


def _experts(blk_e, nact, xs, w_up, b_up, w_down, b_down):
    P = xs.shape[0]
    n_blocks = P // EXPERT_TILE
    act = lambda i, na: jnp.minimum(i, na[0] - 1)
    grid_spec = pltpu.PrefetchScalarGridSpec(
        num_scalar_prefetch=2,
        grid=(n_blocks,),
        in_specs=[pl.BlockSpec((EXPERT_TILE, D_MODEL), lambda i, be, na: (act(i, na), 0)),
                  pl.BlockSpec((1, D_MODEL, 2 * D_EXPERT), lambda i, be, na: (be[act(i, na)], 0, 0)),
                  pl.BlockSpec((1, 1, 2 * D_EXPERT), lambda i, be, na: (be[act(i, na)], 0, 0)),
                  pl.BlockSpec((1, D_EXPERT, D_MODEL), lambda i, be, na: (be[act(i, na)], 0, 0)),
                  pl.BlockSpec((1, 1, D_MODEL), lambda i, be, na: (be[act(i, na)], 0, 0))],
        out_specs=pl.BlockSpec((EXPERT_TILE, D_MODEL), lambda i, be, na: (i, 0)),
        scratch_shapes=[pltpu.VMEM((D_MODEL, 2 * D_EXPERT), BF16),
                        pltpu.VMEM((D_EXPERT, D_MODEL), BF16),
                        pltpu.SMEM((1,), jnp.int32)],
    )
    return pl.pallas_call(
        _expert_kernel,
        out_shape=jax.ShapeDtypeStruct((P, D_MODEL), BF16),
        grid_spec=grid_spec,
        compiler_params=_cparams("arbitrary"),
        name="routed_experts",
    )(blk_e, nact, xs, w_up, b_up.reshape(N_EXPERTS, 1, 2 * D_EXPERT),
      w_down, b_down.reshape(N_EXPERTS, 1, D_MODEL))


def _combine_kernel(h_ref, y_ref, gate_ref, g_ref, b_ref, o_ref):
    gate = gate_ref[...]
    ffn = jnp.zeros(h_ref.shape, F32)
    for kk in range(TOP_K):
        ffn = ffn + gate[:, kk:kk + 1] * y_ref[kk].astype(F32)
    o_ref[...] = _layernorm(DN_ALPHA * h_ref[...] + ffn, g_ref[...], b_ref[...])


def _combine(h, yg, gate, g, b):
    T = h.shape[0]
    return pl.pallas_call(
        _combine_kernel,
        out_shape=jax.ShapeDtypeStruct((T, D_MODEL), F32),
        grid=(T // ROW_TILE,),
        in_specs=[pl.BlockSpec((ROW_TILE, D_MODEL), lambda i: (i, 0)),
                  pl.BlockSpec((TOP_K, ROW_TILE, D_MODEL), lambda i: (0, i, 0)),
                  pl.BlockSpec((ROW_TILE, LANES), lambda i: (i, 0)),
                  _const_spec(g.shape), _const_spec(b.shape)],
        out_specs=pl.BlockSpec((ROW_TILE, D_MODEL), lambda i: (i, 0)),
        compiler_params=_cparams("parallel"),
        name="combine_ln",
    )(h, yg, gate, g, b)


def _rot_cols(w):
    half = w.shape[-1] // 2
    return jnp.concatenate([-w[:, half:], w[:, :half]], axis=-1)


def _pack_w_in(w_in):
    lat = Q_LORA + KV_LORA
    w_kr = w_in[:, lat:lat + MLA_ROPE]
    return jnp.concatenate([w_in[:, lat + MLA_ROPE:], w_in[:, :lat + MLA_ROPE], _rot_cols(w_kr)],
                           axis=-1).astype(BF16)


def _pack_w_uq(w_uq):
    w = w_uq.reshape(Q_LORA, MLA_HEADS, MLA_NOPE + MLA_ROPE)
    rope = w[:, :, MLA_NOPE:]
    rot = jnp.concatenate([-rope[:, :, MLA_ROPE // 2:], rope[:, :, :MLA_ROPE // 2]], axis=-1)
    return jnp.concatenate([w, rot], axis=-1).reshape(Q_LORA, MLA_HEADS * MLA_QK).astype(BF16)


def _rope_freqs():
    ret = 1.0 / (10000.0 ** jnp.linspace(0.0, 1.0, RET_DK // 2, dtype=F32))
    mla = 1.0 / (ROPE_BASE ** (jnp.arange(0, MLA_ROPE, 2, dtype=F32) / MLA_ROPE))
    return jnp.concatenate([ret, mla, jnp.zeros((LANES - ret.shape[0] - mla.shape[0],), F32)])[None, :]


def _routing_plan(idx_out, cnt):
    top_i = idx_out[:, :TOP_K]
    rank = idx_out[:, TOP_K:2 * TOP_K]
    counts = cnt[0, :N_EXPERTS]
    padded = ((counts + EXPERT_TILE - 1) // EXPERT_TILE) * EXPERT_TILE
    pad_end = jnp.cumsum(padded)
    pad_start = pad_end - padded
    dest = pad_start[top_i] + rank
    n_rows = top_i.shape[0] * TOP_K + N_EXPERTS * EXPERT_TILE
    n_blocks = n_rows // EXPERT_TILE
    blk_e = jnp.minimum(jnp.searchsorted(pad_end, jnp.arange(n_blocks, dtype=jnp.int32) * EXPERT_TILE,
                                         side='right'), N_EXPERTS - 1).astype(jnp.int32)
    nact = (pad_end[-1:] // EXPERT_TILE).astype(jnp.int32)
    return dest.astype(jnp.int32), blk_e, nact, n_rows


def kernel(x, positions, w_in, q_norm_g, w_uq, kv_norm_g, w_ukv, w_o, ln1_g, ln1_b,
           w_router, b_router, w_up, b_up, w_down, b_down, ln2_g, ln2_b):
    B, S, _ = x.shape
    T = B * S
    pos_col = positions.reshape(T, 1).astype(jnp.int32)
    rc, rs, mc, ms = _rope_tables(pos_col, _rope_freqs())
    dec, qw, kw, cd = _retention_constants()
    h = x.reshape(T, D_MODEL)
    for l in range(DEPTH):
        proj = _input_projection(h, _pack_w_in(w_in[l]))
        w_kv = w_ukv[l].reshape(KV_LORA, MLA_HEADS, MLA_NOPE + MLA_V)
        w_k = w_kv[:, :, :MLA_NOPE].reshape(KV_LORA, MLA_HEADS * MLA_NOPE).astype(BF16)
        w_vt = w_kv[:, :, MLA_NOPE:].reshape(KV_LORA, MLA_HEADS * MLA_V).T.astype(BF16)
        q_cat, kn, vt, kr = _mla_prep(proj, mc, ms, _pack_w_uq(w_uq[l]), w_k, w_vt,
                                      q_norm_g[l][None, :], kv_norm_g[l][None, :])
        o_mla = _attention(q_cat, kn, kr, vt, B, S)
        o_ret = _retention(proj, rc, rs, dec, qw, kw, cd, B, S)
        h1, h1b = _merge(h, o_mla, o_ret, proj, w_o[l].astype(BF16), ln1_g[l][None, :], ln1_b[l][None, :])
        idx_out, gate, cnt = _router(h1, w_router[l], b_router[l])
        dest, blk_e, nact, n_rows = _routing_plan(idx_out, cnt)
        row_tok = jnp.zeros((n_rows,), jnp.int32).at[dest.reshape(-1)].set(
            jnp.arange(T * TOP_K, dtype=jnp.int32) // TOP_K)
        xs = h1b[row_tok]
        ys = _experts(blk_e, nact, xs, w_up[l], b_up[l], w_down[l], b_down[l])
        yg = ys[dest.T]
        h = _combine(h1, yg, gate, ln2_g[l][None, :], ln2_b[l][None, :])
    return h.reshape(B, S, D_MODEL)
```

```python
import functools

import numpy as np
import jax
import jax.numpy as jnp
from jax import lax
from jax.experimental import pallas as pl
from jax.experimental.pallas import tpu as pltpu

F32 = jnp.float32
BF16 = jnp.bfloat16

D_MODEL = 1024
DEPTH = 1
MLA_HEADS = 8
MLA_NOPE = 128
MLA_ROPE = 64
MLA_V = 128
Q_LORA = 384
KV_LORA = 256
ROPE_BASE = 10000.0
RET_HEADS = 8
RET_DK = 128
RET_DV = 128
RET_CHUNK = 128
N_EXPERTS = 32
TOP_K = 4
D_EXPERT = 1024
SWIGLU_LIMIT = 7.0
SWIGLU_ALPHA = 1.702
DN_ALPHA = (2.0 * DEPTH) ** 0.25
LN_EPS = 1e-5
RMS_EPS = 1e-6
GN_EPS = 1e-6
NEG_INF = -1e30

LANES = 128
MLA_QK = 2 * LANES
PACKED_W = D_MODEL // 2
LATENT_W = 768
N_WIDE = 6 * D_MODEL
N_PROJ = N_WIDE + LATENT_W
VMEM_LIMIT = 56 * 1024 * 1024

ROW_TILE = 512
ATTN_TILE = ROW_TILE
ATTN_Q_SPLIT = 2
RET_GROUP = 4
EXPERT_TILE = 256
MOVE_TILE = 256


def _cparams(*sem):
    return pltpu.CompilerParams(dimension_semantics=sem, vmem_limit_bytes=VMEM_LIMIT)


def _const_spec(shape):
    nd = len(shape)
    return pl.BlockSpec(shape, lambda *_: (0,) * nd, pipeline_mode=pl.Buffered(1))


def _sigmoid(x):
    return 1.0 / (1.0 + jnp.exp(-x))


def _pack_rows(x):
    n = x.shape[-1] // 2
    xb = x.astype(BF16).astype(F32)
    lo = pltpu.bitcast(xb[:, :n], jnp.uint32) >> 16
    hi = pltpu.bitcast(xb[:, n:], jnp.uint32) & jnp.uint32(0xFFFF0000)
    return lo | hi


def _unpack_rows(w):
    lo = pltpu.bitcast(w << 16, F32)
    hi = pltpu.bitcast(w & jnp.uint32(0xFFFF0000), F32)
    return lo, hi


def _swap_halves(x):
    return pltpu.roll(x, LANES // 2, 1)


def _proj_kernel(x_ref, w_ref, o_ref):
    xb = x_ref[...].astype(BF16)
    for c in range(N_PROJ // LATENT_W):
        sl = slice(c * LATENT_W, (c + 1) * LATENT_W)
        o_ref[:, sl] = jnp.dot(xb, w_ref[:, sl], preferred_element_type=F32).astype(o_ref.dtype)


def _input_projection(x2, w_cat):
    T = x2.shape[0]
    return pl.pallas_call(
        _proj_kernel,
        out_shape=jax.ShapeDtypeStruct((T, N_PROJ), BF16),
        grid=(T // ROW_TILE,),
        in_specs=[pl.BlockSpec((ROW_TILE, D_MODEL), lambda i: (i, 0)),
                  _const_spec((D_MODEL, N_PROJ))],
        out_specs=pl.BlockSpec((ROW_TILE, N_PROJ), lambda i: (i, 0)),
        compiler_params=_cparams("parallel"),
        name="input_projection",
    )(x2, w_cat)


def _rope_table_kernel(pos_ref, freq_ref, rc_ref, rs_ref, mc_ref, ms_ref):
    ang = pos_ref[...].astype(F32) * freq_ref[...]
    c = jnp.cos(ang)
    s = jnp.sin(ang)
    lane = lax.broadcasted_iota(jnp.int32, ang.shape, 1)
    c64 = pltpu.roll(c, 64, 1)
    s64 = pltpu.roll(s, 64, 1)
    rc_ref[...] = jnp.where(lane < 64, c, c64)
    rs_ref[...] = jnp.where(lane < 64, -s, s64)
    c96 = pltpu.roll(c, 96, 1)
    s96 = pltpu.roll(s, 96, 1)
    mc_ref[...] = jnp.where(lane < 32, c64, jnp.where(lane < 64, c96, 0.0))
    ms_ref[...] = jnp.where(lane < 32, s64, jnp.where(lane < 64, s96, 0.0))


def _rope_tables(pos_col, freq_row):
    T = pos_col.shape[0]
    rows = min(2048, T)
    tab =jax.ShapeDtypeStruct((T, LANES), F32)
    spec = pl.BlockSpec((rows, LANES), lambda i: (i, 0))
    return pl.pallas_call(
        _rope_table_kernel,
        out_shape=(tab, tab, tab, tab),
        grid=(T // rows,),
        in_specs=[pl.BlockSpec((rows, 1), lambda i: (i, 0)), _const_spec((1, LANES))],
        out_specs=(spec, spec, spec, spec),
        compiler_params=_cparams("parallel"),
        name="rope_tables",
    )(pos_col, freq_row)


def _mla_prep_kernel(a_ref, mc_ref, ms_ref, wq_ref, wk_ref, wvt_ref, qg_ref, kvg_ref,
                     q_ref, kn_ref, vt_ref, kr_ref):
    a = a_ref[...].astype(F32)
    mc = mc_ref[...]
    ms = ms_ref[...]
    cq = a[:, :Q_LORA]
    qn = cq * lax.rsqrt(jnp.mean(cq * cq, axis=-1, keepdims=True) + RMS_EPS) * qg_ref[...]
    q = jnp.dot(qn.astype(BF16), wq_ref[...], preferred_element_type=F32)
    scale = (MLA_NOPE + MLA_ROPE) ** -0.5
    for h in range(MLA_HEADS):
        nope = q[:, h * MLA_QK:h * MLA_QK + LANES]
        blk = q[:, h * MLA_QK + LANES:(h + 1) * MLA_QK]
        rope = blk * mc + _swap_halves(blk) * ms
        q_ref[:, h * MLA_QK:h * MLA_QK + LANES] = (nope * scale).astype(BF16)
        q_ref[:, h * MLA_QK + LANES:(h + 1) * MLA_QK] = (rope * scale).astype(BF16)
    ckv = a[:, Q_LORA:Q_LORA + KV_LORA]
    kvn = ckv * lax.rsqrt(jnp.mean(ckv * ckv, axis=-1, keepdims=True) + RMS_EPS) * kvg_ref[...]
    kvb = kvn.astype(BF16)
    kn_ref[...] = jnp.dot(kvb, wk_ref[...], preferred_element_type=F32).astype(BF16)
    vt_ref[0] = lax.dot_general(wvt_ref[...], kvb, (((1,), (1,)), ((), ())),
                                preferred_element_type=F32).astype(BF16)
    krb =a[:, Q_LORA + KV_LORA:]
    kr_ref[...] = (krb * mc + _swap_halves(krb) * ms).astype(BF16)


def _mla_prep(proj, mc, ms, wq, wk, wvt, qg, kvg):
    T = proj.shape[0]
    row = lambda w: pl.BlockSpec((ROW_TILE, w), lambda i: (i, 0))
    return pl.pallas_call(
        _mla_prep_kernel,
        out_shape=(jax.ShapeDtypeStruct((T, MLA_HEADS * MLA_QK), BF16),
                   jax.ShapeDtypeStruct((T, MLA_HEADS * MLA_NOPE), BF16),
                   jax.ShapeDtypeStruct((T // ROW_TILE, MLA_HEADS * MLA_V, ROW_TILE), BF16),
                   jax.ShapeDtypeStruct((T, LANES), BF16)),
        grid=(T // ROW_TILE,),
        in_specs=[pl.BlockSpec((ROW_TILE, LATENT_W), lambda i: (i, N_WIDE // LATENT_W)),
                  row(LANES), row(LANES),
                  _const_spec(wq.shape), _const_spec(wk.shape), _const_spec(wvt.shape),
                  _const_spec(qg.shape), _const_spec(kvg.shape)],
        out_specs=(row(MLA_HEADS * MLA_QK), row(MLA_HEADS * MLA_NOPE),
                   pl.BlockSpec((1, MLA_HEADS * MLA_V, ROW_TILE), lambda i: (i, 0, 0)), row(LANES)),
        compiler_params=_cparams("parallel"),
        name="mla_prep",
    )(proj, mc, ms, wq, wk, wvt, qg, kvg)


def _attn_kernel(q_ref, kn_ref, kr_ref, vt_ref, o_ref, acc_scr):
    qi = pl.program_id(2)
    W = ATTN_TILE // ATTN_Q_SPLIT
    qs = [q_ref[c * W:(c + 1) * W, :] for c in range(ATTN_Q_SPLIT)]
    acc_scr[...] = jnp.zeros(acc_scr.shape, F32)

    def scores(j):
        rows = pl.ds(pl.multiple_of(j * ATTN_TILE, ATTN_TILE), ATTN_TILE)
        k = jnp.concatenate([kn_ref[rows, :], kr_ref[rows, :]], axis=-1)
        return tuple(lax.dot_general(k, qs[c], (((1,), (1,)), ((), ())), preferred_element_type=F32)
                     for c in range(ATTN_Q_SPLIT))

    def softmax_pv(j, s_all, stats, masked):
        vt = vt_ref[j]
        out = []
        for c in range(ATTN_Q_SPLIT):
            m_prev, l_prev = stats[c]
            s = s_all[c]
            if masked:
                kp = lax.broadcasted_iota(jnp.int32, s.shape, 0)
                qp = lax.broadcasted_iota(jnp.int32, s.shape, 1) + c * W
                s = jnp.where(kp <= qp, s, NEG_INF)
            m_new = jnp.maximum(m_prev, jnp.max(s, axis=0, keepdims=True))
            alpha = jnp.exp(m_prev - m_new)
            p = jnp.exp(s - m_new)
            l_new = alpha * l_prev + jnp.sum(p, axis=0, keepdims=True)
            acc_scr[c] = alpha * acc_scr[c] + jnp.dot(vt, p.astype(BF16), preferred_element_type=F32)
            out.append((m_new, l_new))
        return tuple(out)

    def body(j, carry):
        s_cur, stats = carry
        s_next = scores(j + 1)
        return s_next, softmax_pv(j, s_cur, stats, False)

    init = tuple((jnp.full((1, W), NEG_INF, F32), jnp.zeros((1, W), F32)) for _ in range(ATTN_Q_SPLIT))
    s_last, stats = lax.fori_loop(0, qi, body, (scores(0), init))
    stats = softmax_pv(qi, s_last, stats, True)
    for c in range(ATTN_Q_SPLIT):
        o_ref[c * W:(c + 1) * W, :] = (acc_scr[c] / stats[c][1]).T.astype(o_ref.dtype)


def _attention(q_cat, kn, kr, vt, B, S):
    T = B * S
    nq = S // ATTN_TILE
    return pl.pallas_call(
        _attn_kernel,
        out_shape=jax.ShapeDtypeStruct((T, MLA_HEADS * MLA_V), BF16),
        grid=(B, MLA_HEADS, nq),
        in_specs=[pl.BlockSpec((ATTN_TILE, MLA_QK), lambda b, h, i: (b * nq + i, h)),
                  pl.BlockSpec((S, MLA_NOPE), lambda b, h, i: (b, h)),
                  pl.BlockSpec((S, LANES), lambda b, h, i: (b, 0)),
                  pl.BlockSpec((nq, MLA_V, ATTN_TILE), lambda b, h, i: (b, h, 0))],
        out_specs=pl.BlockSpec((ATTN_TILE, MLA_V), lambda b, h, i: (b * nq + i, h)),
        scratch_shapes=[pltpu.VMEM((ATTN_Q_SPLIT, MLA_V, ATTN_TILE // ATTN_Q_SPLIT), F32)],
        compiler_params=_cparams("parallel", "parallel", "arbitrary"),
        name="mla_attention",
    )(q_cat, kn, kr, vt)


def _ret_kernel(q_ref, k_ref, v_ref, g_ref, rc_ref, rs_ref, dec_ref, qw_ref, kw_ref, cd_ref,
                o_ref, qb_scr, qwb_scr, kb_scr, kwb_scr):
    S = q_ref.shape[0]
    C = RET_CHUNK
    n_chunks = S // C
    rc = rc_ref[...]
    rs = rs_ref[...]
    q = q_ref[...].astype(F32)
    k = k_ref[...].astype(F32)
    q = q * rc + _swap_halves(q) * rs
    k = (k * rc + _swap_halves(k) * rs) * (RET_DK ** -0.5)
    qw = qw_ref[0]
    kw = kw_ref[0]
    q3 = q.reshape(n_chunks, C, RET_DK)
    k3 = k.reshape(n_chunks, C, RET_DK)
    qb_scr[...] = q.astype(BF16)
    kb_scr[...] = k.astype(BF16)
    qwb_scr[...] = (q3 * qw[None]).reshape(S, RET_DK).astype(BF16)
    kwb_scr[...] = (k3 * kw[None]).reshape(S, RET_DK)
    dec = dec_ref[0]
    cd = cd_ref[0]

    def group(gi, R):
        for u in range(RET_GROUP):
            rows = pl.ds(pl.multiple_of((gi * RET_GROUP + u) * C, C), C)
            qn = qb_scr[rows, :]
            kn = kb_scr[rows, :]
            vn = v_ref[rows, :]
            s = lax.dot_general(qn, kn, (((1,), (1,)), ((), ())), preferred_element_type=F32) * dec
            o = jnp.dot(s.astype(BF16), vn, preferred_element_type=F32)
            o = o + jnp.dot(qwb_scr[rows, :], R.astype(BF16), preferred_element_type=F32)
            kwt = kwb_scr[rows, :].T.astype(BF16)
            R = cd * R + jnp.dot(kwt, vn, preferred_element_type=F32)
            mu = jnp.mean(o, axis=-1, keepdims=True)
            d = o - mu
            var = jnp.mean(d * d, axis=-1, keepdims=True)
            gate = g_ref[rows, :].astype(F32)
            o_ref[rows, :] = (d * lax.rsqrt(var + GN_EPS) * (gate * _sigmoid(gate))).astype(o_ref.dtype)
        return R

    lax.fori_loop(0, n_chunks // RET_GROUP, group, jnp.zeros((RET_DK, RET_DV), F32))


def _retention(proj, rc, rs, dec, qw, kw, cd, B, S):
    T = B * S
    H = RET_HEADS
    col = lambda off: pl.BlockSpec((S, LANES), lambda b, h: (b, off + h))
    tab = pl.BlockSpec((S, LANES), lambda b, h: (b, 0))
    per_head = lambda r: pl.BlockSpec((1, r, LANES), lambda b, h: (h, 0, 0))
    return pl.pallas_call(
        _ret_kernel,
        out_shape=jax.ShapeDtypeStruct((T, H * RET_DV), BF16),
        grid=(B, H),
        in_specs=[col(0), col(H), col(2 * H), col(3 * H), tab, tab,
                  per_head(RET_CHUNK), per_head(RET_CHUNK), per_head(RET_CHUNK), per_head(1)],
        out_specs=pl.BlockSpec((S, RET_DV), lambda b, h: (b, h)),
        scratch_shapes=[pltpu.VMEM((S, RET_DK), BF16), pltpu.VMEM((S, RET_DK), BF16),
                        pltpu.VMEM((S, RET_DK), BF16), pltpu.VMEM((S, RET_DK), F32)],
        compiler_params=_cparams("parallel", "parallel"),
        name="retention",
    )(proj, proj, proj, proj, rc, rs, dec, qw, kw, cd)


def _retention_constants():
    C = RET_CHUNK
    h = np.arange(RET_HEADS, dtype=np.float64)
    log_gamma = np.log(1.0 - 2.0 ** (-5.0 - h))
    idx = np.arange(C, dtype=np.float64)
    rel = idx[:, None] - idx[None, :]
    dec = np.where(rel[None] >= 0, np.exp(np.maximum(rel, 0.0)[None] * log_gamma[:, None, None]), 0.0)
    qw = np.exp((idx + 1.0)[None, :] * log_gamma[:, None])
    kw = np.exp((C - 1.0 - idx)[None, :] * log_gamma[:, None])
    cd = np.exp(C * log_gamma)
    bc = lambda a: np.ascontiguousarray(np.broadcast_to(a[..., None], a.shape + (LANES,))).astype(np.float32)
    return dec.astype(np.float32), bc(qw), bc(kw), bc(cd[:, None])


def _layernorm(y, g, b):
    mu = jnp.mean(y, axis=-1, keepdims=True)
    d = y - mu
    var = jnp.mean(d * d, axis=-1, keepdims=True)
    return d * lax.rsqrt(var + LN_EPS) * g + b


def _merge_kernel(x_ref, om_ref, or_ref, gm_ref, gr_ref, wo_ref, g_ref, b_ref, h_ref, hp_ref):
    mixed = (_sigmoid(gm_ref[...].astype(F32)) * om_ref[...].astype(F32)
             + _sigmoid(gr_ref[...].astype(F32)) * or_ref[...].astype(F32))
    mix = jnp.dot(mixed.astype(BF16), wo_ref[...], preferred_element_type=F32)
    h = _layernorm(DN_ALPHA * x_ref[...] + mix, g_ref[...], b_ref[...])
    h_ref[...] = h
    hp_ref[...] = _pack_rows(h)


def _merge(x2, o_mla, o_ret, proj, wo, g, b):
    T = x2.shape[0]
    row = lambda c: pl.BlockSpec((ROW_TILE, D_MODEL), lambda i: (i, c))
    return pl.pallas_call(
        _merge_kernel,
        out_shape=(jax.ShapeDtypeStruct((T, D_MODEL), F32), jax.ShapeDtypeStruct((T, PACKED_W), jnp.uint32)),
        grid=(T // ROW_TILE,),
        in_specs=[row(0), row(0), row(0), row(4), row(5),
                  _const_spec(wo.shape), _const_spec(g.shape), _const_spec(b.shape)],
        out_specs=(row(0), pl.BlockSpec((ROW_TILE, PACKED_W), lambda i: (i, 0))),
        compiler_params=_cparams("parallel"),
        name="merge_out_proj_ln",
    )(x2, o_mla, o_ret, proj, proj, wo, g, b)


def _split3(a):
    a0 = a.astype(BF16)
    r1 = a - a0.astype(F32)
    a1 = r1.astype(BF16)
    a2 = (r1 - a1.astype(F32)).astype(BF16)
    return a0, a1, a2


def _router_kernel(h_ref, w0_ref, w1_ref, w2_ref, b_ref, idx_ref, gate_ref, cnt_ref, run_scr):
    i = pl.program_id(0)

    @pl.when(i == 0)
    def _():
        run_scr[...] = jnp.zeros(run_scr.shape, F32)

    h0, h1, h2 = _split3(h_ref[...])
    w0, w1, w2 = w0_ref[...], w1_ref[...], w2_ref[...]
    dot = lambda a, w: jnp.dot(a, w, preferred_element_type=F32)
    logits = (dot(h0, w0) + (dot(h0, w1) + dot(h1, w0))
              + (dot(h0, w2) + dot(h1, w1) + dot(h2, w0))) + b_ref[...]
    rows = logits.shape[0]
    lane = lax.broadcasted_iota(jnp.int32, (rows, LANES), 1)
    lane_f = lane.astype(F32)
    work = jnp.where(lane < N_EXPERTS, logits, -jnp.inf)
    vals, ids, sels = [], [], []
    for _ in range(TOP_K):
        m = jnp.max(work, axis=-1, keepdims=True)
        first = jnp.min(jnp.where(work == m, lane_f, float(LANES)), axis=-1, keepdims=True)
        sel = lane_f == first
        work = jnp.where(sel, -jnp.inf, work)
        vals.append(m)
        ids.append(first)
        sels.append(sel)
    exps = [jnp.exp(v - vals[0]) for v in vals]
    denom = exps[0] + exps[1] + exps[2] + exps[3]
    chosen = jnp.zeros((rows, LANES), F32)
    for sel in sels:
        chosen = jnp.where(sel, 1.0, chosen)
    r = lax.broadcasted_iota(jnp.int32, (rows, rows), 0)
    c = lax.broadcasted_iota(jnp.int32, (rows, rows), 1)
    lower = jnp.where(r > c, 1.0, 0.0).astype(BF16)
    before = jnp.dot(lower, chosen.astype(BF16), preferred_element_type=F32) + run_scr[...]
    idx_out = jnp.zeros((rows, LANES), jnp.int32)
    gate_out = jnp.zeros((rows, LANES), F32)
    for kk in range(TOP_K):
        rank = jnp.sum(jnp.where(sels[kk], before, 0.0), axis=-1, keepdims=True).astype(jnp.int32)
        idx_out = jnp.where(lane == kk, ids[kk].astype(jnp.int32), idx_out)
        idx_out = jnp.where(lane == TOP_K + kk, rank, idx_out)
        gate_out = jnp.where(lane == kk, exps[kk] / denom, gate_out)
    idx_ref[...] = idx_out
    gate_ref[...] = gate_out
    total = run_scr[...] + jnp.sum(chosen, axis=0, keepdims=True)
    run_scr[...] = total
    cnt_ref[...] = total.astype(jnp.int32)


def _router(h, w_router, b_router):
    T = h.shape[0]
    wpad = jnp.zeros((D_MODEL, LANES), F32).at[:, :N_EXPERTS].set(w_router)
    w0 = wpad.astype(BF16)
    r1 = wpad - w0.astype(F32)
    w1 = r1.astype(BF16)
    w2 = (r1 - w1.astype(F32)).astype(BF16)
    bpad = jnp.zeros((1, LANES), F32).at[0, :N_EXPERTS].set(b_router)
    row = pl.BlockSpec((ROW_TILE, LANES), lambda i: (i, 0))
    return pl.pallas_call(
        _router_kernel,
        out_shape=(jax.ShapeDtypeStruct((T, LANES), jnp.int32),
                   jax.ShapeDtypeStruct((T, LANES), F32),
                   jax.ShapeDtypeStruct((1, LANES), jnp.int32)),
        grid=(T // ROW_TILE,),
        in_specs=[pl.BlockSpec((ROW_TILE, D_MODEL), lambda i: (i, 0)),
                  _const_spec(w0.shape), _const_spec(w0.shape), _const_spec(w0.shape),
                  _const_spec(bpad.shape)],
        out_specs=(row, row, pl.BlockSpec((1, LANES), lambda i: (0, 0))),
        scratch_shapes=[pltpu.VMEM((1, LANES), F32)],
        compiler_params=_cparams("arbitrary"),
        name="router",
    )(h, w0, w1, w2, bpad)


def _row_copy(src, src_row, dst, dst_row, sem):
    return pltpu.make_async_copy(src.at[pl.ds(src_row, 1), :], dst.at[pl.ds(dst_row, 1), :], sem)


def _dispatch_kernel(dest_ref, h_ref, xs_in_ref, xs_ref, sem):
    del xs_in_ref
    base = pl.program_id(0) * (MOVE_TILE * TOP_K)

    def issue(r, carry):
        for kk in range(TOP_K):
            _row_copy(h_ref, r, xs_ref, dest_ref[base + r * TOP_K + kk], sem).start(priority=kk % 2)
        return carry

    lax.fori_loop(0, MOVE_TILE, issue, 0)
    for _ in range(TOP_K):
        pltpu.make_async_copy(h_ref, xs_ref.at[pl.ds(0, MOVE_TILE), :], sem).wait()


def _dispatch(dest_flat, hp, n_rows):
    T = hp.shape[0]
    grid_spec = pltpu.PrefetchScalarGridSpec(
        num_scalar_prefetch=1,
        grid=(T // MOVE_TILE,),
        in_specs=[pl.BlockSpec((MOVE_TILE, PACKED_W), lambda i, d: (i, 0)),
                  pl.BlockSpec(memory_space=pl.ANY)],
        out_specs=pl.BlockSpec(memory_space=pl.ANY),
        scratch_shapes=[pltpu.SemaphoreType.DMA],
    )
    return pl.pallas_call(
        _dispatch_kernel,
        out_shape=jax.ShapeDtypeStruct((n_rows, PACKED_W), jnp.uint32),
        grid_spec=grid_spec,
        input_output_aliases={2: 0},
        compiler_params=_cparams("arbitrary"),
        name="dispatch_rows",
    )(dest_flat, hp, jnp.zeros((n_rows, PACKED_W), jnp.uint32))


def _expert_kernel(blk_e_ref, nact_ref, x_ref, wu_ref, bu_ref, wd_ref, bd_ref, y_ref,
                   wub_scr, wdb_scr, last_scr):
    i = pl.program_id(0)
    e = blk_e_ref[i]

    @pl.when(i == 0)
    def _():
        last_scr[0] = -1

    @pl.when(jnp.logical_and(i < nact_ref[0], e != last_scr[0]))
    def _():
        wub_scr[...] = wu_ref[0].astype(BF16)
        wdb_scr[...] = wd_ref[0].astype(BF16)
        last_scr[0] = e

    @pl.when(i < nact_ref[0])
    def _():
        lo, hi = _unpack_rows(x_ref[...])
        x = jnp.concatenate([lo.astype(BF16), hi.astype(BF16)], axis=-1)
        acc = jnp.zeros((EXPERT_TILE, D_MODEL), F32)
        step = 256
        for c in range(D_EXPERT // step):
            glu = jnp.dot(x, wub_scr[:, c * step:(c + 1) * step], preferred_element_type=F32)
            lin = jnp.dot(x, wub_scr[:, D_EXPERT + c * step:D_EXPERT + (c + 1) * step],
                          preferred_element_type=F32)
            glu = jnp.minimum(glu + bu_ref[0, :, c * step:(c + 1) * step], SWIGLU_LIMIT)
            lin = jnp.clip(lin + bu_ref[0, :, D_EXPERT + c * step:D_EXPERT + (c + 1) * step],
                           -SWIGLU_LIMIT, SWIGLU_LIMIT)
            a = glu * _sigmoid(SWIGLU_ALPHA * glu) * (lin + 1.0)
            acc = acc + jnp.dot(a.astype(BF16), wdb_scr[c * step:(c + 1) * step, :],
                                preferred_element_type=F32)
        y_ref[...] = _pack_rows(acc + bd_ref[0])

    @pl.when(i >= nact_ref[0])
    def _():
        y_ref[...] = jnp.zeros(y_ref.shape, y_ref.dtype)


def _experts(blk_e, nact, xs, w_up, b_up, w_down, b_down):
    P = xs.shape[0]
    n_blocks = P // EXPERT_TILE
    act = lambda i, na: jnp.minimum(i, na[0] - 1)
    grid_spec = pltpu.PrefetchScalarGridSpec(
        num_scalar_prefetch=2,
        grid=(n_blocks,),
        in_specs=[pl.BlockSpec((EXPERT_TILE, PACKED_W), lambda i, be, na: (act(i, na), 0)),
                  pl.BlockSpec((1, D_MODEL, 2 * D_EXPERT), lambda i, be, na: (be[act(i, na)], 0, 0)),
                  pl.BlockSpec((1, 1, 2 * D_EXPERT), lambda i, be, na: (be[act(i, na)], 0, 0)),
                  pl.BlockSpec((1, D_EXPERT, D_MODEL), lambda i, be, na: (be[act(i, na)], 0, 0)),
                  pl.BlockSpec((1, 1, D_MODEL), lambda i, be, na: (be[act(i, na)], 0, 0))],
        out_specs=pl.BlockSpec((EXPERT_TILE, PACKED_W), lambda i, be, na: (i, 0)),
        scratch_shapes=[pltpu.VMEM((D_MODEL, 2 * D_EXPERT), BF16),
                        pltpu.VMEM((D_EXPERT, D_MODEL), BF16),
                        pltpu.SMEM((1,), jnp.int32)],
    )
    return pl.pallas_call(
        _expert_kernel,
        out_shape=jax.ShapeDtypeStruct((P, PACKED_W), jnp.uint32),
        grid_spec=grid_spec,
        compiler_params=_cparams("arbitrary"),
        name="routed_experts",
    )(blk_e, nact, xs, w_up, b_up.reshape(N_EXPERTS, 1, 2 * D_EXPERT),
      w_down, b_down.reshape(N_EXPERTS, 1, D_MODEL))


def _combine_kernel(dest_ref, h_ref, gate_ref, g_ref, b_ref, ys_ref, o_ref, ybuf, sems):
    i = pl.program_id(0)
    n = pl.num_programs(0)

    def gather(step, slot):
        base = step * (MOVE_TILE * TOP_K)

        def issue(r, carry):
            for kk in range(TOP_K):
                _row_copy(ys_ref, dest_ref[base + r * TOP_K + kk], ybuf.at[slot, kk], r,
                          sems.at[slot]).start(priority=kk % 2)
            return carry

        lax.fori_loop(0, MOVE_TILE, issue, 0)

    @pl.when(i == 0)
    def _():
        gather(0, 0)

    @pl.when(i + 1 < n)
    def _():
        gather(i + 1, (i + 1) % 2)

    slot = i % 2
    for kk in range(TOP_K):
        pltpu.make_async_copy(ys_ref.at[pl.ds(0, MOVE_TILE), :], ybuf.at[slot, kk], sems.at[slot]).wait()

    gate = gate_ref[...]
    f_lo = jnp.zeros((MOVE_TILE, PACKED_W), F32)
    f_hi = jnp.zeros((MOVE_TILE, PACKED_W), F32)
    for kk in range(TOP_K):
        lo, hi = _unpack_rows(ybuf[slot, kk])
        f_lo = f_lo + gate[:, kk:kk + 1] * lo
        f_hi = f_hi + gate[:, kk:kk + 1] * hi
    y_lo = DN_ALPHA * h_ref[:, :PACKED_W] + f_lo
    y_hi = DN_ALPHA * h_ref[:, PACKED_W:] + f_hi
    mu = (jnp.sum(y_lo, axis=-1, keepdims=True) + jnp.sum(y_hi, axis=-1, keepdims=True)) / D_MODEL
    d_lo = y_lo - mu
    d_hi = y_hi - mu
    var = (jnp.sum(d_lo * d_lo, axis=-1, keepdims=True) + jnp.sum(d_hi * d_hi, axis=-1, keepdims=True)) / D_MODEL
    inv = lax.rsqrt(var + LN_EPS)
    o_ref[:, :PACKED_W] = d_lo * inv * g_ref[:, :PACKED_W] + b_ref[:, :PACKED_W]
    o_ref[:, PACKED_W:] = d_hi * inv * g_ref[:, PACKED_W:] + b_ref[:, PACKED_W:]


def _combine(dest_flat, h, ys, gate, g, b):
    T = h.shape[0]
    grid_spec = pltpu.PrefetchScalarGridSpec(
        num_scalar_prefetch=1,
        grid=(T // MOVE_TILE,),
        in_specs=[pl.BlockSpec((MOVE_TILE, D_MODEL), lambda i, d: (i, 0)),
                  pl.BlockSpec((MOVE_TILE, LANES), lambda i, d: (i, 0)),
                  pl.BlockSpec(g.shape, lambda i, d: (0, 0)),
                  pl.BlockSpec(b.shape, lambda i, d: (0, 0)),
                  pl.BlockSpec(memory_space=pl.ANY)],
        out_specs=pl.BlockSpec((MOVE_TILE, D_MODEL), lambda i, d: (i, 0)),
        scratch_shapes=[pltpu.VMEM((2, TOP_K, MOVE_TILE, PACKED_W), jnp.uint32),
                        pltpu.SemaphoreType.DMA((2,))],
    )
    return pl.pallas_call(
        _combine_kernel,
        out_shape=jax.ShapeDtypeStruct((T, D_MODEL), F32),
        grid_spec=grid_spec,
        compiler_params=_cparams("arbitrary"),
        name="combine_ln",
    )(dest_flat, h, gate, g, b, ys)


def _rot_cols(w):
    half = w.shape[-1] // 2
    return jnp.concatenate([-w[:, half:], w[:, :half]], axis=-1)


def _pack_w_in(w_in):
    lat = Q_LORA + KV_LORA
    w_kr = w_in[:, lat:lat + MLA_ROPE]
    return jnp.concatenate([w_in[:, lat + MLA_ROPE:], w_in[:, :lat + MLA_ROPE], _rot_cols(w_kr)],
                           axis=-1).astype(BF16)


def _pack_w_uq(w_uq):
    w = w_uq.reshape(Q_LORA, MLA_HEADS, MLA_NOPE + MLA_ROPE)
    rope = w[:, :, MLA_NOPE:]
    rot = jnp.concatenate([-rope[:, :, MLA_ROPE // 2:], rope[:, :, :MLA_ROPE // 2]], axis=-1)
    return jnp.concatenate([w, rot], axis=-1).reshape(Q_LORA, MLA_HEADS * MLA_QK).astype(BF16)


def _rope_freqs():
    ret = 1.0 / (10000.0 ** jnp.linspace(0.0, 1.0, RET_DK // 2, dtype=F32))
    mla = 1.0 / (ROPE_BASE ** (jnp.arange(0, MLA_ROPE, 2, dtype=F32) / MLA_ROPE))
    return jnp.concatenate([ret, mla, jnp.zeros((LANES - ret.shape[0] - mla.shape[0],), F32)])[None, :]


def _routing_plan(idx_out, cnt):
    top_i = idx_out[:, :TOP_K]
    rank = idx_out[:, TOP_K:2 * TOP_K]
    counts = cnt[0, :N_EXPERTS]
    padded = ((counts + EXPERT_TILE - 1) // EXPERT_TILE) * EXPERT_TILE
    pad_end = jnp.cumsum(padded)
    pad_start = pad_end - padded
    experts = jnp.arange(N_EXPERTS, dtype=jnp.int32)
    dest = rank + jnp.sum(jnp.where(top_i[..., None] == experts, pad_start, 0), axis=-1)
    n_rows = top_i.shape[0] * TOP_K + N_EXPERTS * EXPERT_TILE
    n_blocks = n_rows // EXPERT_TILE
    blk_row = jnp.arange(n_blocks, dtype=jnp.int32) * EXPERT_TILE
    blk_e = jnp.minimum(jnp.sum((pad_end[None, :] <= blk_row[:, None]).astype(jnp.int32), axis=-1),
                        N_EXPERTS - 1)
    nact = (pad_end[-1:] // EXPERT_TILE).astype(jnp.int32)
    return dest.astype(jnp.int32), blk_e, nact, n_rows


def kernel(x, positions, w_in, q_norm_g, w_uq, kv_norm_g, w_ukv, w_o, ln1_g, ln1_b,
           w_router, b_router, w_up, b_up, w_down, b_down, ln2_g, ln2_b):
    B, S, _ = x.shape
    T = B * S
    pos_col = positions.reshape(T, 1).astype(jnp.int32)
    rc, rs, mc, ms = _rope_tables(pos_col, _rope_freqs())
    dec, qw, kw, cd = _retention_constants()
    h = x.reshape(T, D_MODEL)
    for l in range(DEPTH):
        proj = _input_projection(h, _pack_w_in(w_in[l]))
        w_kv = w_ukv[l].reshape(KV_LORA, MLA_HEADS, MLA_NOPE + MLA_V)
        w_k = w_kv[:, :, :MLA_NOPE].reshape(KV_LORA, MLA_HEADS * MLA_NOPE).astype(BF16)
        w_vt = w_kv[:, :, MLA_NOPE:].reshape(KV_LORA, MLA_HEADS * MLA_V).T.astype(BF16)
        q_cat, kn, vt, kr = _mla_prep(proj, mc, ms, _pack_w_uq(w_uq[l]), w_k, w_vt,
                                      q_norm_g[l][None, :], kv_norm_g[l][None, :])
        o_mla = _attention(q_cat, kn, kr, vt, B, S)
        o_ret = _retention(proj, rc, rs, dec, qw, kw, cd, B, S)
        h1, h1p = _merge(h, o_mla, o_ret, proj, w_o[l].astype(BF16), ln1_g[l][None, :], ln1_b[l][None, :])
        idx_out, gate, cnt = _router(h1, w_router[l], b_router[l])
        dest, blk_e, nact, n_rows = _routing_plan(idx_out, cnt)
        dest_flat = dest.reshape(-1)
        xs = _dispatch(dest_flat, h1p, n_rows)
        ys = _experts(blk_e, nact, xs, w_up[l], b_up[l], w_down[l], b_down[l])
        h = _combine(dest_flat, h1, ys, gate, ln2_g[l][None, :], ln2_b[l][None, :])
    return h.reshape(B, S, D_MODEL)
```

```python
import functools

import numpy as np
import jax
import jax.numpy as jnp
from jax import lax
from jax.experimental import pallas as pl
from jax.experimental.pallas import tpu as pltpu

F32 = jnp.float32
BF16 = jnp.bfloat16

D_MODEL = 1024
DEPTH = 1
MLA_HEADS = 8
MLA_NOPE = 128
MLA_ROPE = 64
MLA_V = 128
Q_LORA = 384
KV_LORA = 256
ROPE_BASE = 10000.0
RET_HEADS = 8
RET_DK = 128
RET_DV = 128
RET_CHUNK = 128
N_EXPERTS = 32
TOP_K = 4
D_EXPERT = 1024
SWIGLU_LIMIT = 7.0
SWIGLU_ALPHA = 1.702
DN_ALPHA = (2.0 * DEPTH) ** 0.25
LN_EPS = 1e-5
RMS_EPS = 1e-6
GN_EPS = 1e-6
NEG_INF = -1e30

LANES = 128
MLA_QK = 2 * LANES
PACKED_W = D_MODEL // 2
LATENT_W = 768
N_WIDE = 6 * D_MODEL
N_PROJ = N_WIDE + LATENT_W
VMEM_LIMIT = 56 * 1024 * 1024

ROW_TILE = 512
ATTN_TILE = ROW_TILE
ATTN_Q_SPLIT = 2
RET_GROUP = 4
EXPERT_TILE = 256
MOVE_TILE = 256


def _cparams(*sem):
    return pltpu.CompilerParams(dimension_semantics=sem, vmem_limit_bytes=VMEM_LIMIT)


def _const_spec(shape):
    nd = len(shape)
    return pl.BlockSpec(shape, lambda *_: (0,) * nd, pipeline_mode=pl.Buffered(1))


def _sigmoid(x):
    return 1.0 / (1.0 + jnp.exp(-x))


def _pack_rows(x):
    n = x.shape[-1] // 2
    xb = x.astype(BF16).astype(F32)
    lo = pltpu.bitcast(xb[:, :n], jnp.uint32) >> 16
    hi = pltpu.bitcast(xb[:, n:], jnp.uint32) & jnp.uint32(0xFFFF0000)
    return lo | hi


def _unpack_rows(w):
    lo = pltpu.bitcast(w << 16, F32)
    hi = pltpu.bitcast(w & jnp.uint32(0xFFFF0000), F32)
    return lo, hi


def _swap_halves(x):
    return pltpu.roll(x, LANES // 2, 1)


def _proj_kernel(x_ref, w_ref, o_ref):
    xb = x_ref[...].astype(BF16)
    for c in range(N_PROJ // LATENT_W):
        sl = slice(c * LATENT_W, (c + 1) * LATENT_W)
        o_ref[:, sl] = jnp.dot(xb, w_ref[:, sl], preferred_element_type=F32).astype(o_ref.dtype)


def _input_projection(x2, w_cat):
    T = x2.shape[0]
    return pl.pallas_call(
        _proj_kernel,
        out_shape=jax.ShapeDtypeStruct((T, N_PROJ), BF16),
        grid=(T // ROW_TILE,),
        in_specs=[pl.BlockSpec((ROW_TILE, D_MODEL), lambda i: (i, 0)),
                  _const_spec((D_MODEL, N_PROJ))],
        out_specs=pl.BlockSpec((ROW_TILE, N_PROJ), lambda i: (i, 0)),
        compiler_params=_cparams("parallel"),
        name="input_projection",
    )(x2, w_cat)


def _rope_table_kernel(pos_ref, freq_ref, rc_ref, rs_ref, mc_ref, ms_ref):
    ang = pos_ref[...].astype(F32) * freq_ref[...]
    c = jnp.cos(ang)
    s = jnp.sin(ang)
    lane = lax.broadcasted_iota(jnp.int32, ang.shape, 1)
    c64 = pltpu.roll(c, 64, 1)
    s64 = pltpu.roll(s, 64, 1)
    rc_ref[...] = jnp.where(lane < 64, c, c64)
    rs_ref[...] = jnp.where(lane < 64, -s, s64)
    c96 = pltpu.roll(c, 96, 1)
    s96 = pltpu.roll(s, 96, 1)
    mc_ref[...] = jnp.where(lane < 32, c64, jnp.where(lane < 64, c96, 0.0))
    ms_ref[...] = jnp.where(lane < 32, s64, jnp.where(lane < 64, s96, 0.0))


def _rope_tables(pos_col, freq_row):
    T = pos_col.shape[0]
    rows = min(2048, T)
    tab =jax.ShapeDtypeStruct((T, LANES), F32)
    spec = pl.BlockSpec((rows, LANES), lambda i: (i, 0))
    return pl.pallas_call(
        _rope_table_kernel,
        out_shape=(tab, tab, tab, tab),
        grid=(T // rows,),
        in_specs=[pl.BlockSpec((rows, 1), lambda i: (i, 0)), _const_spec((1, LANES))],
        out_specs=(spec, spec, spec, spec),
        compiler_params=_cparams("parallel"),
        name="rope_tables",
    )(pos_col, freq_row)


def _mla_prep_kernel(a_ref, mc_ref, ms_ref, wq_ref, wk_ref, wvt_ref, qg_ref, kvg_ref,
                     q_ref, kn_ref, vt_ref, kr_ref):
    a = a_ref[...].astype(F32)
    mc = mc_ref[...]
    ms = ms_ref[...]
    cq = a[:, :Q_LORA]
    qn = cq * lax.rsqrt(jnp.mean(cq * cq, axis=-1, keepdims=True) + RMS_EPS) * qg_ref[...]
    q = jnp.dot(qn.astype(BF16), wq_ref[...], preferred_element_type=F32)
    scale = (MLA_NOPE + MLA_ROPE) ** -0.5
    for h in range(MLA_HEADS):
        nope = q[:, h * MLA_QK:h * MLA_QK + LANES]
        blk = q[:, h * MLA_QK + LANES:(h + 1) * MLA_QK]
        rope = blk * mc + _swap_halves(blk) * ms
        q_ref[:, h * MLA_QK:h * MLA_QK + LANES] = (nope * scale).astype(BF16)
        q_ref[:, h * MLA_QK + LANES:(h + 1) * MLA_QK] = (rope * scale).astype(BF16)
    ckv = a[:, Q_LORA:Q_LORA + KV_LORA]
    kvn = ckv * lax.rsqrt(jnp.mean(ckv * ckv, axis=-1, keepdims=True) + RMS_EPS) * kvg_ref[...]
    kvb = kvn.astype(BF16)
    kn_ref[...] = jnp.dot(kvb, wk_ref[...], preferred_element_type=F32).astype(BF16)
    vt_ref[0] = lax.dot_general(wvt_ref[...], kvb, (((1,), (1,)), ((), ())),
                                preferred_element_type=F32).astype(BF16)
    krb =a[:, Q_LORA + KV_LORA:]
    kr_ref[...] = (krb * mc + _swap_halves(krb) * ms).astype(BF16)


def _mla_prep(proj, mc, ms, wq, wk, wvt, qg, kvg):
    T = proj.shape[0]
    row = lambda w: pl.BlockSpec((ROW_TILE, w), lambda i: (i, 0))
    return pl.pallas_call(
        _mla_prep_kernel,
        out_shape=(jax.ShapeDtypeStruct((T, MLA_HEADS * MLA_QK), BF16),
                   jax.ShapeDtypeStruct((T, MLA_HEADS * MLA_NOPE), BF16),
                   jax.ShapeDtypeStruct((T // ROW_TILE, MLA_HEADS * MLA_V, ROW_TILE), BF16),
                   jax.ShapeDtypeStruct((T, LANES), BF16)),
        grid=(T // ROW_TILE,),
        in_specs=[pl.BlockSpec((ROW_TILE, LATENT_W), lambda i: (i, N_WIDE // LATENT_W)),
                  row(LANES), row(LANES),
                  _const_spec(wq.shape), _const_spec(wk.shape), _const_spec(wvt.shape),
                  _const_spec(qg.shape), _const_spec(kvg.shape)],
        out_specs=(row(MLA_HEADS * MLA_QK), row(MLA_HEADS * MLA_NOPE),
                   pl.BlockSpec((1, MLA_HEADS * MLA_V, ROW_TILE), lambda i: (i, 0, 0)), row(LANES)),
        compiler_params=_cparams("parallel"),
        name="mla_prep",
    )(proj, mc, ms, wq, wk, wvt, qg, kvg)


def _attn_kernel(q_ref, kn_ref, kr_ref, vt_ref, o_ref, acc_scr):
    qi = pl.program_id(2)
    W = ATTN_TILE // ATTN_Q_SPLIT
    qs = [q_ref[c * W:(c + 1) * W, :] for c in range(ATTN_Q_SPLIT)]
    acc_scr[...] = jnp.zeros(acc_scr.shape, F32)

    def scores(j):
        rows = pl.ds(pl.multiple_of(j * ATTN_TILE, ATTN_TILE), ATTN_TILE)
        k = jnp.concatenate([kn_ref[rows, :], kr_ref[rows, :]], axis=-1)
        return tuple(lax.dot_general(k, qs[c], (((1,), (1,)), ((), ())), preferred_element_type=F32)
                     for c in range(ATTN_Q_SPLIT))

    def softmax_pv(j, s_all, stats, masked):
        vt = vt_ref[j]
        out = []
        for c in range(ATTN_Q_SPLIT):
            m_prev, l_prev = stats[c]
            s = s_all[c]
            if masked:
                kp = lax.broadcasted_iota(jnp.int32, s.shape, 0)
                qp = lax.broadcasted_iota(jnp.int32, s.shape, 1) + c * W
                s = jnp.where(kp <= qp, s, NEG_INF)
            m_new = jnp.maximum(m_prev, jnp.max(s, axis=0, keepdims=True))
            alpha = jnp.exp(m_prev - m_new)
            p = jnp.exp(s - m_new)
            l_new = alpha * l_prev + jnp.sum(p, axis=0, keepdims=True)
            acc_scr[c] = alpha * acc_scr[c] + jnp.dot(vt, p.astype(BF16), preferred_element_type=F32)
            out.append((m_new, l_new))
        return tuple(out)

    def body(j, carry):
        s_cur, stats = carry
        s_next = scores(j + 1)
        return s_next, softmax_pv(j, s_cur, stats, False)

    init = tuple((jnp.full((1, W), NEG_INF, F32), jnp.zeros((1, W), F32)) for _ in range(ATTN_Q_SPLIT))
    s_last, stats = lax.fori_loop(0, qi, body, (scores(0), init))
    stats = softmax_pv(qi, s_last, stats, True)
    for c in range(ATTN_Q_SPLIT):
        o_ref[c * W:(c + 1) * W, :] = (acc_scr[c] / stats[c][1]).T.astype(o_ref.dtype)


def _attention(q_cat, kn, kr, vt, B, S):
    T = B * S
    nq = S // ATTN_TILE
    return pl.pallas_call(
        _attn_kernel,
        out_shape=jax.ShapeDtypeStruct((T, MLA_HEADS * MLA_V), BF16),
        grid=(B, MLA_HEADS, nq),
        in_specs=[pl.BlockSpec((ATTN_TILE, MLA_QK), lambda b, h, i: (b * nq + i, h)),
                  pl.BlockSpec((S, MLA_NOPE), lambda b, h, i: (b, h)),
                  pl.BlockSpec((S, LANES), lambda b, h, i: (b, 0)),
                  pl.BlockSpec((nq, MLA_V, ATTN_TILE), lambda b, h, i: (b, h, 0))],
        out_specs=pl.BlockSpec((ATTN_TILE, MLA_V), lambda b, h, i: (b * nq + i, h)),
        scratch_shapes=[pltpu.VMEM((ATTN_Q_SPLIT, MLA_V, ATTN_TILE // ATTN_Q_SPLIT), F32)],
        compiler_params=_cparams("parallel", "parallel", "arbitrary"),
        name="mla_attention",
    )(q_cat, kn, kr, vt)


def _ret_kernel(q_ref, k_ref, v_ref, g_ref, rc_ref, rs_ref, dec_ref, qw_ref, kw_ref, cd_ref,
                o_ref, qb_scr, qwb_scr, kb_scr, kwb_scr):
    S = q_ref.shape[0]
    C = RET_CHUNK
    n_chunks = S // C
    rc = rc_ref[...]
    rs = rs_ref[...]
    q = q_ref[...].astype(F32)
    k = k_ref[...].astype(F32)
    q = q * rc + _swap_halves(q) * rs
    k = (k * rc + _swap_halves(k) * rs) * (RET_DK ** -0.5)
    qw = qw_ref[0]
    kw = kw_ref[0]
    q3 = q.reshape(n_chunks, C, RET_DK)
    k3 = k.reshape(n_chunks, C, RET_DK)
    qb_scr[...] = q.astype(BF16)
    kb_scr[...] = k.astype(BF16)
    qwb_scr[...] = (q3 * qw[None]).reshape(S, RET_DK).astype(BF16)
    kwb_scr[...] = (k3 * kw[None]).reshape(S, RET_DK)
    dec = dec_ref[0]
    cd = cd_ref[0]

    def group(gi, R):
        for u in range(RET_GROUP):
            rows = pl.ds(pl.multiple_of((gi * RET_GROUP + u) * C, C), C)
            qn = qb_scr[rows, :]
            kn = kb_scr[rows, :]
            vn = v_ref[rows, :]
            s = lax.dot_general(qn, kn, (((1,), (1,)), ((), ())), preferred_element_type=F32) * dec
            o = jnp.dot(s.astype(BF16), vn, preferred_element_type=F32)
            o = o + jnp.dot(qwb_scr[rows, :], R.astype(BF16), preferred_element_type=F32)
            kwt = kwb_scr[rows, :].T.astype(BF16)
            R = cd * R + jnp.dot(kwt, vn, preferred_element_type=F32)
            mu = jnp.mean(o, axis=-1, keepdims=True)
            d = o - mu
            var = jnp.mean(d * d, axis=-1, keepdims=True)
            gate = g_ref[rows, :].astype(F32)
            o_ref[rows, :] = (d * lax.rsqrt(var + GN_EPS) * (gate * _sigmoid(gate))).astype(o_ref.dtype)
        return R

    lax.fori_loop(0, n_chunks // RET_GROUP, group, jnp.zeros((RET_DK, RET_DV), F32))


def _retention(proj, rc, rs, dec, qw, kw, cd, B, S):
    T = B * S
    H = RET_HEADS
    col = lambda off: pl.BlockSpec((S, LANES), lambda b, h: (b, off + h))
    tab = pl.BlockSpec((S, LANES), lambda b, h: (b, 0))
    per_head = lambda r: pl.BlockSpec((1, r, LANES), lambda b, h: (h, 0, 0))
    return pl.pallas_call(
        _ret_kernel,
        out_shape=jax.ShapeDtypeStruct((T, H * RET_DV), BF16),
        grid=(B, H),
        in_specs=[col(0), col(H), col(2 * H), col(3 * H), tab, tab,
                  per_head(RET_CHUNK), per_head(RET_CHUNK), per_head(RET_CHUNK), per_head(1)],
        out_specs=pl.BlockSpec((S, RET_DV), lambda b, h: (b, h)),
        scratch_shapes=[pltpu.VMEM((S, RET_DK), BF16), pltpu.VMEM((S, RET_DK), BF16),
                        pltpu.VMEM((S, RET_DK), BF16), pltpu.VMEM((S, RET_DK), F32)],
        compiler_params=_cparams("parallel", "parallel"),
        name="retention",
    )(proj, proj, proj, proj, rc, rs, dec, qw, kw, cd)


def _retention_constants():
    C = RET_CHUNK
    h = np.arange(RET_HEADS, dtype=np.float64)
    log_gamma = np.log(1.0 - 2.0 ** (-5.0 - h))
    idx = np.arange(C, dtype=np.float64)
    rel = idx[:, None] - idx[None, :]
    dec = np.where(rel[None] >= 0, np.exp(np.maximum(rel, 0.0)[None] * log_gamma[:, None, None]), 0.0)
    qw = np.exp((idx + 1.0)[None, :] * log_gamma[:, None])
    kw = np.exp((C - 1.0 - idx)[None, :] * log_gamma[:, None])
    cd = np.exp(C * log_gamma)
    bc = lambda a: np.ascontiguousarray(np.broadcast_to(a[..., None], a.shape + (LANES,))).astype(np.float32)
    return dec.astype(np.float32), bc(qw), bc(kw), bc(cd[:, None])


def _layernorm(y, g, b):
    mu = jnp.mean(y, axis=-1, keepdims=True)
    d = y - mu
    var = jnp.mean(d * d, axis=-1, keepdims=True)
    return d * lax.rsqrt(var + LN_EPS) * g + b


def _merge_kernel(x_ref, om_ref, or_ref, gm_ref, gr_ref, wo_ref, g_ref, b_ref, h_ref, hp_ref):
    mixed = (_sigmoid(gm_ref[...].astype(F32)) * om_ref[...].astype(F32)
             + _sigmoid(gr_ref[...].astype(F32)) * or_ref[...].astype(F32))
    mix = jnp.dot(mixed.astype(BF16), wo_ref[...], preferred_element_type=F32)
    h = _layernorm(DN_ALPHA * x_ref[...] + mix, g_ref[...], b_ref[...])
    h_ref[...] = h
    hp_ref[...] = _pack_rows(h)


def _merge(x2, o_mla, o_ret, proj, wo, g, b):
    T = x2.shape[0]
    row = lambda c: pl.BlockSpec((ROW_TILE, D_MODEL), lambda i: (i, c))
    return pl.pallas_call(
        _merge_kernel,
        out_shape=(jax.ShapeDtypeStruct((T, D_MODEL), F32), jax.ShapeDtypeStruct((T, PACKED_W), jnp.uint32)),
        grid=(T // ROW_TILE,),
        in_specs=[row(0), row(0), row(0), row(4), row(5),
                  _const_spec(wo.shape), _const_spec(g.shape), _const_spec(b.shape)],
        out_specs=(row(0), pl.BlockSpec((ROW_TILE, PACKED_W), lambda i: (i, 0))),
        compiler_params=_cparams("parallel"),
        name="merge_out_proj_ln",
    )(x2, o_mla, o_ret, proj, proj, wo, g, b)


def _split3(a):
    a0 = a.astype(BF16)
    r1 = a - a0.astype(F32)
    a1 = r1.astype(BF16)
    a2 = (r1 - a1.astype(F32)).astype(BF16)
    return a0, a1, a2


def _router_kernel(h_ref, w0_ref, w1_ref, w2_ref, b_ref, idx_ref, gate_ref, cnt_ref, run_scr):
    i = pl.program_id(0)

    @pl.when(i == 0)
    def _():
        run_scr[...] = jnp.zeros(run_scr.shape, F32)

    h0, h1, h2 = _split3(h_ref[...])
    w0, w1, w2 = w0_ref[...], w1_ref[...], w2_ref[...]
    dot = lambda a, w: jnp.dot(a, w, preferred_element_type=F32)
    logits = (dot(h0, w0) + (dot(h0, w1) + dot(h1, w0))
              + (dot(h0, w2) + dot(h1, w1) + dot(h2, w0))) + b_ref[...]
    rows = logits.shape[0]
    lane = lax.broadcasted_iota(jnp.int32, (rows, LANES), 1)
    lane_f = lane.astype(F32)
    work = jnp.where(lane < N_EXPERTS, logits, -jnp.inf)
    vals, ids, sels = [], [], []
    for _ in range(TOP_K):
        m = jnp.max(work, axis=-1, keepdims=True)
        first = jnp.min(jnp.where(work == m, lane_f, float(LANES)), axis=-1, keepdims=True)
        sel = lane_f == first
        work = jnp.where(sel, -jnp.inf, work)
        vals.append(m)
        ids.append(first)
        sels.append(sel)
    exps = [jnp.exp(v - vals[0]) for v in vals]
    denom = exps[0] + exps[1] + exps[2] + exps[3]
    chosen = jnp.zeros((rows, LANES), F32)
    for sel in sels:
        chosen = jnp.where(sel, 1.0, chosen)
    r = lax.broadcasted_iota(jnp.int32, (rows, rows), 0)
    c = lax.broadcasted_iota(jnp.int32, (rows, rows), 1)
    lower = jnp.where(r > c, 1.0, 0.0).astype(BF16)
    before = jnp.dot(lower, chosen.astype(BF16), preferred_element_type=F32) + run_scr[...]
    idx_out = jnp.zeros((rows, LANES), jnp.int32)
    gate_out = jnp.zeros((rows, LANES), F32)
    for kk in range(TOP_K):
        rank = jnp.sum(jnp.where(sels[kk], before, 0.0), axis=-1, keepdims=True).astype(jnp.int32)
        idx_out = jnp.where(lane == kk, ids[kk].astype(jnp.int32), idx_out)
        idx_out = jnp.where(lane == TOP_K + kk, rank, idx_out)
        gate_out = jnp.where(lane == kk, exps[kk] / denom, gate_out)
    idx_ref[...] = idx_out
    gate_ref[...] = gate_out
    total = run_scr[...] + jnp.sum(chosen, axis=0, keepdims=True)
    run_scr[...] = total
    cnt_ref[...] = total.astype(jnp.int32)


def _router(h, w_router, b_router):
    T = h.shape[0]
    wpad = jnp.zeros((D_MODEL, LANES), F32).at[:, :N_EXPERTS].set(w_router)
    w0 = wpad.astype(BF16)
    r1 = wpad - w0.astype(F32)
    w1 = r1.astype(BF16)
    w2 = (r1 - w1.astype(F32)).astype(BF16)
    bpad = jnp.zeros((1, LANES), F32).at[0, :N_EXPERTS].set(b_router)
    row = pl.BlockSpec((ROW_TILE, LANES), lambda i: (i, 0))
    return pl.pallas_call(
        _router_kernel,
        out_shape=(jax.ShapeDtypeStruct((T, LANES), jnp.int32),
                   jax.ShapeDtypeStruct((T, LANES), F32),
                   jax.ShapeDtypeStruct((1, LANES), jnp.int32)),
        grid=(T // ROW_TILE,),
        in_specs=[pl.BlockSpec((ROW_TILE, D_MODEL), lambda i: (i, 0)),
                  _const_spec(w0.shape), _const_spec(w0.shape), _const_spec(w0.shape),
                  _const_spec(bpad.shape)],
        out_specs=(row, row, pl.BlockSpec((1, LANES), lambda i: (0, 0))),
        scratch_shapes=[pltpu.VMEM((1, LANES), F32)],
        compiler_params=_cparams("arbitrary"),
        name="router",
    )(h, w0, w1, w2, bpad)


def _row_copy(src, src_row, dst, dst_row, sem):
    return pltpu.make_async_copy(src.at[pl.ds(src_row, 1), :], dst.at[pl.ds(dst_row, 1), :], sem)


def _dispatch_kernel(dest_ref, h_ref, xs_in_ref, xs_ref, sem):
    del xs_in_ref
    base = pl.program_id(0) * (MOVE_TILE * TOP_K)

    def issue(r, carry):
        for kk in range(TOP_K):
            _row_copy(h_ref, r, xs_ref, dest_ref[base + r * TOP_K + kk], sem).start(priority=kk % 2)
        return carry

    lax.fori_loop(0, MOVE_TILE, issue, 0)
    for _ in range(TOP_K):
        pltpu.make_async_copy(h_ref, xs_ref.at[pl.ds(0, MOVE_TILE), :], sem).wait()


def _dispatch(dest_flat, hp, n_rows):
    T = hp.shape[0]
    grid_spec = pltpu.PrefetchScalarGridSpec(
        num_scalar_prefetch=1,
        grid=(T // MOVE_TILE,),
        in_specs=[pl.BlockSpec((MOVE_TILE, PACKED_W), lambda i, d: (i, 0)),
                  pl.BlockSpec(memory_space=pl.ANY)],
        out_specs=pl.BlockSpec(memory_space=pl.ANY),
        scratch_shapes=[pltpu.SemaphoreType.DMA],
    )
    return pl.pallas_call(
        _dispatch_kernel,
        out_shape=jax.ShapeDtypeStruct((n_rows, PACKED_W), jnp.uint32),
        grid_spec=grid_spec,
        input_output_aliases={2: 0},
        compiler_params=_cparams("arbitrary"),
        name="dispatch_rows",
    )(dest_flat, hp, jnp.zeros((n_rows, PACKED_W), jnp.uint32))


def _expert_kernel(blk_e_ref, nact_ref, next_e_ref, x_ref, bu_ref, bd_ref, wu_hbm, wd_hbm, y_ref,
                   wuf_scr, wdf_scr, wub_scr, wdb_scr, sems):
    i = pl.program_id(0)
    e = blk_e_ref[i]

    def weight_copies(expert):
        return (pltpu.make_async_copy(wu_hbm.at[expert], wuf_scr, sems.at[0]),
                pltpu.make_async_copy(wd_hbm.at[expert], wdf_scr, sems.at[1]))

    @pl.when(i == 0)
    def _():
        for cp in weight_copies(e):
            cp.start()

    first_of_group = jnp.logical_or(i == 0, e != blk_e_ref[jnp.maximum(i - 1, 0)])

    @pl.when(jnp.logical_and(i < nact_ref[0], first_of_group))
    def _():
        for cp in weight_copies(e):
            cp.wait()
        wub_scr[...] = wuf_scr[...].astype(BF16)
        wdb_scr[...] = wdf_scr[...].astype(BF16)

        @pl.when(next_e_ref[i] >= 0)
        def _():
            for cp in weight_copies(next_e_ref[i]):
                cp.start()

    @pl.when(i < nact_ref[0])
    def _():
        lo, hi = _unpack_rows(x_ref[...])
        x = jnp.concatenate([lo.astype(BF16), hi.astype(BF16)], axis=-1)
        acc = jnp.zeros((EXPERT_TILE, D_MODEL), F32)
        step = 256
        for c in range(D_EXPERT // step):
            glu = jnp.dot(x, wub_scr[:, c * step:(c + 1) * step], preferred_element_type=F32)
            lin = jnp.dot(x, wub_scr[:, D_EXPERT + c * step:D_EXPERT + (c + 1) * step],
                          preferred_element_type=F32)
            glu = jnp.minimum(glu + bu_ref[0, :, c * step:(c + 1) * step], SWIGLU_LIMIT)
            lin = jnp.clip(lin + bu_ref[0, :, D_EXPERT + c * step:D_EXPERT + (c + 1) * step],
                           -SWIGLU_LIMIT, SWIGLU_LIMIT)
            a = glu * _sigmoid(SWIGLU_ALPHA * glu) * (lin + 1.0)
            acc = acc + jnp.dot(a.astype(BF16), wdb_scr[c * step:(c + 1) * step, :],
                                preferred_element_type=F32)
        y_ref[...] = _pack_rows(acc + bd_ref[0])

    @pl.when(i >= nact_ref[0])
    def _():
        y_ref[...] = jnp.zeros(y_ref.shape, y_ref.dtype)


def _experts(blk_e, nact, next_e, xs, w_up, b_up, w_down, b_down):
    P = xs.shape[0]
    n_blocks = P // EXPERT_TILE
    act = lambda i, na: jnp.minimum(i, na[0] - 1)
    grid_spec = pltpu.PrefetchScalarGridSpec(
        num_scalar_prefetch=3,
        grid=(n_blocks,),
        in_specs=[pl.BlockSpec((EXPERT_TILE, PACKED_W), lambda i, be, na, ne: (act(i, na), 0)),
                  pl.BlockSpec((1, 1, 2 * D_EXPERT), lambda i, be, na, ne: (be[act(i, na)], 0, 0)),
                  pl.BlockSpec((1, 1, D_MODEL), lambda i, be, na, ne: (be[act(i, na)], 0, 0)),
                  pl.BlockSpec(memory_space=pl.ANY),
                  pl.BlockSpec(memory_space=pl.ANY)],
        out_specs=pl.BlockSpec((EXPERT_TILE, PACKED_W), lambda i, be, na, ne: (i, 0)),
        scratch_shapes=[pltpu.VMEM((D_MODEL, 2 * D_EXPERT), F32),
                        pltpu.VMEM((D_EXPERT, D_MODEL), F32),
                        pltpu.VMEM((D_MODEL, 2 * D_EXPERT), BF16),
                        pltpu.VMEM((D_EXPERT, D_MODEL), BF16),
                        pltpu.SemaphoreType.DMA((2,))],
    )
    return pl.pallas_call(
        _expert_kernel,
        out_shape=jax.ShapeDtypeStruct((P, PACKED_W), jnp.uint32),
        grid_spec=grid_spec,
        compiler_params=_cparams("arbitrary"),
        name="routed_experts",
    )(blk_e, nact, next_e, xs, b_up.reshape(N_EXPERTS, 1, 2 * D_EXPERT),
      b_down.reshape(N_EXPERTS, 1, D_MODEL), w_up, w_down)


def _combine_kernel(dest_ref, h_ref, gate_ref, g_ref, b_ref, ys_ref, o_ref, ybuf, sems):
    i = pl.program_id(0)
    n = pl.num_programs(0)

    def gather(step, slot):
        base = step * (MOVE_TILE * TOP_K)

        def issue(r, carry):
            for kk in range(TOP_K):
                _row_copy(ys_ref, dest_ref[base + r * TOP_K + kk], ybuf.at[slot, kk], r,
                          sems.at[slot]).start(priority=kk % 2)
            return carry

        lax.fori_loop(0, MOVE_TILE, issue, 0)

    @pl.when(i == 0)
    def _():
        gather(0, 0)

    @pl.when(i + 1 < n)
    def _():
        gather(i + 1, (i + 1) % 2)

    slot = i % 2
    for kk in range(TOP_K):
        pltpu.make_async_copy(ys_ref.at[pl.ds(0, MOVE_TILE), :], ybuf.at[slot, kk], sems.at[slot]).wait()

    gate = gate_ref[...]
    f_lo = jnp.zeros((MOVE_TILE, PACKED_W), F32)
    f_hi = jnp.zeros((MOVE_TILE, PACKED_W), F32)
    for kk in range(TOP_K):
        lo, hi = _unpack_rows(ybuf[slot, kk])
        f_lo = f_lo + gate[:, kk:kk + 1] * lo
        f_hi = f_hi + gate[:, kk:kk + 1] * hi
    y_lo = DN_ALPHA * h_ref[:, :PACKED_W] + f_lo
    y_hi = DN_ALPHA * h_ref[:, PACKED_W:] + f_hi
    mu = (jnp.sum(y_lo, axis=-1, keepdims=True) + jnp.sum(y_hi, axis=-1, keepdims=True)) / D_MODEL
    d_lo = y_lo - mu
    d_hi = y_hi - mu
    var = (jnp.sum(d_lo * d_lo, axis=-1, keepdims=True) + jnp.sum(d_hi * d_hi, axis=-1, keepdims=True)) / D_MODEL
    inv = lax.rsqrt(var + LN_EPS)
    o_ref[:, :PACKED_W] = d_lo * inv * g_ref[:, :PACKED_W] + b_ref[:, :PACKED_W]
    o_ref[:, PACKED_W:] = d_hi * inv * g_ref[:, PACKED_W:] + b_ref[:, PACKED_W:]


def _combine(dest_flat, h, ys, gate, g, b):
    T = h.shape[0]
    grid_spec = pltpu.PrefetchScalarGridSpec(
        num_scalar_prefetch=1,
        grid=(T // MOVE_TILE,),
        in_specs=[pl.BlockSpec((MOVE_TILE, D_MODEL), lambda i, d: (i, 0)),
                  pl.BlockSpec((MOVE_TILE, LANES), lambda i, d: (i, 0)),
                  pl.BlockSpec(g.shape, lambda i, d: (0, 0)),
                  pl.BlockSpec(b.shape, lambda i, d: (0, 0)),
                  pl.BlockSpec(memory_space=pl.ANY)],
        out_specs=pl.BlockSpec((MOVE_TILE, D_MODEL), lambda i, d: (i, 0)),
        scratch_shapes=[pltpu.VMEM((2, TOP_K, MOVE_TILE, PACKED_W), jnp.uint32),
                        pltpu.SemaphoreType.DMA((2,))],
    )
    return pl.pallas_call(
        _combine_kernel,
        out_shape=jax.ShapeDtypeStruct((T, D_MODEL), F32),
        grid_spec=grid_spec,
        compiler_params=_cparams("arbitrary"),
        name="combine_ln",
    )(dest_flat, h, gate, g, b, ys)


def _rot_cols(w):
    half = w.shape[-1] // 2
    return jnp.concatenate([-w[:, half:], w[:, :half]], axis=-1)


def _pack_w_in(w_in):
    lat = Q_LORA + KV_LORA
    w_kr = w_in[:, lat:lat + MLA_ROPE]
    return jnp.concatenate([w_in[:, lat + MLA_ROPE:], w_in[:, :lat + MLA_ROPE], _rot_cols(w_kr)],
                           axis=-1).astype(BF16)


def _pack_w_uq(w_uq):
    w = w_uq.reshape(Q_LORA, MLA_HEADS, MLA_NOPE + MLA_ROPE)
    rope = w[:, :, MLA_NOPE:]
    rot = jnp.concatenate([-rope[:, :, MLA_ROPE // 2:], rope[:, :, :MLA_ROPE // 2]], axis=-1)
    return jnp.concatenate([w, rot], axis=-1).reshape(Q_LORA, MLA_HEADS * MLA_QK).astype(BF16)


def _rope_freqs():
    ret = 1.0 / (10000.0 ** jnp.linspace(0.0, 1.0, RET_DK // 2, dtype=F32))
    mla = 1.0 / (ROPE_BASE ** (jnp.arange(0, MLA_ROPE, 2, dtype=F32) / MLA_ROPE))
    return jnp.concatenate([ret, mla, jnp.zeros((LANES - ret.shape[0] - mla.shape[0],), F32)])[None, :]


def _routing_plan(idx_out, cnt):
    top_i = idx_out[:, :TOP_K]
    rank = idx_out[:, TOP_K:2 * TOP_K]
    counts = cnt[0, :N_EXPERTS]
    padded = ((counts + EXPERT_TILE - 1) // EXPERT_TILE) * EXPERT_TILE
    pad_end = jnp.cumsum(padded)
    pad_start = pad_end - padded
    experts = jnp.arange(N_EXPERTS, dtype=jnp.int32)
    dest = rank + jnp.sum(jnp.where(top_i[..., None] == experts, pad_start, 0), axis=-1)
    n_rows = top_i.shape[0] * TOP_K + N_EXPERTS * EXPERT_TILE
    n_blocks = n_rows // EXPERT_TILE
    blk_row = jnp.arange(n_blocks, dtype=jnp.int32) * EXPERT_TILE
    blk_e = jnp.minimum(jnp.sum((pad_end[None, :] <= blk_row[:, None]).astype(jnp.int32), axis=-1),
                        N_EXPERTS - 1)
    nact = (pad_end[-1:] // EXPERT_TILE).astype(jnp.int32)
    later = jnp.where((counts > 0)[None, :] & (experts[None, :] > experts[:, None]), experts[None, :], N_EXPERTS)
    nxt = jnp.min(later, axis=-1)
    nxt = jnp.where(nxt == N_EXPERTS, -1, nxt)
    next_e = jnp.sum(jnp.where(blk_e[:, None] == experts[None, :], nxt[None, :], 0), axis=-1).astype(jnp.int32)
    return dest.astype(jnp.int32), blk_e, nact, next_e, n_rows


def kernel(x, positions, w_in, q_norm_g, w_uq, kv_norm_g, w_ukv, w_o, ln1_g, ln1_b,
           w_router, b_router, w_up, b_up, w_down, b_down, ln2_g, ln2_b):
    B, S, _ = x.shape
    T = B * S
    pos_col = positions.reshape(T, 1).astype(jnp.int32)
    rc, rs, mc, ms = _rope_tables(pos_col, _rope_freqs())
    dec, qw, kw, cd = _retention_constants()
    h = x.reshape(T, D_MODEL)
    for l in range(DEPTH):
        proj = _input_projection(h, _pack_w_in(w_in[l]))
        w_kv = w_ukv[l].reshape(KV_LORA, MLA_HEADS, MLA_NOPE + MLA_V)
        w_k = w_kv[:, :, :MLA_NOPE].reshape(KV_LORA, MLA_HEADS * MLA_NOPE).astype(BF16)
        w_vt = w_kv[:, :, MLA_NOPE:].reshape(KV_LORA, MLA_HEADS * MLA_V).T.astype(BF16)
        q_cat, kn, vt, kr = _mla_prep(proj, mc, ms, _pack_w_uq(w_uq[l]), w_k, w_vt,
                                      q_norm_g[l][None, :], kv_norm_g[l][None, :])
        o_mla = _attention(q_cat, kn, kr, vt, B, S)
        o_ret = _retention(proj, rc, rs, dec, qw, kw, cd, B, S)
        h1, h1p = _merge(h, o_mla, o_ret, proj, w_o[l].astype(BF16), ln1_g[l][None, :], ln1_b[l][None, :])
        idx_out, gate, cnt = _router(h1, w_router[l], b_router[l])
        dest, blk_e, nact, next_e, n_rows = _routing_plan(idx_out, cnt)
        dest_flat = dest.reshape(-1)
        xs = _dispatch(dest_flat, h1p, n_rows)
        ys = _experts(blk_e, nact, next_e, xs, w_up[l], b_up[l], w_down[l], b_down[l])
        h = _combine(dest_flat, h1, ys, gate, ln2_g[l][None, :], ln2_b[l][None, :])
    return h.reshape(B, S, D_MODEL)
```

```python
import functools

import numpy as np
import jax
import jax.numpy as jnp
from jax import lax
from jax.experimental import pallas as pl
from jax.experimental.pallas import tpu as pltpu

F32 = jnp.float32
BF16 = jnp.bfloat16

D_MODEL = 1024
DEPTH = 1
MLA_HEADS = 8
MLA_NOPE = 128
MLA_ROPE = 64
MLA_V = 128
Q_LORA = 384
KV_LORA = 256
ROPE_BASE = 10000.0
RET_HEADS = 8
RET_DK = 128
RET_DV = 128
RET_CHUNK = 128
N_EXPERTS = 32
TOP_K = 4
D_EXPERT = 1024
SWIGLU_LIMIT = 7.0
SWIGLU_ALPHA = 1.702
DN_ALPHA = (2.0 * DEPTH) ** 0.25
LN_EPS = 1e-5
RMS_EPS = 1e-6
GN_EPS = 1e-6
NEG_INF = -1e30

LANES = 128
SUBLANES = 8
MLA_QK = 2 * LANES
PACKED_W = D_MODEL // 2
LATENT_W = 768
N_WIDE = 6 * D_MODEL
N_PROJ = N_WIDE + LATENT_W
VMEM_LIMIT = 56 * 1024 * 1024

ROW_TILE = 512
ATTN_TILE = ROW_TILE
ATTN_Q_SPLIT = 2
RET_GROUP = 4
EXPERT_TILE = 256
MOVE_TILE = 256


def _cparams(*sem):
    return pltpu.CompilerParams(dimension_semantics=sem, vmem_limit_bytes=VMEM_LIMIT)


def _const_spec(shape):
    nd = len(shape)
    return pl.BlockSpec(shape, lambda *_: (0,) * nd, pipeline_mode=pl.Buffered(1))


def _sigmoid(x):
    return 1.0 / (1.0 + jnp.exp(-x))


def _pack_rows(x):
    n = x.shape[-1] // 2
    xb = x.astype(BF16).astype(F32)
    lo = pltpu.bitcast(xb[:, :n], jnp.uint32) >> 16
    hi = pltpu.bitcast(xb[:, n:], jnp.uint32) & jnp.uint32(0xFFFF0000)
    return lo | hi


def _unpack_rows(w):
    lo = pltpu.bitcast(w << 16, F32)
    hi = pltpu.bitcast(w & jnp.uint32(0xFFFF0000), F32)
    return lo, hi


def _swap_halves(x):
    return pltpu.roll(x, LANES // 2, 1)


def _proj_kernel(x_ref, w_ref, o_ref):
    xb = x_ref[...].astype(BF16)
    for c in range(N_PROJ // LATENT_W):
        sl = slice(c * LATENT_W, (c + 1) * LATENT_W)
        o_ref[:, sl] = jnp.dot(xb, w_ref[:, sl], preferred_element_type=F32).astype(o_ref.dtype)


def _input_projection(x2, w_cat):
    T = x2.shape[0]
    return pl.pallas_call(
        _proj_kernel,
        out_shape=jax.ShapeDtypeStruct((T, N_PROJ), BF16),
        grid=(T // ROW_TILE,),
        in_specs=[pl.BlockSpec((ROW_TILE, D_MODEL), lambda i: (i, 0)),
                  _const_spec((D_MODEL, N_PROJ))],
        out_specs=pl.BlockSpec((ROW_TILE, N_PROJ), lambda i: (i, 0)),
        compiler_params=_cparams("parallel"),
        name="input_projection",
    )(x2, w_cat)


def _rope_table_kernel(pos_ref, freq_ref, rc_ref, rs_ref, mc_ref, ms_ref):
    ang = pos_ref[...].astype(F32) * freq_ref[...]
    c = jnp.cos(ang)
    s = jnp.sin(ang)
    lane = lax.broadcasted_iota(jnp.int32, ang.shape, 1)
    c64 = pltpu.roll(c, 64, 1)
    s64 = pltpu.roll(s, 64, 1)
    rc_ref[...] = jnp.where(lane < 64, c, c64)
    rs_ref[...] = jnp.where(lane < 64, -s, s64)
    c96 = pltpu.roll(c, 96, 1)
    s96 = pltpu.roll(s, 96, 1)
    mc_ref[...] = jnp.where(lane < 32, c64, jnp.where(lane < 64, c96, 0.0))
    ms_ref[...] = jnp.where(lane < 32, s64, jnp.where(lane < 64, s96, 0.0))


def _rope_tables(pos_col, freq_row):
    T = pos_col.shape[0]
    rows = min(2048, T)
    tab =jax.ShapeDtypeStruct((T, LANES), F32)
    spec = pl.BlockSpec((rows, LANES), lambda i: (i, 0))
    return pl.pallas_call(
        _rope_table_kernel,
        out_shape=(tab, tab, tab, tab),
        grid=(T // rows,),
        in_specs=[pl.BlockSpec((rows, 1), lambda i: (i, 0)), _const_spec((1, LANES))],
        out_specs=(spec, spec, spec, spec),
        compiler_params=_cparams("parallel"),
        name="rope_tables",
    )(pos_col, freq_row)


def _mla_prep_kernel(a_ref, mc_ref, ms_ref, wq_ref, wk_ref, wvt_ref, qg_ref, kvg_ref,
                     q_ref, kn_ref, vt_ref, kr_ref):
    a = a_ref[...].astype(F32)
    mc = mc_ref[...]
    ms = ms_ref[...]
    cq = a[:, :Q_LORA]
    qn = cq * lax.rsqrt(jnp.mean(cq * cq, axis=-1, keepdims=True) + RMS_EPS) * qg_ref[...]
    q = jnp.dot(qn.astype(BF16), wq_ref[...], preferred_element_type=F32)
    scale = (MLA_NOPE + MLA_ROPE) ** -0.5
    for h in range(MLA_HEADS):
        nope = q[:, h * MLA_QK:h * MLA_QK + LANES]
        blk = q[:, h * MLA_QK + LANES:(h + 1) * MLA_QK]
        rope = blk * mc + _swap_halves(blk) * ms
        q_ref[:, h * MLA_QK:h * MLA_QK + LANES] = (nope * scale).astype(BF16)
        q_ref[:, h * MLA_QK + LANES:(h + 1) * MLA_QK] = (rope * scale).astype(BF16)
    ckv = a[:, Q_LORA:Q_LORA + KV_LORA]
    kvn = ckv * lax.rsqrt(jnp.mean(ckv * ckv, axis=-1, keepdims=True) + RMS_EPS) * kvg_ref[...]
    kvb = kvn.astype(BF16)
    kn_ref[...] = jnp.dot(kvb, wk_ref[...], preferred_element_type=F32).astype(BF16)
    vt_ref[0] = lax.dot_general(wvt_ref[...], kvb, (((1,), (1,)), ((), ())),
                                preferred_element_type=F32).astype(BF16)
    krb =a[:, Q_LORA + KV_LORA:]
    kr_ref[...] = (krb * mc + _swap_halves(krb) * ms).astype(BF16)


def _mla_prep(proj, mc, ms, wq, wk, wvt, qg, kvg):
    T = proj.shape[0]
    row = lambda w: pl.BlockSpec((ROW_TILE, w), lambda i: (i, 0))
    return pl.pallas_call(
        _mla_prep_kernel,
        out_shape=(jax.ShapeDtypeStruct((T, MLA_HEADS * MLA_QK), BF16),
                   jax.ShapeDtypeStruct((T, MLA_HEADS * MLA_NOPE), BF16),
                   jax.ShapeDtypeStruct((T // ROW_TILE, MLA_HEADS * MLA_V, ROW_TILE), BF16),
                   jax.ShapeDtypeStruct((T, LANES), BF16)),
        grid=(T // ROW_TILE,),
        in_specs=[pl.BlockSpec((ROW_TILE, LATENT_W), lambda i: (i, N_WIDE // LATENT_W)),
                  row(LANES), row(LANES),
                  _const_spec(wq.shape), _const_spec(wk.shape), _const_spec(wvt.shape),
                  _const_spec(qg.shape), _const_spec(kvg.shape)],
        out_specs=(row(MLA_HEADS * MLA_QK), row(MLA_HEADS * MLA_NOPE),
                   pl.BlockSpec((1, MLA_HEADS * MLA_V, ROW_TILE), lambda i: (i, 0, 0)), row(LANES)),
        compiler_params=_cparams("parallel"),
        name="mla_prep",
    )(proj, mc, ms, wq, wk, wvt, qg, kvg)


def _attn_kernel(q_ref, kn_ref, kr_ref, vt_ref, o_ref, acc_scr):
    qi = pl.program_id(2)
    W = ATTN_TILE // ATTN_Q_SPLIT
    qs = [q_ref[c * W:(c + 1) * W, :] for c in range(ATTN_Q_SPLIT)]
    acc_scr[...] = jnp.zeros(acc_scr.shape, F32)

    def scores(j):
        rows = pl.ds(pl.multiple_of(j * ATTN_TILE, ATTN_TILE), ATTN_TILE)
        k = jnp.concatenate([kn_ref[rows, :], kr_ref[rows, :]], axis=-1)
        return tuple(lax.dot_general(k, qs[c], (((1,), (1,)), ((), ())), preferred_element_type=F32)
                     for c in range(ATTN_Q_SPLIT))

    def softmax_pv(j, s_all, stats, masked):
        vt = vt_ref[j]
        out = []
        for c in range(ATTN_Q_SPLIT):
            m_prev, l_prev = stats[c]
            s = s_all[c]
            if masked:
                kp = lax.broadcasted_iota(jnp.int32, s.shape, 0)
                qp = lax.broadcasted_iota(jnp.int32, s.shape, 1) + c * W
                s = jnp.where(kp <= qp, s, NEG_INF)
            m_new = jnp.maximum(m_prev, jnp.max(s, axis=0, keepdims=True))
            alpha = jnp.exp(m_prev - m_new)
            p = jnp.exp(s - m_new)
            l_new = alpha * l_prev + jnp.sum(p, axis=0, keepdims=True)
            acc_scr[c] = alpha * acc_scr[c] + jnp.dot(vt, p.astype(BF16), preferred_element_type=F32)
            out.append((m_new, l_new))
        return tuple(out)

    def body(j, carry):
        s_cur, stats = carry
        s_next = scores(j + 1)
        return s_next, softmax_pv(j, s_cur, stats, False)

    init = tuple((jnp.full((1, W), NEG_INF, F32), jnp.zeros((1, W), F32)) for _ in range(ATTN_Q_SPLIT))
    s_last, stats = lax.fori_loop(0, qi, body, (scores(0), init))
    stats = softmax_pv(qi, s_last, stats, True)
    for c in range(ATTN_Q_SPLIT):
        o_ref[c * W:(c + 1) * W, :] = (acc_scr[c] / stats[c][1]).T.astype(o_ref.dtype)


def _attention(q_cat, kn, kr, vt, B, S):
    T = B * S
    nq = S // ATTN_TILE
    return pl.pallas_call(
        _attn_kernel,
        out_shape=jax.ShapeDtypeStruct((T, MLA_HEADS * MLA_V), BF16),
        grid=(B, MLA_HEADS, nq),
        in_specs=[pl.BlockSpec((ATTN_TILE, MLA_QK), lambda b, h, i: (b * nq + i, h)),
                  pl.BlockSpec((S, MLA_NOPE), lambda b, h, i: (b, h)),
                  pl.BlockSpec((S, LANES), lambda b, h, i: (b, 0)),
                  pl.BlockSpec((nq, MLA_V, ATTN_TILE), lambda b, h, i: (b, h, 0))],
        out_specs=pl.BlockSpec((ATTN_TILE, MLA_V), lambda b, h, i: (b * nq + i, h)),
        scratch_shapes=[pltpu.VMEM((ATTN_Q_SPLIT, MLA_V, ATTN_TILE // ATTN_Q_SPLIT), F32)],
        compiler_params=_cparams("parallel", "parallel", "arbitrary"),
        name="mla_attention",
    )(q_cat, kn, kr, vt)


def _ret_kernel(q_ref, k_ref, v_ref, g_ref, rc_ref, rs_ref, dec_ref, qw_ref, kw_ref, cd_ref,
                o_ref, qb_scr, qwb_scr, kb_scr, kwb_scr):
    S = q_ref.shape[0]
    C = RET_CHUNK
    n_chunks = S // C
    rc = rc_ref[...]
    rs = rs_ref[...]
    q = q_ref[...].astype(F32)
    k = k_ref[...].astype(F32)
    q = q * rc + _swap_halves(q) * rs
    k = (k * rc + _swap_halves(k) * rs) * (RET_DK ** -0.5)
    qw = qw_ref[0]
    kw = kw_ref[0]
    q3 = q.reshape(n_chunks, C, RET_DK)
    k3 = k.reshape(n_chunks, C, RET_DK)
    qb_scr[...] = q.astype(BF16)
    kb_scr[...] = k.astype(BF16)
    qwb_scr[...] = (q3 * qw[None]).reshape(S, RET_DK).astype(BF16)
    kwb_scr[...] = (k3 * kw[None]).reshape(S, RET_DK)
    dec = dec_ref[0]
    cd = cd_ref[0]

    def group(gi, R):
        for u in range(RET_GROUP):
            rows = pl.ds(pl.multiple_of((gi * RET_GROUP + u) * C, C), C)
            qn = qb_scr[rows, :]
            kn = kb_scr[rows, :]
            vn = v_ref[rows, :]
            s = lax.dot_general(qn, kn, (((1,), (1,)), ((), ())), preferred_element_type=F32) * dec
            o = jnp.dot(s.astype(BF16), vn, preferred_element_type=F32)
            o = o + jnp.dot(qwb_scr[rows, :], R.astype(BF16), preferred_element_type=F32)
            kwt = kwb_scr[rows, :].T.astype(BF16)
            R = cd * R + jnp.dot(kwt, vn, preferred_element_type=F32)
            mu = jnp.mean(o, axis=-1, keepdims=True)
            d = o - mu
            var = jnp.mean(d * d, axis=-1, keepdims=True)
            gate = g_ref[rows, :].astype(F32)
            o_ref[rows, :] = (d * lax.rsqrt(var + GN_EPS) * (gate * _sigmoid(gate))).astype(o_ref.dtype)
        return R

    lax.fori_loop(0, n_chunks // RET_GROUP, group, jnp.zeros((RET_DK, RET_DV), F32))


def _retention(proj, rc, rs, dec, qw, kw, cd, B, S):
    T = B * S
    H = RET_HEADS
    col = lambda off: pl.BlockSpec((S, LANES), lambda b, h: (b, off + h))
    tab = pl.BlockSpec((S, LANES), lambda b, h: (b, 0))
    per_head = lambda r: pl.BlockSpec((1, r, LANES), lambda b, h: (h, 0, 0))
    return pl.pallas_call(
        _ret_kernel,
        out_shape=jax.ShapeDtypeStruct((T, H * RET_DV), BF16),
        grid=(B, H),
        in_specs=[col(0), col(H), col(2 * H), col(3 * H), tab, tab,
                  per_head(RET_CHUNK), per_head(RET_CHUNK), per_head(RET_CHUNK), per_head(1)],
        out_specs=pl.BlockSpec((S, RET_DV), lambda b, h: (b, h)),
        scratch_shapes=[pltpu.VMEM((S, RET_DK), BF16), pltpu.VMEM((S, RET_DK), BF16),
                        pltpu.VMEM((S, RET_DK), BF16), pltpu.VMEM((S, RET_DK), F32)],
        compiler_params=_cparams("parallel", "parallel"),
        name="retention",
    )(proj, proj, proj, proj, rc, rs, dec, qw, kw, cd)


def _retention_constants():
    C = RET_CHUNK
    h = np.arange(RET_HEADS, dtype=np.float64)
    log_gamma = np.log(1.0 - 2.0 ** (-5.0 - h))
    idx = np.arange(C, dtype=np.float64)
    rel = idx[:, None] - idx[None, :]
    dec = np.where(rel[None] >= 0, np.exp(np.maximum(rel, 0.0)[None] * log_gamma[:, None, None]), 0.0)
    qw = np.exp((idx + 1.0)[None, :] * log_gamma[:, None])
    kw = np.exp((C - 1.0 - idx)[None, :] * log_gamma[:, None])
    cd = np.exp(C * log_gamma)
    bc = lambda a: np.ascontiguousarray(np.broadcast_to(a[..., None], a.shape + (LANES,))).astype(np.float32)
    return dec.astype(np.float32), bc(qw), bc(kw), bc(cd[:, None])


def _layernorm(y, g, b):
    mu = jnp.mean(y, axis=-1, keepdims=True)
    d = y - mu
    var = jnp.mean(d * d, axis=-1, keepdims=True)
    return d * lax.rsqrt(var + LN_EPS) * g + b


def _merge_kernel(x_ref, om_ref, or_ref, gm_ref, gr_ref, wo_ref, g_ref, b_ref, h_ref, hp_ref):
    mixed = (_sigmoid(gm_ref[...].astype(F32)) * om_ref[...].astype(F32)
             + _sigmoid(gr_ref[...].astype(F32)) * or_ref[...].astype(F32))
    mix = jnp.dot(mixed.astype(BF16), wo_ref[...], preferred_element_type=F32)
    h = _layernorm(DN_ALPHA * x_ref[...] + mix, g_ref[...], b_ref[...])
    h_ref[...] = h
    hp_ref[...] = _pack_rows(h)


def _merge(x2, o_mla, o_ret, proj, wo, g, b):
    T = x2.shape[0]
    row = lambda c: pl.BlockSpec((ROW_TILE, D_MODEL), lambda i: (i, c))
    return pl.pallas_call(
        _merge_kernel,
        out_shape=(jax.ShapeDtypeStruct((T, D_MODEL), F32), jax.ShapeDtypeStruct((T, PACKED_W), jnp.uint32)),
        grid=(T // ROW_TILE,),
        in_specs=[row(0), row(0), row(0), row(4), row(5),
                  _const_spec(wo.shape), _const_spec(g.shape), _const_spec(b.shape)],
        out_specs=(row(0), pl.BlockSpec((ROW_TILE, PACKED_W), lambda i: (i, 0))),
        compiler_params=_cparams("parallel"),
        name="merge_out_proj_ln",
    )(x2, o_mla, o_ret, proj, proj, wo, g, b)


def _split3(a):
    a0 = a.astype(BF16)
    r1 = a - a0.astype(F32)
    a1 = r1.astype(BF16)
    a2 = (r1 - a1.astype(F32)).astype(BF16)
    return a0, a1, a2


def _router_kernel(h_ref, w0_ref, w1_ref, w2_ref, b_ref, idx_ref, gate_ref, cnt_ref, run_scr):
    i = pl.program_id(0)

    @pl.when(i == 0)
    def _():
        run_scr[...] = jnp.zeros(run_scr.shape, F32)

    h0, h1, h2 = _split3(h_ref[...])
    w0, w1, w2 = w0_ref[...], w1_ref[...], w2_ref[...]
    dot = lambda a, w: jnp.dot(a, w, preferred_element_type=F32)
    logits = (dot(h0, w0) + (dot(h0, w1) + dot(h1, w0))
              + (dot(h0, w2) + dot(h1, w1) + dot(h2, w0))) + b_ref[...]
    rows = logits.shape[0]
    lane = lax.broadcasted_iota(jnp.int32, (rows, LANES), 1)
    lane_f = lane.astype(F32)
    work = jnp.where(lane < N_EXPERTS, logits, -jnp.inf)
    vals, ids, sels = [], [], []
    for _ in range(TOP_K):
        m = jnp.max(work, axis=-1, keepdims=True)
        first = jnp.min(jnp.where(work == m, lane_f, float(LANES)), axis=-1, keepdims=True)
        sel = lane_f == first
        work = jnp.where(sel, -jnp.inf, work)
        vals.append(m)
        ids.append(first)
        sels.append(sel)
    exps = [jnp.exp(v - vals[0]) for v in vals]
    denom = exps[0] + exps[1] + exps[2] + exps[3]
    chosen = jnp.zeros((rows, LANES), F32)
    for sel in sels:
        chosen = jnp.where(sel, 1.0, chosen)
    r = lax.broadcasted_iota(jnp.int32, (rows, rows), 0)
    c = lax.broadcasted_iota(jnp.int32, (rows, rows), 1)
    lower = jnp.where(r > c, 1.0, 0.0).astype(BF16)
    before = jnp.dot(lower, chosen.astype(BF16), preferred_element_type=F32) + run_scr[...]
    idx_out = jnp.zeros((rows, LANES), jnp.int32)
    gate_out = jnp.zeros((rows, LANES), F32)
    for kk in range(TOP_K):
        rank = jnp.sum(jnp.where(sels[kk], before, 0.0), axis=-1, keepdims=True).astype(jnp.int32)
        idx_out = jnp.where(lane == kk, ids[kk].astype(jnp.int32), idx_out)
        idx_out = jnp.where(lane == TOP_K + kk, rank, idx_out)
        gate_out = jnp.where(lane == kk, exps[kk] / denom, gate_out)
    idx_ref[...] = idx_out
    gate_ref[...] = gate_out
    total = run_scr[...] + jnp.sum(chosen, axis=0, keepdims=True)
    run_scr[...] = total
    cnt_ref[...] = total.astype(jnp.int32)


def _router(h, w_router, b_router):
    T = h.shape[0]
    wpad = jnp.zeros((D_MODEL, LANES), F32).at[:, :N_EXPERTS].set(w_router)
    w0 = wpad.astype(BF16)
    r1 = wpad - w0.astype(F32)
    w1 = r1.astype(BF16)
    w2 = (r1 - w1.astype(F32)).astype(BF16)
    bpad = jnp.zeros((1, LANES), F32).at[0, :N_EXPERTS].set(b_router)
    row = pl.BlockSpec((ROW_TILE, LANES), lambda i: (i, 0))
    return pl.pallas_call(
        _router_kernel,
        out_shape=(jax.ShapeDtypeStruct((T, LANES), jnp.int32),
                   jax.ShapeDtypeStruct((T, LANES), F32),
                   jax.ShapeDtypeStruct((1, LANES), jnp.int32)),
        grid=(T // ROW_TILE,),
        in_specs=[pl.BlockSpec((ROW_TILE, D_MODEL), lambda i: (i, 0)),
                  _const_spec(w0.shape), _const_spec(w0.shape), _const_spec(w0.shape),
                  _const_spec(bpad.shape)],
        out_specs=(row, row, pl.BlockSpec((1, LANES), lambda i: (0, 0))),
        scratch_shapes=[pltpu.VMEM((1, LANES), F32)],
        compiler_params=_cparams("arbitrary"),
        name="router",
    )(h, w0, w1, w2, bpad)


def _row_copy(src, src_row, dst, dst_row, sem):
    return pltpu.make_async_copy(src.at[pl.ds(src_row, 1), :], dst.at[pl.ds(dst_row, 1), :], sem)


def _dispatch_kernel(dest_ref, h_ref, xs_in_ref, xs_ref, sem):
    del xs_in_ref
    base = pl.program_id(0) * (MOVE_TILE * TOP_K)

    def issue(g, carry):
        r0 = pl.multiple_of(g * SUBLANES, SUBLANES)
        for u in range(SUBLANES):
            for kk in range(TOP_K):
                d = dest_ref[base + (r0 + u) * TOP_K + kk]
                _row_copy(h_ref, r0 + u, xs_ref, d, sem).start(priority=kk % 2)
        return carry

    lax.fori_loop(0, MOVE_TILE // SUBLANES, issue, 0)
    for _ in range(TOP_K):
        pltpu.make_async_copy(h_ref, xs_ref.at[pl.ds(0, MOVE_TILE), :], sem).wait()


def _dispatch(dest_flat, hp, n_rows):
    T = hp.shape[0]
    grid_spec = pltpu.PrefetchScalarGridSpec(
        num_scalar_prefetch=1,
        grid=(T // MOVE_TILE,),
        in_specs=[pl.BlockSpec((MOVE_TILE, PACKED_W), lambda i, d: (i, 0)),
                  pl.BlockSpec(memory_space=pl.ANY)],
        out_specs=pl.BlockSpec(memory_space=pl.ANY),
        scratch_shapes=[pltpu.SemaphoreType.DMA],
    )
    return pl.pallas_call(
        _dispatch_kernel,
        out_shape=jax.ShapeDtypeStruct((n_rows, PACKED_W), jnp.uint32),
        grid_spec=grid_spec,
        input_output_aliases={2: 0},
        compiler_params=_cparams("arbitrary"),
        name="dispatch_rows",
    )(dest_flat, hp, jnp.zeros((n_rows, PACKED_W), jnp.uint32))


def _expert_kernel(blk_e_ref, nact_ref, next_e_ref, x_ref, bu_ref, bd_ref, wu_hbm, wd_hbm, y_ref,
                   wuf_scr, wdf_scr, wub_scr, wdb_scr, sems):
    i = pl.program_id(0)
    e = blk_e_ref[i]

    def weight_copies(expert):
        return (pltpu.make_async_copy(wu_hbm.at[expert], wuf_scr, sems.at[0]),
                pltpu.make_async_copy(wd_hbm.at[expert], wdf_scr, sems.at[1]))

    @pl.when(i == 0)
    def _():
        for cp in weight_copies(e):
            cp.start()

    first_of_group = jnp.logical_or(i == 0, e != blk_e_ref[jnp.maximum(i - 1, 0)])

    @pl.when(jnp.logical_and(i < nact_ref[0], first_of_group))
    def _():
        for cp in weight_copies(e):
            cp.wait()
        wub_scr[...] = wuf_scr[...].astype(BF16)
        wdb_scr[...] = wdf_scr[...].astype(BF16)

        @pl.when(next_e_ref[i] >= 0)
        def _():
            for cp in weight_copies(next_e_ref[i]):
                cp.start()

    @pl.when(i < nact_ref[0])
    def _():
        lo, hi = _unpack_rows(x_ref[...])
        x = jnp.concatenate([lo.astype(BF16), hi.astype(BF16)], axis=-1)
        acc = jnp.zeros((EXPERT_TILE, D_MODEL), F32)
        step = 256
        for c in range(D_EXPERT // step):
            glu = jnp.dot(x, wub_scr[:, c * step:(c + 1) * step], preferred_element_type=F32)
            lin = jnp.dot(x, wub_scr[:, D_EXPERT + c * step:D_EXPERT + (c + 1) * step],
                          preferred_element_type=F32)
            glu = jnp.minimum(glu + bu_ref[0, :, c * step:(c + 1) * step], SWIGLU_LIMIT)
            lin = jnp.clip(lin + bu_ref[0, :, D_EXPERT + c * step:D_EXPERT + (c + 1) * step],
                           -SWIGLU_LIMIT, SWIGLU_LIMIT)
            a = glu * _sigmoid(SWIGLU_ALPHA * glu) * (lin + 1.0)
            acc = acc + jnp.dot(a.astype(BF16), wdb_scr[c * step:(c + 1) * step, :],
                                preferred_element_type=F32)
        y_ref[...] = _pack_rows(acc + bd_ref[0])

    @pl.when(i >= nact_ref[0])
    def _():
        y_ref[...] = jnp.zeros(y_ref.shape, y_ref.dtype)


def _experts(blk_e, nact, next_e, xs, w_up, b_up, w_down, b_down):
    P = xs.shape[0]
    n_blocks = P // EXPERT_TILE
    act = lambda i, na: jnp.minimum(i, na[0] - 1)
    grid_spec = pltpu.PrefetchScalarGridSpec(
        num_scalar_prefetch=3,
        grid=(n_blocks,),
        in_specs=[pl.BlockSpec((EXPERT_TILE, PACKED_W), lambda i, be, na, ne: (act(i, na), 0)),
                  pl.BlockSpec((1, 1, 2 * D_EXPERT), lambda i, be, na, ne: (be[act(i, na)], 0, 0)),
                  pl.BlockSpec((1, 1, D_MODEL), lambda i, be, na, ne: (be[act(i, na)], 0, 0)),
                  pl.BlockSpec(memory_space=pl.ANY),
                  pl.BlockSpec(memory_space=pl.ANY)],
        out_specs=pl.BlockSpec((EXPERT_TILE, PACKED_W), lambda i, be, na, ne: (i, 0)),
        scratch_shapes=[pltpu.VMEM((D_MODEL, 2 * D_EXPERT), F32),
                        pltpu.VMEM((D_EXPERT, D_MODEL), F32),
                        pltpu.VMEM((D_MODEL, 2 * D_EXPERT), BF16),
                        pltpu.VMEM((D_EXPERT, D_MODEL), BF16),
                        pltpu.SemaphoreType.DMA((2,))],
    )
    return pl.pallas_call(
        _expert_kernel,
        out_shape=jax.ShapeDtypeStruct((P, PACKED_W), jnp.uint32),
        grid_spec=grid_spec,
        compiler_params=_cparams("arbitrary"),
        name="routed_experts",
    )(blk_e, nact, next_e, xs, b_up.reshape(N_EXPERTS, 1, 2 * D_EXPERT),
      b_down.reshape(N_EXPERTS, 1, D_MODEL), w_up, w_down)


def _combine_kernel(dest_ref, h_ref, gate_ref, g_ref, b_ref, ys_ref, o_ref, ybuf, sems):
    i = pl.program_id(0)
    n = pl.num_programs(0)

    def gather(step, slot):
        base = step * (MOVE_TILE * TOP_K)

        def issue(g, carry):
            r0 = pl.multiple_of(g * SUBLANES, SUBLANES)
            for u in range(SUBLANES):
                for kk in range(TOP_K):
                    d = dest_ref[base + (r0 + u) * TOP_K + kk]
                    _row_copy(ys_ref, d, ybuf.at[slot, kk], r0 + u, sems.at[slot]).start(priority=kk % 2)
            return carry

        lax.fori_loop(0, MOVE_TILE // SUBLANES, issue, 0)

    @pl.when(i == 0)
    def _():
        gather(0, 0)

    @pl.when(i + 1 < n)
    def _():
        gather(i + 1, (i + 1) % 2)

    slot = i % 2
    for kk in range(TOP_K):
        pltpu.make_async_copy(ys_ref.at[pl.ds(0, MOVE_TILE), :], ybuf.at[slot, kk], sems.at[slot]).wait()

    gate = gate_ref[...]
    f_lo = jnp.zeros((MOVE_TILE, PACKED_W), F32)
    f_hi = jnp.zeros((MOVE_TILE, PACKED_W), F32)
    for kk in range(TOP_K):
        lo, hi = _unpack_rows(ybuf[slot, kk])
        f_lo = f_lo + gate[:, kk:kk + 1] * lo
        f_hi = f_hi + gate[:, kk:kk + 1] * hi
    y_lo = DN_ALPHA * h_ref[:, :PACKED_W] + f_lo
    y_hi = DN_ALPHA * h_ref[:, PACKED_W:] + f_hi
    mu = (jnp.sum(y_lo, axis=-1, keepdims=True) + jnp.sum(y_hi, axis=-1, keepdims=True)) / D_MODEL
    d_lo = y_lo - mu
    d_hi = y_hi - mu
    var = (jnp.sum(d_lo * d_lo, axis=-1, keepdims=True) + jnp.sum(d_hi * d_hi, axis=-1, keepdims=True)) / D_MODEL
    inv = lax.rsqrt(var + LN_EPS)
    o_ref[:, :PACKED_W] = d_lo * inv * g_ref[:, :PACKED_W] + b_ref[:, :PACKED_W]
    o_ref[:, PACKED_W:] = d_hi * inv * g_ref[:, PACKED_W:] + b_ref[:, PACKED_W:]


def _combine(dest_flat, h, ys, gate, g, b):
    T = h.shape[0]
    grid_spec = pltpu.PrefetchScalarGridSpec(
        num_scalar_prefetch=1,
        grid=(T // MOVE_TILE,),
        in_specs=[pl.BlockSpec((MOVE_TILE, D_MODEL), lambda i, d: (i, 0)),
                  pl.BlockSpec((MOVE_TILE, LANES), lambda i, d: (i, 0)),
                  pl.BlockSpec(g.shape, lambda i, d: (0, 0)),
                  pl.BlockSpec(b.shape, lambda i, d: (0, 0)),
                  pl.BlockSpec(memory_space=pl.ANY)],
        out_specs=pl.BlockSpec((MOVE_TILE, D_MODEL), lambda i, d: (i, 0)),
        scratch_shapes=[pltpu.VMEM((2, TOP_K, MOVE_TILE, PACKED_W), jnp.uint32),
                        pltpu.SemaphoreType.DMA((2,))],
    )
    return pl.pallas_call(
        _combine_kernel,
        out_shape=jax.ShapeDtypeStruct((T, D_MODEL), F32),
        grid_spec=grid_spec,
        compiler_params=_cparams("arbitrary"),
        name="combine_ln",
    )(dest_flat, h, gate, g, b, ys)


def _rot_cols(w):
    half = w.shape[-1] // 2
    return jnp.concatenate([-w[:, half:], w[:, :half]], axis=-1)


def _pack_w_in(w_in):
    lat = Q_LORA + KV_LORA
    w_kr = w_in[:, lat:lat + MLA_ROPE]
    return jnp.concatenate([w_in[:, lat + MLA_ROPE:], w_in[:, :lat + MLA_ROPE], _rot_cols(w_kr)],
                           axis=-1).astype(BF16)


def _pack_w_uq(w_uq):
    w = w_uq.reshape(Q_LORA, MLA_HEADS, MLA_NOPE + MLA_ROPE)
    rope = w[:, :, MLA_NOPE:]
    rot = jnp.concatenate([-rope[:, :, MLA_ROPE // 2:], rope[:, :, :MLA_ROPE // 2]], axis=-1)
    return jnp.concatenate([w, rot], axis=-1).reshape(Q_LORA, MLA_HEADS * MLA_QK).astype(BF16)


def _rope_freqs():
    ret = 1.0 / (10000.0 ** jnp.linspace(0.0, 1.0, RET_DK // 2, dtype=F32))
    mla = 1.0 / (ROPE_BASE ** (jnp.arange(0, MLA_ROPE, 2, dtype=F32) / MLA_ROPE))
    return jnp.concatenate([ret, mla, jnp.zeros((LANES - ret.shape[0] - mla.shape[0],), F32)])[None, :]


def _routing_plan(idx_out, cnt):
    top_i = idx_out[:, :TOP_K]
    rank = idx_out[:, TOP_K:2 * TOP_K]
    counts = cnt[0, :N_EXPERTS]
    padded = ((counts + EXPERT_TILE - 1) // EXPERT_TILE) * EXPERT_TILE
    pad_end = jnp.cumsum(padded)
    pad_start = pad_end - padded
    experts = jnp.arange(N_EXPERTS, dtype=jnp.int32)
    dest = rank + jnp.sum(jnp.where(top_i[..., None] == experts, pad_start, 0), axis=-1)
    n_rows = top_i.shape[0] * TOP_K + N_EXPERTS * EXPERT_TILE
    n_blocks = n_rows // EXPERT_TILE
    blk_row = jnp.arange(n_blocks, dtype=jnp.int32) * EXPERT_TILE
    blk_e = jnp.minimum(jnp.sum((pad_end[None, :] <= blk_row[:, None]).astype(jnp.int32), axis=-1),
                        N_EXPERTS - 1)
    nact = (pad_end[-1:] // EXPERT_TILE).astype(jnp.int32)
    later = jnp.where((counts > 0)[None, :] & (experts[None, :] > experts[:, None]), experts[None, :], N_EXPERTS)
    nxt = jnp.min(later, axis=-1)
    nxt = jnp.where(nxt == N_EXPERTS, -1, nxt)
    next_e = jnp.sum(jnp.where(blk_e[:, None] == experts[None, :], nxt[None, :], 0), axis=-1).astype(jnp.int32)
    return dest.astype(jnp.int32), blk_e, nact, next_e, n_rows


def kernel(x, positions, w_in, q_norm_g, w_uq, kv_norm_g, w_ukv, w_o, ln1_g, ln1_b,
           w_router, b_router, w_up, b_up, w_down, b_down, ln2_g, ln2_b):
    B, S, _ = x.shape
    T = B * S
    pos_col = positions.reshape(T, 1).astype(jnp.int32)
    rc, rs, mc, ms = _rope_tables(pos_col, _rope_freqs())
    dec, qw, kw, cd = _retention_constants()
    h = x.reshape(T, D_MODEL)
    for l in range(DEPTH):
        proj = _input_projection(h, _pack_w_in(w_in[l]))
        w_kv = w_ukv[l].reshape(KV_LORA, MLA_HEADS, MLA_NOPE + MLA_V)
        w_k = w_kv[:, :, :MLA_NOPE].reshape(KV_LORA, MLA_HEADS * MLA_NOPE).astype(BF16)
        w_vt = w_kv[:, :, MLA_NOPE:].reshape(KV_LORA, MLA_HEADS * MLA_V).T.astype(BF16)
        q_cat, kn, vt, kr = _mla_prep(proj, mc, ms, _pack_w_uq(w_uq[l]), w_k, w_vt,
                                      q_norm_g[l][None, :], kv_norm_g[l][None, :])
        o_mla = _attention(q_cat, kn, kr, vt, B, S)
        o_ret = _retention(proj, rc, rs, dec, qw, kw, cd, B, S)
        h1, h1p = _merge(h, o_mla, o_ret, proj, w_o[l].astype(BF16), ln1_g[l][None, :], ln1_b[l][None, :])
        idx_out, gate, cnt = _router(h1, w_router[l], b_router[l])
        dest, blk_e, nact, next_e, n_rows = _routing_plan(idx_out, cnt)
        dest_flat = dest.reshape(-1)
        xs = _dispatch(dest_flat, h1p, n_rows)
        ys = _experts(blk_e, nact, next_e, xs, w_up[l], b_up[l], w_down[l], b_down[l])
        h = _combine(dest_flat, h1, ys, gate, ln2_g[l][None, :], ln2_b[l][None, :])
    return h.reshape(B, S, D_MODEL)
```

```python
import functools

import numpy as np
import jax
import jax.numpy as jnp
from jax import lax
from jax.experimental import pallas as pl
from jax.experimental.pallas import tpu as pltpu

F32 = jnp.float32
BF16 = jnp.bfloat16

D_MODEL = 1024
DEPTH = 1
MLA_HEADS = 8
MLA_NOPE = 128
MLA_ROPE = 64
MLA_V = 128
Q_LORA = 384
KV_LORA = 256
ROPE_BASE = 10000.0
RET_HEADS = 8
RET_DK = 128
RET_DV = 128
RET_CHUNK = 128
N_EXPERTS = 32
TOP_K = 4
D_EXPERT = 1024
SWIGLU_LIMIT = 7.0
SWIGLU_ALPHA = 1.702
DN_ALPHA = (2.0 * DEPTH) ** 0.25
LN_EPS = 1e-5
RMS_EPS = 1e-6
GN_EPS = 1e-6
NEG_INF = -1e30

LANES = 128
SUBLANES = 8
MLA_QK = 2 * LANES
PACKED_W = D_MODEL // 2
LATENT_W = 768
N_WIDE = 6 * D_MODEL
N_PROJ = N_WIDE + LATENT_W
VMEM_LIMIT = 56 * 1024 * 1024

ROW_TILE = 512
ATTN_TILE = ROW_TILE
ATTN_Q_TILE = 2 * ATTN_TILE
ATTN_Q_SPLIT = 4
VT_ROWS = MLA_V + 16
LOG2_E = 1.4426950408889634
RET_GROUP = 4
EXPERT_TILE = 256
MOVE_TILE = 256


def _cparams(*sem):
    return pltpu.CompilerParams(dimension_semantics=sem, vmem_limit_bytes=VMEM_LIMIT)


def _const_spec(shape):
    nd = len(shape)
    return pl.BlockSpec(shape, lambda *_: (0,) * nd, pipeline_mode=pl.Buffered(1))


def _sigmoid(x):
    return 1.0 / (1.0 + jnp.exp(-x))


def _pack_rows(x):
    n = x.shape[-1] // 2
    xb = x.astype(BF16).astype(F32)
    lo = pltpu.bitcast(xb[:, :n], jnp.uint32) >> 16
    hi = pltpu.bitcast(xb[:, n:], jnp.uint32) & jnp.uint32(0xFFFF0000)
    return lo | hi


def _unpack_rows(w):
    lo = pltpu.bitcast(w << 16, F32)
    hi = pltpu.bitcast(w & jnp.uint32(0xFFFF0000), F32)
    return lo, hi


def _swap_halves(x):
    return pltpu.roll(x, LANES // 2, 1)


def _proj_kernel(x_ref, w_ref, o_ref):
    xb = x_ref[...].astype(BF16)
    for c in range(N_PROJ // LATENT_W):
        sl = slice(c * LATENT_W, (c + 1) * LATENT_W)
        o_ref[:, sl] = jnp.dot(xb, w_ref[:, sl], preferred_element_type=F32).astype(o_ref.dtype)


def _input_projection(x2, w_cat):
    T = x2.shape[0]
    return pl.pallas_call(
        _proj_kernel,
        out_shape=jax.ShapeDtypeStruct((T, N_PROJ), BF16),
        grid=(T // ROW_TILE,),
        in_specs=[pl.BlockSpec((ROW_TILE, D_MODEL), lambda i: (i, 0)),
                  _const_spec((D_MODEL, N_PROJ))],
        out_specs=pl.BlockSpec((ROW_TILE, N_PROJ), lambda i: (i, 0)),
        compiler_params=_cparams("parallel"),
        name="input_projection",
    )(x2, w_cat)


def _rope_table_kernel(pos_ref, freq_ref, rc_ref, rs_ref, mc_ref, ms_ref):
    ang = pos_ref[...].astype(F32) * freq_ref[...]
    c = jnp.cos(ang)
    s = jnp.sin(ang)
    lane = lax.broadcasted_iota(jnp.int32, ang.shape, 1)
    c64 = pltpu.roll(c, 64, 1)
    s64 = pltpu.roll(s, 64, 1)
    rc_ref[...] = jnp.where(lane < 64, c, c64)
    rs_ref[...] = jnp.where(lane < 64, -s, s64)
    c96 = pltpu.roll(c, 96, 1)
    s96 = pltpu.roll(s, 96, 1)
    mc_ref[...] = jnp.where(lane < 32, c64, jnp.where(lane < 64, c96, 0.0))
    ms_ref[...] = jnp.where(lane < 32, s64, jnp.where(lane < 64, s96, 0.0))


def _rope_tables(pos_col, freq_row):
    T = pos_col.shape[0]
    rows = min(2048, T)
    tab =jax.ShapeDtypeStruct((T, LANES), F32)
    spec = pl.BlockSpec((rows, LANES), lambda i: (i, 0))
    return pl.pallas_call(
        _rope_table_kernel,
        out_shape=(tab, tab, tab, tab),
        grid=(T // rows,),
        in_specs=[pl.BlockSpec((rows, 1), lambda i: (i, 0)), _const_spec((1, LANES))],
        out_specs=(spec, spec, spec, spec),
        compiler_params=_cparams("parallel"),
        name="rope_tables",
    )(pos_col, freq_row)


def _mla_prep_kernel(a_ref, mc_ref, ms_ref, wq_ref, wk_ref, wvt_ref, qg_ref, kvg_ref,
                     q_ref, kn_ref, vt_ref, kr_ref):
    a = a_ref[...].astype(F32)
    mc = mc_ref[...]
    ms = ms_ref[...]
    cq = a[:, :Q_LORA]
    qn = cq * lax.rsqrt(jnp.mean(cq * cq, axis=-1, keepdims=True) + RMS_EPS) * qg_ref[...]
    q = jnp.dot(qn.astype(BF16), wq_ref[...], preferred_element_type=F32)
    scale = (MLA_NOPE + MLA_ROPE) ** -0.5 * LOG2_E
    for h in range(MLA_HEADS):
        nope = q[:, h * MLA_QK:h * MLA_QK + LANES]
        blk = q[:, h * MLA_QK + LANES:(h + 1) * MLA_QK]
        rope = blk * mc + _swap_halves(blk) * ms
        q_ref[:, h * MLA_QK:h * MLA_QK + LANES] = (nope * scale).astype(BF16)
        q_ref[:, h * MLA_QK + LANES:(h + 1) * MLA_QK] = (rope * scale).astype(BF16)
    ckv = a[:, Q_LORA:Q_LORA + KV_LORA]
    kvn = ckv * lax.rsqrt(jnp.mean(ckv * ckv, axis=-1, keepdims=True) + RMS_EPS) * kvg_ref[...]
    kvb = kvn.astype(BF16)
    kn_ref[...] = jnp.dot(kvb, wk_ref[...], preferred_element_type=F32).astype(BF16)
    vt = lax.dot_general(wvt_ref[...], kvb, (((1,), (1,)), ((), ())),
                         preferred_element_type=F32).astype(BF16)
    ones = jnp.ones((VT_ROWS - MLA_V, vt.shape[1]), BF16)
    for h in range(MLA_HEADS):
        vt_ref[0, h * VT_ROWS:h * VT_ROWS + MLA_V, :] = vt[h * MLA_V:(h + 1) * MLA_V, :]
        vt_ref[0, h * VT_ROWS + MLA_V:(h + 1) * VT_ROWS, :] = ones
    krb =a[:, Q_LORA + KV_LORA:]
    kr_ref[...] = (krb * mc + _swap_halves(krb) * ms).astype(BF16)


def _mla_prep(proj, mc, ms, wq, wk, wvt, qg, kvg):
    T = proj.shape[0]
    row = lambda w: pl.BlockSpec((ROW_TILE, w), lambda i: (i, 0))
    return pl.pallas_call(
        _mla_prep_kernel,
        out_shape=(jax.ShapeDtypeStruct((T, MLA_HEADS * MLA_QK), BF16),
                   jax.ShapeDtypeStruct((T, MLA_HEADS * MLA_NOPE), BF16),
                   jax.ShapeDtypeStruct((T // ROW_TILE, MLA_HEADS * VT_ROWS, ROW_TILE), BF16),
                   jax.ShapeDtypeStruct((T, LANES), BF16)),
        grid=(T // ROW_TILE,),
        in_specs=[pl.BlockSpec((ROW_TILE, LATENT_W), lambda i: (i, N_WIDE // LATENT_W)),
                  row(LANES), row(LANES),
                  _const_spec(wq.shape), _const_spec(wk.shape), _const_spec(wvt.shape),
                  _const_spec(qg.shape), _const_spec(kvg.shape)],
        out_specs=(row(MLA_HEADS * MLA_QK), row(MLA_HEADS * MLA_NOPE),
                   pl.BlockSpec((1, MLA_HEADS * VT_ROWS, ROW_TILE), lambda i: (i, 0, 0)), row(LANES)),
        compiler_params=_cparams("parallel"),
        name="mla_prep",
    )(proj, mc, ms, wq, wk, wvt, qg, kvg)


def _attn_kernel(q_ref, kn_ref, kr_ref, vt_ref, o_ref, acc_scr):
    qi = pl.program_id(2)
    W = ATTN_Q_TILE // ATTN_Q_SPLIT
    n_diag = ATTN_Q_TILE // ATTN_TILE
    groups = tuple(range(ATTN_Q_SPLIT))
    qs = [q_ref[c * W:(c + 1) * W, :] for c in groups]
    acc_scr[...] = jnp.zeros(acc_scr.shape, F32)

    def visible(d, c):
        return min(max((c + 1) * W - d * ATTN_TILE, 0), ATTN_TILE)

    def scores(j, which=groups):
        rows = pl.ds(pl.multiple_of(j * ATTN_TILE, ATTN_TILE), ATTN_TILE)
        k = jnp.concatenate([kn_ref[rows, :], kr_ref[rows, :]], axis=-1)
        return {c: lax.dot_general(k, qs[c], (((1,), (1,)), ((), ())), preferred_element_type=F32)
                for c in which}

    def softmax_pv(j, s_all, m_all, diag=None):
        vt = vt_ref[j]
        out = list(m_all)
        for c in s_all:
            s = s_all[c]
            nk = ATTN_TILE if diag is None else visible(diag, c)
            if diag is not None:
                s = s[:nk]
                if diag * ATTN_TILE + nk - 1 > c * W:
                    kp = lax.broadcasted_iota(jnp.int32, s.shape, 0) + diag * ATTN_TILE
                    qp = lax.broadcasted_iota(jnp.int32, s.shape, 1) + c * W
                    s = jnp.where(kp <= qp, s, NEG_INF)
            m_new = jnp.maximum(m_all[c], jnp.max(s, axis=0, keepdims=True))
            alpha = jnp.exp2(m_all[c] - m_new)
            p = jnp.exp2(s - m_new).astype(BF16)
            acc_scr[c] = alpha * acc_scr[c] + jnp.dot(vt[:, :nk], p, preferred_element_type=F32)
            out[c] = m_new
        return tuple(out)

    def body(jj, m_all):
        tiles = [jj * n_diag + u for u in range(n_diag)]
        s_tiles = [scores(t) for t in tiles]
        for t, s_all in zip(tiles, s_tiles):
            m_all = softmax_pv(t, s_all, m_all)
        return m_all

    first = n_diag * qi
    init = tuple(jnp.full((1, W), NEG_INF, F32) for _ in groups)
    m_all = lax.fori_loop(0, qi, body, init)
    s_diag = [scores(first + d, tuple(c for c in groups if visible(d, c) > 0)) for d in range(n_diag)]
    for d in range(n_diag):
        m_all = softmax_pv(first + d, s_diag[d], m_all, diag=d)
    for c in groups:
        acc = acc_scr[c]
        o_ref[c * W:(c + 1) * W, :] = (acc[:MLA_V] / acc[MLA_V:MLA_V + 1]).T.astype(o_ref.dtype)


def _attention(q_cat, kn, kr, vt, B, S):
    T = B * S
    nq = S // ATTN_Q_TILE
    return pl.pallas_call(
        _attn_kernel,
        out_shape=jax.ShapeDtypeStruct((T, MLA_HEADS * MLA_V), BF16),
        grid=(B, MLA_HEADS, nq),
        in_specs=[pl.BlockSpec((ATTN_Q_TILE, MLA_QK), lambda b, h, i: (b * nq + i, h)),
                  pl.BlockSpec((S, MLA_NOPE), lambda b, h, i: (b, h)),
                  pl.BlockSpec((S, LANES), lambda b, h, i: (b, 0)),
                  pl.BlockSpec((S // ATTN_TILE, VT_ROWS, ATTN_TILE), lambda b, h, i: (b, h, 0))],
        out_specs=pl.BlockSpec((ATTN_Q_TILE, MLA_V), lambda b, h, i: (b * nq + i, h)),
        scratch_shapes=[pltpu.VMEM((ATTN_Q_SPLIT, VT_ROWS, ATTN_Q_TILE // ATTN_Q_SPLIT), F32)],
        compiler_params=_cparams("parallel", "parallel", "arbitrary"),
        name="mla_attention",
    )(q_cat, kn, kr, vt)


def _ret_kernel(q_ref, k_ref, v_ref, g_ref, rc_ref, rs_ref, dec_ref, qw_ref, kw_ref, cd_ref,
                o_ref, qb_scr, qwb_scr, kb_scr, kwb_scr):
    S = q_ref.shape[0]
    C = RET_CHUNK
    n_chunks = S // C
    rc = rc_ref[...]
    rs = rs_ref[...]
    q = q_ref[...].astype(F32)
    k = k_ref[...].astype(F32)
    q = q * rc + _swap_halves(q) * rs
    k = (k * rc + _swap_halves(k) * rs) * (RET_DK ** -0.5)
    qw = qw_ref[0]
    kw = kw_ref[0]
    q3 = q.reshape(n_chunks, C, RET_DK)
    k3 = k.reshape(n_chunks, C, RET_DK)
    qb_scr[...] = q.astype(BF16)
    kb_scr[...] = k.astype(BF16)
    qwb_scr[...] = (q3 * qw[None]).reshape(S, RET_DK).astype(BF16)
    kwb_scr[...] = (k3 * kw[None]).reshape(S, RET_DK)
    dec = dec_ref[0]
    cd = cd_ref[0]

    def group(gi, R):
        for u in range(RET_GROUP):
            rows = pl.ds(pl.multiple_of((gi * RET_GROUP + u) * C, C), C)
            qn = qb_scr[rows, :]
            kn = kb_scr[rows, :]
            vn = v_ref[rows, :]
            s = lax.dot_general(qn, kn, (((1,), (1,)), ((), ())), preferred_element_type=F32) * dec
            o = jnp.dot(s.astype(BF16), vn, preferred_element_type=F32)
            o = o + jnp.dot(qwb_scr[rows, :], R.astype(BF16), preferred_element_type=F32)
            kwt = kwb_scr[rows, :].T.astype(BF16)
            R = cd * R + jnp.dot(kwt, vn, preferred_element_type=F32)
            mu = jnp.mean(o, axis=-1, keepdims=True)
            d = o - mu
            var = jnp.mean(d * d, axis=-1, keepdims=True)
            gate = g_ref[rows, :].astype(F32)
            o_ref[rows, :] = (d * lax.rsqrt(var + GN_EPS) * (gate * _sigmoid(gate))).astype(o_ref.dtype)
        return R

    lax.fori_loop(0, n_chunks // RET_GROUP, group, jnp.zeros((RET_DK, RET_DV), F32))


def _retention(proj, rc, rs, dec, qw, kw, cd, B, S):
    T = B * S
    H = RET_HEADS
    col = lambda off: pl.BlockSpec((S, LANES), lambda b, h: (b, off + h))
    tab = pl.BlockSpec((S, LANES), lambda b, h: (b, 0))
    per_head = lambda r: pl.BlockSpec((1, r, LANES), lambda b, h: (h, 0, 0))
    return pl.pallas_call(
        _ret_kernel,
        out_shape=jax.ShapeDtypeStruct((T, H * RET_DV), BF16),
        grid=(B, H),
        in_specs=[col(0), col(H), col(2 * H), col(3 * H), tab, tab,
                  per_head(RET_CHUNK), per_head(RET_CHUNK), per_head(RET_CHUNK), per_head(1)],
        out_specs=pl.BlockSpec((S, RET_DV), lambda b, h: (b, h)),
        scratch_shapes=[pltpu.VMEM((S, RET_DK), BF16), pltpu.VMEM((S, RET_DK), BF16),
                        pltpu.VMEM((S, RET_DK), BF16), pltpu.VMEM((S, RET_DK), F32)],
        compiler_params=_cparams("parallel", "parallel"),
        name="retention",
    )(proj, proj, proj, proj, rc, rs, dec, qw, kw, cd)


def _retention_constants():
    C = RET_CHUNK
    h = np.arange(RET_HEADS, dtype=np.float64)
    log_gamma = np.log(1.0 - 2.0 ** (-5.0 - h))
    idx = np.arange(C, dtype=np.float64)
    rel = idx[:, None] - idx[None, :]
    dec = np.where(rel[None] >= 0, np.exp(np.maximum(rel, 0.0)[None] * log_gamma[:, None, None]), 0.0)
    qw = np.exp((idx + 1.0)[None, :] * log_gamma[:, None])
    kw = np.exp((C - 1.0 - idx)[None, :] * log_gamma[:, None])
    cd = np.exp(C * log_gamma)
    bc = lambda a: np.ascontiguousarray(np.broadcast_to(a[..., None], a.shape + (LANES,))).astype(np.float32)
    return dec.astype(np.float32), bc(qw), bc(kw), bc(cd[:, None])


def _layernorm(y, g, b):
    mu = jnp.mean(y, axis=-1, keepdims=True)
    d = y - mu
    var = jnp.mean(d * d, axis=-1, keepdims=True)
    return d * lax.rsqrt(var + LN_EPS) * g + b


def _merge_kernel(x_ref, om_ref, or_ref, gm_ref, gr_ref, wo_ref, g_ref, b_ref, h_ref, hp_ref):
    mixed = (_sigmoid(gm_ref[...].astype(F32)) * om_ref[...].astype(F32)
             + _sigmoid(gr_ref[...].astype(F32)) * or_ref[...].astype(F32))
    mix = jnp.dot(mixed.astype(BF16), wo_ref[...], preferred_element_type=F32)
    h = _layernorm(DN_ALPHA * x_ref[...] + mix, g_ref[...], b_ref[...])
    h_ref[...] = h
    hp_ref[...] = _pack_rows(h)


def _merge(x2, o_mla, o_ret, proj, wo, g, b):
    T = x2.shape[0]
    row = lambda c: pl.BlockSpec((ROW_TILE, D_MODEL), lambda i: (i, c))
    return pl.pallas_call(
        _merge_kernel,
        out_shape=(jax.ShapeDtypeStruct((T, D_MODEL), F32), jax.ShapeDtypeStruct((T, PACKED_W), jnp.uint32)),
        grid=(T // ROW_TILE,),
        in_specs=[row(0), row(0), row(0), row(4), row(5),
                  _const_spec(wo.shape), _const_spec(g.shape), _const_spec(b.shape)],
        out_specs=(row(0), pl.BlockSpec((ROW_TILE, PACKED_W), lambda i: (i, 0))),
        compiler_params=_cparams("parallel"),
        name="merge_out_proj_ln",
    )(x2, o_mla, o_ret, proj, proj, wo, g, b)


def _split3(a):
    a0 = a.astype(BF16)
    r1 = a - a0.astype(F32)
    a1 = r1.astype(BF16)
    a2 = (r1 - a1.astype(F32)).astype(BF16)
    return a0, a1, a2


def _router_kernel(h_ref, w0_ref, w1_ref, w2_ref, b_ref, idx_ref, gate_ref, cnt_ref, run_scr):
    i = pl.program_id(0)

    @pl.when(i == 0)
    def _():
        run_scr[...] = jnp.zeros(run_scr.shape, F32)

    h0, h1, h2 = _split3(h_ref[...])
    w0, w1, w2 = w0_ref[...], w1_ref[...], w2_ref[...]
    dot = lambda a, w: jnp.dot(a, w, preferred_element_type=F32)
    logits = (dot(h0, w0) + (dot(h0, w1) + dot(h1, w0))
              + (dot(h0, w2) + dot(h1, w1) + dot(h2, w0))) + b_ref[...]
    rows = logits.shape[0]
    lane = lax.broadcasted_iota(jnp.int32, (rows, LANES), 1)
    lane_f = lane.astype(F32)
    work = jnp.where(lane < N_EXPERTS, logits, -jnp.inf)
    vals, ids, sels = [], [], []
    for _ in range(TOP_K):
        m = jnp.max(work, axis=-1, keepdims=True)
        first = jnp.min(jnp.where(work == m, lane_f, float(LANES)), axis=-1, keepdims=True)
        sel = lane_f == first
        work = jnp.where(sel, -jnp.inf, work)
        vals.append(m)
        ids.append(first)
        sels.append(sel)
    exps = [jnp.exp(v - vals[0]) for v in vals]
    denom = exps[0] + exps[1] + exps[2] + exps[3]
    chosen = jnp.zeros((rows, LANES), F32)
    for sel in sels:
        chosen = jnp.where(sel, 1.0, chosen)
    r = lax.broadcasted_iota(jnp.int32, (rows, rows), 0)
    c = lax.broadcasted_iota(jnp.int32, (rows, rows), 1)
    lower = jnp.where(r > c, 1.0, 0.0).astype(BF16)
    before = jnp.dot(lower, chosen.astype(BF16), preferred_element_type=F32) + run_scr[...]
    idx_out = jnp.zeros((rows, LANES), jnp.int32)
    gate_out = jnp.zeros((rows, LANES), F32)
    for kk in range(TOP_K):
        rank = jnp.sum(jnp.where(sels[kk], before, 0.0), axis=-1, keepdims=True).astype(jnp.int32)
        idx_out = jnp.where(lane == kk, ids[kk].astype(jnp.int32), idx_out)
        idx_out = jnp.where(lane == TOP_K + kk, rank, idx_out)
        gate_out = jnp.where(lane == kk, exps[kk] / denom, gate_out)
    idx_ref[...] = idx_out
    gate_ref[...] = gate_out
    total = run_scr[...] + jnp.sum(chosen, axis=0, keepdims=True)
    run_scr[...] = total
    cnt_ref[...] = total.astype(jnp.int32)


def _router(h, w_router, b_router):
    T = h.shape[0]
    wpad = jnp.zeros((D_MODEL, LANES), F32).at[:, :N_EXPERTS].set(w_router)
    w0 = wpad.astype(BF16)
    r1 = wpad - w0.astype(F32)
    w1 = r1.astype(BF16)
    w2 = (r1 - w1.astype(F32)).astype(BF16)
    bpad = jnp.zeros((1, LANES), F32).at[0, :N_EXPERTS].set(b_router)
    row = pl.BlockSpec((ROW_TILE, LANES), lambda i: (i, 0))
    return pl.pallas_call(
        _router_kernel,
        out_shape=(jax.ShapeDtypeStruct((T, LANES), jnp.int32),
                   jax.ShapeDtypeStruct((T, LANES), F32),
                   jax.ShapeDtypeStruct((1, LANES), jnp.int32)),
        grid=(T // ROW_TILE,),
        in_specs=[pl.BlockSpec((ROW_TILE, D_MODEL), lambda i: (i, 0)),
                  _const_spec(w0.shape), _const_spec(w0.shape), _const_spec(w0.shape),
                  _const_spec(bpad.shape)],
        out_specs=(row, row, pl.BlockSpec((1, LANES), lambda i: (0, 0))),
        scratch_shapes=[pltpu.VMEM((1, LANES), F32)],
        compiler_params=_cparams("arbitrary"),
        name="router",
    )(h, w0, w1, w2, bpad)


def _row_copy(src, src_row, dst, dst_row, sem):
    return pltpu.make_async_copy(src.at[pl.ds(src_row, 1), :], dst.at[pl.ds(dst_row, 1), :], sem)


def _dispatch_kernel(dest_ref, h_ref, xs_in_ref, xs_ref, sem):
    del xs_in_ref
    base = pl.program_id(0) * (MOVE_TILE * TOP_K)

    def issue(g, carry):
        r0 = pl.multiple_of(g * SUBLANES, SUBLANES)
        for u in range(SUBLANES):
            for kk in range(TOP_K):
                d = dest_ref[base + (r0 + u) * TOP_K + kk]
                _row_copy(h_ref, r0 + u, xs_ref, d, sem).start(priority=kk % 2)
        return carry

    lax.fori_loop(0, MOVE_TILE // SUBLANES, issue, 0)
    for _ in range(TOP_K):
        pltpu.make_async_copy(h_ref, xs_ref.at[pl.ds(0, MOVE_TILE), :], sem).wait()


def _dispatch(dest_flat, hp, n_rows):
    T = hp.shape[0]
    grid_spec = pltpu.PrefetchScalarGridSpec(
        num_scalar_prefetch=1,
        grid=(T // MOVE_TILE,),
        in_specs=[pl.BlockSpec((MOVE_TILE, PACKED_W), lambda i, d: (i, 0)),
                  pl.BlockSpec(memory_space=pl.ANY)],
        out_specs=pl.BlockSpec(memory_space=pl.ANY),
        scratch_shapes=[pltpu.SemaphoreType.DMA],
    )
    return pl.pallas_call(
        _dispatch_kernel,
        out_shape=jax.ShapeDtypeStruct((n_rows, PACKED_W), jnp.uint32),
        grid_spec=grid_spec,
        input_output_aliases={2: 0},
        compiler_params=_cparams("arbitrary"),
        name="dispatch_rows",
    )(dest_flat, hp, jnp.zeros((n_rows, PACKED_W), jnp.uint32))


def _expert_kernel(blk_e_ref, nact_ref, next_e_ref, x_ref, bu_ref, bd_ref, wu_hbm, wd_hbm, y_ref,
                   wuf_scr, wdf_scr, wub_scr, wdb_scr, sems):
    i = pl.program_id(0)
    e = blk_e_ref[i]

    def weight_copies(expert):
        return (pltpu.make_async_copy(wu_hbm.at[expert], wuf_scr, sems.at[0]),
                pltpu.make_async_copy(wd_hbm.at[expert], wdf_scr, sems.at[1]))

    @pl.when(i == 0)
    def _():
        for cp in weight_copies(e):
            cp.start()

    first_of_group = jnp.logical_or(i == 0, e != blk_e_ref[jnp.maximum(i - 1, 0)])

    @pl.when(jnp.logical_and(i < nact_ref[0], first_of_group))
    def _():
        for cp in weight_copies(e):
            cp.wait()
        wub_scr[...] = wuf_scr[...].astype(BF16)
        wdb_scr[...] = wdf_scr[...].astype(BF16)

        @pl.when(next_e_ref[i] >= 0)
        def _():
            for cp in weight_copies(next_e_ref[i]):
                cp.start()

    @pl.when(i < nact_ref[0])
    def _():
        lo, hi = _unpack_rows(x_ref[...])
        x = jnp.concatenate([lo.astype(BF16), hi.astype(BF16)], axis=-1)
        acc = jnp.zeros((EXPERT_TILE, D_MODEL), F32)
        step = 256
        for c in range(D_EXPERT // step):
            glu = jnp.dot(x, wub_scr[:, c * step:(c + 1) * step], preferred_element_type=F32)
            lin = jnp.dot(x, wub_scr[:, D_EXPERT + c * step:D_EXPERT + (c + 1) * step],
                          preferred_element_type=F32)
            glu = jnp.minimum(glu + bu_ref[0, :, c * step:(c + 1) * step], SWIGLU_LIMIT)
            lin = jnp.clip(lin + bu_ref[0, :, D_EXPERT + c * step:D_EXPERT + (c + 1) * step],
                           -SWIGLU_LIMIT, SWIGLU_LIMIT)
            a = glu * _sigmoid(SWIGLU_ALPHA * glu) * (lin + 1.0)
            acc = acc + jnp.dot(a.astype(BF16), wdb_scr[c * step:(c + 1) * step, :],
                                preferred_element_type=F32)
        y_ref[...] = _pack_rows(acc + bd_ref[0])

    @pl.when(i >= nact_ref[0])
    def _():
        y_ref[...] = jnp.zeros(y_ref.shape, y_ref.dtype)


def _experts(blk_e, nact, next_e, xs, w_up, b_up, w_down, b_down):
    P = xs.shape[0]
    n_blocks = P // EXPERT_TILE
    act = lambda i, na: jnp.minimum(i, na[0] - 1)
    grid_spec = pltpu.PrefetchScalarGridSpec(
        num_scalar_prefetch=3,
        grid=(n_blocks,),
        in_specs=[pl.BlockSpec((EXPERT_TILE, PACKED_W), lambda i, be, na, ne: (act(i, na), 0)),
                  pl.BlockSpec((1, 1, 2 * D_EXPERT), lambda i, be, na, ne: (be[act(i, na)], 0, 0)),
                  pl.BlockSpec((1, 1, D_MODEL), lambda i, be, na, ne: (be[act(i, na)], 0, 0)),
                  pl.BlockSpec(memory_space=pl.ANY),
                  pl.BlockSpec(memory_space=pl.ANY)],
        out_specs=pl.BlockSpec((EXPERT_TILE, PACKED_W), lambda i, be, na, ne: (i, 0)),
        scratch_shapes=[pltpu.VMEM((D_MODEL, 2 * D_EXPERT), F32),
                        pltpu.VMEM((D_EXPERT, D_MODEL), F32),
                        pltpu.VMEM((D_MODEL, 2 * D_EXPERT), BF16),
                        pltpu.VMEM((D_EXPERT, D_MODEL), BF16),
                        pltpu.SemaphoreType.DMA((2,))],
    )
    return pl.pallas_call(
        _expert_kernel,
        out_shape=jax.ShapeDtypeStruct((P, PACKED_W), jnp.uint32),
        grid_spec=grid_spec,
        compiler_params=_cparams("arbitrary"),
        name="routed_experts",
    )(blk_e, nact, next_e, xs, b_up.reshape(N_EXPERTS, 1, 2 * D_EXPERT),
      b_down.reshape(N_EXPERTS, 1, D_MODEL), w_up, w_down)


def _combine_kernel(dest_ref, h_ref, gate_ref, g_ref, b_ref, ys_ref, o_ref, ybuf, sems):
    i = pl.program_id(0)
    n = pl.num_programs(0)

    def gather(step, slot):
        base = step * (MOVE_TILE * TOP_K)

        def issue(g, carry):
            r0 = pl.multiple_of(g * SUBLANES, SUBLANES)
            for u in range(SUBLANES):
                for kk in range(TOP_K):
                    d = dest_ref[base + (r0 + u) * TOP_K + kk]
                    _row_copy(ys_ref, d, ybuf.at[slot, kk], r0 + u, sems.at[slot]).start(priority=kk % 2)
            return carry

        lax.fori_loop(0, MOVE_TILE // SUBLANES, issue, 0)

    @pl.when(i == 0)
    def _():
        gather(0, 0)

    @pl.when(i + 1 < n)
    def _():
        gather(i + 1, (i + 1) % 2)

    slot = i % 2
    for kk in range(TOP_K):
        pltpu.make_async_copy(ys_ref.at[pl.ds(0, MOVE_TILE), :], ybuf.at[slot, kk], sems.at[slot]).wait()

    gate = gate_ref[...]
    f_lo = jnp.zeros((MOVE_TILE, PACKED_W), F32)
    f_hi = jnp.zeros((MOVE_TILE, PACKED_W), F32)
    for kk in range(TOP_K):
        lo, hi = _unpack_rows(ybuf[slot, kk])
        f_lo = f_lo + gate[:, kk:kk + 1] * lo
        f_hi = f_hi + gate[:, kk:kk + 1] * hi
    y_lo = DN_ALPHA * h_ref[:, :PACKED_W] + f_lo
    y_hi = DN_ALPHA * h_ref[:, PACKED_W:] + f_hi
    mu = (jnp.sum(y_lo, axis=-1, keepdims=True) + jnp.sum(y_hi, axis=-1, keepdims=True)) / D_MODEL
    d_lo = y_lo - mu
    d_hi = y_hi - mu
    var = (jnp.sum(d_lo * d_lo, axis=-1, keepdims=True) + jnp.sum(d_hi * d_hi, axis=-1, keepdims=True)) / D_MODEL
    inv = lax.rsqrt(var + LN_EPS)
    o_ref[:, :PACKED_W] = d_lo * inv * g_ref[:, :PACKED_W] + b_ref[:, :PACKED_W]
    o_ref[:, PACKED_W:] = d_hi * inv * g_ref[:, PACKED_W:] + b_ref[:, PACKED_W:]


def _combine(dest_flat, h, ys, gate, g, b):
    T = h.shape[0]
    grid_spec = pltpu.PrefetchScalarGridSpec(
        num_scalar_prefetch=1,
        grid=(T // MOVE_TILE,),
        in_specs=[pl.BlockSpec((MOVE_TILE, D_MODEL), lambda i, d: (i, 0)),
                  pl.BlockSpec((MOVE_TILE, LANES), lambda i, d: (i, 0)),
                  pl.BlockSpec(g.shape, lambda i, d: (0, 0)),
                  pl.BlockSpec(b.shape, lambda i, d: (0, 0)),
                  pl.BlockSpec(memory_space=pl.ANY)],
        out_specs=pl.BlockSpec((MOVE_TILE, D_MODEL), lambda i, d: (i, 0)),
        scratch_shapes=[pltpu.VMEM((2, TOP_K, MOVE_TILE, PACKED_W), jnp.uint32),
                        pltpu.SemaphoreType.DMA((2,))],
    )
    return pl.pallas_call(
        _combine_kernel,
        out_shape=jax.ShapeDtypeStruct((T, D_MODEL), F32),
        grid_spec=grid_spec,
        compiler_params=_cparams("arbitrary"),
        name="combine_ln",
    )(dest_flat, h, gate, g, b, ys)


def _rot_cols(w):
    half = w.shape[-1] // 2
    return jnp.concatenate([-w[:, half:], w[:, :half]], axis=-1)


def _pack_w_in(w_in):
    lat = Q_LORA + KV_LORA
    w_kr = w_in[:, lat:lat + MLA_ROPE]
    return jnp.concatenate([w_in[:, lat + MLA_ROPE:], w_in[:, :lat + MLA_ROPE], _rot_cols(w_kr)],
                           axis=-1).astype(BF16)


def _pack_w_uq(w_uq):
    w = w_uq.reshape(Q_LORA, MLA_HEADS, MLA_NOPE + MLA_ROPE)
    rope = w[:, :, MLA_NOPE:]
    rot = jnp.concatenate([-rope[:, :, MLA_ROPE // 2:], rope[:, :, :MLA_ROPE // 2]], axis=-1)
    return jnp.concatenate([w, rot], axis=-1).reshape(Q_LORA, MLA_HEADS * MLA_QK).astype(BF16)


def _rope_freqs():
    ret = 1.0 / (10000.0 ** jnp.linspace(0.0, 1.0, RET_DK // 2, dtype=F32))
    mla = 1.0 / (ROPE_BASE ** (jnp.arange(0, MLA_ROPE, 2, dtype=F32) / MLA_ROPE))
    return jnp.concatenate([ret, mla, jnp.zeros((LANES - ret.shape[0] - mla.shape[0],), F32)])[None, :]


def _routing_plan(idx_out, cnt):
    top_i = idx_out[:, :TOP_K]
    rank = idx_out[:, TOP_K:2 * TOP_K]
    counts = cnt[0, :N_EXPERTS]
    padded = ((counts + EXPERT_TILE - 1) // EXPERT_TILE) * EXPERT_TILE
    pad_end = jnp.cumsum(padded)
    pad_start = pad_end - padded
    experts = jnp.arange(N_EXPERTS, dtype=jnp.int32)
    dest = rank + jnp.sum(jnp.where(top_i[..., None] == experts, pad_start, 0), axis=-1)
    n_rows = top_i.shape[0] * TOP_K + N_EXPERTS * EXPERT_TILE
    n_blocks = n_rows // EXPERT_TILE
    blk_row = jnp.arange(n_blocks, dtype=jnp.int32) * EXPERT_TILE
    blk_e = jnp.minimum(jnp.sum((pad_end[None, :] <= blk_row[:, None]).astype(jnp.int32), axis=-1),
                        N_EXPERTS - 1)
    nact = (pad_end[-1:] // EXPERT_TILE).astype(jnp.int32)
    later = jnp.where((counts > 0)[None, :] & (experts[None, :] > experts[:, None]), experts[None, :], N_EXPERTS)
    nxt = jnp.min(later, axis=-1)
    nxt = jnp.where(nxt == N_EXPERTS, -1, nxt)
    next_e = jnp.sum(jnp.where(blk_e[:, None] == experts[None, :], nxt[None, :], 0), axis=-1).astype(jnp.int32)
    return dest.astype(jnp.int32), blk_e, nact, next_e, n_rows


def kernel(x, positions, w_in, q_norm_g, w_uq, kv_norm_g, w_ukv, w_o, ln1_g, ln1_b,
           w_router, b_router, w_up, b_up, w_down, b_down, ln2_g, ln2_b):
    B, S, _ = x.shape
    T = B * S
    pos_col = positions.reshape(T, 1).astype(jnp.int32)
    rc, rs, mc, ms = _rope_tables(pos_col, _rope_freqs())
    dec, qw, kw, cd = _retention_constants()
    h = x.reshape(T, D_MODEL)
    for l in range(DEPTH):
        proj = _input_projection(h, _pack_w_in(w_in[l]))
        w_kv = w_ukv[l].reshape(KV_LORA, MLA_HEADS, MLA_NOPE + MLA_V)
        w_k = w_kv[:, :, :MLA_NOPE].reshape(KV_LORA, MLA_HEADS * MLA_NOPE).astype(BF16)
        w_vt = w_kv[:, :, MLA_NOPE:].reshape(KV_LORA, MLA_HEADS * MLA_V).T.astype(BF16)
        q_cat, kn, vt, kr = _mla_prep(proj, mc, ms, _pack_w_uq(w_uq[l]), w_k, w_vt,
                                      q_norm_g[l][None, :], kv_norm_g[l][None, :])
        o_mla = _attention(q_cat, kn, kr, vt, B, S)
        o_ret = _retention(proj, rc, rs, dec, qw, kw, cd, B, S)
        h1, h1p = _merge(h, o_mla, o_ret, proj, w_o[l].astype(BF16), ln1_g[l][None, :], ln1_b[l][None, :])
        idx_out, gate, cnt = _router(h1, w_router[l], b_router[l])
        dest, blk_e, nact, next_e, n_rows = _routing_plan(idx_out, cnt)
        dest_flat = dest.reshape(-1)
        xs = _dispatch(dest_flat, h1p, n_rows)
        ys = _experts(blk_e, nact, next_e, xs, w_up[l], b_up[l], w_down[l], b_down[l])
        h = _combine(dest_flat, h1, ys, gate, ln2_g[l][None, :], ln2_b[l][None, :])
    return h.reshape(B, S, D_MODEL)
```

```python
import functools

import numpy as np
import jax
import jax.numpy as jnp
from jax import lax
from jax.experimental import pallas as pl
from jax.experimental.pallas import tpu as pltpu

F32 = jnp.float32
BF16 = jnp.bfloat16

D_MODEL = 1024
DEPTH = 1
MLA_HEADS = 8
MLA_NOPE = 128
MLA_ROPE = 64
MLA_V = 128
Q_LORA = 384
KV_LORA = 256
ROPE_BASE = 10000.0
RET_HEADS = 8
RET_DK = 128
RET_DV = 128
RET_CHUNK = 128
N_EXPERTS = 32
TOP_K = 4
D_EXPERT = 1024
SWIGLU_LIMIT = 7.0
SWIGLU_ALPHA = 1.702
DN_ALPHA = (2.0 * DEPTH) ** 0.25
LN_EPS = 1e-5
RMS_EPS = 1e-6
GN_EPS = 1e-6
NEG_INF = -1e30

LANES = 128
SUBLANES = 8
MLA_QK = 2 * LANES
PACKED_W = D_MODEL // 2
LATENT_W = 768
N_WIDE = 6 * D_MODEL
N_PROJ = N_WIDE + LATENT_W
VMEM_LIMIT = 56 * 1024 * 1024

ROW_TILE = 512
ATTN_TILE = ROW_TILE
ATTN_Q_TILE = 2 * ATTN_TILE
ATTN_Q_SPLIT = 4
VT_ROWS = MLA_V + 16
LOG2_E = 1.4426950408889634
RET_GROUP = 8
EXPERT_TILE = 256
MOVE_TILE = 256


def _cparams(*sem):
    return pltpu.CompilerParams(dimension_semantics=sem, vmem_limit_bytes=VMEM_LIMIT)


def _const_spec(shape):
    nd = len(shape)
    return pl.BlockSpec(shape, lambda *_: (0,) * nd, pipeline_mode=pl.Buffered(1))


def _sigmoid(x):
    return 1.0 / (1.0 + jnp.exp(-x))


def _pack_rows(x):
    n = x.shape[-1] // 2
    xb = x.astype(BF16).astype(F32)
    lo = pltpu.bitcast(xb[:, :n], jnp.uint32) >> 16
    hi = pltpu.bitcast(xb[:, n:], jnp.uint32) & jnp.uint32(0xFFFF0000)
    return lo | hi


def _unpack_rows(w):
    lo = pltpu.bitcast(w << 16, F32)
    hi = pltpu.bitcast(w & jnp.uint32(0xFFFF0000), F32)
    return lo, hi


def _swap_halves(x):
    return pltpu.roll(x, LANES // 2, 1)


def _proj_kernel(x_ref, w_ref, o_ref):
    xb = x_ref[...].astype(BF16)
    for c in range(N_PROJ // LATENT_W):
        sl = slice(c * LATENT_W, (c + 1) * LATENT_W)
        o_ref[:, sl] = jnp.dot(xb, w_ref[:, sl], preferred_element_type=F32).astype(o_ref.dtype)


def _input_projection(x2, w_cat):
    T = x2.shape[0]
    return pl.pallas_call(
        _proj_kernel,
        out_shape=jax.ShapeDtypeStruct((T, N_PROJ), BF16),
        grid=(T // ROW_TILE,),
        in_specs=[pl.BlockSpec((ROW_TILE, D_MODEL), lambda i: (i, 0)),
                  _const_spec((D_MODEL, N_PROJ))],
        out_specs=pl.BlockSpec((ROW_TILE, N_PROJ), lambda i: (i, 0)),
        compiler_params=_cparams("parallel"),
        name="input_projection",
    )(x2, w_cat)


def _rope_table_kernel(pos_ref, freq_ref, rc_ref, rs_ref, mc_ref, ms_ref):
    ang = pos_ref[...].astype(F32) * freq_ref[...]
    c = jnp.cos(ang)
    s = jnp.sin(ang)
    lane = lax.broadcasted_iota(jnp.int32, ang.shape, 1)
    c64 = pltpu.roll(c, 64, 1)
    s64 = pltpu.roll(s, 64, 1)
    rc_ref[...] = jnp.where(lane < 64, c, c64)
    rs_ref[...] = jnp.where(lane < 64, -s, s64)
    c96 = pltpu.roll(c, 96, 1)
    s96 = pltpu.roll(s, 96, 1)
    mc_ref[...] = jnp.where(lane < 32, c64, jnp.where(lane < 64, c96, 0.0))
    ms_ref[...] = jnp.where(lane < 32, s64, jnp.where(lane < 64, s96, 0.0))


def _rope_tables(pos_col, freq_row):
    T = pos_col.shape[0]
    rows = min(2048, T)
    tab =jax.ShapeDtypeStruct((T, LANES), F32)
    spec = pl.BlockSpec((rows, LANES), lambda i: (i, 0))
    return pl.pallas_call(
        _rope_table_kernel,
        out_shape=(tab, tab, tab, tab),
        grid=(T // rows,),
        in_specs=[pl.BlockSpec((rows, 1), lambda i: (i, 0)), _const_spec((1, LANES))],
        out_specs=(spec, spec, spec, spec),
        compiler_params=_cparams("parallel"),
        name="rope_tables",
    )(pos_col, freq_row)


def _mla_prep_kernel(a_ref, mc_ref, ms_ref, wq_ref, wk_ref, wvt_ref, qg_ref, kvg_ref,
                     q_ref, kn_ref, vt_ref, kr_ref):
    a = a_ref[...].astype(F32)
    mc = mc_ref[...]
    ms = ms_ref[...]
    cq = a[:, :Q_LORA]
    qn = cq * lax.rsqrt(jnp.mean(cq * cq, axis=-1, keepdims=True) + RMS_EPS) * qg_ref[...]
    q = jnp.dot(qn.astype(BF16), wq_ref[...], preferred_element_type=F32)
    scale = (MLA_NOPE + MLA_ROPE) ** -0.5 * LOG2_E
    for h in range(MLA_HEADS):
        nope = q[:, h * MLA_QK:h * MLA_QK + LANES]
        blk = q[:, h * MLA_QK + LANES:(h + 1) * MLA_QK]
        rope = blk * mc + _swap_halves(blk) * ms
        q_ref[:, h * MLA_QK:h * MLA_QK + LANES] = (nope * scale).astype(BF16)
        q_ref[:, h * MLA_QK + LANES:(h + 1) * MLA_QK] = (rope * scale).astype(BF16)
    ckv = a[:, Q_LORA:Q_LORA + KV_LORA]
    kvn = ckv * lax.rsqrt(jnp.mean(ckv * ckv, axis=-1, keepdims=True) + RMS_EPS) * kvg_ref[...]
    kvb = kvn.astype(BF16)
    kn_ref[...] = jnp.dot(kvb, wk_ref[...], preferred_element_type=F32).astype(BF16)
    vt = lax.dot_general(wvt_ref[...], kvb, (((1,), (1,)), ((), ())),
                         preferred_element_type=F32).astype(BF16)
    ones = jnp.ones((VT_ROWS - MLA_V, vt.shape[1]), BF16)
    for h in range(MLA_HEADS):
        vt_ref[0, h * VT_ROWS:h * VT_ROWS + MLA_V, :] = vt[h * MLA_V:(h + 1) * MLA_V, :]
        vt_ref[0, h * VT_ROWS + MLA_V:(h + 1) * VT_ROWS, :] = ones
    krb =a[:, Q_LORA + KV_LORA:]
    kr_ref[...] = (krb * mc + _swap_halves(krb) * ms).astype(BF16)


def _mla_prep(proj, mc, ms, wq, wk, wvt, qg, kvg):
    T = proj.shape[0]
    row = lambda w: pl.BlockSpec((ROW_TILE, w), lambda i: (i, 0))
    return pl.pallas_call(
        _mla_prep_kernel,
        out_shape=(jax.ShapeDtypeStruct((T, MLA_HEADS * MLA_QK), BF16),
                   jax.ShapeDtypeStruct((T, MLA_HEADS * MLA_NOPE), BF16),
                   jax.ShapeDtypeStruct((T // ROW_TILE, MLA_HEADS * VT_ROWS, ROW_TILE), BF16),
                   jax.ShapeDtypeStruct((T, LANES), BF16)),
        grid=(T // ROW_TILE,),
        in_specs=[pl.BlockSpec((ROW_TILE, LATENT_W), lambda i: (i, N_WIDE // LATENT_W)),
                  row(LANES), row(LANES),
                  _const_spec(wq.shape), _const_spec(wk.shape), _const_spec(wvt.shape),
                  _const_spec(qg.shape), _const_spec(kvg.shape)],
        out_specs=(row(MLA_HEADS * MLA_QK), row(MLA_HEADS * MLA_NOPE),
                   pl.BlockSpec((1, MLA_HEADS * VT_ROWS, ROW_TILE), lambda i: (i, 0, 0)), row(LANES)),
        compiler_params=_cparams("parallel"),
        name="mla_prep",
    )(proj, mc, ms, wq, wk, wvt, qg, kvg)


def _attn_kernel(q_ref, kn_ref, kr_ref, vt_ref, o_ref, acc_scr):
    qi = pl.program_id(2)
    W = ATTN_Q_TILE // ATTN_Q_SPLIT
    n_diag = ATTN_Q_TILE // ATTN_TILE
    groups = tuple(range(ATTN_Q_SPLIT))
    qs = [q_ref[c * W:(c + 1) * W, :] for c in groups]
    acc_scr[...] = jnp.zeros(acc_scr.shape, F32)

    def visible(d, c):
        return min(max((c + 1) * W - d * ATTN_TILE, 0), ATTN_TILE)

    def scores(j, which=groups):
        rows = pl.ds(pl.multiple_of(j * ATTN_TILE, ATTN_TILE), ATTN_TILE)
        k = jnp.concatenate([kn_ref[rows, :], kr_ref[rows, :]], axis=-1)
        return {c: lax.dot_general(k, qs[c], (((1,), (1,)), ((), ())), preferred_element_type=F32)
                for c in which}

    def softmax_pv(j, s_all, m_all, diag=None):
        vt = vt_ref[j]
        out = list(m_all)
        for c in s_all:
            s = s_all[c]
            nk = ATTN_TILE if diag is None else visible(diag, c)
            if diag is not None:
                s = s[:nk]
                if diag * ATTN_TILE + nk - 1 > c * W:
                    kp = lax.broadcasted_iota(jnp.int32, s.shape, 0) + diag * ATTN_TILE
                    qp = lax.broadcasted_iota(jnp.int32, s.shape, 1) + c * W
                    s = jnp.where(kp <= qp, s, NEG_INF)
            m_new = jnp.maximum(m_all[c], jnp.max(s, axis=0, keepdims=True))
            alpha = jnp.exp2(m_all[c] - m_new)
            p = jnp.exp2(s - m_new).astype(BF16)
            acc_scr[c] = alpha * acc_scr[c] + jnp.dot(vt[:, :nk], p, preferred_element_type=F32)
            out[c] = m_new
        return tuple(out)

    def body(jj, m_all):
        tiles = [jj * n_diag + u for u in range(n_diag)]
        s_tiles = [scores(t) for t in tiles]
        for t, s_all in zip(tiles, s_tiles):
            m_all = softmax_pv(t, s_all, m_all)
        return m_all

    first = n_diag * qi
    init = tuple(jnp.full((1, W), NEG_INF, F32) for _ in groups)
    m_all = lax.fori_loop(0, qi, body, init)
    s_diag = [scores(first + d, tuple(c for c in groups if visible(d, c) > 0)) for d in range(n_diag)]
    for d in range(n_diag):
        m_all = softmax_pv(first + d, s_diag[d], m_all, diag=d)
    for c in groups:
        acc = acc_scr[c]
        o_ref[c * W:(c + 1) * W, :] = (acc[:MLA_V] / acc[MLA_V:MLA_V + 1]).T.astype(o_ref.dtype)


def _attention(q_cat, kn, kr, vt, B, S):
    T = B * S
    nq = S // ATTN_Q_TILE
    return pl.pallas_call(
        _attn_kernel,
        out_shape=jax.ShapeDtypeStruct((T, MLA_HEADS * MLA_V), BF16),
        grid=(B, MLA_HEADS, nq),
        in_specs=[pl.BlockSpec((ATTN_Q_TILE, MLA_QK), lambda b, h, i: (b * nq + i, h)),
                  pl.BlockSpec((S, MLA_NOPE), lambda b, h, i: (b, h)),
                  pl.BlockSpec((S, LANES), lambda b, h, i: (b, 0)),
                  pl.BlockSpec((S // ATTN_TILE, VT_ROWS, ATTN_TILE), lambda b, h, i: (b, h, 0))],
        out_specs=pl.BlockSpec((ATTN_Q_TILE, MLA_V), lambda b, h, i: (b * nq + i, h)),
        scratch_shapes=[pltpu.VMEM((ATTN_Q_SPLIT, VT_ROWS, ATTN_Q_TILE // ATTN_Q_SPLIT), F32)],
        compiler_params=_cparams("parallel", "parallel", "arbitrary"),
        name="mla_attention",
    )(q_cat, kn, kr, vt)


def _ret_kernel(q_ref, k_ref, v_ref, g_ref, rc_ref, rs_ref, dec_ref, qw_ref, kw_ref, cd_ref,
                o_ref, qb_scr, qwb_scr, kb_scr, kwb_scr):
    S = q_ref.shape[0]
    C = RET_CHUNK
    n_chunks = S // C
    rc = rc_ref[...]
    rs = rs_ref[...]
    q = q_ref[...].astype(F32)
    k = k_ref[...].astype(F32)
    q = q * rc + _swap_halves(q) * rs
    k = (k * rc + _swap_halves(k) * rs) * (RET_DK ** -0.5)
    qw = qw_ref[0]
    kw = kw_ref[0]
    q3 = q.reshape(n_chunks, C, RET_DK)
    k3 = k.reshape(n_chunks, C, RET_DK)
    qb_scr[...] = q.astype(BF16)
    kb_scr[...] = k.astype(BF16)
    qwb_scr[...] = (q3 * qw[None]).reshape(S, RET_DK).astype(BF16)
    kwb_scr[...] = (k3 * kw[None]).reshape(S, RET_DK)
    dec = dec_ref[0]
    cd = cd_ref[0]

    def group(gi, R):
        rows_of = [pl.ds(pl.multiple_of((gi * RET_GROUP + u) * C, C), C) for u in range(RET_GROUP)]
        scores, updates = [], []
        for rows in rows_of:
            scores.append(lax.dot_general(qb_scr[rows, :], kb_scr[rows, :], (((1,), (1,)), ((), ())),
                                          preferred_element_type=F32))
            kwt = kwb_scr[rows, :].T.astype(BF16)
            updates.append(jnp.dot(kwt, v_ref[rows, :], preferred_element_type=F32))
        states = []
        for u in range(RET_GROUP):
            states.append(R.astype(BF16))
            R = cd * R + updates[u]
        for u, rows in enumerate(rows_of):
            o = jnp.dot((scores[u] * dec).astype(BF16), v_ref[rows, :], preferred_element_type=F32)
            o = o + jnp.dot(qwb_scr[rows, :], states[u], preferred_element_type=F32)
            mu = jnp.mean(o, axis=-1, keepdims=True)
            d = o - mu
            var = jnp.mean(d * d, axis=-1, keepdims=True)
            gate = g_ref[rows, :].astype(F32)
            o_ref[rows, :] = (d * lax.rsqrt(var + GN_EPS) * (gate * _sigmoid(gate))).astype(o_ref.dtype)
        return R

    lax.fori_loop(0, n_chunks // RET_GROUP, group, jnp.zeros((RET_DK, RET_DV), F32))


def _retention(proj, rc, rs, dec, qw, kw, cd, B, S):
    T = B * S
    H = RET_HEADS
    col = lambda off: pl.BlockSpec((S, LANES), lambda b, h: (b, off + h))
    tab = pl.BlockSpec((S, LANES), lambda b, h: (b, 0))
    per_head = lambda r: pl.BlockSpec((1, r, LANES), lambda b, h: (h, 0, 0))
    return pl.pallas_call(
        _ret_kernel,
        out_shape=jax.ShapeDtypeStruct((T, H * RET_DV), BF16),
        grid=(B, H),
        in_specs=[col(0), col(H), col(2 * H), col(3 * H), tab, tab,
                  per_head(RET_CHUNK), per_head(RET_CHUNK), per_head(RET_CHUNK), per_head(1)],
        out_specs=pl.BlockSpec((S, RET_DV), lambda b, h: (b, h)),
        scratch_shapes=[pltpu.VMEM((S, RET_DK), BF16), pltpu.VMEM((S, RET_DK), BF16),
                        pltpu.VMEM((S, RET_DK), BF16), pltpu.VMEM((S, RET_DK), F32)],
        compiler_params=_cparams("parallel", "parallel"),
        name="retention",
    )(proj, proj, proj, proj, rc, rs, dec, qw, kw, cd)


def _retention_constants():
    C = RET_CHUNK
    h = np.arange(RET_HEADS, dtype=np.float64)
    log_gamma = np.log(1.0 - 2.0 ** (-5.0 - h))
    idx = np.arange(C, dtype=np.float64)
    rel = idx[:, None] - idx[None, :]
    dec = np.where(rel[None] >= 0, np.exp(np.maximum(rel, 0.0)[None] * log_gamma[:, None, None]), 0.0)
    qw = np.exp((idx + 1.0)[None, :] * log_gamma[:, None])
    kw = np.exp((C - 1.0 - idx)[None, :] * log_gamma[:, None])
    cd = np.exp(C * log_gamma)
    bc = lambda a: np.ascontiguousarray(np.broadcast_to(a[..., None], a.shape + (LANES,))).astype(np.float32)
    return dec.astype(np.float32), bc(qw), bc(kw), bc(cd[:, None])


def _layernorm(y, g, b):
    mu = jnp.mean(y, axis=-1, keepdims=True)
    d = y - mu
    var = jnp.mean(d * d, axis=-1, keepdims=True)
    return d * lax.rsqrt(var + LN_EPS) * g + b


def _merge_kernel(x_ref, om_ref, or_ref, gm_ref, gr_ref, wo_ref, g_ref, b_ref, h_ref, hp_ref):
    mixed = (_sigmoid(gm_ref[...].astype(F32)) * om_ref[...].astype(F32)
             + _sigmoid(gr_ref[...].astype(F32)) * or_ref[...].astype(F32))
    mix = jnp.dot(mixed.astype(BF16), wo_ref[...], preferred_element_type=F32)
    h = _layernorm(DN_ALPHA * x_ref[...] + mix, g_ref[...], b_ref[...])
    h_ref[...] = h
    hp_ref[...] = _pack_rows(h)


def _merge(x2, o_mla, o_ret, proj, wo, g, b):
    T = x2.shape[0]
    row = lambda c: pl.BlockSpec((ROW_TILE, D_MODEL), lambda i: (i, c))
    return pl.pallas_call(
        _merge_kernel,
        out_shape=(jax.ShapeDtypeStruct((T, D_MODEL), F32), jax.ShapeDtypeStruct((T, PACKED_W), jnp.uint32)),
        grid=(T // ROW_TILE,),
        in_specs=[row(0), row(0), row(0), row(4), row(5),
                  _const_spec(wo.shape), _const_spec(g.shape), _const_spec(b.shape)],
        out_specs=(row(0), pl.BlockSpec((ROW_TILE, PACKED_W), lambda i: (i, 0))),
        compiler_params=_cparams("parallel"),
        name="merge_out_proj_ln",
    )(x2, o_mla, o_ret, proj, proj, wo, g, b)


def _split3(a):
    a0 = a.astype(BF16)
    r1 = a - a0.astype(F32)
    a1 = r1.astype(BF16)
    a2 = (r1 - a1.astype(F32)).astype(BF16)
    return a0, a1, a2


def _router_kernel(h_ref, w0_ref, w1_ref, w2_ref, b_ref, idx_ref, gate_ref, cnt_ref, run_scr):
    i = pl.program_id(0)

    @pl.when(i == 0)
    def _():
        run_scr[...] = jnp.zeros(run_scr.shape, F32)

    h0, h1, h2 = _split3(h_ref[...])
    w0, w1, w2 = w0_ref[...], w1_ref[...], w2_ref[...]
    dot = lambda a, w: jnp.dot(a, w, preferred_element_type=F32)
    logits = (dot(h0, w0) + (dot(h0, w1) + dot(h1, w0))
              + (dot(h0, w2) + dot(h1, w1) + dot(h2, w0))) + b_ref[...]
    rows = logits.shape[0]
    lane = lax.broadcasted_iota(jnp.int32, (rows, LANES), 1)
    lane_f = lane.astype(F32)
    work = jnp.where(lane < N_EXPERTS, logits, -jnp.inf)
    vals, ids, sels = [], [], []
    for _ in range(TOP_K):
        m = jnp.max(work, axis=-1, keepdims=True)
        first = jnp.min(jnp.where(work == m, lane_f, float(LANES)), axis=-1, keepdims=True)
        sel = lane_f == first
        work = jnp.where(sel, -jnp.inf, work)
        vals.append(m)
        ids.append(first)
        sels.append(sel)
    exps = [jnp.exp(v - vals[0]) for v in vals]
    denom = exps[0] + exps[1] + exps[2] + exps[3]
    chosen = jnp.zeros((rows, LANES), F32)
    for sel in sels:
        chosen = jnp.where(sel, 1.0, chosen)
    r = lax.broadcasted_iota(jnp.int32, (rows, rows), 0)
    c = lax.broadcasted_iota(jnp.int32, (rows, rows), 1)
    lower = jnp.where(r > c, 1.0, 0.0).astype(BF16)
    before = jnp.dot(lower, chosen.astype(BF16), preferred_element_type=F32) + run_scr[...]
    idx_out = jnp.zeros((rows, LANES), jnp.int32)
    gate_out = jnp.zeros((rows, LANES), F32)
    for kk in range(TOP_K):
        rank = jnp.sum(jnp.where(sels[kk], before, 0.0), axis=-1, keepdims=True).astype(jnp.int32)
        idx_out = jnp.where(lane == kk, ids[kk].astype(jnp.int32), idx_out)
        idx_out = jnp.where(lane == TOP_K + kk, rank, idx_out)
        gate_out = jnp.where(lane == kk, exps[kk] / denom, gate_out)
    idx_ref[...] = idx_out
    gate_ref[...] = gate_out
    total = run_scr[...] + jnp.sum(chosen, axis=0, keepdims=True)
    run_scr[...] = total
    cnt_ref[...] = total.astype(jnp.int32)


def _router(h, w_router, b_router):
    T = h.shape[0]
    wpad = jnp.zeros((D_MODEL, LANES), F32).at[:, :N_EXPERTS].set(w_router)
    w0 = wpad.astype(BF16)
    r1 = wpad - w0.astype(F32)
    w1 = r1.astype(BF16)
    w2 = (r1 - w1.astype(F32)).astype(BF16)
    bpad = jnp.zeros((1, LANES), F32).at[0, :N_EXPERTS].set(b_router)
    row = pl.BlockSpec((ROW_TILE, LANES), lambda i: (i, 0))
    return pl.pallas_call(
        _router_kernel,
        out_shape=(jax.ShapeDtypeStruct((T, LANES), jnp.int32),
                   jax.ShapeDtypeStruct((T, LANES), F32),
                   jax.ShapeDtypeStruct((1, LANES), jnp.int32)),
        grid=(T // ROW_TILE,),
        in_specs=[pl.BlockSpec((ROW_TILE, D_MODEL), lambda i: (i, 0)),
                  _const_spec(w0.shape), _const_spec(w0.shape), _const_spec(w0.shape),
                  _const_spec(bpad.shape)],
        out_specs=(row, row, pl.BlockSpec((1, LANES), lambda i: (0, 0))),
        scratch_shapes=[pltpu.VMEM((1, LANES), F32)],
        compiler_params=_cparams("arbitrary"),
        name="router",
    )(h, w0, w1, w2, bpad)


def _row_copy(src, src_row, dst, dst_row, sem):
    return pltpu.make_async_copy(src.at[pl.ds(src_row, 1), :], dst.at[pl.ds(dst_row, 1), :], sem)


def _dispatch_kernel(dest_ref, h_ref, xs_in_ref, xs_ref, sem):
    del xs_in_ref
    base = pl.program_id(0) * (MOVE_TILE * TOP_K)

    def issue(g, carry):
        r0 = pl.multiple_of(g * SUBLANES, SUBLANES)
        for u in range(SUBLANES):
            for kk in range(TOP_K):
                d = dest_ref[base + (r0 + u) * TOP_K + kk]
                _row_copy(h_ref, r0 + u, xs_ref, d, sem).start(priority=kk % 2)
        return carry

    lax.fori_loop(0, MOVE_TILE // SUBLANES, issue, 0)
    for _ in range(TOP_K):
        pltpu.make_async_copy(h_ref, xs_ref.at[pl.ds(0, MOVE_TILE), :], sem).wait()


def _dispatch(dest_flat, hp, n_rows):
    T = hp.shape[0]
    grid_spec = pltpu.PrefetchScalarGridSpec(
        num_scalar_prefetch=1,
        grid=(T // MOVE_TILE,),
        in_specs=[pl.BlockSpec((MOVE_TILE, PACKED_W), lambda i, d: (i, 0)),
                  pl.BlockSpec(memory_space=pl.ANY)],
        out_specs=pl.BlockSpec(memory_space=pl.ANY),
        scratch_shapes=[pltpu.SemaphoreType.DMA],
    )
    return pl.pallas_call(
        _dispatch_kernel,
        out_shape=jax.ShapeDtypeStruct((n_rows, PACKED_W), jnp.uint32),
        grid_spec=grid_spec,
        input_output_aliases={2: 0},
        compiler_params=_cparams("arbitrary"),
        name="dispatch_rows",
    )(dest_flat, hp, jnp.zeros((n_rows, PACKED_W), jnp.uint32))


def _expert_kernel(blk_e_ref, nact_ref, next_e_ref, x_ref, bu_ref, bd_ref, wu_hbm, wd_hbm, y_ref,
                   wuf_scr, wdf_scr, wub_scr, wdb_scr, sems):
    i = pl.program_id(0)
    e = blk_e_ref[i]

    def weight_copies(expert):
        return (pltpu.make_async_copy(wu_hbm.at[expert], wuf_scr, sems.at[0]),
                pltpu.make_async_copy(wd_hbm.at[expert], wdf_scr, sems.at[1]))

    @pl.when(i == 0)
    def _():
        for cp in weight_copies(e):
            cp.start()

    first_of_group = jnp.logical_or(i == 0, e != blk_e_ref[jnp.maximum(i - 1, 0)])

    @pl.when(jnp.logical_and(i < nact_ref[0], first_of_group))
    def _():
        for cp in weight_copies(e):
            cp.wait()
        wub_scr[...] = wuf_scr[...].astype(BF16)
        wdb_scr[...] = wdf_scr[...].astype(BF16)

        @pl.when(next_e_ref[i] >= 0)
        def _():
            for cp in weight_copies(next_e_ref[i]):
                cp.start()

    @pl.when(i < nact_ref[0])
    def _():
        lo, hi = _unpack_rows(x_ref[...])
        x = jnp.concatenate([lo.astype(BF16), hi.astype(BF16)], axis=-1)
        acc = jnp.zeros((EXPERT_TILE, D_MODEL), F32)
        step = 256
        for c in range(D_EXPERT // step):
            glu = jnp.dot(x, wub_scr[:, c * step:(c + 1) * step], preferred_element_type=F32)
            lin = jnp.dot(x, wub_scr[:, D_EXPERT + c * step:D_EXPERT + (c + 1) * step],
                          preferred_element_type=F32)
            glu = jnp.minimum(glu + bu_ref[0, :, c * step:(c + 1) * step], SWIGLU_LIMIT)
            lin = jnp.clip(lin + bu_ref[0, :, D_EXPERT + c * step:D_EXPERT + (c + 1) * step],
                           -SWIGLU_LIMIT, SWIGLU_LIMIT)
            a = glu * _sigmoid(SWIGLU_ALPHA * glu) * (lin + 1.0)
            acc = acc + jnp.dot(a.astype(BF16), wdb_scr[c * step:(c + 1) * step, :],
                                preferred_element_type=F32)
        y_ref[...] = _pack_rows(acc + bd_ref[0])

    @pl.when(i >= nact_ref[0])
    def _():
        y_ref[...] = jnp.zeros(y_ref.shape, y_ref.dtype)


def _experts(blk_e, nact, next_e, xs, w_up, b_up, w_down, b_down):
    P = xs.shape[0]
    n_blocks = P // EXPERT_TILE
    act = lambda i, na: jnp.minimum(i, na[0] - 1)
    grid_spec = pltpu.PrefetchScalarGridSpec(
        num_scalar_prefetch=3,
        grid=(n_blocks,),
        in_specs=[pl.BlockSpec((EXPERT_TILE, PACKED_W), lambda i, be, na, ne: (act(i, na), 0)),
                  pl.BlockSpec((1, 1, 2 * D_EXPERT), lambda i, be, na, ne: (be[act(i, na)], 0, 0)),
                  pl.BlockSpec((1, 1, D_MODEL), lambda i, be, na, ne: (be[act(i, na)], 0, 0)),
                  pl.BlockSpec(memory_space=pl.ANY),
                  pl.BlockSpec(memory_space=pl.ANY)],
        out_specs=pl.BlockSpec((EXPERT_TILE, PACKED_W), lambda i, be, na, ne: (i, 0)),
        scratch_shapes=[pltpu.VMEM((D_MODEL, 2 * D_EXPERT), F32),
                        pltpu.VMEM((D_EXPERT, D_MODEL), F32),
                        pltpu.VMEM((D_MODEL, 2 * D_EXPERT), BF16),
                        pltpu.VMEM((D_EXPERT, D_MODEL), BF16),
                        pltpu.SemaphoreType.DMA((2,))],
    )
    return pl.pallas_call(
        _expert_kernel,
        out_shape=jax.ShapeDtypeStruct((P, PACKED_W), jnp.uint32),
        grid_spec=grid_spec,
        compiler_params=_cparams("arbitrary"),
        name="routed_experts",
    )(blk_e, nact, next_e, xs, b_up.reshape(N_EXPERTS, 1, 2 * D_EXPERT),
      b_down.reshape(N_EXPERTS, 1, D_MODEL), w_up, w_down)


def _combine_kernel(dest_ref, h_ref, gate_ref, g_ref, b_ref, ys_ref, o_ref, ybuf, sems):
    i = pl.program_id(0)
    n = pl.num_programs(0)

    def gather(step, slot):
        base = step * (MOVE_TILE * TOP_K)

        def issue(g, carry):
            r0 = pl.multiple_of(g * SUBLANES, SUBLANES)
            for u in range(SUBLANES):
                for kk in range(TOP_K):
                    d = dest_ref[base + (r0 + u) * TOP_K + kk]
                    _row_copy(ys_ref, d, ybuf.at[slot, kk], r0 + u, sems.at[slot]).start(priority=kk % 2)
            return carry

        lax.fori_loop(0, MOVE_TILE // SUBLANES, issue, 0)

    @pl.when(i == 0)
    def _():
        gather(0, 0)

    @pl.when(i + 1 < n)
    def _():
        gather(i + 1, (i + 1) % 2)

    slot = i % 2
    for kk in range(TOP_K):
        pltpu.make_async_copy(ys_ref.at[pl.ds(0, MOVE_TILE), :], ybuf.at[slot, kk], sems.at[slot]).wait()

    gate = gate_ref[...]
    f_lo = jnp.zeros((MOVE_TILE, PACKED_W), F32)
    f_hi = jnp.zeros((MOVE_TILE, PACKED_W), F32)
    for kk in range(TOP_K):
        lo, hi = _unpack_rows(ybuf[slot, kk])
        f_lo = f_lo + gate[:, kk:kk + 1] * lo
        f_hi = f_hi + gate[:, kk:kk + 1] * hi
    y_lo = DN_ALPHA * h_ref[:, :PACKED_W] + f_lo
    y_hi = DN_ALPHA * h_ref[:, PACKED_W:] + f_hi
    mu = (jnp.sum(y_lo, axis=-1, keepdims=True) + jnp.sum(y_hi, axis=-1, keepdims=True)) / D_MODEL
    d_lo = y_lo - mu
    d_hi = y_hi - mu
    var = (jnp.sum(d_lo * d_lo, axis=-1, keepdims=True) + jnp.sum(d_hi * d_hi, axis=-1, keepdims=True)) / D_MODEL
    inv = lax.rsqrt(var + LN_EPS)
    o_ref[:, :PACKED_W] = d_lo * inv * g_ref[:, :PACKED_W] + b_ref[:, :PACKED_W]
    o_ref[:, PACKED_W:] = d_hi * inv * g_ref[:, PACKED_W:] + b_ref[:, PACKED_W:]


def _combine(dest_flat, h, ys, gate, g, b):
    T = h.shape[0]
    grid_spec = pltpu.PrefetchScalarGridSpec(
        num_scalar_prefetch=1,
        grid=(T // MOVE_TILE,),
        in_specs=[pl.BlockSpec((MOVE_TILE, D_MODEL), lambda i, d: (i, 0)),
                  pl.BlockSpec((MOVE_TILE, LANES), lambda i, d: (i, 0)),
                  pl.BlockSpec(g.shape, lambda i, d: (0, 0)),
                  pl.BlockSpec(b.shape, lambda i, d: (0, 0)),
                  pl.BlockSpec(memory_space=pl.ANY)],
        out_specs=pl.BlockSpec((MOVE_TILE, D_MODEL), lambda i, d: (i, 0)),
        scratch_shapes=[pltpu.VMEM((2, TOP_K, MOVE_TILE, PACKED_W), jnp.uint32),
                        pltpu.SemaphoreType.DMA((2,))],
    )
    return pl.pallas_call(
        _combine_kernel,
        out_shape=jax.ShapeDtypeStruct((T, D_MODEL), F32),
        grid_spec=grid_spec,
        compiler_params=_cparams("arbitrary"),
        name="combine_ln",
    )(dest_flat, h, gate, g, b, ys)


def _rot_cols(w):
    half = w.shape[-1] // 2
    return jnp.concatenate([-w[:, half:], w[:, :half]], axis=-1)


def _pack_w_in(w_in):
    lat = Q_LORA + KV_LORA
    w_kr = w_in[:, lat:lat + MLA_ROPE]
    return jnp.concatenate([w_in[:, lat + MLA_ROPE:], w_in[:, :lat + MLA_ROPE], _rot_cols(w_kr)],
                           axis=-1).astype(BF16)


def _pack_w_uq(w_uq):
    w = w_uq.reshape(Q_LORA, MLA_HEADS, MLA_NOPE + MLA_ROPE)
    rope = w[:, :, MLA_NOPE:]
    rot = jnp.concatenate([-rope[:, :, MLA_ROPE // 2:], rope[:, :, :MLA_ROPE // 2]], axis=-1)
    return jnp.concatenate([w, rot], axis=-1).reshape(Q_LORA, MLA_HEADS * MLA_QK).astype(BF16)


def _rope_freqs():
    ret = 1.0 / (10000.0 ** jnp.linspace(0.0, 1.0, RET_DK // 2, dtype=F32))
    mla = 1.0 / (ROPE_BASE ** (jnp.arange(0, MLA_ROPE, 2, dtype=F32) / MLA_ROPE))
    return jnp.concatenate([ret, mla, jnp.zeros((LANES - ret.shape[0] - mla.shape[0],), F32)])[None, :]


def _routing_plan(idx_out, cnt):
    top_i = idx_out[:, :TOP_K]
    rank = idx_out[:, TOP_K:2 * TOP_K]
    counts = cnt[0, :N_EXPERTS]
    padded = ((counts + EXPERT_TILE - 1) // EXPERT_TILE) * EXPERT_TILE
    pad_end = jnp.cumsum(padded)
    pad_start = pad_end - padded
    experts = jnp.arange(N_EXPERTS, dtype=jnp.int32)
    dest = rank + jnp.sum(jnp.where(top_i[..., None] == experts, pad_start, 0), axis=-1)
    n_rows = top_i.shape[0] * TOP_K + N_EXPERTS * EXPERT_TILE
    n_blocks = n_rows // EXPERT_TILE
    blk_row = jnp.arange(n_blocks, dtype=jnp.int32) * EXPERT_TILE
    blk_e = jnp.minimum(jnp.sum((pad_end[None, :] <= blk_row[:, None]).astype(jnp.int32), axis=-1),
                        N_EXPERTS - 1)
    nact = (pad_end[-1:] // EXPERT_TILE).astype(jnp.int32)
    later = jnp.where((counts > 0)[None, :] & (experts[None, :] > experts[:, None]), experts[None, :], N_EXPERTS)
    nxt = jnp.min(later, axis=-1)
    nxt = jnp.where(nxt == N_EXPERTS, -1, nxt)
    next_e = jnp.sum(jnp.where(blk_e[:, None] == experts[None, :], nxt[None, :], 0), axis=-1).astype(jnp.int32)
    return dest.astype(jnp.int32), blk_e, nact, next_e, n_rows


def kernel(x, positions, w_in, q_norm_g, w_uq, kv_norm_g, w_ukv, w_o, ln1_g, ln1_b,
           w_router, b_router, w_up, b_up, w_down, b_down, ln2_g, ln2_b):
    B, S, _ = x.shape
    T = B * S
    pos_col = positions.reshape(T, 1).astype(jnp.int32)
    rc, rs, mc, ms = _rope_tables(pos_col, _rope_freqs())
    dec, qw, kw, cd = _retention_constants()
    h = x.reshape(T, D_MODEL)
    for l in range(DEPTH):
        proj = _input_projection(h, _pack_w_in(w_in[l]))
        w_kv = w_ukv[l].reshape(KV_LORA, MLA_HEADS, MLA_NOPE + MLA_V)
        w_k = w_kv[:, :, :MLA_NOPE].reshape(KV_LORA, MLA_HEADS * MLA_NOPE).astype(BF16)
        w_vt = w_kv[:, :, MLA_NOPE:].reshape(KV_LORA, MLA_HEADS * MLA_V).T.astype(BF16)
        q_cat, kn, vt, kr = _mla_prep(proj, mc, ms, _pack_w_uq(w_uq[l]), w_k, w_vt,
                                      q_norm_g[l][None, :], kv_norm_g[l][None, :])
        o_mla = _attention(q_cat, kn, kr, vt, B, S)
        o_ret = _retention(proj, rc, rs, dec, qw, kw, cd, B, S)
        h1, h1p = _merge(h, o_mla, o_ret, proj, w_o[l].astype(BF16), ln1_g[l][None, :], ln1_b[l][None, :])
        idx_out, gate, cnt = _router(h1, w_router[l], b_router[l])
        dest, blk_e, nact, next_e, n_rows = _routing_plan(idx_out, cnt)
        dest_flat = dest.reshape(-1)
        xs = _dispatch(dest_flat, h1p, n_rows)
        ys = _experts(blk_e, nact, next_e, xs, w_up[l], b_up[l], w_down[l], b_down[l])
        h = _combine(dest_flat, h1, ys, gate, ln2_g[l][None, :], ln2_b[l][None, :])
    return h.reshape(B, S, D_MODEL)
```

```python
import functools

import numpy as np
import jax
import jax.numpy as jnp
from jax import lax
from jax.experimental import pallas as pl
from jax.experimental.pallas import tpu as pltpu

F32 = jnp.float32
BF16 = jnp.bfloat16

D_MODEL = 1024
DEPTH = 1
MLA_HEADS = 8
MLA_NOPE = 128
MLA_ROPE = 64
MLA_V = 128
Q_LORA = 384
KV_LORA = 256
ROPE_BASE = 10000.0
RET_HEADS = 8
RET_DK = 128
RET_DV = 128
RET_CHUNK = 128
N_EXPERTS = 32
TOP_K = 4
D_EXPERT = 1024
SWIGLU_LIMIT = 7.0
SWIGLU_ALPHA = 1.702
DN_ALPHA = (2.0 * DEPTH) ** 0.25
LN_EPS = 1e-5
RMS_EPS = 1e-6
GN_EPS = 1e-6
NEG_INF = -1e30

LANES = 128
SUBLANES = 8
MLA_QK = 2 * LANES
PACKED_W = D_MODEL // 2
LATENT_W = 768
N_WIDE = 6 * D_MODEL
N_PROJ = N_WIDE + LATENT_W
VMEM_LIMIT = 56 * 1024 * 1024

ROW_TILE = 512
ATTN_TILE = ROW_TILE
ATTN_Q_TILE = 2 * ATTN_TILE
ATTN_Q_SPLIT = 4
VT_ROWS = MLA_V + 16
LOG2_E = 1.4426950408889634
RET_GROUP = 8
EXPERT_TILE = 256
STAGE_ROWS = -(-(ROW_TILE * TOP_K + (SUBLANES - 1) * N_EXPERTS) // EXPERT_TILE) * EXPERT_TILE


def _cparams(*sem):
    return pltpu.CompilerParams(dimension_semantics=sem, vmem_limit_bytes=VMEM_LIMIT)


def _const_spec(shape):
    nd = len(shape)
    return pl.BlockSpec(shape, lambda *_: (0,) * nd, pipeline_mode=pl.Buffered(1))


def _sigmoid(x):
    return 1.0 / (1.0 + jnp.exp(-x))


def _pack_rows(x):
    n = x.shape[-1] // 2
    xb = x.astype(BF16).astype(F32)
    lo = pltpu.bitcast(xb[:, :n], jnp.uint32) >> 16
    hi = pltpu.bitcast(xb[:, n:], jnp.uint32) & jnp.uint32(0xFFFF0000)
    return lo | hi


def _unpack_rows(w):
    lo = pltpu.bitcast(w << 16, F32)
    hi = pltpu.bitcast(w & jnp.uint32(0xFFFF0000), F32)
    return lo, hi


def _swap_halves(x):
    return pltpu.roll(x, LANES // 2, 1)


def _proj_kernel(x_ref, w_ref, o_ref):
    xb = x_ref[...].astype(BF16)
    for c in range(N_PROJ // LATENT_W):
        sl = slice(c * LATENT_W, (c + 1) * LATENT_W)
        o_ref[:, sl] = jnp.dot(xb, w_ref[:, sl], preferred_element_type=F32).astype(o_ref.dtype)


def _input_projection(x2, w_cat):
    T = x2.shape[0]
    return pl.pallas_call(
        _proj_kernel,
        out_shape=jax.ShapeDtypeStruct((T, N_PROJ), BF16),
        grid=(T // ROW_TILE,),
        in_specs=[pl.BlockSpec((ROW_TILE, D_MODEL), lambda i: (i, 0)),
                  _const_spec((D_MODEL, N_PROJ))],
        out_specs=pl.BlockSpec((ROW_TILE, N_PROJ), lambda i: (i, 0)),
        compiler_params=_cparams("parallel"),
        name="input_projection",
    )(x2, w_cat)


def _rope_table_kernel(pos_ref, freq_ref, rc_ref, rs_ref, mc_ref, ms_ref):
    ang = pos_ref[...].astype(F32) * freq_ref[...]
    c = jnp.cos(ang)
    s = jnp.sin(ang)
    lane = lax.broadcasted_iota(jnp.int32, ang.shape, 1)
    c64 = pltpu.roll(c, 64, 1)
    s64 = pltpu.roll(s, 64, 1)
    rc_ref[...] = jnp.where(lane < 64, c, c64)
    rs_ref[...] = jnp.where(lane < 64, -s, s64)
    c96 = pltpu.roll(c, 96, 1)
    s96 = pltpu.roll(s, 96, 1)
    mc_ref[...] = jnp.where(lane < 32, c64, jnp.where(lane < 64, c96, 0.0))
    ms_ref[...] = jnp.where(lane < 32, s64, jnp.where(lane < 64, s96, 0.0))


def _rope_tables(pos_col, freq_row):
    T = pos_col.shape[0]
    rows = min(2048, T)
    tab =jax.ShapeDtypeStruct((T, LANES), F32)
    spec = pl.BlockSpec((rows, LANES), lambda i: (i, 0))
    return pl.pallas_call(
        _rope_table_kernel,
        out_shape=(tab, tab, tab, tab),
        grid=(T // rows,),
        in_specs=[pl.BlockSpec((rows, 1), lambda i: (i, 0)), _const_spec((1, LANES))],
        out_specs=(spec, spec, spec, spec),
        compiler_params=_cparams("parallel"),
        name="rope_tables",
    )(pos_col, freq_row)


def _mla_prep_kernel(a_ref, mc_ref, ms_ref, wq_ref, wk_ref, wvt_ref, qg_ref, kvg_ref,
                     q_ref, kn_ref, vt_ref, kr_ref):
    a = a_ref[...].astype(F32)
    mc = mc_ref[...]
    ms = ms_ref[...]
    cq = a[:, :Q_LORA]
    qn = cq * lax.rsqrt(jnp.mean(cq * cq, axis=-1, keepdims=True) + RMS_EPS) * qg_ref[...]
    q = jnp.dot(qn.astype(BF16), wq_ref[...], preferred_element_type=F32)
    scale = (MLA_NOPE + MLA_ROPE) ** -0.5 * LOG2_E
    for h in range(MLA_HEADS):
        nope = q[:, h * MLA_QK:h * MLA_QK + LANES]
        blk = q[:, h * MLA_QK + LANES:(h + 1) * MLA_QK]
        rope = blk * mc + _swap_halves(blk) * ms
        q_ref[:, h * MLA_QK:h * MLA_QK + LANES] = (nope * scale).astype(BF16)
        q_ref[:, h * MLA_QK + LANES:(h + 1) * MLA_QK] = (rope * scale).astype(BF16)
    ckv = a[:, Q_LORA:Q_LORA + KV_LORA]
    kvn = ckv * lax.rsqrt(jnp.mean(ckv * ckv, axis=-1, keepdims=True) + RMS_EPS) * kvg_ref[...]
    kvb = kvn.astype(BF16)
    kn_ref[...] = jnp.dot(kvb, wk_ref[...], preferred_element_type=F32).astype(BF16)
    vt = lax.dot_general(wvt_ref[...], kvb, (((1,), (1,)), ((), ())),
                         preferred_element_type=F32).astype(BF16)
    ones = jnp.ones((VT_ROWS - MLA_V, vt.shape[1]), BF16)
    for h in range(MLA_HEADS):
        vt_ref[0, h * VT_ROWS:h * VT_ROWS + MLA_V, :] = vt[h * MLA_V:(h + 1) * MLA_V, :]
        vt_ref[0, h * VT_ROWS + MLA_V:(h + 1) * VT_ROWS, :] = ones
    krb =a[:, Q_LORA + KV_LORA:]
    kr_ref[...] = (krb * mc + _swap_halves(krb) * ms).astype(BF16)


def _mla_prep(proj, mc, ms, wq, wk, wvt, qg, kvg):
    T = proj.shape[0]
    row = lambda w: pl.BlockSpec((ROW_TILE, w), lambda i: (i, 0))
    return pl.pallas_call(
        _mla_prep_kernel,
        out_shape=(jax.ShapeDtypeStruct((T, MLA_HEADS * MLA_QK), BF16),
                   jax.ShapeDtypeStruct((T, MLA_HEADS * MLA_NOPE), BF16),
                   jax.ShapeDtypeStruct((T // ROW_TILE, MLA_HEADS * VT_ROWS, ROW_TILE), BF16),
                   jax.ShapeDtypeStruct((T, LANES), BF16)),
        grid=(T // ROW_TILE,),
        in_specs=[pl.BlockSpec((ROW_TILE, LATENT_W), lambda i: (i, N_WIDE // LATENT_W)),
                  row(LANES), row(LANES),
                  _const_spec(wq.shape), _const_spec(wk.shape), _const_spec(wvt.shape),
                  _const_spec(qg.shape), _const_spec(kvg.shape)],
        out_specs=(row(MLA_HEADS * MLA_QK), row(MLA_HEADS * MLA_NOPE),
                   pl.BlockSpec((1, MLA_HEADS * VT_ROWS, ROW_TILE), lambda i: (i, 0, 0)), row(LANES)),
        compiler_params=_cparams("parallel"),
        name="mla_prep",
    )(proj, mc, ms, wq, wk, wvt, qg, kvg)


def _attn_kernel(q_ref, kn_ref, kr_ref, vt_ref, o_ref, acc_scr):
    qi = pl.program_id(2)
    W = ATTN_Q_TILE // ATTN_Q_SPLIT
    n_diag = ATTN_Q_TILE // ATTN_TILE
    groups = tuple(range(ATTN_Q_SPLIT))
    qs = [q_ref[c * W:(c + 1) * W, :] for c in groups]
    acc_scr[...] = jnp.zeros(acc_scr.shape, F32)

    def visible(d, c):
        return min(max((c + 1) * W - d * ATTN_TILE, 0), ATTN_TILE)

    def scores(j, which=groups):
        rows = pl.ds(pl.multiple_of(j * ATTN_TILE, ATTN_TILE), ATTN_TILE)
        k = jnp.concatenate([kn_ref[rows, :], kr_ref[rows, :]], axis=-1)
        return {c: lax.dot_general(k, qs[c], (((1,), (1,)), ((), ())), preferred_element_type=F32)
                for c in which}

    def softmax_pv(j, s_all, m_all, diag=None):
        vt = vt_ref[j]
        out = list(m_all)
        for c in s_all:
            s = s_all[c]
            nk = ATTN_TILE if diag is None else visible(diag, c)
            if diag is not None:
                s = s[:nk]
                if diag * ATTN_TILE + nk - 1 > c * W:
                    kp = lax.broadcasted_iota(jnp.int32, s.shape, 0) + diag * ATTN_TILE
                    qp = lax.broadcasted_iota(jnp.int32, s.shape, 1) + c * W
                    s = jnp.where(kp <= qp, s, NEG_INF)
            m_new = jnp.maximum(m_all[c], jnp.max(s, axis=0, keepdims=True))
            alpha = jnp.exp2(m_all[c] - m_new)
            p = jnp.exp2(s - m_new).astype(BF16)
            acc_scr[c] = alpha * acc_scr[c] + jnp.dot(vt[:, :nk], p, preferred_element_type=F32)
            out[c] = m_new
        return tuple(out)

    def body(jj, m_all):
        tiles = [jj * n_diag + u for u in range(n_diag)]
        s_tiles = [scores(t) for t in tiles]
        for t, s_all in zip(tiles, s_tiles):
            m_all = softmax_pv(t, s_all, m_all)
        return m_all

    first = n_diag * qi
    init = tuple(jnp.full((1, W), NEG_INF, F32) for _ in groups)
    m_all = lax.fori_loop(0, qi, body, init)
    s_diag = [scores(first + d, tuple(c for c in groups if visible(d, c) > 0)) for d in range(n_diag)]
    for d in range(n_diag):
        m_all = softmax_pv(first + d, s_diag[d], m_all, diag=d)
    for c in groups:
        acc = acc_scr[c]
        o_ref[c * W:(c + 1) * W, :] = (acc[:MLA_V] / acc[MLA_V:MLA_V + 1]).T.astype(o_ref.dtype)


def _attention(q_cat, kn, kr, vt, B, S):
    T = B * S
    nq = S // ATTN_Q_TILE
    return pl.pallas_call(
        _attn_kernel,
        out_shape=jax.ShapeDtypeStruct((T, MLA_HEADS * MLA_V), BF16),
        grid=(B, MLA_HEADS, nq),
        in_specs=[pl.BlockSpec((ATTN_Q_TILE, MLA_QK), lambda b, h, i: (b * nq + i, h)),
                  pl.BlockSpec((S, MLA_NOPE), lambda b, h, i: (b, h)),
                  pl.BlockSpec((S, LANES), lambda b, h, i: (b, 0)),
                  pl.BlockSpec((S // ATTN_TILE, VT_ROWS, ATTN_TILE), lambda b, h, i: (b, h, 0))],
        out_specs=pl.BlockSpec((ATTN_Q_TILE, MLA_V), lambda b, h, i: (b * nq + i, h)),
        scratch_shapes=[pltpu.VMEM((ATTN_Q_SPLIT, VT_ROWS, ATTN_Q_TILE // ATTN_Q_SPLIT), F32)],
        compiler_params=_cparams("parallel", "parallel", "arbitrary"),
        name="mla_attention",
    )(q_cat, kn, kr, vt)


def _ret_kernel(q_ref, k_ref, v_ref, g_ref, rc_ref, rs_ref, dec_ref, qw_ref, kw_ref, cd_ref,
                o_ref, qb_scr, qwb_scr, kb_scr, kwb_scr):
    S = q_ref.shape[0]
    C = RET_CHUNK
    n_chunks = S // C
    rc = rc_ref[...]
    rs = rs_ref[...]
    q = q_ref[...].astype(F32)
    k = k_ref[...].astype(F32)
    q = q * rc + _swap_halves(q) * rs
    k = (k * rc + _swap_halves(k) * rs) * (RET_DK ** -0.5)
    qw = qw_ref[0]
    kw = kw_ref[0]
    q3 = q.reshape(n_chunks, C, RET_DK)
    k3 = k.reshape(n_chunks, C, RET_DK)
    qb_scr[...] = q.astype(BF16)
    kb_scr[...] = k.astype(BF16)
    qwb_scr[...] = (q3 * qw[None]).reshape(S, RET_DK).astype(BF16)
    kwb_scr[...] = (k3 * kw[None]).reshape(S, RET_DK)
    dec = dec_ref[0]
    cd = cd_ref[0]

    def group(gi, R):
        rows_of = [pl.ds(pl.multiple_of((gi * RET_GROUP + u) * C, C), C) for u in range(RET_GROUP)]
        scores, updates = [], []
        for rows in rows_of:
            scores.append(lax.dot_general(qb_scr[rows, :], kb_scr[rows, :], (((1,), (1,)), ((), ())),
                                          preferred_element_type=F32))
            kwt = kwb_scr[rows, :].T.astype(BF16)
            updates.append(jnp.dot(kwt, v_ref[rows, :], preferred_element_type=F32))
        states = []
        for u in range(RET_GROUP):
            states.append(R.astype(BF16))
            R = cd * R + updates[u]
        for u, rows in enumerate(rows_of):
            o = jnp.dot((scores[u] * dec).astype(BF16), v_ref[rows, :], preferred_element_type=F32)
            o = o + jnp.dot(qwb_scr[rows, :], states[u], preferred_element_type=F32)
            mu = jnp.mean(o, axis=-1, keepdims=True)
            d = o - mu
            var = jnp.mean(d * d, axis=-1, keepdims=True)
            gate = g_ref[rows, :].astype(F32)
            o_ref[rows, :] = (d * lax.rsqrt(var + GN_EPS) * (gate * _sigmoid(gate))).astype(o_ref.dtype)
        return R

    lax.fori_loop(0, n_chunks // RET_GROUP, group, jnp.zeros((RET_DK, RET_DV), F32))


def _retention(proj, rc, rs, dec, qw, kw, cd, B, S):
    T = B * S
    H = RET_HEADS
    col = lambda off: pl.BlockSpec((S, LANES), lambda b, h: (b, off + h))
    tab = pl.BlockSpec((S, LANES), lambda b, h: (b, 0))
    per_head = lambda r: pl.BlockSpec((1, r, LANES), lambda b, h: (h, 0, 0))
    return pl.pallas_call(
        _ret_kernel,
        out_shape=jax.ShapeDtypeStruct((T, H * RET_DV), BF16),
        grid=(B, H),
        in_specs=[col(0), col(H), col(2 * H), col(3 * H), tab, tab,
                  per_head(RET_CHUNK), per_head(RET_CHUNK), per_head(RET_CHUNK), per_head(1)],
        out_specs=pl.BlockSpec((S, RET_DV), lambda b, h: (b, h)),
        scratch_shapes=[pltpu.VMEM((S, RET_DK), BF16), pltpu.VMEM((S, RET_DK), BF16),
                        pltpu.VMEM((S, RET_DK), BF16), pltpu.VMEM((S, RET_DK), F32)],
        compiler_params=_cparams("parallel", "parallel"),
        name="retention",
    )(proj, proj, proj, proj, rc, rs, dec, qw, kw, cd)


def _retention_constants():
    C = RET_CHUNK
    h = np.arange(RET_HEADS, dtype=np.float64)
    log_gamma = np.log(1.0 - 2.0 ** (-5.0 - h))
    idx = np.arange(C, dtype=np.float64)
    rel = idx[:, None] - idx[None, :]
    dec = np.where(rel[None] >= 0, np.exp(np.maximum(rel, 0.0)[None] * log_gamma[:, None, None]), 0.0)
    qw = np.exp((idx + 1.0)[None, :] * log_gamma[:, None])
    kw = np.exp((C - 1.0 - idx)[None, :] * log_gamma[:, None])
    cd = np.exp(C * log_gamma)
    bc = lambda a: np.ascontiguousarray(np.broadcast_to(a[..., None], a.shape + (LANES,))).astype(np.float32)
    return dec.astype(np.float32), bc(qw), bc(kw), bc(cd[:, None])


def _layernorm(y, g, b):
    mu = jnp.mean(y, axis=-1, keepdims=True)
    d = y - mu
    var = jnp.mean(d * d, axis=-1, keepdims=True)
    return d * lax.rsqrt(var + LN_EPS) * g + b


def _merge_kernel(x_ref, om_ref, or_ref, gm_ref, gr_ref, wo_ref, g_ref, b_ref, h_ref, hb_ref):
    mixed = (_sigmoid(gm_ref[...].astype(F32)) * om_ref[...].astype(F32)
             + _sigmoid(gr_ref[...].astype(F32)) * or_ref[...].astype(F32))
    mix = jnp.dot(mixed.astype(BF16), wo_ref[...], preferred_element_type=F32)
    h = _layernorm(DN_ALPHA * x_ref[...] + mix, g_ref[...], b_ref[...])
    h_ref[...] = h
    hb_ref[...] = h.astype(BF16)


def _merge(x2, o_mla, o_ret, proj, wo, g, b):
    T = x2.shape[0]
    row = lambda c: pl.BlockSpec((ROW_TILE, D_MODEL), lambda i: (i, c))
    return pl.pallas_call(
        _merge_kernel,
        out_shape=(jax.ShapeDtypeStruct((T, D_MODEL), F32), jax.ShapeDtypeStruct((T, D_MODEL), BF16)),
        grid=(T // ROW_TILE,),
        in_specs=[row(0), row(0), row(0), row(4), row(5),
                  _const_spec(wo.shape), _const_spec(g.shape), _const_spec(b.shape)],
        out_specs=(row(0), row(0)),
        compiler_params=_cparams("parallel"),
        name="merge_out_proj_ln",
    )(x2, o_mla, o_ret, proj, proj, wo, g, b)


def _split3(a):
    a0 = a.astype(BF16)
    r1 = a - a0.astype(F32)
    a1 = r1.astype(BF16)
    a2 = (r1 - a1.astype(F32)).astype(BF16)
    return a0, a1, a2


def _router_kernel(h_ref, w0_ref, w1_ref, w2_ref, b_ref, idx_ref, gate_ref, cnt_ref):
    h0, h1, h2 = _split3(h_ref[...])
    w0, w1, w2 = w0_ref[...], w1_ref[...], w2_ref[...]
    dot = lambda a, w: jnp.dot(a, w, preferred_element_type=F32)
    logits = (dot(h0, w0) + (dot(h0, w1) + dot(h1, w0))
              + (dot(h0, w2) + dot(h1, w1) + dot(h2, w0))) + b_ref[...]
    rows = logits.shape[0]
    lane = lax.broadcasted_iota(jnp.int32, (rows, LANES), 1)
    lane_f = lane.astype(F32)
    work = jnp.where(lane < N_EXPERTS, logits, -jnp.inf)
    vals, ids, sels = [], [], []
    for _ in range(TOP_K):
        m = jnp.max(work, axis=-1, keepdims=True)
        first = jnp.min(jnp.where(work == m, lane_f, float(LANES)), axis=-1, keepdims=True)
        sel = lane_f == first
        work = jnp.where(sel, -jnp.inf, work)
        vals.append(m)
        ids.append(first)
        sels.append(sel)
    exps = [jnp.exp(v - vals[0]) for v in vals]
    denom = exps[0] + exps[1] + exps[2] + exps[3]
    chosen = jnp.zeros((rows, LANES), F32)
    for sel in sels:
        chosen = jnp.where(sel, 1.0, chosen)
    r = lax.broadcasted_iota(jnp.int32, (rows, rows), 0)
    c = lax.broadcasted_iota(jnp.int32, (rows, rows), 1)
    lower = jnp.where(r > c, 1.0, 0.0).astype(BF16)
    before = jnp.dot(lower, chosen.astype(BF16), preferred_element_type=F32)
    counts = jnp.sum(chosen, axis=0, keepdims=True)
    run_len = jnp.floor((counts + (SUBLANES - 1)) * (1.0 / SUBLANES)) * SUBLANES
    er = lax.broadcasted_iota(jnp.int32, (LANES, LANES), 0)
    ec = lax.broadcasted_iota(jnp.int32, (LANES, LANES), 1)
    earlier = jnp.where(er < ec, 1.0, 0.0).astype(BF16)
    run_start = jnp.dot(jnp.broadcast_to(run_len, (SUBLANES, LANES)).astype(BF16), earlier,
                        preferred_element_type=F32)[:1]
    slot = before + run_start
    idx_out = jnp.zeros((rows, LANES), jnp.int32)
    gate_out = jnp.zeros((rows, LANES), F32)
    for kk in range(TOP_K):
        pos = jnp.sum(jnp.where(sels[kk], slot, 0.0), axis=-1, keepdims=True).astype(jnp.int32)
        idx_out = jnp.where(lane == kk, ids[kk].astype(jnp.int32), idx_out)
        idx_out = jnp.where(lane == TOP_K + kk, pos, idx_out)
        gate_out = jnp.where(lane == kk, exps[kk] / denom, gate_out)
    idx_ref[...] = idx_out
    gate_ref[...] = gate_out
    cnt_ref[...] = jnp.broadcast_to(counts, cnt_ref.shape).astype(jnp.int32)


def _router(h, w_router, b_router):
    T = h.shape[0]
    wpad = jnp.zeros((D_MODEL, LANES), F32).at[:, :N_EXPERTS].set(w_router)
    w0 = wpad.astype(BF16)
    r1 = wpad - w0.astype(F32)
    w1 = r1.astype(BF16)
    w2 = (r1 - w1.astype(F32)).astype(BF16)
    bpad = jnp.zeros((1, LANES), F32).at[0, :N_EXPERTS].set(b_router)
    row = pl.BlockSpec((ROW_TILE, LANES), lambda i: (i, 0))
    return pl.pallas_call(
        _router_kernel,
        out_shape=(jax.ShapeDtypeStruct((T, LANES), jnp.int32),
                   jax.ShapeDtypeStruct((T, LANES), F32),
                   jax.ShapeDtypeStruct((T // ROW_TILE * SUBLANES, LANES), jnp.int32)),
        grid=(T // ROW_TILE,),
        in_specs=[pl.BlockSpec((ROW_TILE, D_MODEL), lambda i: (i, 0)),
                  _const_spec(w0.shape), _const_spec(w0.shape), _const_spec(w0.shape),
                  _const_spec(bpad.shape)],
        out_specs=(row, row, pl.BlockSpec((SUBLANES, LANES), lambda i: (i, 0))),
        compiler_params=_cparams("parallel"),
        name="router",
    )(h, w0, w1, w2, bpad)


def _run_copy(runs_ref, tile, e, stage, hbm, sem, to_hbm):
    base = (tile * N_EXPERTS + e) * 3
    s = pl.multiple_of(runs_ref[base], SUBLANES)
    g = pl.multiple_of(runs_ref[base + 1], SUBLANES)
    n = pl.multiple_of(runs_ref[base + 2], SUBLANES)
    src, dst = stage.at[pl.ds(s, n), :], hbm.at[pl.ds(g, n), :]
    if not to_hbm:
        src, dst = dst, src
    return n, pltpu.make_async_copy(src, dst, sem)


def _for_each_run(runs_ref, tile, stage, hbm, sem, to_hbm, action):
    def one(e, carry):
        n, cp = _run_copy(runs_ref, tile, e, stage, hbm, sem, to_hbm)

        @pl.when(n > 0)
        def _():
            getattr(cp, action)()

        return carry

    lax.fori_loop(0, N_EXPERTS, one, 0)


def _dispatch_kernel(runs_ref, gaps_ref, h_ref, idx_ref, xs_ref, stage, zeros, sems):
    i = pl.program_id(0)

    def gap_copy(e):
        g = pl.multiple_of(gaps_ref[2 * e], SUBLANES)
        n = pl.multiple_of(gaps_ref[2 * e + 1], SUBLANES)
        return n, pltpu.make_async_copy(zeros.at[pl.ds(0, n), :], xs_ref.at[pl.ds(g, n), :], sems.at[1])

    def for_each_gap(action):
        def one(e, carry):
            n, cp = gap_copy(e)

            @pl.when(n > 0)
            def _():
                getattr(cp, action)()

            return carry

        lax.fori_loop(0, N_EXPERTS, one, 0)

        def tail(b, carry):
            rows = pl.ds(pl.multiple_of(b * EXPERT_TILE, EXPERT_TILE), EXPERT_TILE)
            getattr(pltpu.make_async_copy(zeros, xs_ref.at[rows, :], sems.at[1]), action)()
            return carry

        lax.fori_loop(gaps_ref[2 * N_EXPERTS], xs_ref.shape[0] // EXPERT_TILE, tail, 0)

    @pl.when(i == 0)
    def _():
        zeros[...] = jnp.zeros(zeros.shape, zeros.dtype)
        for_each_gap("start")

    pos_t = idx_ref[...].astype(F32).T.astype(jnp.int32)
    h = h_ref[...]
    chunk = EXPERT_TILE
    row_id = lax.broadcasted_iota(jnp.int32, (chunk, ROW_TILE), 0)
    for c in range(STAGE_ROWS // chunk):
        onehot = jnp.zeros((chunk, ROW_TILE), F32)
        for kk in range(TOP_K):
            onehot = jnp.where(row_id == (pos_t[TOP_K + kk:TOP_K + kk + 1] - c * chunk), 1.0, onehot)
        stage[c * chunk:(c + 1) * chunk, :] = _pack_rows(
            jnp.dot(onehot.astype(BF16), h, preferred_element_type=F32))

    _for_each_run(runs_ref, i, stage, xs_ref, sems.at[0], True, "start")
    _for_each_run(runs_ref, i, stage, xs_ref, sems.at[0], True, "wait")

    @pl.when(i == 0)
    def _():
        for_each_gap("wait")


def _dispatch(runs, gaps, hb, idx, n_rows):
    T = hb.shape[0]
    grid_spec = pltpu.PrefetchScalarGridSpec(
        num_scalar_prefetch=2,
        grid=(T // ROW_TILE,),
        in_specs=[pl.BlockSpec((ROW_TILE, D_MODEL), lambda i, r, g: (i, 0)),
                  pl.BlockSpec((ROW_TILE, LANES), lambda i, r, g: (i, 0))],
        out_specs=pl.BlockSpec(memory_space=pl.ANY),
        scratch_shapes=[pltpu.VMEM((STAGE_ROWS, PACKED_W), jnp.uint32),
                        pltpu.VMEM((EXPERT_TILE, PACKED_W), jnp.uint32),
                        pltpu.SemaphoreType.DMA((2,))],
    )
    return pl.pallas_call(
        _dispatch_kernel,
        out_shape=jax.ShapeDtypeStruct((n_rows, PACKED_W), jnp.uint32),
        grid_spec=grid_spec,
        compiler_params=_cparams("arbitrary"),
        name="dispatch_rows",
    )(runs, gaps, hb, idx)


def _expert_kernel(blk_e_ref, nact_ref, next_e_ref, x_ref, bu_ref, bd_ref, wu_hbm, wd_hbm, y_ref,
                   wuf_scr, wdf_scr, wub_scr, wdb_scr, sems):
    i = pl.program_id(0)
    e = blk_e_ref[i]

    def weight_copies(expert):
        return (pltpu.make_async_copy(wu_hbm.at[expert], wuf_scr, sems.at[0]),
                pltpu.make_async_copy(wd_hbm.at[expert], wdf_scr, sems.at[1]))

    @pl.when(i == 0)
    def _():
        for cp in weight_copies(e):
            cp.start()

    first_of_group = jnp.logical_or(i == 0, e != blk_e_ref[jnp.maximum(i - 1, 0)])

    @pl.when(jnp.logical_and(i < nact_ref[0], first_of_group))
    def _():
        for cp in weight_copies(e):
            cp.wait()
        wub_scr[...] = wuf_scr[...].astype(BF16)
        wdb_scr[...] = wdf_scr[...].astype(BF16)

        @pl.when(next_e_ref[i] >= 0)
        def _():
            for cp in weight_copies(next_e_ref[i]):
                cp.start()

    @pl.when(i < nact_ref[0])
    def _():
        lo, hi = _unpack_rows(x_ref[...])
        x = jnp.concatenate([lo.astype(BF16), hi.astype(BF16)], axis=-1)
        acc = jnp.zeros((EXPERT_TILE, D_MODEL), F32)
        step = 256
        for c in range(D_EXPERT // step):
            glu = jnp.dot(x, wub_scr[:, c * step:(c + 1) * step], preferred_element_type=F32)
            lin = jnp.dot(x, wub_scr[:, D_EXPERT + c * step:D_EXPERT + (c + 1) * step],
                          preferred_element_type=F32)
            glu = jnp.minimum(glu + bu_ref[0, :, c * step:(c + 1) * step], SWIGLU_LIMIT)
            lin = jnp.clip(lin + bu_ref[0, :, D_EXPERT + c * step:D_EXPERT + (c + 1) * step],
                           -SWIGLU_LIMIT, SWIGLU_LIMIT)
            a = glu * _sigmoid(SWIGLU_ALPHA * glu) * (lin + 1.0)
            acc = acc + jnp.dot(a.astype(BF16), wdb_scr[c * step:(c + 1) * step, :],
                                preferred_element_type=F32)
        y_ref[...] = _pack_rows(acc + bd_ref[0])

    @pl.when(i >= nact_ref[0])
    def _():
        y_ref[...] = jnp.zeros(y_ref.shape, y_ref.dtype)


def _experts(blk_e, nact, next_e, xs, w_up, b_up, w_down, b_down):
    P = xs.shape[0]
    n_blocks = P // EXPERT_TILE
    act = lambda i, na: jnp.minimum(i, na[0] - 1)
    grid_spec = pltpu.PrefetchScalarGridSpec(
        num_scalar_prefetch=3,
        grid=(n_blocks,),
        in_specs=[pl.BlockSpec((EXPERT_TILE, PACKED_W), lambda i, be, na, ne: (act(i, na), 0)),
                  pl.BlockSpec((1, 1, 2 * D_EXPERT), lambda i, be, na, ne: (be[act(i, na)], 0, 0)),
                  pl.BlockSpec((1, 1, D_MODEL), lambda i, be, na, ne: (be[act(i, na)], 0, 0)),
                  pl.BlockSpec(memory_space=pl.ANY),
                  pl.BlockSpec(memory_space=pl.ANY)],
        out_specs=pl.BlockSpec((EXPERT_TILE, PACKED_W), lambda i, be, na, ne: (i, 0)),
        scratch_shapes=[pltpu.VMEM((D_MODEL, 2 * D_EXPERT), F32),
                        pltpu.VMEM((D_EXPERT, D_MODEL), F32),
                        pltpu.VMEM((D_MODEL, 2 * D_EXPERT), BF16),
                        pltpu.VMEM((D_EXPERT, D_MODEL), BF16),
                        pltpu.SemaphoreType.DMA((2,))],
    )
    return pl.pallas_call(
        _expert_kernel,
        out_shape=jax.ShapeDtypeStruct((P, PACKED_W), jnp.uint32),
        grid_spec=grid_spec,
        compiler_params=_cparams("arbitrary"),
        name="routed_experts",
    )(blk_e, nact, next_e, xs, b_up.reshape(N_EXPERTS, 1, 2 * D_EXPERT),
      b_down.reshape(N_EXPERTS, 1, D_MODEL), w_up, w_down)


def _combine_kernel(runs_ref, h_ref, idx_ref, gate_ref, g_ref, b_ref, ys_ref, o_ref, ybuf, sems):
    i = pl.program_id(0)
    n = pl.num_programs(0)

    def gather(tile, slot, action):
        _for_each_run(runs_ref, tile, ybuf.at[slot], ys_ref, sems.at[slot], False, action)

    @pl.when(i == 0)
    def _():
        ybuf[...] = jnp.zeros(ybuf.shape, ybuf.dtype)
        gather(0, 0, "start")

    @pl.when(i + 1 < n)
    def _():
        gather(i + 1, (i + 1) % 2, "start")

    slot = i % 2
    gather(i, slot, "wait")

    idx = idx_ref[...]
    gate = gate_ref[...]
    col_id = lax.broadcasted_iota(jnp.int32, (ROW_TILE, STAGE_ROWS), 1)
    weights = jnp.zeros((ROW_TILE, STAGE_ROWS), F32)
    for kk in range(TOP_K):
        weights = jnp.where(col_id == idx[:, TOP_K + kk:TOP_K + kk + 1], gate[:, kk:kk + 1], weights)
    wb = weights.astype(BF16)
    lo, hi = _unpack_rows(ybuf[slot])
    f_lo = jnp.dot(wb, lo.astype(BF16), preferred_element_type=F32)
    f_hi = jnp.dot(wb, hi.astype(BF16), preferred_element_type=F32)
    y_lo = DN_ALPHA * h_ref[:, :PACKED_W] + f_lo
    y_hi = DN_ALPHA * h_ref[:, PACKED_W:] + f_hi
    mu = (jnp.sum(y_lo, axis=-1, keepdims=True) + jnp.sum(y_hi, axis=-1, keepdims=True)) / D_MODEL
    d_lo = y_lo - mu
    d_hi = y_hi - mu
    var = (jnp.sum(d_lo * d_lo, axis=-1, keepdims=True) + jnp.sum(d_hi * d_hi, axis=-1, keepdims=True)) / D_MODEL
    inv = lax.rsqrt(var + LN_EPS)
    o_ref[:, :PACKED_W] = d_lo * inv * g_ref[:, :PACKED_W] + b_ref[:, :PACKED_W]
    o_ref[:, PACKED_W:] = d_hi * inv * g_ref[:, PACKED_W:] + b_ref[:, PACKED_W:]


def _combine(runs, h, idx, gate, ys, g, b):
    T = h.shape[0]
    grid_spec = pltpu.PrefetchScalarGridSpec(
        num_scalar_prefetch=1,
        grid=(T // ROW_TILE,),
        in_specs=[pl.BlockSpec((ROW_TILE, D_MODEL), lambda i, r: (i, 0)),
                  pl.BlockSpec((ROW_TILE, LANES), lambda i, r: (i, 0)),
                  pl.BlockSpec((ROW_TILE, LANES), lambda i, r: (i, 0)),
                  pl.BlockSpec(g.shape, lambda i, r: (0, 0)),
                  pl.BlockSpec(b.shape, lambda i, r: (0, 0)),
                  pl.BlockSpec(memory_space=pl.ANY)],
        out_specs=pl.BlockSpec((ROW_TILE, D_MODEL), lambda i, r: (i, 0)),
        scratch_shapes=[pltpu.VMEM((2, STAGE_ROWS, PACKED_W), jnp.uint32),
                        pltpu.SemaphoreType.DMA((2,))],
    )
    return pl.pallas_call(
        _combine_kernel,
        out_shape=jax.ShapeDtypeStruct((T, D_MODEL), F32),
        grid_spec=grid_spec,
        compiler_params=_cparams("arbitrary"),
        name="combine_ln",
    )(runs, h, idx, gate, g, b, ys)


def _rot_cols(w):
    half = w.shape[-1] // 2
    return jnp.concatenate([-w[:, half:], w[:, :half]], axis=-1)


def _pack_w_in(w_in):
    lat = Q_LORA + KV_LORA
    w_kr = w_in[:, lat:lat + MLA_ROPE]
    return jnp.concatenate([w_in[:, lat + MLA_ROPE:], w_in[:, :lat + MLA_ROPE], _rot_cols(w_kr)],
                           axis=-1).astype(BF16)


def _pack_w_uq(w_uq):
    w = w_uq.reshape(Q_LORA, MLA_HEADS, MLA_NOPE + MLA_ROPE)
    rope = w[:, :, MLA_NOPE:]
    rot = jnp.concatenate([-rope[:, :, MLA_ROPE // 2:], rope[:, :, :MLA_ROPE // 2]], axis=-1)
    return jnp.concatenate([w, rot], axis=-1).reshape(Q_LORA, MLA_HEADS * MLA_QK).astype(BF16)


def _rope_freqs():
    ret = 1.0 / (10000.0 ** jnp.linspace(0.0, 1.0, RET_DK // 2, dtype=F32))
    mla = 1.0 / (ROPE_BASE ** (jnp.arange(0, MLA_ROPE, 2, dtype=F32) / MLA_ROPE))
    return jnp.concatenate([ret, mla, jnp.zeros((LANES - ret.shape[0] - mla.shape[0],), F32)])[None, :]


def _routing_plan(cnt, n_tokens):
    counts = cnt[::SUBLANES, :N_EXPERTS]
    n_tiles = counts.shape[0]
    run_len = ((counts + SUBLANES - 1) // SUBLANES) * SUBLANES
    stage_off = jnp.cumsum(run_len, axis=1) - run_len
    tile_off = jnp.cumsum(run_len, axis=0) - run_len
    rows_used = jnp.sum(run_len, axis=0)
    padded = ((rows_used + EXPERT_TILE - 1) // EXPERT_TILE) * EXPERT_TILE
    pad_end = jnp.cumsum(padded)
    pad_start = pad_end - padded
    runs = jnp.stack([stage_off, pad_start[None, :] + tile_off, run_len], axis=-1).reshape(-1).astype(jnp.int32)
    nact = (pad_end[-1:] // EXPERT_TILE).astype(jnp.int32)
    gaps = jnp.concatenate([jnp.stack([pad_start + rows_used, padded - rows_used], axis=-1).reshape(-1),
                            nact]).astype(jnp.int32)
    n_rows = n_tokens * TOP_K + (SUBLANES - 1) * n_tiles * N_EXPERTS + N_EXPERTS * EXPERT_TILE
    n_rows = -(-n_rows // EXPERT_TILE) * EXPERT_TILE
    n_blocks = n_rows // EXPERT_TILE
    experts = jnp.arange(N_EXPERTS, dtype=jnp.int32)
    blk_row = jnp.arange(n_blocks, dtype=jnp.int32) * EXPERT_TILE
    blk_e = jnp.minimum(jnp.sum((pad_end[None, :] <= blk_row[:, None]).astype(jnp.int32), axis=-1),
                        N_EXPERTS - 1)
    later = jnp.where((rows_used > 0)[None, :] & (experts[None, :] > experts[:, None]), experts[None, :], N_EXPERTS)
    nxt = jnp.min(later, axis=-1)
    nxt = jnp.where(nxt == N_EXPERTS, -1, nxt)
    next_e = jnp.sum(jnp.where(blk_e[:, None] == experts[None, :], nxt[None, :], 0), axis=-1).astype(jnp.int32)
    return runs, gaps, blk_e, nact, next_e, n_rows


def kernel(x, positions, w_in, q_norm_g, w_uq, kv_norm_g, w_ukv, w_o, ln1_g, ln1_b,
           w_router, b_router, w_up, b_up, w_down, b_down, ln2_g, ln2_b):
    B, S, _ = x.shape
    T = B * S
    pos_col = positions.reshape(T, 1).astype(jnp.int32)
    rc, rs, mc, ms = _rope_tables(pos_col, _rope_freqs())
    dec, qw, kw, cd = _retention_constants()
    h = x.reshape(T, D_MODEL)
    for l in range(DEPTH):
        proj = _input_projection(h, _pack_w_in(w_in[l]))
        w_kv = w_ukv[l].reshape(KV_LORA, MLA_HEADS, MLA_NOPE + MLA_V)
        w_k = w_kv[:, :, :MLA_NOPE].reshape(KV_LORA, MLA_HEADS * MLA_NOPE).astype(BF16)
        w_vt = w_kv[:, :, MLA_NOPE:].reshape(KV_LORA, MLA_HEADS * MLA_V).T.astype(BF16)
        q_cat, kn, vt, kr = _mla_prep(proj, mc, ms, _pack_w_uq(w_uq[l]), w_k, w_vt,
                                      q_norm_g[l][None, :], kv_norm_g[l][None, :])
        o_mla = _attention(q_cat, kn, kr, vt, B, S)
        o_ret = _retention(proj, rc, rs, dec, qw, kw, cd, B, S)
        h1, h1b = _merge(h, o_mla, o_ret, proj, w_o[l].astype(BF16), ln1_g[l][None, :], ln1_b[l][None, :])
        idx_out, gate, cnt = _router(h1, w_router[l], b_router[l])
        runs, gaps, blk_e, nact, next_e, n_rows = _routing_plan(cnt, T)
        xs = _dispatch(runs, gaps, h1b, idx_out, n_rows)
        ys = _experts(blk_e, nact, next_e, xs, w_up[l], b_up[l], w_down[l], b_down[l])
        h = _combine(runs, h1, idx_out, gate, ys, ln2_g[l][None, :], ln2_b[l][None, :])
    return h.reshape(B, S, D_MODEL)
```

```python
import functools

import numpy as np
import jax
import jax.numpy as jnp
from jax import lax
from jax.experimental import pallas as pl
from jax.experimental.pallas import tpu as pltpu

F32 = jnp.float32
BF16 = jnp.bfloat16

D_MODEL = 1024
DEPTH = 1
MLA_HEADS = 8
MLA_NOPE = 128
MLA_ROPE = 64
MLA_V = 128
Q_LORA = 384
KV_LORA = 256
ROPE_BASE = 10000.0
RET_HEADS = 8
RET_DK = 128
RET_DV = 128
RET_CHUNK = 128
N_EXPERTS = 32
TOP_K = 4
D_EXPERT = 1024
SWIGLU_LIMIT = 7.0
SWIGLU_ALPHA = 1.702
DN_ALPHA = (2.0 * DEPTH) ** 0.25
LN_EPS = 1e-5
RMS_EPS = 1e-6
GN_EPS = 1e-6
NEG_INF = -1e30

LANES = 128
SUBLANES = 8
MLA_QK = 2 * LANES
PACKED_W = D_MODEL // 2
LATENT_W = 768
N_WIDE = 6 * D_MODEL
N_PROJ = N_WIDE + LATENT_W
VMEM_LIMIT = 56 * 1024 * 1024

ROW_TILE = 512
ATTN_TILE = ROW_TILE
ATTN_Q_TILE = 2 * ATTN_TILE
ATTN_Q_SPLIT = 4
ATTN_SWEEP = 4
VT_ROWS = MLA_V + 16
LOG2_E = 1.4426950408889634
RET_GROUP = 8
EXPERT_TILE = 256
STAGE_ROWS = -(-(ROW_TILE * TOP_K + (SUBLANES - 1) * N_EXPERTS) // EXPERT_TILE) * EXPERT_TILE


def _cparams(*sem):
    return pltpu.CompilerParams(dimension_semantics=sem, vmem_limit_bytes=VMEM_LIMIT)


def _const_spec(shape):
    nd = len(shape)
    return pl.BlockSpec(shape, lambda *_: (0,) * nd, pipeline_mode=pl.Buffered(1))


def _sigmoid(x):
    return 1.0 / (1.0 + jnp.exp(-x))


def _pack_rows(x):
    n = x.shape[-1] // 2
    xb = x.astype(BF16).astype(F32)
    lo = pltpu.bitcast(xb[:, :n], jnp.uint32) >> 16
    hi = pltpu.bitcast(xb[:, n:], jnp.uint32) & jnp.uint32(0xFFFF0000)
    return lo | hi


def _unpack_rows(w):
    lo = pltpu.bitcast(w << 16, F32)
    hi = pltpu.bitcast(w & jnp.uint32(0xFFFF0000), F32)
    return lo, hi


def _swap_halves(x):
    return pltpu.roll(x, LANES // 2, 1)


def _proj_kernel(x_ref, w_ref, o_ref):
    xb = x_ref[...].astype(BF16)
    for c in range(N_PROJ // LATENT_W):
        sl = slice(c * LATENT_W, (c + 1) * LATENT_W)
        o_ref[:, sl] = jnp.dot(xb, w_ref[:, sl], preferred_element_type=F32).astype(o_ref.dtype)


def _input_projection(x2, w_cat):
    T = x2.shape[0]
    return pl.pallas_call(
        _proj_kernel,
        out_shape=jax.ShapeDtypeStruct((T, N_PROJ), BF16),
        grid=(T // ROW_TILE,),
        in_specs=[pl.BlockSpec((ROW_TILE, D_MODEL), lambda i: (i, 0)),
                  _const_spec((D_MODEL, N_PROJ))],
        out_specs=pl.BlockSpec((ROW_TILE, N_PROJ), lambda i: (i, 0)),
        compiler_params=_cparams("parallel"),
        name="input_projection",
    )(x2, w_cat)


def _rope_table_kernel(pos_ref, freq_ref, rc_ref, rs_ref, mc_ref, ms_ref):
    ang = pos_ref[...].astype(F32) * freq_ref[...]
    c = jnp.cos(ang)
    s = jnp.sin(ang)
    lane = lax.broadcasted_iota(jnp.int32, ang.shape, 1)
    c64 = pltpu.roll(c, 64, 1)
    s64 = pltpu.roll(s, 64, 1)
    rc_ref[...] = jnp.where(lane < 64, c, c64)
    rs_ref[...] = jnp.where(lane < 64, -s, s64)
    c96 = pltpu.roll(c, 96, 1)
    s96 = pltpu.roll(s, 96, 1)
    mc_ref[...] = jnp.where(lane < 32, c64, jnp.where(lane < 64, c96, 0.0))
    ms_ref[...] = jnp.where(lane < 32, s64, jnp.where(lane < 64, s96, 0.0))


def _rope_tables(pos_col, freq_row):
    T = pos_col.shape[0]
    rows = min(2048, T)
    tab =jax.ShapeDtypeStruct((T, LANES), F32)
    spec = pl.BlockSpec((rows, LANES), lambda i: (i, 0))
    return pl.pallas_call(
        _rope_table_kernel,
        out_shape=(tab, tab, tab, tab),
        grid=(T // rows,),
        in_specs=[pl.BlockSpec((rows, 1), lambda i: (i, 0)), _const_spec((1, LANES))],
        out_specs=(spec, spec, spec, spec),
        compiler_params=_cparams("parallel"),
        name="rope_tables",
    )(pos_col, freq_row)


def _mla_prep_kernel(a_ref, mc_ref, ms_ref, wq_ref, wk_ref, wvt_ref, qg_ref, kvg_ref,
                     q_ref, kn_ref, vt_ref, kr_ref):
    a = a_ref[...].astype(F32)
    mc = mc_ref[...]
    ms = ms_ref[...]
    cq = a[:, :Q_LORA]
    qn = cq * lax.rsqrt(jnp.mean(cq * cq, axis=-1, keepdims=True) + RMS_EPS) * qg_ref[...]
    q = jnp.dot(qn.astype(BF16), wq_ref[...], preferred_element_type=F32)
    scale = (MLA_NOPE + MLA_ROPE) ** -0.5 * LOG2_E
    for h in range(MLA_HEADS):
        nope = q[:, h * MLA_QK:h * MLA_QK + LANES]
        blk = q[:, h * MLA_QK + LANES:(h + 1) * MLA_QK]
        rope = blk * mc + _swap_halves(blk) * ms
        q_ref[:, h * MLA_QK:h * MLA_QK + LANES] = (nope * scale).astype(BF16)
        q_ref[:, h * MLA_QK + LANES:(h + 1) * MLA_QK] = (rope * scale).astype(BF16)
    ckv = a[:, Q_LORA:Q_LORA + KV_LORA]
    kvn = ckv * lax.rsqrt(jnp.mean(ckv * ckv, axis=-1, keepdims=True) + RMS_EPS) * kvg_ref[...]
    kvb = kvn.astype(BF16)
    kn_ref[...] = jnp.dot(kvb, wk_ref[...], preferred_element_type=F32).astype(BF16)
    vt = lax.dot_general(wvt_ref[...], kvb, (((1,), (1,)), ((), ())),
                         preferred_element_type=F32).astype(BF16)
    ones = jnp.ones((VT_ROWS - MLA_V, vt.shape[1]), BF16)
    for h in range(MLA_HEADS):
        vt_ref[0, h * VT_ROWS:h * VT_ROWS + MLA_V, :] = vt[h * MLA_V:(h + 1) * MLA_V, :]
        vt_ref[0, h * VT_ROWS + MLA_V:(h + 1) * VT_ROWS, :] = ones
    krb =a[:, Q_LORA + KV_LORA:]
    kr_ref[...] = (krb * mc + _swap_halves(krb) * ms).astype(BF16)


def _mla_prep(proj, mc, ms, wq, wk, wvt, qg, kvg):
    T = proj.shape[0]
    row = lambda w: pl.BlockSpec((ROW_TILE, w), lambda i: (i, 0))
    return pl.pallas_call(
        _mla_prep_kernel,
        out_shape=(jax.ShapeDtypeStruct((T, MLA_HEADS * MLA_QK), BF16),
                   jax.ShapeDtypeStruct((T, MLA_HEADS * MLA_NOPE), BF16),
                   jax.ShapeDtypeStruct((T // ROW_TILE, MLA_HEADS * VT_ROWS, ROW_TILE), BF16),
                   jax.ShapeDtypeStruct((T, LANES), BF16)),
        grid=(T // ROW_TILE,),
        in_specs=[pl.BlockSpec((ROW_TILE, LATENT_W), lambda i: (i, N_WIDE // LATENT_W)),
                  row(LANES), row(LANES),
                  _const_spec(wq.shape), _const_spec(wk.shape), _const_spec(wvt.shape),
                  _const_spec(qg.shape), _const_spec(kvg.shape)],
        out_specs=(row(MLA_HEADS * MLA_QK), row(MLA_HEADS * MLA_NOPE),
                   pl.BlockSpec((1, MLA_HEADS * VT_ROWS, ROW_TILE), lambda i: (i, 0, 0)), row(LANES)),
        compiler_params=_cparams("parallel"),
        name="mla_prep",
    )(proj, mc, ms, wq, wk, wvt, qg, kvg)


def _attn_kernel(q_ref, kn_ref, kr_ref, vt_ref, o_ref, acc_scr):
    qi = pl.program_id(2)
    W = ATTN_Q_TILE // ATTN_Q_SPLIT
    n_diag = ATTN_Q_TILE // ATTN_TILE
    groups = tuple(range(ATTN_Q_SPLIT))
    qs = [q_ref[c * W:(c + 1) * W, :] for c in groups]
    acc_scr[...] = jnp.zeros(acc_scr.shape, F32)

    def visible(d, c):
        return min(max((c + 1) * W - d * ATTN_TILE, 0), ATTN_TILE)

    def scores(j, which=groups):
        rows = pl.ds(pl.multiple_of(j * ATTN_TILE, ATTN_TILE), ATTN_TILE)
        k = jnp.concatenate([kn_ref[rows, :], kr_ref[rows, :]], axis=-1)
        return {c: lax.dot_general(k, qs[c], (((1,), (1,)), ((), ())), preferred_element_type=F32)
                for c in which}

    def softmax_pv(j, s_all, m_all, diag=None):
        vt = vt_ref[j]
        out = list(m_all)
        for c in s_all:
            s = s_all[c]
            nk = ATTN_TILE if diag is None else visible(diag, c)
            if diag is not None:
                s = s[:nk]
                if diag * ATTN_TILE + nk - 1 > c * W:
                    kp = lax.broadcasted_iota(jnp.int32, s.shape, 0) + diag * ATTN_TILE
                    qp = lax.broadcasted_iota(jnp.int32, s.shape, 1) + c * W
                    s = jnp.where(kp <= qp, s, NEG_INF)
            m_new = jnp.maximum(m_all[c], jnp.max(s, axis=0, keepdims=True))
            alpha = jnp.exp2(m_all[c] - m_new)
            p = jnp.exp2(s - m_new).astype(BF16)
            acc_scr[c] = alpha * acc_scr[c] + jnp.dot(vt[:, :nk], p, preferred_element_type=F32)
            out[c] = m_new
        return tuple(out)

    def sweep(tiles, m_all):
        pending = [scores(t) for t in tiles[:2]]
        for k, t in enumerate(tiles):
            m_all = softmax_pv(t, pending.pop(0), m_all)
            if k + 2 < len(tiles):
                pending.append(scores(tiles[k + 2]))
        return m_all

    def body(width):
        return lambda jj, m_all: sweep([jj * width + u for u in range(width)], m_all)

    first = n_diag * qi
    init = tuple(jnp.full((1, W), NEG_INF, F32) for _ in groups)
    n_long = first // ATTN_SWEEP
    m_all = lax.fori_loop(0, n_long, body(ATTN_SWEEP), init)
    m_all = lax.fori_loop(n_long * (ATTN_SWEEP // n_diag), qi, body(n_diag), m_all)
    s_diag = [scores(first + d, tuple(c for c in groups if visible(d, c) > 0)) for d in range(n_diag)]
    for d in range(n_diag):
        m_all = softmax_pv(first + d, s_diag[d], m_all, diag=d)
    for c in groups:
        acc = acc_scr[c]
        o_ref[c * W:(c + 1) * W, :] = (acc[:MLA_V] / acc[MLA_V:MLA_V + 1]).T.astype(o_ref.dtype)


def _attention(q_cat, kn, kr, vt, B, S):
    T = B * S
    nq = S // ATTN_Q_TILE
    return pl.pallas_call(
        _attn_kernel,
        out_shape=jax.ShapeDtypeStruct((T, MLA_HEADS * MLA_V), BF16),
        grid=(B, MLA_HEADS, nq),
        in_specs=[pl.BlockSpec((ATTN_Q_TILE, MLA_QK), lambda b, h, i: (b * nq + i, h)),
                  pl.BlockSpec((S, MLA_NOPE), lambda b, h, i: (b, h)),
                  pl.BlockSpec((S, LANES), lambda b, h, i: (b, 0)),
                  pl.BlockSpec((S // ATTN_TILE, VT_ROWS, ATTN_TILE), lambda b, h, i: (b, h, 0))],
        out_specs=pl.BlockSpec((ATTN_Q_TILE, MLA_V), lambda b, h, i: (b * nq + i, h)),
        scratch_shapes=[pltpu.VMEM((ATTN_Q_SPLIT, VT_ROWS, ATTN_Q_TILE // ATTN_Q_SPLIT), F32)],
        compiler_params=_cparams("parallel", "parallel", "arbitrary"),
        name="mla_attention",
    )(q_cat, kn, kr, vt)


def _ret_kernel(q_ref, k_ref, v_ref, g_ref, rc_ref, rs_ref, dec_ref, qw_ref, kw_ref, cd_ref,
                o_ref, qb_scr, qwb_scr, kb_scr, kwb_scr):
    S = q_ref.shape[0]
    C = RET_CHUNK
    n_chunks = S // C
    rc = rc_ref[...]
    rs = rs_ref[...]
    q = q_ref[...].astype(F32)
    k = k_ref[...].astype(F32)
    q = q * rc + _swap_halves(q) * rs
    k = (k * rc + _swap_halves(k) * rs) * (RET_DK ** -0.5)
    qw = qw_ref[0]
    kw = kw_ref[0]
    q3 = q.reshape(n_chunks, C, RET_DK)
    k3 = k.reshape(n_chunks, C, RET_DK)
    qb_scr[...] = q.astype(BF16)
    kb_scr[...] = k.astype(BF16)
    qwb_scr[...] = (q3 * qw[None]).reshape(S, RET_DK).astype(BF16)
    kwb_scr[...] = (k3 * kw[None]).reshape(S, RET_DK)
    dec = dec_ref[0]
    cd = cd_ref[0]

    def group(gi, R):
        rows_of = [pl.ds(pl.multiple_of((gi * RET_GROUP + u) * C, C), C) for u in range(RET_GROUP)]
        scores, updates = [], []
        for rows in rows_of:
            scores.append(lax.dot_general(qb_scr[rows, :], kb_scr[rows, :], (((1,), (1,)), ((), ())),
                                          preferred_element_type=F32))
            kwt = kwb_scr[rows, :].T.astype(BF16)
            updates.append(jnp.dot(kwt, v_ref[rows, :], preferred_element_type=F32))
        states = []
        for u in range(RET_GROUP):
            states.append(R.astype(BF16))
            R = cd * R + updates[u]
        for u, rows in enumerate(rows_of):
            o = jnp.dot((scores[u] * dec).astype(BF16), v_ref[rows, :], preferred_element_type=F32)
            o = o + jnp.dot(qwb_scr[rows, :], states[u], preferred_element_type=F32)
            mu = jnp.mean(o, axis=-1, keepdims=True)
            d = o - mu
            var = jnp.mean(d * d, axis=-1, keepdims=True)
            gate = g_ref[rows, :].astype(F32)
            o_ref[rows, :] = (d * lax.rsqrt(var + GN_EPS) * (gate * _sigmoid(gate))).astype(o_ref.dtype)
        return R

    lax.fori_loop(0, n_chunks // RET_GROUP, group, jnp.zeros((RET_DK, RET_DV), F32))


def _retention(proj, rc, rs, dec, qw, kw, cd, B, S):
    T = B * S
    H = RET_HEADS
    col = lambda off: pl.BlockSpec((S, LANES), lambda b, h: (b, off + h))
    tab = pl.BlockSpec((S, LANES), lambda b, h: (b, 0))
    per_head = lambda r: pl.BlockSpec((1, r, LANES), lambda b, h: (h, 0, 0))
    return pl.pallas_call(
        _ret_kernel,
        out_shape=jax.ShapeDtypeStruct((T, H * RET_DV), BF16),
        grid=(B, H),
        in_specs=[col(0), col(H), col(2 * H), col(3 * H), tab, tab,
                  per_head(RET_CHUNK), per_head(RET_CHUNK), per_head(RET_CHUNK), per_head(1)],
        out_specs=pl.BlockSpec((S, RET_DV), lambda b, h: (b, h)),
        scratch_shapes=[pltpu.VMEM((S, RET_DK), BF16), pltpu.VMEM((S, RET_DK), BF16),
                        pltpu.VMEM((S, RET_DK), BF16), pltpu.VMEM((S, RET_DK), F32)],
        compiler_params=_cparams("parallel", "parallel"),
        name="retention",
    )(proj, proj, proj, proj, rc, rs, dec, qw, kw, cd)


def _retention_constants():
    C = RET_CHUNK
    h = np.arange(RET_HEADS, dtype=np.float64)
    log_gamma = np.log(1.0 - 2.0 ** (-5.0 - h))
    idx = np.arange(C, dtype=np.float64)
    rel = idx[:, None] - idx[None, :]
    dec = np.where(rel[None] >= 0, np.exp(np.maximum(rel, 0.0)[None] * log_gamma[:, None, None]), 0.0)
    qw = np.exp((idx + 1.0)[None, :] * log_gamma[:, None])
    kw = np.exp((C - 1.0 - idx)[None, :] * log_gamma[:, None])
    cd = np.exp(C * log_gamma)
    bc = lambda a: np.ascontiguousarray(np.broadcast_to(a[..., None], a.shape + (LANES,))).astype(np.float32)
    return dec.astype(np.float32), bc(qw), bc(kw), bc(cd[:, None])


def _layernorm(y, g, b):
    mu = jnp.mean(y, axis=-1, keepdims=True)
    d = y - mu
    var = jnp.mean(d * d, axis=-1, keepdims=True)
    return d * lax.rsqrt(var + LN_EPS) * g + b


def _merge_kernel(x_ref, om_ref, or_ref, gm_ref, gr_ref, wo_ref, g_ref, b_ref, w01_ref, rb_ref,
                  h_ref, hb_ref, idx_ref, gate_ref, cnt_ref):
    mixed = (_sigmoid(gm_ref[...].astype(F32)) * om_ref[...].astype(F32)
             + _sigmoid(gr_ref[...].astype(F32)) * or_ref[...].astype(F32))
    mix = jnp.dot(mixed.astype(BF16), wo_ref[...], preferred_element_type=F32)
    h = _layernorm(DN_ALPHA * x_ref[...] + mix, g_ref[...], b_ref[...])
    h_ref[...] = h
    hb_ref[...] = h.astype(BF16)
    _route_tile(h, w01_ref, rb_ref, idx_ref, gate_ref, cnt_ref)


def _merge_and_route(x2, o_mla, o_ret, proj, wo, g, b, w_router, b_router):
    T = x2.shape[0]
    wpad = jnp.zeros((D_MODEL, LANES), F32).at[:, :N_EXPERTS].set(w_router)
    w0 = wpad.astype(BF16)
    w01 = jnp.concatenate([w0, (wpad - w0.astype(F32)).astype(BF16)], axis=-1)
    bpad = jnp.zeros((1, LANES), F32).at[0, :N_EXPERTS].set(b_router)
    row = lambda c: pl.BlockSpec((ROW_TILE, D_MODEL), lambda i: (i, c))
    lanes = pl.BlockSpec((ROW_TILE, LANES), lambda i: (i, 0))
    return pl.pallas_call(
        _merge_kernel,
        out_shape=(jax.ShapeDtypeStruct((T, D_MODEL), F32), jax.ShapeDtypeStruct((T, D_MODEL), BF16),
                   jax.ShapeDtypeStruct((T, LANES), jnp.int32), jax.ShapeDtypeStruct((T, LANES), F32),
                   jax.ShapeDtypeStruct((T // ROW_TILE * SUBLANES, LANES), jnp.int32)),
        grid=(T // ROW_TILE,),
        in_specs=[row(0), row(0), row(0), row(4), row(5),
                  _const_spec(wo.shape), _const_spec(g.shape), _const_spec(b.shape),
                  _const_spec(w01.shape), _const_spec(bpad.shape)],
        out_specs=(row(0), row(0), lanes, lanes, pl.BlockSpec((SUBLANES, LANES), lambda i: (i, 0))),
        compiler_params=_cparams("parallel"),
        name="merge_out_proj_ln_route",
    )(x2, o_mla, o_ret, proj, proj, wo, g, b, w01, bpad)


def _split2(a):
    a0 = a.astype(BF16)
    return a0, (a - a0.astype(F32)).astype(BF16)


def _route_tile(h, w01_ref, b_ref, idx_ref, gate_ref, cnt_ref):
    h0, h1 = _split2(h)
    p0 = jnp.dot(h0, w01_ref[...], preferred_element_type=F32)
    p1 = jnp.dot(h1, w01_ref[:, :LANES], preferred_element_type=F32)
    logits = p0[:, :LANES] + (p0[:, LANES:] + p1) + b_ref[...]
    rows = logits.shape[0]
    lane = lax.broadcasted_iota(jnp.int32, (rows, LANES), 1)
    lane_f = lane.astype(F32)
    work = jnp.where(lane < N_EXPERTS, logits, -jnp.inf)
    vals, ids, sels = [], [], []
    for _ in range(TOP_K):
        m = jnp.max(work, axis=-1, keepdims=True)
        first = jnp.min(jnp.where(work == m, lane_f, float(LANES)), axis=-1, keepdims=True)
        sel = lane_f == first
        work = jnp.where(sel, -jnp.inf, work)
        vals.append(m)
        ids.append(first)
        sels.append(sel)
    exps = [jnp.exp(v - vals[0]) for v in vals]
    denom = exps[0] + exps[1] + exps[2] + exps[3]
    chosen = jnp.zeros((rows, LANES), F32)
    for sel in sels:
        chosen = jnp.where(sel, 1.0, chosen)
    r = lax.broadcasted_iota(jnp.int32, (rows, rows), 0)
    c = lax.broadcasted_iota(jnp.int32, (rows, rows), 1)
    lower = jnp.where(r > c, 1.0, 0.0).astype(BF16)
    before = jnp.dot(lower, chosen.astype(BF16), preferred_element_type=F32)
    counts = jnp.sum(chosen, axis=0, keepdims=True)
    run_len = jnp.floor((counts + (SUBLANES - 1)) * (1.0 / SUBLANES)) * SUBLANES
    er = lax.broadcasted_iota(jnp.int32, (LANES, LANES), 0)
    ec = lax.broadcasted_iota(jnp.int32, (LANES, LANES), 1)
    earlier = jnp.where(er < ec, 1.0, 0.0).astype(BF16)
    run_start = jnp.dot(jnp.broadcast_to(run_len, (SUBLANES, LANES)).astype(BF16), earlier,
                        preferred_element_type=F32)[:1]
    slot = before + run_start
    idx_out = jnp.zeros((rows, LANES), jnp.int32)
    gate_out = jnp.zeros((rows, LANES), F32)
    for kk in range(TOP_K):
        pos = jnp.sum(jnp.where(sels[kk], slot, 0.0), axis=-1, keepdims=True).astype(jnp.int32)
        idx_out = jnp.where(lane == kk, ids[kk].astype(jnp.int32), idx_out)
        idx_out = jnp.where(lane == TOP_K + kk, pos, idx_out)
        gate_out = jnp.where(lane == kk, exps[kk] / denom, gate_out)
    idx_ref[...] = idx_out
    gate_ref[...] = gate_out
    cnt_ref[...] = jnp.broadcast_to(counts, cnt_ref.shape).astype(jnp.int32)


def _run_copy(runs_ref, tile, e, stage, hbm, sem, to_hbm):
    base = (tile * N_EXPERTS + e) * 3
    s = pl.multiple_of(runs_ref[base], SUBLANES)
    g = pl.multiple_of(runs_ref[base + 1], SUBLANES)
    n = pl.multiple_of(runs_ref[base + 2], SUBLANES)
    src, dst = stage.at[pl.ds(s, n), :], hbm.at[pl.ds(g, n), :]
    if not to_hbm:
        src, dst = dst, src
    return n, pltpu.make_async_copy(src, dst, sem)


def _for_each_run(runs_ref, tile, stage, hbm, sem, to_hbm, action):
    def one(e, carry):
        n, cp = _run_copy(runs_ref, tile, e, stage, hbm, sem, to_hbm)

        @pl.when(n > 0)
        def _():
            getattr(cp, action)()

        return carry

    lax.fori_loop(0, N_EXPERTS, one, 0)


def _dispatch_kernel(runs_ref, gaps_ref, h_ref, idx_ref, xs_ref, stage, zeros, sems):
    i = pl.program_id(0)

    def gap_copy(e):
        g = pl.multiple_of(gaps_ref[2 * e], SUBLANES)
        n = pl.multiple_of(gaps_ref[2 * e + 1], SUBLANES)
        return n, pltpu.make_async_copy(zeros.at[pl.ds(0, n), :], xs_ref.at[pl.ds(g, n), :], sems.at[1])

    def for_each_gap(action):
        def one(e, carry):
            n, cp = gap_copy(e)

            @pl.when(n > 0)
            def _():
                getattr(cp, action)()

            return carry

        lax.fori_loop(0, N_EXPERTS, one, 0)

        def tail(b, carry):
            rows = pl.ds(pl.multiple_of(b * EXPERT_TILE, EXPERT_TILE), EXPERT_TILE)
            getattr(pltpu.make_async_copy(zeros, xs_ref.at[rows, :], sems.at[1]), action)()
            return carry

        lax.fori_loop(gaps_ref[2 * N_EXPERTS], xs_ref.shape[0] // EXPERT_TILE, tail, 0)

    @pl.when(i == 0)
    def _():
        zeros[...] = jnp.zeros(zeros.shape, zeros.dtype)
        for_each_gap("start")

    pos_t = idx_ref[...].astype(F32).T.astype(jnp.int32)
    h = h_ref[...]
    chunk = EXPERT_TILE
    row_id = lax.broadcasted_iota(jnp.int32, (chunk, ROW_TILE), 0)
    for c in range(STAGE_ROWS // chunk):
        onehot = jnp.zeros((chunk, ROW_TILE), F32)
        for kk in range(TOP_K):
            onehot = jnp.where(row_id == (pos_t[TOP_K + kk:TOP_K + kk + 1] - c * chunk), 1.0, onehot)
        stage[c * chunk:(c + 1) * chunk, :] = _pack_rows(
            jnp.dot(onehot.astype(BF16), h, preferred_element_type=F32))

    _for_each_run(runs_ref, i, stage, xs_ref, sems.at[0], True, "start")
    _for_each_run(runs_ref, i, stage, xs_ref, sems.at[0], True, "wait")

    @pl.when(i == 0)
    def _():
        for_each_gap("wait")


def _dispatch(runs, gaps, hb, idx, n_rows):
    T = hb.shape[0]
    grid_spec = pltpu.PrefetchScalarGridSpec(
        num_scalar_prefetch=2,
        grid=(T // ROW_TILE,),
        in_specs=[pl.BlockSpec((ROW_TILE, D_MODEL), lambda i, r, g: (i, 0)),
                  pl.BlockSpec((ROW_TILE, LANES), lambda i, r, g: (i, 0))],
        out_specs=pl.BlockSpec(memory_space=pl.ANY),
        scratch_shapes=[pltpu.VMEM((STAGE_ROWS, PACKED_W), jnp.uint32),
                        pltpu.VMEM((EXPERT_TILE, PACKED_W), jnp.uint32),
                        pltpu.SemaphoreType.DMA((2,))],
    )
    return pl.pallas_call(
        _dispatch_kernel,
        out_shape=jax.ShapeDtypeStruct((n_rows, PACKED_W), jnp.uint32),
        grid_spec=grid_spec,
        compiler_params=_cparams("arbitrary"),
        name="dispatch_rows",
    )(runs, gaps, hb, idx)


def _expert_kernel(blk_e_ref, nact_ref, next_e_ref, x_ref, bu_ref, bd_ref, wu_hbm, wd_hbm, y_ref,
                   wuf_scr, wdf_scr, wub_scr, wdb_scr, sems):
    i = pl.program_id(0)
    e = blk_e_ref[i]

    def weight_copies(expert):
        return (pltpu.make_async_copy(wu_hbm.at[expert], wuf_scr, sems.at[0]),
                pltpu.make_async_copy(wd_hbm.at[expert], wdf_scr, sems.at[1]))

    @pl.when(i == 0)
    def _():
        for cp in weight_copies(e):
            cp.start()

    first_of_group = jnp.logical_or(i == 0, e != blk_e_ref[jnp.maximum(i - 1, 0)])

    @pl.when(jnp.logical_and(i < nact_ref[0], first_of_group))
    def _():
        for cp in weight_copies(e):
            cp.wait()
        wub_scr[...] = wuf_scr[...].astype(BF16)
        wdb_scr[...] = wdf_scr[...].astype(BF16)

        @pl.when(next_e_ref[i] >= 0)
        def _():
            for cp in weight_copies(next_e_ref[i]):
                cp.start()

    @pl.when(i < nact_ref[0])
    def _():
        lo, hi = _unpack_rows(x_ref[...])
        x = jnp.concatenate([lo.astype(BF16), hi.astype(BF16)], axis=-1)
        acc = jnp.zeros((EXPERT_TILE, D_MODEL), F32)
        step = 1024
        for c in range(D_EXPERT // step):
            glu = jnp.dot(x, wub_scr[:, c * step:(c + 1) * step], preferred_element_type=F32)
            lin = jnp.dot(x, wub_scr[:, D_EXPERT + c * step:D_EXPERT + (c + 1) * step],
                          preferred_element_type=F32)
            glu = jnp.minimum(glu + bu_ref[0, :, c * step:(c + 1) * step], SWIGLU_LIMIT)
            lin = jnp.clip(lin + bu_ref[0, :, D_EXPERT + c * step:D_EXPERT + (c + 1) * step],
                           -SWIGLU_LIMIT, SWIGLU_LIMIT)
            a = glu * _sigmoid(SWIGLU_ALPHA * glu) * (lin + 1.0)
            acc = acc + jnp.dot(a.astype(BF16), wdb_scr[c * step:(c + 1) * step, :],
                                preferred_element_type=F32)
        y_ref[...] = _pack_rows(acc + bd_ref[0])

    @pl.when(i >= nact_ref[0])
    def _():
        y_ref[...] = jnp.zeros(y_ref.shape, y_ref.dtype)


def _experts(blk_e, nact, next_e, xs, w_up, b_up, w_down, b_down):
    P = xs.shape[0]
    n_blocks = P // EXPERT_TILE
    act = lambda i, na: jnp.minimum(i, na[0] - 1)
    grid_spec = pltpu.PrefetchScalarGridSpec(
        num_scalar_prefetch=3,
        grid=(n_blocks,),
        in_specs=[pl.BlockSpec((EXPERT_TILE, PACKED_W), lambda i, be, na, ne: (act(i, na), 0)),
                  pl.BlockSpec((1, 1, 2 * D_EXPERT), lambda i, be, na, ne: (be[act(i, na)], 0, 0)),
                  pl.BlockSpec((1, 1, D_MODEL), lambda i, be, na, ne: (be[act(i, na)], 0, 0)),
                  pl.BlockSpec(memory_space=pl.ANY),
                  pl.BlockSpec(memory_space=pl.ANY)],
        out_specs=pl.BlockSpec((EXPERT_TILE, PACKED_W), lambda i, be, na, ne: (i, 0)),
        scratch_shapes=[pltpu.VMEM((D_MODEL, 2 * D_EXPERT), F32),
                        pltpu.VMEM((D_EXPERT, D_MODEL), F32),
                        pltpu.VMEM((D_MODEL, 2 * D_EXPERT), BF16),
                        pltpu.VMEM((D_EXPERT, D_MODEL), BF16),
                        pltpu.SemaphoreType.DMA((2,))],
    )
    return pl.pallas_call(
        _expert_kernel,
        out_shape=jax.ShapeDtypeStruct((P, PACKED_W), jnp.uint32),
        grid_spec=grid_spec,
        compiler_params=_cparams("arbitrary"),
        name="routed_experts",
    )(blk_e, nact, next_e, xs, b_up.reshape(N_EXPERTS, 1, 2 * D_EXPERT),
      b_down.reshape(N_EXPERTS, 1, D_MODEL), w_up, w_down)


def _combine_kernel(runs_ref, h_ref, idx_ref, gate_ref, g_ref, b_ref, ys_ref, o_ref, ybuf, sems):
    i = pl.program_id(0)
    n = pl.num_programs(0)

    def gather(tile, slot, action):
        _for_each_run(runs_ref, tile, ybuf.at[slot], ys_ref, sems.at[slot], False, action)

    @pl.when(i == 0)
    def _():
        ybuf[...] = jnp.zeros(ybuf.shape, ybuf.dtype)
        gather(0, 0, "start")

    @pl.when(i + 1 < n)
    def _():
        gather(i + 1, (i + 1) % 2, "start")

    slot = i % 2
    gather(i, slot, "wait")

    idx = idx_ref[...]
    gate = gate_ref[...]
    col_id = lax.broadcasted_iota(jnp.int32, (ROW_TILE, STAGE_ROWS), 1)
    weights = jnp.zeros((ROW_TILE, STAGE_ROWS), F32)
    for kk in range(TOP_K):
        weights = jnp.where(col_id == idx[:, TOP_K + kk:TOP_K + kk + 1], gate[:, kk:kk + 1], weights)
    wb = weights.astype(BF16)
    lo, hi = _unpack_rows(ybuf[slot])
    f_lo = jnp.dot(wb, lo.astype(BF16), preferred_element_type=F32)
    f_hi = jnp.dot(wb, hi.astype(BF16), preferred_element_type=F32)
    y_lo = DN_ALPHA * h_ref[:, :PACKED_W] + f_lo
    y_hi = DN_ALPHA * h_ref[:, PACKED_W:] + f_hi
    mu = (jnp.sum(y_lo, axis=-1, keepdims=True) + jnp.sum(y_hi, axis=-1, keepdims=True)) / D_MODEL
    d_lo = y_lo - mu
    d_hi = y_hi - mu
    var = (jnp.sum(d_lo * d_lo, axis=-1, keepdims=True) + jnp.sum(d_hi * d_hi, axis=-1, keepdims=True)) / D_MODEL
    inv = lax.rsqrt(var + LN_EPS)
    o_ref[:, :PACKED_W] = d_lo * inv * g_ref[:, :PACKED_W] + b_ref[:, :PACKED_W]
    o_ref[:, PACKED_W:] = d_hi * inv * g_ref[:, PACKED_W:] + b_ref[:, PACKED_W:]


def _combine(runs, h, idx, gate, ys, g, b):
    T = h.shape[0]
    grid_spec = pltpu.PrefetchScalarGridSpec(
        num_scalar_prefetch=1,
        grid=(T // ROW_TILE,),
        in_specs=[pl.BlockSpec((ROW_TILE, D_MODEL), lambda i, r: (i, 0)),
                  pl.BlockSpec((ROW_TILE, LANES), lambda i, r: (i, 0)),
                  pl.BlockSpec((ROW_TILE, LANES), lambda i, r: (i, 0)),
                  pl.BlockSpec(g.shape, lambda i, r: (0, 0)),
                  pl.BlockSpec(b.shape, lambda i, r: (0, 0)),
                  pl.BlockSpec(memory_space=pl.ANY)],
        out_specs=pl.BlockSpec((ROW_TILE, D_MODEL), lambda i, r: (i, 0)),
        scratch_shapes=[pltpu.VMEM((2, STAGE_ROWS, PACKED_W), jnp.uint32),
                        pltpu.SemaphoreType.DMA((2,))],
    )
    return pl.pallas_call(
        _combine_kernel,
        out_shape=jax.ShapeDtypeStruct((T, D_MODEL), F32),
        grid_spec=grid_spec,
        compiler_params=_cparams("arbitrary"),
        name="combine_ln",
    )(runs, h, idx, gate, g, b, ys)


def _rot_cols(w):
    half = w.shape[-1] // 2
    return jnp.concatenate([-w[:, half:], w[:, :half]], axis=-1)


def _pack_w_in(w_in):
    lat = Q_LORA + KV_LORA
    w_kr = w_in[:, lat:lat + MLA_ROPE]
    return jnp.concatenate([w_in[:, lat + MLA_ROPE:], w_in[:, :lat + MLA_ROPE], _rot_cols(w_kr)],
                           axis=-1).astype(BF16)


def _pack_w_uq(w_uq):
    w = w_uq.reshape(Q_LORA, MLA_HEADS, MLA_NOPE + MLA_ROPE)
    rope = w[:, :, MLA_NOPE:]
    rot = jnp.concatenate([-rope[:, :, MLA_ROPE // 2:], rope[:, :, :MLA_ROPE // 2]], axis=-1)
    return jnp.concatenate([w, rot], axis=-1).reshape(Q_LORA, MLA_HEADS * MLA_QK).astype(BF16)


def _rope_freqs():
    ret = 1.0 / (10000.0 ** jnp.linspace(0.0, 1.0, RET_DK // 2, dtype=F32))
    mla = 1.0 / (ROPE_BASE ** (jnp.arange(0, MLA_ROPE, 2, dtype=F32) / MLA_ROPE))
    return jnp.concatenate([ret, mla, jnp.zeros((LANES - ret.shape[0] - mla.shape[0],), F32)])[None, :]


def _routing_plan(cnt, n_tokens):
    counts = cnt[::SUBLANES, :N_EXPERTS]
    n_tiles = counts.shape[0]
    run_len = ((counts + SUBLANES - 1) // SUBLANES) * SUBLANES
    stage_off = jnp.cumsum(run_len, axis=1) - run_len
    tile_off = jnp.cumsum(run_len, axis=0) - run_len
    rows_used = jnp.sum(run_len, axis=0)
    padded = ((rows_used + EXPERT_TILE - 1) // EXPERT_TILE) * EXPERT_TILE
    pad_end = jnp.cumsum(padded)
    pad_start = pad_end - padded
    runs = jnp.stack([stage_off, pad_start[None, :] + tile_off, run_len], axis=-1).reshape(-1).astype(jnp.int32)
    nact = (pad_end[-1:] // EXPERT_TILE).astype(jnp.int32)
    gaps = jnp.concatenate([jnp.stack([pad_start + rows_used, padded - rows_used], axis=-1).reshape(-1),
                            nact]).astype(jnp.int32)
    n_rows = n_tokens * TOP_K + (SUBLANES - 1) * n_tiles * N_EXPERTS + N_EXPERTS * EXPERT_TILE
    n_rows = -(-n_rows // EXPERT_TILE) * EXPERT_TILE
    n_blocks = n_rows // EXPERT_TILE
    experts = jnp.arange(N_EXPERTS, dtype=jnp.int32)
    blk_row = jnp.arange(n_blocks, dtype=jnp.int32) * EXPERT_TILE
    blk_e = jnp.minimum(jnp.sum((pad_end[None, :] <= blk_row[:, None]).astype(jnp.int32), axis=-1),
                        N_EXPERTS - 1)
    later = jnp.where((rows_used > 0)[None, :] & (experts[None, :] > experts[:, None]), experts[None, :], N_EXPERTS)
    nxt = jnp.min(later, axis=-1)
    nxt = jnp.where(nxt == N_EXPERTS, -1, nxt)
    next_e = jnp.sum(jnp.where(blk_e[:, None] == experts[None, :], nxt[None, :], 0), axis=-1).astype(jnp.int32)
    return runs, gaps, blk_e, nact, next_e, n_rows


def kernel(x, positions, w_in, q_norm_g, w_uq, kv_norm_g, w_ukv, w_o, ln1_g, ln1_b,
           w_router, b_router, w_up, b_up, w_down, b_down, ln2_g, ln2_b):
    B, S, _ = x.shape
    T = B * S
    pos_col = positions.reshape(T, 1).astype(jnp.int32)
    rc, rs, mc, ms = _rope_tables(pos_col, _rope_freqs())
    dec, qw, kw, cd = _retention_constants()
    h = x.reshape(T, D_MODEL)
    for l in range(DEPTH):
        proj = _input_projection(h, _pack_w_in(w_in[l]))
        w_kv = w_ukv[l].reshape(KV_LORA, MLA_HEADS, MLA_NOPE + MLA_V)
        w_k = w_kv[:, :, :MLA_NOPE].reshape(KV_LORA, MLA_HEADS * MLA_NOPE).astype(BF16)
        w_vt = w_kv[:, :, MLA_NOPE:].reshape(KV_LORA, MLA_HEADS * MLA_V).T.astype(BF16)
        q_cat, kn, vt, kr = _mla_prep(proj, mc, ms, _pack_w_uq(w_uq[l]), w_k, w_vt,
                                      q_norm_g[l][None, :], kv_norm_g[l][None, :])
        o_mla = _attention(q_cat, kn, kr, vt, B, S)
        o_ret = _retention(proj, rc, rs, dec, qw, kw, cd, B, S)
        h1, h1b, idx_out, gate, cnt = _merge_and_route(
            h, o_mla, o_ret, proj, w_o[l].astype(BF16), ln1_g[l][None, :], ln1_b[l][None, :],
            w_router[l], b_router[l])
        runs, gaps, blk_e, nact, next_e, n_rows = _routing_plan(cnt, T)
        xs = _dispatch(runs, gaps, h1b, idx_out, n_rows)
        ys = _experts(blk_e, nact, next_e, xs, w_up[l], b_up[l], w_down[l], b_down[l])
        h = _combine(runs, h1, idx_out, gate, ys, ln2_g[l][None, :], ln2_b[l][None, :])
    return h.reshape(B, S, D_MODEL)
```

```python
import functools

import numpy as np
import jax
import jax.numpy as jnp
from jax import lax
from jax.experimental import pallas as pl
from jax.experimental.pallas import tpu as pltpu

F32 = jnp.float32
BF16 = jnp.bfloat16

D_MODEL = 1024
DEPTH = 1
MLA_HEADS = 8
MLA_NOPE = 128
MLA_ROPE = 64
MLA_V = 128
Q_LORA = 384
KV_LORA = 256
ROPE_BASE = 10000.0
RET_HEADS = 8
RET_DK = 128
RET_DV = 128
RET_CHUNK = 128
N_EXPERTS = 32
TOP_K = 4
D_EXPERT = 1024
SWIGLU_LIMIT = 7.0
SWIGLU_ALPHA = 1.702
DN_ALPHA = (2.0 * DEPTH) ** 0.25
LN_EPS = 1e-5
RMS_EPS = 1e-6
GN_EPS = 1e-6
NEG_INF = -1e30

LANES = 128
SUBLANES = 8
MLA_QK = 2 * LANES
PACKED_W = D_MODEL // 2
LATENT_W = 768
N_WIDE = 6 * D_MODEL
N_PROJ = N_WIDE + LATENT_W
VMEM_LIMIT = 56 * 1024 * 1024

ROW_TILE = 512
ATTN_TILE = ROW_TILE
ATTN_Q_TILE = 2 * ATTN_TILE
ATTN_Q_SPLIT = 4
VT_ROWS = MLA_V + 16
LOG2_E = 1.4426950408889634
RET_GROUP = 8
EXPERT_TILE = 256
STAGE_ROWS = -(-(ROW_TILE * TOP_K + (SUBLANES - 1) * N_EXPERTS) // EXPERT_TILE) * EXPERT_TILE


def _cparams(*sem):
    return pltpu.CompilerParams(dimension_semantics=sem, vmem_limit_bytes=VMEM_LIMIT)


def _const_spec(shape):
    nd = len(shape)
    return pl.BlockSpec(shape, lambda *_: (0,) * nd, pipeline_mode=pl.Buffered(1))


def _sigmoid(x):
    return 1.0 / (1.0 + jnp.exp(-x))


def _pack_rows(x):
    n = x.shape[-1] // 2
    xb = x.astype(BF16).astype(F32)
    lo = pltpu.bitcast(xb[:, :n], jnp.uint32) >> 16
    hi = pltpu.bitcast(xb[:, n:], jnp.uint32) & jnp.uint32(0xFFFF0000)
    return lo | hi


def _unpack_rows(w):
    lo = pltpu.bitcast(w << 16, F32)
    hi = pltpu.bitcast(w & jnp.uint32(0xFFFF0000), F32)
    return lo, hi


def _swap_halves(x):
    return pltpu.roll(x, LANES // 2, 1)


def _proj_kernel(x_ref, w_ref, o_ref):
    xb = x_ref[...].astype(BF16)
    for c in range(N_PROJ // LATENT_W):
        sl = slice(c * LATENT_W, (c + 1) * LATENT_W)
        o_ref[:, sl] = jnp.dot(xb, w_ref[:, sl], preferred_element_type=F32).astype(o_ref.dtype)


def _input_projection(x2, w_cat):
    T = x2.shape[0]
    return pl.pallas_call(
        _proj_kernel,
        out_shape=jax.ShapeDtypeStruct((T, N_PROJ), BF16),
        grid=(T // ROW_TILE,),
        in_specs=[pl.BlockSpec((ROW_TILE, D_MODEL), lambda i: (i, 0)),
                  _const_spec((D_MODEL, N_PROJ))],
        out_specs=pl.BlockSpec((ROW_TILE, N_PROJ), lambda i: (i, 0)),
        compiler_params=_cparams("parallel"),
        name="input_projection",
    )(x2, w_cat)


def _rope_table_kernel(pos_ref, freq_ref, rc_ref, rs_ref, mc_ref, ms_ref):
    ang = pos_ref[...].astype(F32) * freq_ref[...]
    c = jnp.cos(ang)
    s = jnp.sin(ang)
    lane = lax.broadcasted_iota(jnp.int32, ang.shape, 1)
    c64 = pltpu.roll(c, 64, 1)
    s64 = pltpu.roll(s, 64, 1)
    rc_ref[...] = jnp.where(lane < 64, c, c64)
    rs_ref[...] = jnp.where(lane < 64, -s, s64)
    c96 = pltpu.roll(c, 96, 1)
    s96 = pltpu.roll(s, 96, 1)
    mc_ref[...] = jnp.where(lane < 32, c64, jnp.where(lane < 64, c96, 0.0))
    ms_ref[...] = jnp.where(lane < 32, s64, jnp.where(lane < 64, s96, 0.0))


def _rope_tables(pos_col, freq_row):
    T = pos_col.shape[0]
    rows = min(2048, T)
    tab =jax.ShapeDtypeStruct((T, LANES), F32)
    spec = pl.BlockSpec((rows, LANES), lambda i: (i, 0))
    return pl.pallas_call(
        _rope_table_kernel,
        out_shape=(tab, tab, tab, tab),
        grid=(T // rows,),
        in_specs=[pl.BlockSpec((rows, 1), lambda i: (i, 0)), _const_spec((1, LANES))],
        out_specs=(spec, spec, spec, spec),
        compiler_params=_cparams("parallel"),
        name="rope_tables",
    )(pos_col, freq_row)


def _mla_prep_kernel(a_ref, mc_ref, ms_ref, wq_ref, wk_ref, wvt_ref, qg_ref, kvg_ref,
                     q_ref, kn_ref, vt_ref, kr_ref):
    a = a_ref[...].astype(F32)
    mc = mc_ref[...]
    ms = ms_ref[...]
    cq = a[:, :Q_LORA]
    qn = cq * lax.rsqrt(jnp.mean(cq * cq, axis=-1, keepdims=True) + RMS_EPS) * qg_ref[...]
    q = jnp.dot(qn.astype(BF16), wq_ref[...], preferred_element_type=F32)
    scale = (MLA_NOPE + MLA_ROPE) ** -0.5 * LOG2_E
    for h in range(MLA_HEADS):
        nope = q[:, h * MLA_QK:h * MLA_QK + LANES]
        blk = q[:, h * MLA_QK + LANES:(h + 1) * MLA_QK]
        rope = blk * mc + _swap_halves(blk) * ms
        q_ref[:, h * MLA_QK:h * MLA_QK + LANES] = (nope * scale).astype(BF16)
        q_ref[:, h * MLA_QK + LANES:(h + 1) * MLA_QK] = (rope * scale).astype(BF16)
    ckv = a[:, Q_LORA:Q_LORA + KV_LORA]
    kvn = ckv * lax.rsqrt(jnp.mean(ckv * ckv, axis=-1, keepdims=True) + RMS_EPS) * kvg_ref[...]
    kvb = kvn.astype(BF16)
    kn_ref[...] = jnp.dot(kvb, wk_ref[...], preferred_element_type=F32).astype(BF16)
    vt = lax.dot_general(wvt_ref[...], kvb, (((1,), (1,)), ((), ())),
                         preferred_element_type=F32).astype(BF16)
    ones = jnp.ones((VT_ROWS - MLA_V, vt.shape[1]), BF16)
    for h in range(MLA_HEADS):
        vt_ref[0, h * VT_ROWS:h * VT_ROWS + MLA_V, :] = vt[h * MLA_V:(h + 1) * MLA_V, :]
        vt_ref[0, h * VT_ROWS + MLA_V:(h + 1) * VT_ROWS, :] = ones
    krb =a[:, Q_LORA + KV_LORA:]
    kr_ref[...] = (krb * mc + _swap_halves(krb) * ms).astype(BF16)


def _mla_prep(proj, mc, ms, wq, wk, wvt, qg, kvg):
    T = proj.shape[0]
    row = lambda w: pl.BlockSpec((ROW_TILE, w), lambda i: (i, 0))
    return pl.pallas_call(
        _mla_prep_kernel,
        out_shape=(jax.ShapeDtypeStruct((T, MLA_HEADS * MLA_QK), BF16),
                   jax.ShapeDtypeStruct((T, MLA_HEADS * MLA_NOPE), BF16),
                   jax.ShapeDtypeStruct((T // ROW_TILE, MLA_HEADS * VT_ROWS, ROW_TILE), BF16),
                   jax.ShapeDtypeStruct((T, LANES), BF16)),
        grid=(T // ROW_TILE,),
        in_specs=[pl.BlockSpec((ROW_TILE, LATENT_W), lambda i: (i, N_WIDE // LATENT_W)),
                  row(LANES), row(LANES),
                  _const_spec(wq.shape), _const_spec(wk.shape), _const_spec(wvt.shape),
                  _const_spec(qg.shape), _const_spec(kvg.shape)],
        out_specs=(row(MLA_HEADS * MLA_QK), row(MLA_HEADS * MLA_NOPE),
                   pl.BlockSpec((1, MLA_HEADS * VT_ROWS, ROW_TILE), lambda i: (i, 0, 0)), row(LANES)),
        compiler_params=_cparams("parallel"),
        name="mla_prep",
    )(proj, mc, ms, wq, wk, wvt, qg, kvg)


def _attn_kernel(q_ref, kn_ref, kr_ref, vt_ref, o_ref, acc_scr):
    W = ATTN_Q_TILE // ATTN_Q_SPLIT
    n_diag = ATTN_Q_TILE // ATTN_TILE
    groups = tuple(range(ATTN_Q_SPLIT))

    def visible(d, c):
        return min(max((c + 1) * W - d * ATTN_TILE, 0), ATTN_TILE)

    for qi in range(q_ref.shape[0] // ATTN_Q_TILE):
        q0 = qi * ATTN_Q_TILE
        qs = [q_ref[q0 + c * W:q0 + (c + 1) * W, :] for c in groups]
        acc_scr[...] = jnp.zeros(acc_scr.shape, F32)

        def scores(j, which=groups):
            rows = slice(j * ATTN_TILE, (j + 1) * ATTN_TILE)
            k = jnp.concatenate([kn_ref[rows, :], kr_ref[rows, :]], axis=-1)
            return {c: lax.dot_general(k, qs[c], (((1,), (1,)), ((), ())), preferred_element_type=F32)
                    for c in which}

        def softmax_pv(j, s_all, m_all, diag=None):
            vt = vt_ref[j]
            out = list(m_all)
            for c in s_all:
                s = s_all[c]
                nk = ATTN_TILE if diag is None else visible(diag, c)
                if diag is not None:
                    s = s[:nk]
                    if diag * ATTN_TILE + nk - 1 > c * W:
                        kp = lax.broadcasted_iota(jnp.int32, s.shape, 0) + diag * ATTN_TILE
                        qp = lax.broadcasted_iota(jnp.int32, s.shape, 1) + c * W
                        s = jnp.where(kp <= qp, s, NEG_INF)
                m_new = jnp.maximum(m_all[c], jnp.max(s, axis=0, keepdims=True))
                alpha = jnp.exp2(m_all[c] - m_new)
                p = jnp.exp2(s - m_new).astype(BF16)
                acc_scr[c] = alpha * acc_scr[c] + jnp.dot(vt[:, :nk], p, preferred_element_type=F32)
                out[c] = m_new
            return tuple(out)

        first = n_diag * qi
        work = [(j, None, groups) for j in range(first)]
        work += [(first + d, d, tuple(c for c in groups if visible(d, c) > 0)) for d in range(n_diag)]
        m_all = tuple(jnp.full((1, W), NEG_INF, F32) for _ in groups)
        pending = [scores(j, which) for j, _, which in work[:2]]
        for n, (j, diag, _) in enumerate(work):
            m_all = softmax_pv(j, pending.pop(0), m_all, diag)
            if n + 2 < len(work):
                pending.append(scores(work[n + 2][0], work[n + 2][2]))
        for c in groups:
            acc = acc_scr[c]
            o_ref[q0 + c * W:q0 + (c + 1) * W, :] = (acc[:MLA_V] / acc[MLA_V:MLA_V + 1]).T.astype(o_ref.dtype)


def _attention(q_cat, kn, kr, vt, B, S):
    T = B * S
    return pl.pallas_call(
        _attn_kernel,
        out_shape=jax.ShapeDtypeStruct((T, MLA_HEADS * MLA_V), BF16),
        grid=(B, MLA_HEADS),
        in_specs=[pl.BlockSpec((S, MLA_QK), lambda b, h: (b, h)),
                  pl.BlockSpec((S, MLA_NOPE), lambda b, h: (b, h)),
                  pl.BlockSpec((S, LANES), lambda b, h: (b, 0)),
                  pl.BlockSpec((S // ATTN_TILE, VT_ROWS, ATTN_TILE), lambda b, h: (b, h, 0))],
        out_specs=pl.BlockSpec((S, MLA_V), lambda b, h: (b, h)),
        scratch_shapes=[pltpu.VMEM((ATTN_Q_SPLIT, VT_ROWS, ATTN_Q_TILE // ATTN_Q_SPLIT), F32)],
        compiler_params=_cparams("parallel", "parallel"),
        name="mla_attention",
    )(q_cat, kn, kr, vt)


def _ret_kernel(q_ref, k_ref, v_ref, g_ref, rc_ref, rs_ref, dec_ref, qw_ref, kw_ref, cd_ref,
                o_ref, qb_scr, qwb_scr, kb_scr, kwb_scr):
    S = q_ref.shape[0]
    C = RET_CHUNK
    n_chunks = S // C
    rc = rc_ref[...]
    rs = rs_ref[...]
    q = q_ref[...].astype(F32)
    k = k_ref[...].astype(F32)
    q = q * rc + _swap_halves(q) * rs
    k = (k * rc + _swap_halves(k) * rs) * (RET_DK ** -0.5)
    qw = qw_ref[0]
    kw = kw_ref[0]
    q3 = q.reshape(n_chunks, C, RET_DK)
    k3 = k.reshape(n_chunks, C, RET_DK)
    qb_scr[...] = q.astype(BF16)
    kb_scr[...] = k.astype(BF16)
    qwb_scr[...] = (q3 * qw[None]).reshape(S, RET_DK).astype(BF16)
    kwb_scr[...] = (k3 * kw[None]).reshape(S, RET_DK)
    dec = dec_ref[0]
    cd = cd_ref[0]

    def group(gi, R):
        rows_of = [pl.ds(pl.multiple_of((gi * RET_GROUP + u) * C, C), C) for u in range(RET_GROUP)]
        scores, updates = [], []
        for rows in rows_of:
            scores.append(lax.dot_general(qb_scr[rows, :], kb_scr[rows, :], (((1,), (1,)), ((), ())),
                                          preferred_element_type=F32))
            kwt = kwb_scr[rows, :].T.astype(BF16)
            updates.append(jnp.dot(kwt, v_ref[rows, :], preferred_element_type=F32))
        states = []
        for u in range(RET_GROUP):
            states.append(R.astype(BF16))
            R = cd * R + updates[u]
        for u, rows in enumerate(rows_of):
            o = jnp.dot((scores[u] * dec).astype(BF16), v_ref[rows, :], preferred_element_type=F32)
            o = o + jnp.dot(qwb_scr[rows, :], states[u], preferred_element_type=F32)
            mu = jnp.mean(o, axis=-1, keepdims=True)
            d = o - mu
            var = jnp.mean(d * d, axis=-1, keepdims=True)
            gate = g_ref[rows, :].astype(F32)
            o_ref[rows, :] = (d * lax.rsqrt(var + GN_EPS) * (gate * _sigmoid(gate))).astype(o_ref.dtype)
        return R

    lax.fori_loop(0, n_chunks // RET_GROUP, group, jnp.zeros((RET_DK, RET_DV), F32))


def _retention(proj, rc, rs, dec, qw, kw, cd, B, S):
    T = B * S
    H = RET_HEADS
    col = lambda off: pl.BlockSpec((S, LANES), lambda b, h: (b, off + h))
    tab = pl.BlockSpec((S, LANES), lambda b, h: (b, 0))
    per_head = lambda r: pl.BlockSpec((1, r, LANES), lambda b, h: (h, 0, 0))
    return pl.pallas_call(
        _ret_kernel,
        out_shape=jax.ShapeDtypeStruct((T, H * RET_DV), BF16),
        grid=(B, H),
        in_specs=[col(0), col(H), col(2 * H), col(3 * H), tab, tab,
                  per_head(RET_CHUNK), per_head(RET_CHUNK), per_head(RET_CHUNK), per_head(1)],
        out_specs=pl.BlockSpec((S, RET_DV), lambda b, h: (b, h)),
        scratch_shapes=[pltpu.VMEM((S, RET_DK), BF16), pltpu.VMEM((S, RET_DK), BF16),
                        pltpu.VMEM((S, RET_DK), BF16), pltpu.VMEM((S, RET_DK), F32)],
        compiler_params=_cparams("parallel", "parallel"),
        name="retention",
    )(proj, proj, proj, proj, rc, rs, dec, qw, kw, cd)


def _retention_constants():
    C = RET_CHUNK
    h = np.arange(RET_HEADS, dtype=np.float64)
    log_gamma = np.log(1.0 - 2.0 ** (-5.0 - h))
    idx = np.arange(C, dtype=np.float64)
    rel = idx[:, None] - idx[None, :]
    dec = np.where(rel[None] >= 0, np.exp(np.maximum(rel, 0.0)[None] * log_gamma[:, None, None]), 0.0)
    qw = np.exp((idx + 1.0)[None, :] * log_gamma[:, None])
    kw = np.exp((C - 1.0 - idx)[None, :] * log_gamma[:, None])
    cd = np.exp(C * log_gamma)
    bc = lambda a: np.ascontiguousarray(np.broadcast_to(a[..., None], a.shape + (LANES,))).astype(np.float32)
    return dec.astype(np.float32), bc(qw), bc(kw), bc(cd[:, None])


def _layernorm(y, g, b):
    mu = jnp.mean(y, axis=-1, keepdims=True)
    d = y - mu
    var = jnp.mean(d * d, axis=-1, keepdims=True)
    return d * lax.rsqrt(var + LN_EPS) * g + b


def _merge_kernel(x_ref, om_ref, or_ref, gm_ref, gr_ref, wo_ref, g_ref, b_ref, w01_ref, rb_ref,
                  h_ref, hb_ref, idx_ref, gate_ref, cnt_ref):
    mixed = (_sigmoid(gm_ref[...].astype(F32)) * om_ref[...].astype(F32)
             + _sigmoid(gr_ref[...].astype(F32)) * or_ref[...].astype(F32))
    mix = jnp.dot(mixed.astype(BF16), wo_ref[...], preferred_element_type=F32)
    h = _layernorm(DN_ALPHA * x_ref[...] + mix, g_ref[...], b_ref[...])
    h_ref[...] = h
    hb_ref[...] = h.astype(BF16)
    _route_tile(h, w01_ref, rb_ref, idx_ref, gate_ref, cnt_ref)


def _merge_and_route(x2, o_mla, o_ret, proj, wo, g, b, w_router, b_router):
    T = x2.shape[0]
    wpad = jnp.zeros((D_MODEL, LANES), F32).at[:, :N_EXPERTS].set(w_router)
    w0 = wpad.astype(BF16)
    w01 = jnp.concatenate([w0, (wpad - w0.astype(F32)).astype(BF16)], axis=-1)
    bpad = jnp.zeros((1, LANES), F32).at[0, :N_EXPERTS].set(b_router)
    row = lambda c: pl.BlockSpec((ROW_TILE, D_MODEL), lambda i: (i, c))
    lanes = pl.BlockSpec((ROW_TILE, LANES), lambda i: (i, 0))
    return pl.pallas_call(
        _merge_kernel,
        out_shape=(jax.ShapeDtypeStruct((T, D_MODEL), F32), jax.ShapeDtypeStruct((T, D_MODEL), BF16),
                   jax.ShapeDtypeStruct((T, LANES), jnp.int32), jax.ShapeDtypeStruct((T, LANES), F32),
                   jax.ShapeDtypeStruct((T // ROW_TILE * SUBLANES, LANES), jnp.int32)),
        grid=(T // ROW_TILE,),
        in_specs=[row(0), row(0), row(0), row(4), row(5),
                  _const_spec(wo.shape), _const_spec(g.shape), _const_spec(b.shape),
                  _const_spec(w01.shape), _const_spec(bpad.shape)],
        out_specs=(row(0), row(0), lanes, lanes, pl.BlockSpec((SUBLANES, LANES), lambda i: (i, 0))),
        compiler_params=_cparams("parallel"),
        name="merge_out_proj_ln_route",
    )(x2, o_mla, o_ret, proj, proj, wo, g, b, w01, bpad)


def _split2(a):
    a0 = a.astype(BF16)
    return a0, (a - a0.astype(F32)).astype(BF16)


def _route_tile(h, w01_ref, b_ref, idx_ref, gate_ref, cnt_ref):
    h0, h1 = _split2(h)
    p0 = jnp.dot(h0, w01_ref[...], preferred_element_type=F32)
    p1 = jnp.dot(h1, w01_ref[:, :LANES], preferred_element_type=F32)
    logits = p0[:, :LANES] + (p0[:, LANES:] + p1) + b_ref[...]
    rows = logits.shape[0]
    lane = lax.broadcasted_iota(jnp.int32, (rows, LANES), 1)
    lane_f = lane.astype(F32)
    work = jnp.where(lane < N_EXPERTS, logits, -jnp.inf)
    vals, ids, sels = [], [], []
    for _ in range(TOP_K):
        m = jnp.max(work, axis=-1, keepdims=True)
        first = jnp.min(jnp.where(work == m, lane_f, float(LANES)), axis=-1, keepdims=True)
        sel = lane_f == first
        work = jnp.where(sel, -jnp.inf, work)
        vals.append(m)
        ids.append(first)
        sels.append(sel)
    exps = [jnp.exp(v - vals[0]) for v in vals]
    denom = exps[0] + exps[1] + exps[2] + exps[3]
    chosen = jnp.zeros((rows, LANES), F32)
    for sel in sels:
        chosen = jnp.where(sel, 1.0, chosen)
    r = lax.broadcasted_iota(jnp.int32, (rows, rows), 0)
    c = lax.broadcasted_iota(jnp.int32, (rows, rows), 1)
    lower = jnp.where(r > c, 1.0, 0.0).astype(BF16)
    before = jnp.dot(lower, chosen.astype(BF16), preferred_element_type=F32)
    counts = jnp.sum(chosen, axis=0, keepdims=True)
    run_len = jnp.floor((counts + (SUBLANES - 1)) * (1.0 / SUBLANES)) * SUBLANES
    er = lax.broadcasted_iota(jnp.int32, (LANES, LANES), 0)
    ec = lax.broadcasted_iota(jnp.int32, (LANES, LANES), 1)
    earlier = jnp.where(er < ec, 1.0, 0.0).astype(BF16)
    run_start = jnp.dot(jnp.broadcast_to(run_len, (SUBLANES, LANES)).astype(BF16), earlier,
                        preferred_element_type=F32)[:1]
    slot = before + run_start
    idx_out = jnp.zeros((rows, LANES), jnp.int32)
    gate_out = jnp.zeros((rows, LANES), F32)
    for kk in range(TOP_K):
        pos = jnp.sum(jnp.where(sels[kk], slot, 0.0), axis=-1, keepdims=True).astype(jnp.int32)
        idx_out = jnp.where(lane == kk, ids[kk].astype(jnp.int32), idx_out)
        idx_out = jnp.where(lane == TOP_K + kk, pos, idx_out)
        gate_out = jnp.where(lane == kk, exps[kk] / denom, gate_out)
    idx_ref[...] = idx_out
    gate_ref[...] = gate_out
    cnt_ref[...] = jnp.broadcast_to(counts, cnt_ref.shape).astype(jnp.int32)


def _run_copy(runs_ref, tile, e, stage, hbm, sem, to_hbm):
    base = (tile * N_EXPERTS + e) * 3
    s = pl.multiple_of(runs_ref[base], SUBLANES)
    g = pl.multiple_of(runs_ref[base + 1], SUBLANES)
    n = pl.multiple_of(runs_ref[base + 2], SUBLANES)
    src, dst = stage.at[pl.ds(s, n), :], hbm.at[pl.ds(g, n), :]
    if not to_hbm:
        src, dst = dst, src
    return n, pltpu.make_async_copy(src, dst, sem)


def _for_each_run(runs_ref, tile, stage, hbm, sem, to_hbm, action):
    def one(e, carry):
        n, cp = _run_copy(runs_ref, tile, e, stage, hbm, sem, to_hbm)

        @pl.when(n > 0)
        def _():
            getattr(cp, action)()

        return carry

    lax.fori_loop(0, N_EXPERTS, one, 0)


def _dispatch_kernel(runs_ref, gaps_ref, h_ref, idx_ref, xs_ref, stage, zeros, sems):
    i = pl.program_id(0)

    def gap_copy(e):
        g = pl.multiple_of(gaps_ref[2 * e], SUBLANES)
        n = pl.multiple_of(gaps_ref[2 * e + 1], SUBLANES)
        return n, pltpu.make_async_copy(zeros.at[pl.ds(0, n), :], xs_ref.at[pl.ds(g, n), :], sems.at[2])

    def for_each_gap(action):
        def one(e, carry):
            n, cp = gap_copy(e)

            @pl.when(n > 0)
            def _():
                getattr(cp, action)()

            return carry

        lax.fori_loop(0, N_EXPERTS, one, 0)

        def tail(b, carry):
            rows = pl.ds(pl.multiple_of(b * EXPERT_TILE, EXPERT_TILE), EXPERT_TILE)
            getattr(pltpu.make_async_copy(zeros, xs_ref.at[rows, :], sems.at[2]), action)()
            return carry

        lax.fori_loop(gaps_ref[2 * N_EXPERTS], xs_ref.shape[0] // EXPERT_TILE, tail, 0)

    @pl.when(i == 0)
    def _():
        zeros[...] = jnp.zeros(zeros.shape, zeros.dtype)
        for_each_gap("start")

    pos_t = idx_ref[...].astype(F32).T.astype(jnp.int32)
    h = h_ref[...]
    chunk = EXPERT_TILE
    row_id = lax.broadcasted_iota(jnp.int32, (chunk, ROW_TILE), 0)
    slot = i % 2
    for c in range(STAGE_ROWS // chunk):
        onehot = jnp.zeros((chunk, ROW_TILE), F32)
        for kk in range(TOP_K):
            onehot = jnp.where(row_id == (pos_t[TOP_K + kk:TOP_K + kk + 1] - c * chunk), 1.0, onehot)
        stage[slot, c * chunk:(c + 1) * chunk, :] = _pack_rows(
            jnp.dot(onehot.astype(BF16), h, preferred_element_type=F32))

    _for_each_run(runs_ref, i, stage.at[slot], xs_ref, sems.at[slot], True, "start")

    @pl.when(i > 0)
    def _():
        _for_each_run(runs_ref, i - 1, stage.at[1 - slot], xs_ref, sems.at[1 - slot], True, "wait")

    @pl.when(i == pl.num_programs(0) - 1)
    def _():
        _for_each_run(runs_ref, i, stage.at[slot], xs_ref, sems.at[slot], True, "wait")

    @pl.when(i == 0)
    def _():
        for_each_gap("wait")


def _dispatch(runs, gaps, hb, idx, n_rows):
    T = hb.shape[0]
    grid_spec = pltpu.PrefetchScalarGridSpec(
        num_scalar_prefetch=2,
        grid=(T // ROW_TILE,),
        in_specs=[pl.BlockSpec((ROW_TILE, D_MODEL), lambda i, r, g: (i, 0)),
                  pl.BlockSpec((ROW_TILE, LANES), lambda i, r, g: (i, 0))],
        out_specs=pl.BlockSpec(memory_space=pl.ANY),
        scratch_shapes=[pltpu.VMEM((2, STAGE_ROWS, PACKED_W), jnp.uint32),
                        pltpu.VMEM((EXPERT_TILE, PACKED_W), jnp.uint32),
                        pltpu.SemaphoreType.DMA((3,))],
    )
    return pl.pallas_call(
        _dispatch_kernel,
        out_shape=jax.ShapeDtypeStruct((n_rows, PACKED_W), jnp.uint32),
        grid_spec=grid_spec,
        compiler_params=_cparams("arbitrary"),
        name="dispatch_rows",
    )(runs, gaps, hb, idx)


def _expert_kernel(blk_e_ref, nact_ref, next_e_ref, x_ref, bu_ref, bd_ref, wu_hbm, wd_hbm, y_ref,
                   wuf_scr, wdf_scr, wub_scr, wdb_scr, sems):
    i = pl.program_id(0)
    e = blk_e_ref[i]

    def weight_copies(expert):
        return (pltpu.make_async_copy(wu_hbm.at[expert], wuf_scr, sems.at[0]),
                pltpu.make_async_copy(wd_hbm.at[expert], wdf_scr, sems.at[1]))

    @pl.when(i == 0)
    def _():
        for cp in weight_copies(e):
            cp.start()

    first_of_group = jnp.logical_or(i == 0, e != blk_e_ref[jnp.maximum(i - 1, 0)])

    @pl.when(jnp.logical_and(i < nact_ref[0], first_of_group))
    def _():
        for cp in weight_copies(e):
            cp.wait()
        wub_scr[...] = wuf_scr[...].astype(BF16)
        wdb_scr[...] = wdf_scr[...].astype(BF16)

        @pl.when(next_e_ref[i] >= 0)
        def _():
            for cp in weight_copies(next_e_ref[i]):
                cp.start()

    @pl.when(i < nact_ref[0])
    def _():
        lo, hi = _unpack_rows(x_ref[...])
        x = jnp.concatenate([lo.astype(BF16), hi.astype(BF16)], axis=-1)
        acc = jnp.zeros((EXPERT_TILE, D_MODEL), F32)
        step = 1024
        for c in range(D_EXPERT // step):
            glu = jnp.dot(x, wub_scr[:, c * step:(c + 1) * step], preferred_element_type=F32)
            lin = jnp.dot(x, wub_scr[:, D_EXPERT + c * step:D_EXPERT + (c + 1) * step],
                          preferred_element_type=F32)
            glu = jnp.minimum(glu + bu_ref[0, :, c * step:(c + 1) * step], SWIGLU_LIMIT)
            lin = jnp.clip(lin + bu_ref[0, :, D_EXPERT + c * step:D_EXPERT + (c + 1) * step],
                           -SWIGLU_LIMIT, SWIGLU_LIMIT)
            a = glu * _sigmoid(SWIGLU_ALPHA * glu) * (lin + 1.0)
            acc = acc + jnp.dot(a.astype(BF16), wdb_scr[c * step:(c + 1) * step, :],
                                preferred_element_type=F32)
        y_ref[...] = _pack_rows(acc + bd_ref[0])

    @pl.when(i >= nact_ref[0])
    def _():
        y_ref[...] = jnp.zeros(y_ref.shape, y_ref.dtype)


def _experts(blk_e, nact, next_e, xs, w_up, b_up, w_down, b_down):
    P = xs.shape[0]
    n_blocks = P // EXPERT_TILE
    act = lambda i, na: jnp.minimum(i, na[0] - 1)
    grid_spec = pltpu.PrefetchScalarGridSpec(
        num_scalar_prefetch=3,
        grid=(n_blocks,),
        in_specs=[pl.BlockSpec((EXPERT_TILE, PACKED_W), lambda i, be, na, ne: (act(i, na), 0)),
                  pl.BlockSpec((1, 1, 2 * D_EXPERT), lambda i, be, na, ne: (be[act(i, na)], 0, 0)),
                  pl.BlockSpec((1, 1, D_MODEL), lambda i, be, na, ne: (be[act(i, na)], 0, 0)),
                  pl.BlockSpec(memory_space=pl.ANY),
                  pl.BlockSpec(memory_space=pl.ANY)],
        out_specs=pl.BlockSpec((EXPERT_TILE, PACKED_W), lambda i, be, na, ne: (i, 0)),
        scratch_shapes=[pltpu.VMEM((D_MODEL, 2 * D_EXPERT), F32),
                        pltpu.VMEM((D_EXPERT, D_MODEL), F32),
                        pltpu.VMEM((D_MODEL, 2 * D_EXPERT), BF16),
                        pltpu.VMEM((D_EXPERT, D_MODEL), BF16),
                        pltpu.SemaphoreType.DMA((2,))],
    )
    return pl.pallas_call(
        _expert_kernel,
        out_shape=jax.ShapeDtypeStruct((P, PACKED_W), jnp.uint32),
        grid_spec=grid_spec,
        compiler_params=_cparams("arbitrary"),
        name="routed_experts",
    )(blk_e, nact, next_e, xs, b_up.reshape(N_EXPERTS, 1, 2 * D_EXPERT),
      b_down.reshape(N_EXPERTS, 1, D_MODEL), w_up, w_down)


def _combine_kernel(runs_ref, h_ref, idx_ref, gate_ref, g_ref, b_ref, ys_ref, o_ref, ybuf, sems):
    i = pl.program_id(0)
    n = pl.num_programs(0)

    def gather(tile, slot, action):
        _for_each_run(runs_ref, tile, ybuf.at[slot], ys_ref, sems.at[slot], False, action)

    @pl.when(i == 0)
    def _():
        ybuf[...] = jnp.zeros(ybuf.shape, ybuf.dtype)
        gather(0, 0, "start")

    @pl.when(i + 1 < n)
    def _():
        gather(i + 1, (i + 1) % 2, "start")

    slot = i % 2
    gather(i, slot, "wait")

    idx = idx_ref[...]
    gate = gate_ref[...]
    chunk = EXPERT_TILE
    col_id = lax.broadcasted_iota(jnp.int32, (ROW_TILE, LANES), 1)
    pos_b = [jnp.broadcast_to(idx[:, TOP_K + kk:TOP_K + kk + 1], (ROW_TILE, LANES)) for kk in range(TOP_K)]
    gate_b = [jnp.broadcast_to(gate[:, kk:kk + 1], (ROW_TILE, LANES)) for kk in range(TOP_K)]
    f_lo = jnp.zeros((ROW_TILE, PACKED_W), F32)
    f_hi = jnp.zeros((ROW_TILE, PACKED_W), F32)
    for c in range(STAGE_ROWS // chunk):
        parts = []
        for off in range(c * chunk, (c + 1) * chunk, LANES):
            part = jnp.zeros((ROW_TILE, LANES), F32)
            for kk in range(TOP_K):
                part = jnp.where(col_id + off == pos_b[kk], gate_b[kk], part)
            parts.append(part)
        wb = jnp.concatenate(parts, axis=-1).astype(BF16)
        lo, hi = _unpack_rows(ybuf[slot, c * chunk:(c + 1) * chunk, :])
        f_lo = f_lo + jnp.dot(wb, lo.astype(BF16), preferred_element_type=F32)
        f_hi = f_hi + jnp.dot(wb, hi.astype(BF16), preferred_element_type=F32)
    y_lo = DN_ALPHA * h_ref[:, :PACKED_W] + f_lo
    y_hi = DN_ALPHA * h_ref[:, PACKED_W:] + f_hi
    mu = (jnp.sum(y_lo, axis=-1, keepdims=True) + jnp.sum(y_hi, axis=-1, keepdims=True)) / D_MODEL
    d_lo = y_lo - mu
    d_hi = y_hi - mu
    var = (jnp.sum(d_lo * d_lo, axis=-1, keepdims=True) + jnp.sum(d_hi * d_hi, axis=-1, keepdims=True)) / D_MODEL
    inv = lax.rsqrt(var + LN_EPS)
    o_ref[:, :PACKED_W] = d_lo * inv * g_ref[:, :PACKED_W] + b_ref[:, :PACKED_W]
    o_ref[:, PACKED_W:] = d_hi * inv * g_ref[:, PACKED_W:] + b_ref[:, PACKED_W:]


def _combine(runs, h, idx, gate, ys, g, b):
    T = h.shape[0]
    grid_spec = pltpu.PrefetchScalarGridSpec(
        num_scalar_prefetch=1,
        grid=(T // ROW_TILE,),
        in_specs=[pl.BlockSpec((ROW_TILE, D_MODEL), lambda i, r: (i, 0)),
                  pl.BlockSpec((ROW_TILE, LANES), lambda i, r: (i, 0)),
                  pl.BlockSpec((ROW_TILE, LANES), lambda i, r: (i, 0)),
                  pl.BlockSpec(g.shape, lambda i, r: (0, 0)),
                  pl.BlockSpec(b.shape, lambda i, r: (0, 0)),
                  pl.BlockSpec(memory_space=pl.ANY)],
        out_specs=pl.BlockSpec((ROW_TILE, D_MODEL), lambda i, r: (i, 0)),
        scratch_shapes=[pltpu.VMEM((2, STAGE_ROWS, PACKED_W), jnp.uint32),
                        pltpu.SemaphoreType.DMA((2,))],
    )
    return pl.pallas_call(
        _combine_kernel,
        out_shape=jax.ShapeDtypeStruct((T, D_MODEL), F32),
        grid_spec=grid_spec,
        compiler_params=_cparams("arbitrary"),
        name="combine_ln",
    )(runs, h, idx, gate, g, b, ys)


def _rot_cols(w):
    half = w.shape[-1] // 2
    return jnp.concatenate([-w[:, half:], w[:, :half]], axis=-1)


def _pack_w_in(w_in):
    lat = Q_LORA + KV_LORA
    w_kr = w_in[:, lat:lat + MLA_ROPE]
    return jnp.concatenate([w_in[:, lat + MLA_ROPE:], w_in[:, :lat + MLA_ROPE], _rot_cols(w_kr)],
                           axis=-1).astype(BF16)


def _pack_w_uq(w_uq):
    w = w_uq.reshape(Q_LORA, MLA_HEADS, MLA_NOPE + MLA_ROPE)
    rope = w[:, :, MLA_NOPE:]
    rot = jnp.concatenate([-rope[:, :, MLA_ROPE // 2:], rope[:, :, :MLA_ROPE // 2]], axis=-1)
    return jnp.concatenate([w, rot], axis=-1).reshape(Q_LORA, MLA_HEADS * MLA_QK).astype(BF16)


def _rope_freqs():
    ret = 1.0 / (10000.0 ** jnp.linspace(0.0, 1.0, RET_DK // 2, dtype=F32))
    mla = 1.0 / (ROPE_BASE ** (jnp.arange(0, MLA_ROPE, 2, dtype=F32) / MLA_ROPE))
    return jnp.concatenate([ret, mla, jnp.zeros((LANES - ret.shape[0] - mla.shape[0],), F32)])[None, :]


def _routing_plan(cnt, n_tokens):
    counts = cnt[::SUBLANES, :N_EXPERTS]
    n_tiles = counts.shape[0]
    run_len = ((counts + SUBLANES - 1) // SUBLANES) * SUBLANES
    stage_off = jnp.cumsum(run_len, axis=1) - run_len
    tile_off = jnp.cumsum(run_len, axis=0) - run_len
    rows_used = jnp.sum(run_len, axis=0)
    padded = ((rows_used + EXPERT_TILE - 1) // EXPERT_TILE) * EXPERT_TILE
    pad_end = jnp.cumsum(padded)
    pad_start = pad_end - padded
    runs = jnp.stack([stage_off, pad_start[None, :] + tile_off, run_len], axis=-1).reshape(-1).astype(jnp.int32)
    nact = (pad_end[-1:] // EXPERT_TILE).astype(jnp.int32)
    gaps = jnp.concatenate([jnp.stack([pad_start + rows_used, padded - rows_used], axis=-1).reshape(-1),
                            nact]).astype(jnp.int32)
    n_rows = n_tokens * TOP_K + (SUBLANES - 1) * n_tiles * N_EXPERTS + N_EXPERTS * EXPERT_TILE
    n_rows = -(-n_rows // EXPERT_TILE) * EXPERT_TILE
    n_blocks = n_rows // EXPERT_TILE
    experts = jnp.arange(N_EXPERTS, dtype=jnp.int32)
    blk_row = jnp.arange(n_blocks, dtype=jnp.int32) * EXPERT_TILE
    blk_e = jnp.minimum(jnp.sum((pad_end[None, :] <= blk_row[:, None]).astype(jnp.int32), axis=-1),
                        N_EXPERTS - 1)
    later = jnp.where((rows_used > 0)[None, :] & (experts[None, :] > experts[:, None]), experts[None, :], N_EXPERTS)
    nxt = jnp.min(later, axis=-1)
    nxt = jnp.where(nxt == N_EXPERTS, -1, nxt)
    next_e = jnp.sum(jnp.where(blk_e[:, None] == experts[None, :], nxt[None, :], 0), axis=-1).astype(jnp.int32)
    return runs, gaps, blk_e, nact, next_e, n_rows


def kernel(x, positions, w_in, q_norm_g, w_uq, kv_norm_g, w_ukv, w_o, ln1_g, ln1_b,
           w_router, b_router, w_up, b_up, w_down, b_down, ln2_g, ln2_b):
    B, S, _ = x.shape
    T = B * S
    pos_col = positions.reshape(T, 1).astype(jnp.int32)
    rc, rs, mc, ms = _rope_tables(pos_col, _rope_freqs())
    dec, qw, kw, cd = _retention_constants()
    h = x.reshape(T, D_MODEL)
    for l in range(DEPTH):
        proj = _input_projection(h, _pack_w_in(w_in[l]))
        w_kv = w_ukv[l].reshape(KV_LORA, MLA_HEADS, MLA_NOPE + MLA_V)
        w_k = w_kv[:, :, :MLA_NOPE].reshape(KV_LORA, MLA_HEADS * MLA_NOPE).astype(BF16)
        w_vt = w_kv[:, :, MLA_NOPE:].reshape(KV_LORA, MLA_HEADS * MLA_V).T.astype(BF16)
        q_cat, kn, vt, kr = _mla_prep(proj, mc, ms, _pack_w_uq(w_uq[l]), w_k, w_vt,
                                      q_norm_g[l][None, :], kv_norm_g[l][None, :])
        o_mla = _attention(q_cat, kn, kr, vt, B, S)
        o_ret = _retention(proj, rc, rs, dec, qw, kw, cd, B, S)
        h1, h1b, idx_out, gate, cnt = _merge_and_route(
            h, o_mla, o_ret, proj, w_o[l].astype(BF16), ln1_g[l][None, :], ln1_b[l][None, :],
            w_router[l], b_router[l])
        runs, gaps, blk_e, nact, next_e, n_rows = _routing_plan(cnt, T)
        xs = _dispatch(runs, gaps, h1b, idx_out, n_rows)
        ys = _experts(blk_e, nact, next_e, xs, w_up[l], b_up[l], w_down[l], b_down[l])
        h = _combine(runs, h1, idx_out, gate, ys, ln2_g[l][None, :], ln2_b[l][None, :])
    return h.reshape(B, S, D_MODEL)
```

```python
import functools

import numpy as np
import jax
import jax.numpy as jnp
from jax import lax
from jax.experimental import pallas as pl
from jax.experimental.pallas import tpu as pltpu

F32 = jnp.float32
BF16 = jnp.bfloat16

D_MODEL = 1024
DEPTH = 1
MLA_HEADS = 8
MLA_NOPE = 128
MLA_ROPE = 64
MLA_V = 128
Q_LORA = 384
KV_LORA = 256
ROPE_BASE = 10000.0
RET_HEADS = 8
RET_DK = 128
RET_DV = 128
RET_CHUNK = 128
N_EXPERTS = 32
TOP_K = 4
D_EXPERT = 1024
SWIGLU_LIMIT = 7.0
SWIGLU_ALPHA = 1.702
DN_ALPHA = (2.0 * DEPTH) ** 0.25
LN_EPS = 1e-5
RMS_EPS = 1e-6
GN_EPS = 1e-6
NEG_INF = -1e30

LANES = 128
SUBLANES = 8
MLA_QK = 2 * LANES
PACKED_W = D_MODEL // 2
LATENT_W = 768
N_WIDE = 6 * D_MODEL
N_PROJ = N_WIDE + LATENT_W
VMEM_LIMIT = 56 * 1024 * 1024

ROW_TILE = 512
ATTN_TILE = ROW_TILE
ATTN_Q_TILE = 2 * ATTN_TILE
ATTN_Q_SPLIT = 4
VT_ROWS = MLA_V + 16
LOG2_E = 1.4426950408889634
RET_GROUP = 16
EXPERT_TILE = 256
EXPERT_STEP = 2 * EXPERT_TILE
STAGE_ROWS = -(-(ROW_TILE * TOP_K + (SUBLANES - 1) * N_EXPERTS) // EXPERT_TILE) * EXPERT_TILE


def _cparams(*sem):
    return pltpu.CompilerParams(dimension_semantics=sem, vmem_limit_bytes=VMEM_LIMIT)


def _const_spec(shape):
    nd = len(shape)
    return pl.BlockSpec(shape, lambda *_: (0,) * nd, pipeline_mode=pl.Buffered(1))


def _sigmoid(x):
    return 1.0 / (1.0 + jnp.exp(-x))


def _pack_rows(x):
    n = x.shape[-1] // 2
    xb = x.astype(BF16).astype(F32)
    lo = pltpu.bitcast(xb[:, :n], jnp.uint32) >> 16
    hi = pltpu.bitcast(xb[:, n:], jnp.uint32) & jnp.uint32(0xFFFF0000)
    return lo | hi


def _unpack_rows(w):
    lo = pltpu.bitcast(w << 16, F32)
    hi = pltpu.bitcast(w & jnp.uint32(0xFFFF0000), F32)
    return lo, hi


def _swap_halves(x):
    return pltpu.roll(x, LANES // 2, 1)


def _proj_kernel(x_ref, w_ref, o_ref):
    xb = x_ref[...].astype(BF16)
    for c in range(N_PROJ // LATENT_W):
        sl = slice(c * LATENT_W, (c + 1) * LATENT_W)
        o_ref[:, sl] = jnp.dot(xb, w_ref[:, sl], preferred_element_type=F32).astype(o_ref.dtype)


def _input_projection(x2, w_cat):
    T = x2.shape[0]
    return pl.pallas_call(
        _proj_kernel,
        out_shape=jax.ShapeDtypeStruct((T, N_PROJ), BF16),
        grid=(T // ROW_TILE,),
        in_specs=[pl.BlockSpec((ROW_TILE, D_MODEL), lambda i: (i, 0)),
                  _const_spec((D_MODEL, N_PROJ))],
        out_specs=pl.BlockSpec((ROW_TILE, N_PROJ), lambda i: (i, 0)),
        compiler_params=_cparams("parallel"),
        name="input_projection",
    )(x2, w_cat)


def _rope_table_kernel(pos_ref, freq_ref, rc_ref, rs_ref, mc_ref, ms_ref):
    ang = pos_ref[...].astype(F32) * freq_ref[...]
    c = jnp.cos(ang)
    s = jnp.sin(ang)
    lane = lax.broadcasted_iota(jnp.int32, ang.shape, 1)
    c64 = pltpu.roll(c, 64, 1)
    s64 = pltpu.roll(s, 64, 1)
    rc_ref[...] = jnp.where(lane < 64, c, c64)
    rs_ref[...] = jnp.where(lane < 64, -s, s64)
    c96 = pltpu.roll(c, 96, 1)
    s96 = pltpu.roll(s, 96, 1)
    mc_ref[...] = jnp.where(lane < 32, c64, jnp.where(lane < 64, c96, 0.0))
    ms_ref[...] = jnp.where(lane < 32, s64, jnp.where(lane < 64, s96, 0.0))


def _rope_tables(pos_col, freq_row):
    T = pos_col.shape[0]
    rows = min(2048, T)
    tab =jax.ShapeDtypeStruct((T, LANES), F32)
    spec = pl.BlockSpec((rows, LANES), lambda i: (i, 0))
    return pl.pallas_call(
        _rope_table_kernel,
        out_shape=(tab, tab, tab, tab),
        grid=(T // rows,),
        in_specs=[pl.BlockSpec((rows, 1), lambda i: (i, 0)), _const_spec((1, LANES))],
        out_specs=(spec, spec, spec, spec),
        compiler_params=_cparams("parallel"),
        name="rope_tables",
    )(pos_col, freq_row)


def _mla_prep_kernel(a_ref, mc_ref, ms_ref, wq_ref, wk_ref, wvt_ref, qg_ref, kvg_ref,
                     q_ref, kn_ref, vt_ref, kr_ref):
    a = a_ref[...].astype(F32)
    mc = mc_ref[...]
    ms = ms_ref[...]
    cq = a[:, :Q_LORA]
    qn = cq * lax.rsqrt(jnp.mean(cq * cq, axis=-1, keepdims=True) + RMS_EPS) * qg_ref[...]
    q = jnp.dot(qn.astype(BF16), wq_ref[...], preferred_element_type=F32)
    scale = (MLA_NOPE + MLA_ROPE) ** -0.5 * LOG2_E
    for h in range(MLA_HEADS):
        nope = q[:, h * MLA_QK:h * MLA_QK + LANES]
        blk = q[:, h * MLA_QK + LANES:(h + 1) * MLA_QK]
        rope = blk * mc + _swap_halves(blk) * ms
        q_ref[:, h * MLA_QK:h * MLA_QK + LANES] = (nope * scale).astype(BF16)
        q_ref[:, h * MLA_QK + LANES:(h + 1) * MLA_QK] = (rope * scale).astype(BF16)
    ckv = a[:, Q_LORA:Q_LORA + KV_LORA]
    kvn = ckv * lax.rsqrt(jnp.mean(ckv * ckv, axis=-1, keepdims=True) + RMS_EPS) * kvg_ref[...]
    kvb = kvn.astype(BF16)
    kn_ref[...] = jnp.dot(kvb, wk_ref[...], preferred_element_type=F32).astype(BF16)
    vt = lax.dot_general(wvt_ref[...], kvb, (((1,), (1,)), ((), ())),
                         preferred_element_type=F32).astype(BF16)
    ones = jnp.ones((VT_ROWS - MLA_V, vt.shape[1]), BF16)
    for h in range(MLA_HEADS):
        vt_ref[0, h * VT_ROWS:h * VT_ROWS + MLA_V, :] = vt[h * MLA_V:(h + 1) * MLA_V, :]
        vt_ref[0, h * VT_ROWS + MLA_V:(h + 1) * VT_ROWS, :] = ones
    krb =a[:, Q_LORA + KV_LORA:]
    kr_ref[...] = (krb * mc + _swap_halves(krb) * ms).astype(BF16)


def _mla_prep(proj, mc, ms, wq, wk, wvt, qg, kvg):
    T = proj.shape[0]
    row = lambda w: pl.BlockSpec((ROW_TILE, w), lambda i: (i, 0))
    return pl.pallas_call(
        _mla_prep_kernel,
        out_shape=(jax.ShapeDtypeStruct((T, MLA_HEADS * MLA_QK), BF16),
                   jax.ShapeDtypeStruct((T, MLA_HEADS * MLA_NOPE), BF16),
                   jax.ShapeDtypeStruct((T // ROW_TILE, MLA_HEADS * VT_ROWS, ROW_TILE), BF16),
                   jax.ShapeDtypeStruct((T, LANES), BF16)),
        grid=(T // ROW_TILE,),
        in_specs=[pl.BlockSpec((ROW_TILE, LATENT_W), lambda i: (i, N_WIDE // LATENT_W)),
                  row(LANES), row(LANES),
                  _const_spec(wq.shape), _const_spec(wk.shape), _const_spec(wvt.shape),
                  _const_spec(qg.shape), _const_spec(kvg.shape)],
        out_specs=(row(MLA_HEADS * MLA_QK), row(MLA_HEADS * MLA_NOPE),
                   pl.BlockSpec((1, MLA_HEADS * VT_ROWS, ROW_TILE), lambda i: (i, 0, 0)), row(LANES)),
        compiler_params=_cparams("parallel"),
        name="mla_prep",
    )(proj, mc, ms, wq, wk, wvt, qg, kvg)


def _attn_kernel(q_ref, kn_ref, kr_ref, vt_ref, o_ref, acc_scr):
    W = ATTN_Q_TILE // ATTN_Q_SPLIT
    n_diag = ATTN_Q_TILE // ATTN_TILE
    groups = tuple(range(ATTN_Q_SPLIT))

    def visible(d, c):
        return min(max((c + 1) * W - d * ATTN_TILE, 0), ATTN_TILE)

    for qi in range(q_ref.shape[0] // ATTN_Q_TILE):
        q0 = qi * ATTN_Q_TILE
        qs = [q_ref[q0 + c * W:q0 + (c + 1) * W, :] for c in groups]
        acc_scr[...] = jnp.zeros(acc_scr.shape, F32)

        def scores(j, which=groups):
            rows = slice(j * ATTN_TILE, (j + 1) * ATTN_TILE)
            k = jnp.concatenate([kn_ref[rows, :], kr_ref[rows, :]], axis=-1)
            return {c: lax.dot_general(k, qs[c], (((1,), (1,)), ((), ())), preferred_element_type=F32)
                    for c in which}

        def softmax_pv(j, s_all, m_all, diag=None):
            vt = vt_ref[j]
            out = list(m_all)
            for c in s_all:
                s = s_all[c]
                nk = ATTN_TILE if diag is None else visible(diag, c)
                if diag is not None:
                    s = s[:nk]
                    if diag * ATTN_TILE + nk - 1 > c * W:
                        kp = lax.broadcasted_iota(jnp.int32, s.shape, 0) + diag * ATTN_TILE
                        qp = lax.broadcasted_iota(jnp.int32, s.shape, 1) + c * W
                        s = jnp.where(kp <= qp, s, NEG_INF)
                m_new = jnp.maximum(m_all[c], jnp.max(s, axis=0, keepdims=True))
                alpha = jnp.exp2(m_all[c] - m_new)
                p = jnp.exp2(s - m_new).astype(BF16)
                acc_scr[c] = alpha * acc_scr[c] + jnp.dot(vt[:, :nk], p, preferred_element_type=F32)
                out[c] = m_new
            return tuple(out)

        first = n_diag * qi
        work = [(j, None, groups) for j in range(first)]
        work += [(first + d, d, tuple(c for c in groups if visible(d, c) > 0)) for d in range(n_diag)]
        m_all = tuple(jnp.full((1, W), NEG_INF, F32) for _ in groups)
        pending = [scores(j, which) for j, _, which in work[:2]]
        for n, (j, diag, _) in enumerate(work):
            m_all = softmax_pv(j, pending.pop(0), m_all, diag)
            if n + 2 < len(work):
                pending.append(scores(work[n + 2][0], work[n + 2][2]))
        for c in groups:
            acc = acc_scr[c]
            o_ref[q0 + c * W:q0 + (c + 1) * W, :] = (acc[:MLA_V] / acc[MLA_V:MLA_V + 1]).T.astype(o_ref.dtype)


def _attention(q_cat, kn, kr, vt, B, S):
    T = B * S
    return pl.pallas_call(
        _attn_kernel,
        out_shape=jax.ShapeDtypeStruct((T, MLA_HEADS * MLA_V), BF16),
        grid=(B, MLA_HEADS),
        in_specs=[pl.BlockSpec((S, MLA_QK), lambda b, h: (b, h)),
                  pl.BlockSpec((S, MLA_NOPE), lambda b, h: (b, h)),
                  pl.BlockSpec((S, LANES), lambda b, h: (b, 0)),
                  pl.BlockSpec((S // ATTN_TILE, VT_ROWS, ATTN_TILE), lambda b, h: (b, h, 0))],
        out_specs=pl.BlockSpec((S, MLA_V), lambda b, h: (b, h)),
        scratch_shapes=[pltpu.VMEM((ATTN_Q_SPLIT, VT_ROWS, ATTN_Q_TILE // ATTN_Q_SPLIT), F32)],
        compiler_params=_cparams("parallel", "parallel"),
        name="mla_attention",
    )(q_cat, kn, kr, vt)


def _ret_kernel(q_ref, k_ref, v_ref, g_ref, rc_ref, rs_ref, dec_ref, qw_ref, kw_ref, cd_ref,
                o_ref, qb_scr, qwb_scr, kb_scr, kwb_scr):
    S = q_ref.shape[0]
    C = RET_CHUNK
    n_chunks = S // C
    rc = rc_ref[...]
    rs = rs_ref[...]
    q = q_ref[...].astype(F32)
    k = k_ref[...].astype(F32)
    q = q * rc + _swap_halves(q) * rs
    k = (k * rc + _swap_halves(k) * rs) * (RET_DK ** -0.5)
    qw = qw_ref[0]
    kw = kw_ref[0]
    q3 = q.reshape(n_chunks, C, RET_DK)
    k3 = k.reshape(n_chunks, C, RET_DK)
    qb_scr[...] = q.astype(BF16)
    kb_scr[...] = k.astype(BF16)
    qwb_scr[...] = (q3 * qw[None]).reshape(S, RET_DK).astype(BF16)
    kwb_scr[...] = (k3 * kw[None]).reshape(S, RET_DK)
    dec = dec_ref[0]
    cd = cd_ref[0]

    def group(gi, R):
        rows_of = [pl.ds(pl.multiple_of((gi * RET_GROUP + u) * C, C), C) for u in range(RET_GROUP)]
        scores, updates = [], []
        for rows in rows_of:
            scores.append(lax.dot_general(qb_scr[rows, :], kb_scr[rows, :], (((1,), (1,)), ((), ())),
                                          preferred_element_type=F32))
            kwt = kwb_scr[rows, :].T.astype(BF16)
            updates.append(jnp.dot(kwt, v_ref[rows, :], preferred_element_type=F32))
        states = []
        for u in range(RET_GROUP):
            states.append(R.astype(BF16))
            R = cd * R + updates[u]
        for u, rows in enumerate(rows_of):
            o = jnp.dot((scores[u] * dec).astype(BF16), v_ref[rows, :], preferred_element_type=F32)
            o = o + jnp.dot(qwb_scr[rows, :], states[u], preferred_element_type=F32)
            mu = jnp.mean(o, axis=-1, keepdims=True)
            d = o - mu
            var = jnp.mean(d * d, axis=-1, keepdims=True)
            gate = g_ref[rows, :].astype(F32)
            o_ref[rows, :] = (d * lax.rsqrt(var + GN_EPS) * (gate * _sigmoid(gate))).astype(o_ref.dtype)
        return R

    lax.fori_loop(0, n_chunks // RET_GROUP, group, jnp.zeros((RET_DK, RET_DV), F32))


def _retention(proj, rc, rs, dec, qw, kw, cd, B, S):
    assert S % (RET_CHUNK * RET_GROUP) == 0, "sequence must hold whole groups of retention chunks"
    T = B * S
    H = RET_HEADS
    col = lambda off: pl.BlockSpec((S, LANES), lambda b, h: (b, off + h))
    tab = pl.BlockSpec((S, LANES), lambda b, h: (b, 0))
    per_head = lambda r: pl.BlockSpec((1, r, LANES), lambda b, h: (h, 0, 0))
    return pl.pallas_call(
        _ret_kernel,
        out_shape=jax.ShapeDtypeStruct((T, H * RET_DV), BF16),
        grid=(B, H),
        in_specs=[col(0), col(H), col(2 * H), col(3 * H), tab, tab,
                  per_head(RET_CHUNK), per_head(RET_CHUNK), per_head(RET_CHUNK), per_head(1)],
        out_specs=pl.BlockSpec((S, RET_DV), lambda b, h: (b, h)),
        scratch_shapes=[pltpu.VMEM((S, RET_DK), BF16), pltpu.VMEM((S, RET_DK), BF16),
                        pltpu.VMEM((S, RET_DK), BF16), pltpu.VMEM((S, RET_DK), F32)],
        compiler_params=_cparams("parallel", "parallel"),
        name="retention",
    )(proj, proj, proj, proj, rc, rs, dec, qw, kw, cd)


def _retention_constants():
    C = RET_CHUNK
    h = np.arange(RET_HEADS, dtype=np.float64)
    log_gamma = np.log(1.0 - 2.0 ** (-5.0 - h))
    idx = np.arange(C, dtype=np.float64)
    rel = idx[:, None] - idx[None, :]
    dec = np.where(rel[None] >= 0, np.exp(np.maximum(rel, 0.0)[None] * log_gamma[:, None, None]), 0.0)
    qw = np.exp((idx + 1.0)[None, :] * log_gamma[:, None])
    kw = np.exp((C - 1.0 - idx)[None, :] * log_gamma[:, None])
    cd = np.exp(C * log_gamma)
    bc = lambda a: np.ascontiguousarray(np.broadcast_to(a[..., None], a.shape + (LANES,))).astype(np.float32)
    return dec.astype(np.float32), bc(qw), bc(kw), bc(cd[:, None])


def _layernorm(y, g, b):
    mu = jnp.mean(y, axis=-1, keepdims=True)
    d = y - mu
    var = jnp.mean(d * d, axis=-1, keepdims=True)
    return d * lax.rsqrt(var + LN_EPS) * g + b


def _merge_kernel(x_ref, om_ref, or_ref, gm_ref, gr_ref, wo_ref, g_ref, b_ref, w01_ref, rb_ref,
                  h_ref, hb_ref, idx_ref, gate_ref, cnt_ref):
    mixed = (_sigmoid(gm_ref[...].astype(F32)) * om_ref[...].astype(F32)
             + _sigmoid(gr_ref[...].astype(F32)) * or_ref[...].astype(F32))
    mix = jnp.dot(mixed.astype(BF16), wo_ref[...], preferred_element_type=F32)
    h = _layernorm(DN_ALPHA * x_ref[...] + mix, g_ref[...], b_ref[...])
    h_ref[...] = h
    hb_ref[...] = h.astype(BF16)
    _route_tile(h, w01_ref, rb_ref, idx_ref, gate_ref, cnt_ref)


def _merge_and_route(x2, o_mla, o_ret, proj, wo, g, b, w_router, b_router):
    T = x2.shape[0]
    wpad = jnp.zeros((D_MODEL, LANES), F32).at[:, :N_EXPERTS].set(w_router)
    w0 = wpad.astype(BF16)
    w01 = jnp.concatenate([w0, (wpad - w0.astype(F32)).astype(BF16)], axis=-1)
    bpad = jnp.zeros((1, LANES), F32).at[0, :N_EXPERTS].set(b_router)
    row = lambda c: pl.BlockSpec((ROW_TILE, D_MODEL), lambda i: (i, c))
    lanes = pl.BlockSpec((ROW_TILE, LANES), lambda i: (i, 0))
    return pl.pallas_call(
        _merge_kernel,
        out_shape=(jax.ShapeDtypeStruct((T, D_MODEL), F32), jax.ShapeDtypeStruct((T, D_MODEL), BF16),
                   jax.ShapeDtypeStruct((T, LANES), jnp.int32), jax.ShapeDtypeStruct((T, LANES), F32),
                   jax.ShapeDtypeStruct((T // ROW_TILE * SUBLANES, LANES), jnp.int32)),
        grid=(T // ROW_TILE,),
        in_specs=[row(0), row(0), row(0), row(4), row(5),
                  _const_spec(wo.shape), _const_spec(g.shape), _const_spec(b.shape),
                  _const_spec(w01.shape), _const_spec(bpad.shape)],
        out_specs=(row(0), row(0), lanes, lanes, pl.BlockSpec((SUBLANES, LANES), lambda i: (i, 0))),
        compiler_params=_cparams("parallel"),
        name="merge_out_proj_ln_route",
    )(x2, o_mla, o_ret, proj, proj, wo, g, b, w01, bpad)


def _split2(a):
    a0 = a.astype(BF16)
    return a0, (a - a0.astype(F32)).astype(BF16)


def _route_tile(h, w01_ref, b_ref, idx_ref, gate_ref, cnt_ref):
    h0, h1 = _split2(h)
    p0 = jnp.dot(h0, w01_ref[...], preferred_element_type=F32)
    p1 = jnp.dot(h1, w01_ref[:, :LANES], preferred_element_type=F32)
    logits = p0[:, :LANES] + (p0[:, LANES:] + p1) + b_ref[...]
    rows = logits.shape[0]
    lane = lax.broadcasted_iota(jnp.int32, (rows, LANES), 1)
    lane_f = lane.astype(F32)
    work = jnp.where(lane < N_EXPERTS, logits, -jnp.inf)
    vals, ids, sels = [], [], []
    for _ in range(TOP_K):
        m = jnp.max(work, axis=-1, keepdims=True)
        first = jnp.min(jnp.where(work == m, lane_f, float(LANES)), axis=-1, keepdims=True)
        sel = lane_f == first
        work = jnp.where(sel, -jnp.inf, work)
        vals.append(m)
        ids.append(first)
        sels.append(sel)
    exps = [jnp.exp(v - vals[0]) for v in vals]
    denom = exps[0] + exps[1] + exps[2] + exps[3]
    chosen = jnp.zeros((rows, LANES), F32)
    for sel in sels:
        chosen = jnp.where(sel, 1.0, chosen)
    r = lax.broadcasted_iota(jnp.int32, (rows, rows), 0)
    c = lax.broadcasted_iota(jnp.int32, (rows, rows), 1)
    lower = jnp.where(r > c, 1.0, 0.0).astype(BF16)
    before = jnp.dot(lower, chosen.astype(BF16), preferred_element_type=F32)
    counts = jnp.sum(chosen, axis=0, keepdims=True)
    run_len = jnp.floor((counts + (SUBLANES - 1)) * (1.0 / SUBLANES)) * SUBLANES
    er = lax.broadcasted_iota(jnp.int32, (LANES, LANES), 0)
    ec = lax.broadcasted_iota(jnp.int32, (LANES, LANES), 1)
    earlier = jnp.where(er < ec, 1.0, 0.0).astype(BF16)
    run_start = jnp.dot(jnp.broadcast_to(run_len, (SUBLANES, LANES)).astype(BF16), earlier,
                        preferred_element_type=F32)[:1]
    slot = before + run_start
    idx_out = jnp.zeros((rows, LANES), jnp.int32)
    gate_out = jnp.zeros((rows, LANES), F32)
    for kk in range(TOP_K):
        pos = jnp.sum(jnp.where(sels[kk], slot, 0.0), axis=-1, keepdims=True).astype(jnp.int32)
        idx_out = jnp.where(lane == kk, ids[kk].astype(jnp.int32), idx_out)
        idx_out = jnp.where(lane == TOP_K + kk, pos, idx_out)
        gate_out = jnp.where(lane == kk, exps[kk] / denom, gate_out)
    idx_ref[...] = idx_out
    gate_ref[...] = gate_out
    cnt_ref[...] = jnp.broadcast_to(counts, cnt_ref.shape).astype(jnp.int32)


def _run_copy(runs_ref, tile, e, stage, hbm, sem, to_hbm):
    base = (tile * N_EXPERTS + e) * 3
    s = pl.multiple_of(runs_ref[base], SUBLANES)
    g = pl.multiple_of(runs_ref[base + 1], SUBLANES)
    n = pl.multiple_of(runs_ref[base + 2], SUBLANES)
    src, dst = stage.at[pl.ds(s, n), :], hbm.at[pl.ds(g, n), :]
    if not to_hbm:
        src, dst = dst, src
    return n, pltpu.make_async_copy(src, dst, sem)


def _for_each_run(runs_ref, tile, stage, hbm, sem, to_hbm, action):
    def one(e, carry):
        n, cp = _run_copy(runs_ref, tile, e, stage, hbm, sem, to_hbm)

        @pl.when(n > 0)
        def _():
            getattr(cp, action)()

        return carry

    lax.fori_loop(0, N_EXPERTS, one, 0)


def _dispatch_kernel(runs_ref, gaps_ref, h_ref, idx_ref, xs_ref, stage, zeros, sems):
    i = pl.program_id(0)

    def gap_copy(e):
        g = pl.multiple_of(gaps_ref[2 * e], SUBLANES)
        n = pl.multiple_of(gaps_ref[2 * e + 1], SUBLANES)
        return n, pltpu.make_async_copy(zeros.at[pl.ds(0, n), :], xs_ref.at[pl.ds(g, n), :], sems.at[2])

    def for_each_gap(action):
        def one(e, carry):
            n, cp = gap_copy(e)

            @pl.when(n > 0)
            def _():
                getattr(cp, action)()

            return carry

        lax.fori_loop(0, N_EXPERTS, one, 0)

        def tail(b, carry):
            rows = pl.ds(pl.multiple_of(b * EXPERT_TILE, EXPERT_TILE), EXPERT_TILE)
            getattr(pltpu.make_async_copy(zeros.at[pl.ds(0, EXPERT_TILE), :], xs_ref.at[rows, :], sems.at[2]),
                    action)()
            return carry

        lax.fori_loop(gaps_ref[2 * N_EXPERTS], xs_ref.shape[0] // EXPERT_TILE, tail, 0)

    @pl.when(i == 0)
    def _():
        zeros[...] = jnp.zeros(zeros.shape, zeros.dtype)
        for_each_gap("start")

    pos_t = idx_ref[...].astype(F32).T.astype(jnp.int32)
    h = h_ref[...]
    chunk = EXPERT_TILE
    row_id = lax.broadcasted_iota(jnp.int32, (chunk, ROW_TILE), 0)
    slot = i % 2
    for c in range(STAGE_ROWS // chunk):
        onehot = jnp.zeros((chunk, ROW_TILE), F32)
        for kk in range(TOP_K):
            onehot = jnp.where(row_id == (pos_t[TOP_K + kk:TOP_K + kk + 1] - c * chunk), 1.0, onehot)
        stage[slot, c * chunk:(c + 1) * chunk, :] = _pack_rows(
            jnp.dot(onehot.astype(BF16), h, preferred_element_type=F32))

    _for_each_run(runs_ref, i, stage.at[slot], xs_ref, sems.at[slot], True, "start")

    @pl.when(i > 0)
    def _():
        _for_each_run(runs_ref, i - 1, stage.at[1 - slot], xs_ref, sems.at[1 - slot], True, "wait")

    @pl.when(i == pl.num_programs(0) - 1)
    def _():
        _for_each_run(runs_ref, i, stage.at[slot], xs_ref, sems.at[slot], True, "wait")

    @pl.when(i == 0)
    def _():
        for_each_gap("wait")


def _dispatch(runs, gaps, hb, idx, n_rows):
    T = hb.shape[0]
    grid_spec = pltpu.PrefetchScalarGridSpec(
        num_scalar_prefetch=2,
        grid=(T // ROW_TILE,),
        in_specs=[pl.BlockSpec((ROW_TILE, D_MODEL), lambda i, r, g: (i, 0)),
                  pl.BlockSpec((ROW_TILE, LANES), lambda i, r, g: (i, 0))],
        out_specs=pl.BlockSpec(memory_space=pl.ANY),
        scratch_shapes=[pltpu.VMEM((2, STAGE_ROWS, PACKED_W), jnp.uint32),
                        pltpu.VMEM((EXPERT_STEP, PACKED_W), jnp.uint32),
                        pltpu.SemaphoreType.DMA((3,))],
    )
    return pl.pallas_call(
        _dispatch_kernel,
        out_shape=jax.ShapeDtypeStruct((n_rows, PACKED_W), jnp.uint32),
        grid_spec=grid_spec,
        compiler_params=_cparams("arbitrary"),
        name="dispatch_rows",
    )(runs, gaps, hb, idx)


def _expert_kernel(blk_e_ref, nact_ref, next_e_ref, nsub_ref, x_ref, bu_ref, bd_ref, wu_hbm, wd_hbm, y_ref,
                   wuf_scr, wdf_scr, wub_scr, wdb_scr, sems):
    i = pl.program_id(0)
    e = blk_e_ref[i]

    def weight_copies(expert):
        return (pltpu.make_async_copy(wu_hbm.at[expert], wuf_scr, sems.at[0]),
                pltpu.make_async_copy(wd_hbm.at[expert], wdf_scr, sems.at[1]))

    @pl.when(i == 0)
    def _():
        for cp in weight_copies(e):
            cp.start()

    first_of_group = jnp.logical_or(i == 0, e != blk_e_ref[jnp.maximum(i - 1, 0)])

    @pl.when(jnp.logical_and(i < nact_ref[0], first_of_group))
    def _():
        for cp in weight_copies(e):
            cp.wait()
        wub_scr[...] = wuf_scr[...].astype(BF16)
        wdb_scr[...] = wdf_scr[...].astype(BF16)

        @pl.when(next_e_ref[i] >= 0)
        def _():
            for cp in weight_copies(next_e_ref[i]):
                cp.start()

    for sub in range(EXPERT_STEP // EXPERT_TILE):
        rows = slice(sub * EXPERT_TILE, (sub + 1) * EXPERT_TILE)

        @pl.when(sub < nsub_ref[i])
        def _():
            lo, hi = _unpack_rows(x_ref[rows, :])
            x = jnp.concatenate([lo.astype(BF16), hi.astype(BF16)], axis=-1)
            glu = jnp.dot(x, wub_scr[:, :D_EXPERT], preferred_element_type=F32)
            lin = jnp.dot(x, wub_scr[:, D_EXPERT:], preferred_element_type=F32)
            glu = jnp.minimum(glu + bu_ref[0, :, :D_EXPERT], SWIGLU_LIMIT)
            lin = jnp.clip(lin + bu_ref[0, :, D_EXPERT:], -SWIGLU_LIMIT, SWIGLU_LIMIT)
            a = glu * _sigmoid(SWIGLU_ALPHA * glu) * (lin + 1.0)
            y = jnp.dot(a.astype(BF16), wdb_scr[...], preferred_element_type=F32)
            y_ref[rows, :] = _pack_rows(y + bd_ref[0])

        @pl.when(sub >= nsub_ref[i])
        def _():
            y_ref[rows, :] = jnp.zeros((EXPERT_TILE, PACKED_W), y_ref.dtype)


def _experts(blk_e, nact, next_e, nsub, xs, w_up, b_up, w_down, b_down):
    P = xs.shape[0]
    n_blocks = P // EXPERT_STEP
    act = lambda i, na: jnp.minimum(i, na[0] - 1)
    grid_spec = pltpu.PrefetchScalarGridSpec(
        num_scalar_prefetch=4,
        grid=(n_blocks,),
        in_specs=[pl.BlockSpec((EXPERT_STEP, PACKED_W), lambda i, be, na, ne, ns: (act(i, na), 0)),
                  pl.BlockSpec((1, 1, 2 * D_EXPERT), lambda i, be, na, ne, ns: (be[act(i, na)], 0, 0)),
                  pl.BlockSpec((1, 1, D_MODEL), lambda i, be, na, ne, ns: (be[act(i, na)], 0, 0)),
                  pl.BlockSpec(memory_space=pl.ANY),
                  pl.BlockSpec(memory_space=pl.ANY)],
        out_specs=pl.BlockSpec((EXPERT_STEP, PACKED_W), lambda i, be, na, ne, ns: (i, 0)),
        scratch_shapes=[pltpu.VMEM((D_MODEL, 2 * D_EXPERT), F32),
                        pltpu.VMEM((D_EXPERT, D_MODEL), F32),
                        pltpu.VMEM((D_MODEL, 2 * D_EXPERT), BF16),
                        pltpu.VMEM((D_EXPERT, D_MODEL), BF16),
                        pltpu.SemaphoreType.DMA((2,))],
    )
    return pl.pallas_call(
        _expert_kernel,
        out_shape=jax.ShapeDtypeStruct((P, PACKED_W), jnp.uint32),
        grid_spec=grid_spec,
        compiler_params=_cparams("arbitrary"),
        name="routed_experts",
    )(blk_e, nact, next_e, nsub, xs, b_up.reshape(N_EXPERTS, 1, 2 * D_EXPERT),
      b_down.reshape(N_EXPERTS, 1, D_MODEL), w_up, w_down)


def _combine_kernel(runs_ref, h_ref, idx_ref, gate_ref, g_ref, b_ref, ys_ref, o_ref, ybuf, sems):
    i = pl.program_id(0)
    n = pl.num_programs(0)

    def gather(tile, slot, action):
        _for_each_run(runs_ref, tile, ybuf.at[slot], ys_ref, sems.at[slot], False, action)

    @pl.when(i == 0)
    def _():
        ybuf[...] = jnp.zeros(ybuf.shape, ybuf.dtype)
        gather(0, 0, "start")

    @pl.when(i + 1 < n)
    def _():
        gather(i + 1, (i + 1) % 2, "start")

    slot = i % 2
    gather(i, slot, "wait")

    idx = idx_ref[...]
    gate = gate_ref[...]
    chunk = EXPERT_TILE
    col_id = lax.broadcasted_iota(jnp.int32, (ROW_TILE, LANES), 1)
    pos_b = [jnp.broadcast_to(idx[:, TOP_K + kk:TOP_K + kk + 1], (ROW_TILE, LANES)) for kk in range(TOP_K)]
    gate_b = [jnp.broadcast_to(gate[:, kk:kk + 1], (ROW_TILE, LANES)) for kk in range(TOP_K)]
    f_lo = jnp.zeros((ROW_TILE, PACKED_W), F32)
    f_hi = jnp.zeros((ROW_TILE, PACKED_W), F32)
    for c in range(STAGE_ROWS // chunk):
        parts = []
        for off in range(c * chunk, (c + 1) * chunk, LANES):
            part = jnp.zeros((ROW_TILE, LANES), F32)
            for kk in range(TOP_K):
                part = jnp.where(col_id + off == pos_b[kk], gate_b[kk], part)
            parts.append(part)
        wb = jnp.concatenate(parts, axis=-1).astype(BF16)
        lo, hi = _unpack_rows(ybuf[slot, c * chunk:(c + 1) * chunk, :])
        f_lo = f_lo + jnp.dot(wb, lo.astype(BF16), preferred_element_type=F32)
        f_hi = f_hi + jnp.dot(wb, hi.astype(BF16), preferred_element_type=F32)
    y_lo = DN_ALPHA * h_ref[:, :PACKED_W] + f_lo
    y_hi = DN_ALPHA * h_ref[:, PACKED_W:] + f_hi
    mu = (jnp.sum(y_lo, axis=-1, keepdims=True) + jnp.sum(y_hi, axis=-1, keepdims=True)) / D_MODEL
    d_lo = y_lo - mu
    d_hi = y_hi - mu
    var = (jnp.sum(d_lo * d_lo, axis=-1, keepdims=True) + jnp.sum(d_hi * d_hi, axis=-1, keepdims=True)) / D_MODEL
    inv = lax.rsqrt(var + LN_EPS)
    o_ref[:, :PACKED_W] = d_lo * inv * g_ref[:, :PACKED_W] + b_ref[:, :PACKED_W]
    o_ref[:, PACKED_W:] = d_hi * inv * g_ref[:, PACKED_W:] + b_ref[:, PACKED_W:]


def _combine(runs, h, idx, gate, ys, g, b):
    T = h.shape[0]
    grid_spec = pltpu.PrefetchScalarGridSpec(
        num_scalar_prefetch=1,
        grid=(T // ROW_TILE,),
        in_specs=[pl.BlockSpec((ROW_TILE, D_MODEL), lambda i, r: (i, 0)),
                  pl.BlockSpec((ROW_TILE, LANES), lambda i, r: (i, 0)),
                  pl.BlockSpec((ROW_TILE, LANES), lambda i, r: (i, 0)),
                  pl.BlockSpec(g.shape, lambda i, r: (0, 0)),
                  pl.BlockSpec(b.shape, lambda i, r: (0, 0)),
                  pl.BlockSpec(memory_space=pl.ANY)],
        out_specs=pl.BlockSpec((ROW_TILE, D_MODEL), lambda i, r: (i, 0)),
        scratch_shapes=[pltpu.VMEM((2, STAGE_ROWS, PACKED_W), jnp.uint32),
                        pltpu.SemaphoreType.DMA((2,))],
    )
    return pl.pallas_call(
        _combine_kernel,
        out_shape=jax.ShapeDtypeStruct((T, D_MODEL), F32),
        grid_spec=grid_spec,
        compiler_params=_cparams("arbitrary"),
        name="combine_ln",
    )(runs, h, idx, gate, g, b, ys)


def _rot_cols(w):
    half = w.shape[-1] // 2
    return jnp.concatenate([-w[:, half:], w[:, :half]], axis=-1)


def _pack_w_in(w_in):
    lat = Q_LORA + KV_LORA
    w_kr = w_in[:, lat:lat + MLA_ROPE]
    return jnp.concatenate([w_in[:, lat + MLA_ROPE:], w_in[:, :lat + MLA_ROPE], _rot_cols(w_kr)],
                           axis=-1).astype(BF16)


def _pack_w_uq(w_uq):
    w = w_uq.reshape(Q_LORA, MLA_HEADS, MLA_NOPE + MLA_ROPE)
    rope = w[:, :, MLA_NOPE:]
    rot = jnp.concatenate([-rope[:, :, MLA_ROPE // 2:], rope[:, :, :MLA_ROPE // 2]], axis=-1)
    return jnp.concatenate([w, rot], axis=-1).reshape(Q_LORA, MLA_HEADS * MLA_QK).astype(BF16)


def _rope_freqs():
    ret = 1.0 / (10000.0 ** jnp.linspace(0.0, 1.0, RET_DK // 2, dtype=F32))
    mla = 1.0 / (ROPE_BASE ** (jnp.arange(0, MLA_ROPE, 2, dtype=F32) / MLA_ROPE))
    return jnp.concatenate([ret, mla, jnp.zeros((LANES - ret.shape[0] - mla.shape[0],), F32)])[None, :]


def _routing_plan(cnt, n_tokens):
    counts = cnt[::SUBLANES, :N_EXPERTS]
    n_tiles = counts.shape[0]
    run_len = ((counts + SUBLANES - 1) // SUBLANES) * SUBLANES
    stage_off = jnp.cumsum(run_len, axis=1) - run_len
    tile_off = jnp.cumsum(run_len, axis=0) - run_len
    rows_used = jnp.sum(run_len, axis=0)
    padded = ((rows_used + EXPERT_STEP - 1) // EXPERT_STEP) * EXPERT_STEP
    pad_end = jnp.cumsum(padded)
    pad_start = pad_end - padded
    runs = jnp.stack([stage_off, pad_start[None, :] + tile_off, run_len], axis=-1).reshape(-1).astype(jnp.int32)
    nact = (pad_end[-1:] // EXPERT_STEP).astype(jnp.int32)
    gaps = jnp.concatenate([jnp.stack([pad_start + rows_used, padded - rows_used], axis=-1).reshape(-1),
                            pad_end[-1:] // EXPERT_TILE]).astype(jnp.int32)
    n_rows = n_tokens * TOP_K + (SUBLANES - 1) * n_tiles * N_EXPERTS + N_EXPERTS * EXPERT_STEP
    n_rows = -(-n_rows // EXPERT_STEP) * EXPERT_STEP
    n_blocks = n_rows // EXPERT_STEP
    experts = jnp.arange(N_EXPERTS, dtype=jnp.int32)
    blk_row = jnp.arange(n_blocks, dtype=jnp.int32) * EXPERT_STEP
    blk_e = jnp.minimum(jnp.sum((pad_end[None, :] <= blk_row[:, None]).astype(jnp.int32), axis=-1),
                        N_EXPERTS - 1)
    used_end = jnp.sum(jnp.where(blk_e[:, None] == experts[None, :], (pad_start + rows_used)[None, :], 0), axis=-1)
    nsub = jnp.clip((used_end - blk_row + EXPERT_TILE - 1) // EXPERT_TILE, 0,
                    EXPERT_STEP // EXPERT_TILE).astype(jnp.int32)
    later = jnp.where((rows_used > 0)[None, :] & (experts[None, :] > experts[:, None]), experts[None, :], N_EXPERTS)
    nxt = jnp.min(later, axis=-1)
    nxt = jnp.where(nxt == N_EXPERTS, -1, nxt)
    next_e = jnp.sum(jnp.where(blk_e[:, None] == experts[None, :], nxt[None, :], 0), axis=-1).astype(jnp.int32)
    return runs, gaps, blk_e, nact, next_e, nsub, n_rows


def kernel(x, positions, w_in, q_norm_g, w_uq, kv_norm_g, w_ukv, w_o, ln1_g, ln1_b,
           w_router, b_router, w_up, b_up, w_down, b_down, ln2_g, ln2_b):
    B, S, _ = x.shape
    T = B * S
    assert S % ATTN_Q_TILE == 0 and T % ROW_TILE == 0, "sequence / token count must be whole tiles"
    pos_col =positions.reshape(T, 1).astype(jnp.int32)
    rc, rs, mc, ms = _rope_tables(pos_col, _rope_freqs())
    dec, qw, kw, cd = _retention_constants()
    h = x.reshape(T, D_MODEL)
    for l in range(DEPTH):
        proj = _input_projection(h, _pack_w_in(w_in[l]))
        w_kv = w_ukv[l].reshape(KV_LORA, MLA_HEADS, MLA_NOPE + MLA_V)
        w_k = w_kv[:, :, :MLA_NOPE].reshape(KV_LORA, MLA_HEADS * MLA_NOPE).astype(BF16)
        w_vt = w_kv[:, :, MLA_NOPE:].reshape(KV_LORA, MLA_HEADS * MLA_V).T.astype(BF16)
        q_cat, kn, vt, kr = _mla_prep(proj, mc, ms, _pack_w_uq(w_uq[l]), w_k, w_vt,
                                      q_norm_g[l][None, :], kv_norm_g[l][None, :])
        o_mla = _attention(q_cat, kn, kr, vt, B, S)
        o_ret = _retention(proj, rc, rs, dec, qw, kw, cd, B, S)
        h1, h1b, idx_out, gate, cnt = _merge_and_route(
            h, o_mla, o_ret, proj, w_o[l].astype(BF16), ln1_g[l][None, :], ln1_b[l][None, :],
            w_router[l], b_router[l])
        runs, gaps, blk_e, nact, next_e, nsub, n_rows = _routing_plan(cnt, T)
        xs = _dispatch(runs, gaps, h1b, idx_out, n_rows)
        ys = _experts(blk_e, nact, next_e, nsub, xs, w_up[l], b_up[l], w_down[l], b_down[l])
        h = _combine(runs, h1, idx_out, gate, ys, ln2_g[l][None, :], ln2_b[l][None, :])
    return h.reshape(B, S, D_MODEL)
```

```python
import functools

import numpy as np
import jax
import jax.numpy as jnp
from jax import lax
from jax.experimental import pallas as pl
from jax.experimental.pallas import tpu as pltpu

F32 = jnp.float32
BF16 = jnp.bfloat16

D_MODEL = 1024
DEPTH = 1
MLA_HEADS = 8
MLA_NOPE = 128
MLA_ROPE = 64
MLA_V = 128
Q_LORA = 384
KV_LORA = 256
ROPE_BASE = 10000.0
RET_HEADS = 8
RET_DK = 128
RET_DV = 128
RET_CHUNK = 128
N_EXPERTS = 32
TOP_K = 4
D_EXPERT = 1024
SWIGLU_LIMIT = 7.0
SWIGLU_ALPHA = 1.702
DN_ALPHA = (2.0 * DEPTH) ** 0.25
LN_EPS = 1e-5
RMS_EPS = 1e-6
GN_EPS = 1e-6
NEG_INF = -1e30

LANES = 128
SUBLANES = 8
MLA_QK = 2 * LANES
PACKED_W = D_MODEL // 2
LATENT_W = 768
N_WIDE = 6 * D_MODEL
N_PROJ = N_WIDE + LATENT_W
VMEM_LIMIT = 56 * 1024 * 1024

ROW_TILE = 512
ATTN_TILE = ROW_TILE
ATTN_Q_TILE = 2 * ATTN_TILE
ATTN_Q_SPLIT = 4
VT_ROWS = MLA_V + 16
LOG2_E = 1.4426950408889634
RET_GROUP = 16
EXPERT_TILE = 256
EXPERT_STEP = 2 * EXPERT_TILE
STAGE_ROWS = -(-(ROW_TILE * TOP_K + (SUBLANES - 1) * N_EXPERTS) // EXPERT_TILE) * EXPERT_TILE


def _cparams(*sem):
    return pltpu.CompilerParams(dimension_semantics=sem, vmem_limit_bytes=VMEM_LIMIT)


def _const_spec(shape):
    nd = len(shape)
    return pl.BlockSpec(shape, lambda *_: (0,) * nd, pipeline_mode=pl.Buffered(1))


def _sigmoid(x):
    return 1.0 / (1.0 + jnp.exp(-x))


def _pack_rows(x):
    n = x.shape[-1] // 2
    xb = x.astype(BF16).astype(F32)
    lo = pltpu.bitcast(xb[:, :n], jnp.uint32) >> 16
    hi = pltpu.bitcast(xb[:, n:], jnp.uint32) & jnp.uint32(0xFFFF0000)
    return lo | hi


def _unpack_rows(w):
    lo = pltpu.bitcast(w << 16, F32)
    hi = pltpu.bitcast(w & jnp.uint32(0xFFFF0000), F32)
    return lo, hi


def _swap_halves(x):
    return pltpu.roll(x, LANES // 2, 1)


def _proj_kernel(x_ref, w_wide_ref, w_lat_ref, o_ref):
    xb = x_ref[...].astype(BF16)
    for c in range(N_WIDE // LATENT_W):
        sl = slice(c * LATENT_W, (c + 1) * LATENT_W)
        o_ref[:, sl] = jnp.dot(xb, w_wide_ref[:, sl], preferred_element_type=F32).astype(o_ref.dtype)
    o_ref[:, N_WIDE:] = jnp.dot(xb, w_lat_ref[...], preferred_element_type=F32).astype(o_ref.dtype)


def _input_projection(x2, w_wide, w_lat):
    T = x2.shape[0]
    return pl.pallas_call(
        _proj_kernel,
        out_shape=jax.ShapeDtypeStruct((T, N_PROJ), BF16),
        grid=(T // ROW_TILE,),
        in_specs=[pl.BlockSpec((ROW_TILE, D_MODEL), lambda i: (i, 0)),
                  _const_spec((D_MODEL, N_WIDE)), _const_spec((D_MODEL, LATENT_W))],
        out_specs=pl.BlockSpec((ROW_TILE, N_PROJ), lambda i: (i, 0)),
        compiler_params=_cparams("parallel"),
        name="input_projection",
    )(x2, w_wide, w_lat)


def _rope_table_kernel(pos_ref, freq_ref, rc_ref, rs_ref, mc_ref, ms_ref):
    pos = pos_ref[...].astype(F32)
    pos_t = jnp.concatenate([pos, jnp.zeros((LANES - pos.shape[0], LANES), F32)], axis=0).T
    freq = freq_ref[...]
    ang = jnp.concatenate([pos_t[:, j:j + 1] * freq for j in range(pos.shape[0])], axis=0)
    c = jnp.cos(ang)
    s = jnp.sin(ang)
    lane = lax.broadcasted_iota(jnp.int32, ang.shape, 1)
    c64 = pltpu.roll(c, 64, 1)
    s64 = pltpu.roll(s, 64, 1)
    rc_ref[...] = jnp.where(lane < 64, c, c64)
    rs_ref[...] = jnp.where(lane < 64, -s, s64)
    c96 = pltpu.roll(c, 96, 1)
    s96 = pltpu.roll(s, 96, 1)
    mc_ref[...] = jnp.where(lane < 32, c64, jnp.where(lane < 64, c96, 0.0))
    ms_ref[...] = jnp.where(lane < 32, s64, jnp.where(lane < 64, s96, 0.0))


def _rope_tables(pos2d, freq_row):
    T = pos2d.shape[0] * LANES
    rows = SUBLANES * LANES
    tab = jax.ShapeDtypeStruct((T, LANES), F32)
    spec = pl.BlockSpec((rows, LANES), lambda i: (i, 0))
    return pl.pallas_call(
        _rope_table_kernel,
        out_shape=(tab, tab, tab, tab),
        grid=(T // rows,),
        in_specs=[pl.BlockSpec((SUBLANES, LANES), lambda i: (i, 0)), _const_spec((1, LANES))],
        out_specs=(spec, spec, spec, spec),
        compiler_params=_cparams("parallel"),
        name="rope_tables",
    )(pos2d, freq_row)


def _mla_prep_kernel(a_ref, mc_ref, ms_ref, wq_ref, wk_ref, wvt_ref, qg_ref, kvg_ref,
                     q_ref, kn_ref, vt_ref, kr_ref):
    a = a_ref[...].astype(F32)
    mc = mc_ref[...]
    ms = ms_ref[...]
    cq = a[:, :Q_LORA]
    qn = cq * lax.rsqrt(jnp.mean(cq * cq, axis=-1, keepdims=True) + RMS_EPS) * qg_ref[...]
    q = jnp.dot(qn.astype(BF16), wq_ref[...], preferred_element_type=F32)
    scale = (MLA_NOPE + MLA_ROPE) ** -0.5 * LOG2_E
    for h in range(MLA_HEADS):
        nope = q[:, h * MLA_QK:h * MLA_QK + LANES]
        blk = q[:, h * MLA_QK + LANES:(h + 1) * MLA_QK]
        rope = blk * mc + _swap_halves(blk) * ms
        q_ref[:, h * MLA_QK:h * MLA_QK + LANES] = (nope * scale).astype(BF16)
        q_ref[:, h * MLA_QK + LANES:(h + 1) * MLA_QK] = (rope * scale).astype(BF16)
    ckv = a[:, Q_LORA:Q_LORA + KV_LORA]
    kvn = ckv * lax.rsqrt(jnp.mean(ckv * ckv, axis=-1, keepdims=True) + RMS_EPS) * kvg_ref[...]
    kvb = kvn.astype(BF16)
    kn_ref[...] = jnp.dot(kvb, wk_ref[...], preferred_element_type=F32).astype(BF16)
    vt = lax.dot_general(wvt_ref[...], kvb, (((1,), (1,)), ((), ())),
                         preferred_element_type=F32).astype(BF16)
    ones = jnp.ones((VT_ROWS - MLA_V, vt.shape[1]), BF16)
    for h in range(MLA_HEADS):
        vt_ref[0, h * VT_ROWS:h * VT_ROWS + MLA_V, :] = vt[h * MLA_V:(h + 1) * MLA_V, :]
        vt_ref[0, h * VT_ROWS + MLA_V:(h + 1) * VT_ROWS, :] = ones
    krb =a[:, Q_LORA + KV_LORA:]
    kr_ref[...] = (krb * mc + _swap_halves(krb) * ms).astype(BF16)


def _mla_prep(proj, mc, ms, wq, wk, wvt, qg, kvg):
    T = proj.shape[0]
    row = lambda w: pl.BlockSpec((ROW_TILE, w), lambda i: (i, 0))
    return pl.pallas_call(
        _mla_prep_kernel,
        out_shape=(jax.ShapeDtypeStruct((T, MLA_HEADS * MLA_QK), BF16),
                   jax.ShapeDtypeStruct((T, MLA_HEADS * MLA_NOPE), BF16),
                   jax.ShapeDtypeStruct((T // ROW_TILE, MLA_HEADS * VT_ROWS, ROW_TILE), BF16),
                   jax.ShapeDtypeStruct((T, LANES), BF16)),
        grid=(T // ROW_TILE,),
        in_specs=[pl.BlockSpec((ROW_TILE, LATENT_W), lambda i: (i, N_WIDE // LATENT_W)),
                  row(LANES), row(LANES),
                  _const_spec(wq.shape), _const_spec(wk.shape), _const_spec(wvt.shape),
                  _const_spec(qg.shape), _const_spec(kvg.shape)],
        out_specs=(row(MLA_HEADS * MLA_QK), row(MLA_HEADS * MLA_NOPE),
                   pl.BlockSpec((1, MLA_HEADS * VT_ROWS, ROW_TILE), lambda i: (i, 0, 0)), row(LANES)),
        compiler_params=_cparams("parallel"),
        name="mla_prep",
    )(proj, mc, ms, wq, wk, wvt, qg, kvg)


def _attn_kernel(q_ref, kn_ref, kr_ref, vt_ref, o_ref, acc_scr):
    W = ATTN_Q_TILE // ATTN_Q_SPLIT
    n_diag = ATTN_Q_TILE // ATTN_TILE
    groups = tuple(range(ATTN_Q_SPLIT))

    def visible(d, c):
        return min(max((c + 1) * W - d * ATTN_TILE, 0), ATTN_TILE)

    for qi in range(q_ref.shape[0] // ATTN_Q_TILE):
        q0 = qi * ATTN_Q_TILE
        qs = [q_ref[q0 + c * W:q0 + (c + 1) * W, :] for c in groups]
        acc_scr[...] = jnp.zeros(acc_scr.shape, F32)

        def scores(j, which=groups):
            rows = slice(j * ATTN_TILE, (j + 1) * ATTN_TILE)
            k = jnp.concatenate([kn_ref[rows, :], kr_ref[rows, :]], axis=-1)
            return {c: lax.dot_general(k, qs[c], (((1,), (1,)), ((), ())), preferred_element_type=F32)
                    for c in which}

        def softmax_pv(j, s_all, m_all, diag=None):
            vt = vt_ref[j]
            out = list(m_all)
            for c in s_all:
                s = s_all[c]
                nk = ATTN_TILE if diag is None else visible(diag, c)
                if diag is not None:
                    s = s[:nk]
                    if diag * ATTN_TILE + nk - 1 > c * W:
                        kp = lax.broadcasted_iota(jnp.int32, s.shape, 0) + diag * ATTN_TILE
                        qp = lax.broadcasted_iota(jnp.int32, s.shape, 1) + c * W
                        s = jnp.where(kp <= qp, s, NEG_INF)
                m_new = jnp.maximum(m_all[c], jnp.max(s, axis=0, keepdims=True))
                alpha = jnp.exp2(m_all[c] - m_new)
                p = jnp.exp2(s - m_new).astype(BF16)
                acc_scr[c] = alpha * acc_scr[c] + jnp.dot(vt[:, :nk], p, preferred_element_type=F32)
                out[c] = m_new
            return tuple(out)

        first = n_diag * qi
        work = [(j, None, groups) for j in range(first)]
        work += [(first + d, d, tuple(c for c in groups if visible(d, c) > 0)) for d in range(n_diag)]
        m_all = tuple(jnp.full((1, W), NEG_INF, F32) for _ in groups)
        pending = [scores(j, which) for j, _, which in work[:2]]
        for n, (j, diag, _) in enumerate(work):
            m_all = softmax_pv(j, pending.pop(0), m_all, diag)
            if n + 2 < len(work):
                pending.append(scores(work[n + 2][0], work[n + 2][2]))
        for c in groups:
            acc = acc_scr[c]
            o_ref[q0 + c * W:q0 + (c + 1) * W, :] = (acc[:MLA_V] / acc[MLA_V:MLA_V + 1]).T.astype(o_ref.dtype)


def _attention(q_cat, kn, kr, vt, B, S):
    T = B * S
    return pl.pallas_call(
        _attn_kernel,
        out_shape=jax.ShapeDtypeStruct((T, MLA_HEADS * MLA_V), BF16),
        grid=(B, MLA_HEADS),
        in_specs=[pl.BlockSpec((S, MLA_QK), lambda b, h: (b, h)),
                  pl.BlockSpec((S, MLA_NOPE), lambda b, h: (b, h)),
                  pl.BlockSpec((S, LANES), lambda b, h: (b, 0)),
                  pl.BlockSpec((S // ATTN_TILE, VT_ROWS, ATTN_TILE), lambda b, h: (b, h, 0))],
        out_specs=pl.BlockSpec((S, MLA_V), lambda b, h: (b, h)),
        scratch_shapes=[pltpu.VMEM((ATTN_Q_SPLIT, VT_ROWS, ATTN_Q_TILE // ATTN_Q_SPLIT), F32)],
        compiler_params=_cparams("parallel", "parallel"),
        name="mla_attention",
    )(q_cat, kn, kr, vt)


def _ret_kernel(q_ref, k_ref, v_ref, g_ref, rc_ref, rs_ref, dec_ref, qw_ref, kw_ref, cd_ref,
                o_ref, qb_scr, qwb_scr, kb_scr, kwb_scr):
    S = q_ref.shape[0]
    C = RET_CHUNK
    n_chunks = S // C
    rc = rc_ref[...]
    rs = rs_ref[...]
    q = q_ref[...].astype(F32)
    k = k_ref[...].astype(F32)
    q = q * rc + _swap_halves(q) * rs
    k = (k * rc + _swap_halves(k) * rs) * (RET_DK ** -0.5)
    qw = qw_ref[0]
    kw = kw_ref[0]
    q3 = q.reshape(n_chunks, C, RET_DK)
    k3 = k.reshape(n_chunks, C, RET_DK)
    qb_scr[...] = q.astype(BF16)
    kb_scr[...] = k.astype(BF16)
    qwb_scr[...] = (q3 * qw[None]).reshape(S, RET_DK).astype(BF16)
    kwb_scr[...] = (k3 * kw[None]).reshape(S, RET_DK)
    dec = dec_ref[0]
    cd = cd_ref[0]

    def group(gi, R):
        rows_of = [pl.ds(pl.multiple_of((gi * RET_GROUP + u) * C, C), C) for u in range(RET_GROUP)]
        scores, updates = [], []
        for rows in rows_of:
            scores.append(lax.dot_general(qb_scr[rows, :], kb_scr[rows, :], (((1,), (1,)), ((), ())),
                                          preferred_element_type=F32))
            kwt = kwb_scr[rows, :].T.astype(BF16)
            updates.append(jnp.dot(kwt, v_ref[rows, :], preferred_element_type=F32))
        states = []
        for u in range(RET_GROUP):
            states.append(R.astype(BF16))
            R = cd * R + updates[u]
        for u, rows in enumerate(rows_of):
            o = jnp.dot((scores[u] * dec).astype(BF16), v_ref[rows, :], preferred_element_type=F32)
            o = o + jnp.dot(qwb_scr[rows, :], states[u], preferred_element_type=F32)
            mu = jnp.mean(o, axis=-1, keepdims=True)
            d = o - mu
            var = jnp.mean(d * d, axis=-1, keepdims=True)
            gate = g_ref[rows, :].astype(F32)
            o_ref[rows, :] = (d * lax.rsqrt(var + GN_EPS) * (gate * _sigmoid(gate))).astype(o_ref.dtype)
        return R

    lax.fori_loop(0, n_chunks // RET_GROUP, group, jnp.zeros((RET_DK, RET_DV), F32))


def _retention(proj, rc, rs, dec, qw, kw, cd, B, S):
    assert S % (RET_CHUNK * RET_GROUP) == 0, "sequence must hold whole groups of retention chunks"
    T = B * S
    H = RET_HEADS
    col = lambda off: pl.BlockSpec((S, LANES), lambda b, h: (b, off + h))
    tab = pl.BlockSpec((S, LANES), lambda b, h: (b, 0))
    per_head = lambda r: pl.BlockSpec((1, r, LANES), lambda b, h: (h, 0, 0))
    return pl.pallas_call(
        _ret_kernel,
        out_shape=jax.ShapeDtypeStruct((T, H * RET_DV), BF16),
        grid=(B, H),
        in_specs=[col(0), col(H), col(2 * H), col(3 * H), tab, tab,
                  per_head(RET_CHUNK), per_head(RET_CHUNK), per_head(RET_CHUNK), per_head(1)],
        out_specs=pl.BlockSpec((S, RET_DV), lambda b, h: (b, h)),
        scratch_shapes=[pltpu.VMEM((S, RET_DK), BF16), pltpu.VMEM((S, RET_DK), BF16),
                        pltpu.VMEM((S, RET_DK), BF16), pltpu.VMEM((S, RET_DK), F32)],
        compiler_params=_cparams("parallel", "parallel"),
        name="retention",
    )(proj, proj, proj, proj, rc, rs, dec, qw, kw, cd)


def _retention_constants():
    C = RET_CHUNK
    h = np.arange(RET_HEADS, dtype=np.float64)
    log_gamma = np.log(1.0 - 2.0 ** (-5.0 - h))
    idx = np.arange(C, dtype=np.float64)
    rel = idx[:, None] - idx[None, :]
    dec = np.where(rel[None] >= 0, np.exp(np.maximum(rel, 0.0)[None] * log_gamma[:, None, None]), 0.0)
    qw = np.exp((idx + 1.0)[None, :] * log_gamma[:, None])
    kw = np.exp((C - 1.0 - idx)[None, :] * log_gamma[:, None])
    cd = np.exp(C * log_gamma)
    bc = lambda a: np.ascontiguousarray(np.broadcast_to(a[..., None], a.shape + (LANES,))).astype(np.float32)
    return dec.astype(np.float32), bc(qw), bc(kw), bc(cd[:, None])


def _layernorm(y, g, b):
    mu = jnp.mean(y, axis=-1, keepdims=True)
    d = y - mu
    var = jnp.mean(d * d, axis=-1, keepdims=True)
    return d * lax.rsqrt(var + LN_EPS) * g + b


def _merge_kernel(x_ref, om_ref, or_ref, gm_ref, gr_ref, wo_ref, g_ref, b_ref, w01_ref, rb_ref,
                  h_ref, hb_ref, idx_ref, gate_ref, cnt_ref):
    mixed = (_sigmoid(gm_ref[...].astype(F32)) * om_ref[...].astype(F32)
             + _sigmoid(gr_ref[...].astype(F32)) * or_ref[...].astype(F32))
    mix = jnp.dot(mixed.astype(BF16), wo_ref[...], preferred_element_type=F32)
    h = _layernorm(DN_ALPHA * x_ref[...] + mix, g_ref[...], b_ref[...])
    h_ref[...] = h
    hb_ref[...] = h.astype(BF16)
    _route_tile(h, w01_ref, rb_ref, idx_ref, gate_ref, cnt_ref)


def _merge_and_route(x2, o_mla, o_ret, proj, wo, g, b, w_router, b_router):
    T = x2.shape[0]
    wpad = jnp.zeros((D_MODEL, LANES), F32).at[:, :N_EXPERTS].set(w_router)
    w0 = wpad.astype(BF16)
    w01 = jnp.concatenate([w0, (wpad - w0.astype(F32)).astype(BF16)], axis=-1)
    bpad = jnp.zeros((1, LANES), F32).at[0, :N_EXPERTS].set(b_router)
    row = lambda c: pl.BlockSpec((ROW_TILE, D_MODEL), lambda i: (i, c))
    lanes = pl.BlockSpec((ROW_TILE, LANES), lambda i: (i, 0))
    return pl.pallas_call(
        _merge_kernel,
        out_shape=(jax.ShapeDtypeStruct((T, D_MODEL), F32), jax.ShapeDtypeStruct((T, D_MODEL), BF16),
                   jax.ShapeDtypeStruct((T, LANES), jnp.int32), jax.ShapeDtypeStruct((T, LANES), F32),
                   jax.ShapeDtypeStruct((T // ROW_TILE * SUBLANES, LANES), jnp.int32)),
        grid=(T // ROW_TILE,),
        in_specs=[row(0), row(0), row(0), row(4), row(5),
                  _const_spec(wo.shape), _const_spec(g.shape), _const_spec(b.shape),
                  _const_spec(w01.shape), _const_spec(bpad.shape)],
        out_specs=(row(0), row(0), lanes, lanes, pl.BlockSpec((SUBLANES, LANES), lambda i: (i, 0))),
        compiler_params=_cparams("parallel"),
        name="merge_out_proj_ln_route",
    )(x2, o_mla, o_ret, proj, proj, wo, g, b, w01, bpad)


def _split2(a):
    a0 = a.astype(BF16)
    return a0, (a - a0.astype(F32)).astype(BF16)


def _route_tile(h, w01_ref, b_ref, idx_ref, gate_ref, cnt_ref):
    h0, h1 = _split2(h)
    p0 = jnp.dot(h0, w01_ref[...], preferred_element_type=F32)
    p1 = jnp.dot(h1, w01_ref[:, :LANES], preferred_element_type=F32)
    logits = p0[:, :LANES] + (p0[:, LANES:] + p1) + b_ref[...]
    rows = logits.shape[0]
    lane = lax.broadcasted_iota(jnp.int32, (rows, LANES), 1)
    lane_f = lane.astype(F32)
    work = jnp.where(lane < N_EXPERTS, logits, -jnp.inf)
    vals, ids, sels = [], [], []
    for _ in range(TOP_K):
        m = jnp.max(work, axis=-1, keepdims=True)
        first = jnp.min(jnp.where(work == m, lane_f, float(LANES)), axis=-1, keepdims=True)
        sel = lane_f == first
        work = jnp.where(sel, -jnp.inf, work)
        vals.append(m)
        ids.append(first)
        sels.append(sel)
    exps = [jnp.exp(v - vals[0]) for v in vals]
    denom = exps[0] + exps[1] + exps[2] + exps[3]
    chosen = jnp.zeros((rows, LANES), F32)
    for sel in sels:
        chosen = jnp.where(sel, 1.0, chosen)
    r = lax.broadcasted_iota(jnp.int32, (rows, rows), 0)
    c = lax.broadcasted_iota(jnp.int32, (rows, rows), 1)
    lower = jnp.where(r > c, 1.0, 0.0).astype(BF16)
    before = jnp.dot(lower, chosen.astype(BF16), preferred_element_type=F32)
    counts = jnp.sum(chosen, axis=0, keepdims=True)
    run_len = jnp.floor((counts + (SUBLANES - 1)) * (1.0 / SUBLANES)) * SUBLANES
    er = lax.broadcasted_iota(jnp.int32, (LANES, LANES), 0)
    ec = lax.broadcasted_iota(jnp.int32, (LANES, LANES), 1)
    earlier = jnp.where(er < ec, 1.0, 0.0).astype(BF16)
    run_start = jnp.dot(jnp.broadcast_to(run_len, (SUBLANES, LANES)).astype(BF16), earlier,
                        preferred_element_type=F32)[:1]
    slot = before + run_start
    idx_out = jnp.zeros((rows, LANES), jnp.int32)
    gate_out = jnp.zeros((rows, LANES), F32)
    for kk in range(TOP_K):
        pos = jnp.sum(jnp.where(sels[kk], slot, 0.0), axis=-1, keepdims=True).astype(jnp.int32)
        idx_out = jnp.where(lane == kk, ids[kk].astype(jnp.int32), idx_out)
        idx_out = jnp.where(lane == TOP_K + kk, pos, idx_out)
        gate_out = jnp.where(lane == kk, exps[kk] / denom, gate_out)
    idx_ref[...] = idx_out
    gate_ref[...] = gate_out
    cnt_ref[...] = jnp.broadcast_to(counts, cnt_ref.shape).astype(jnp.int32)


def _run_copy(runs_ref, tile, e, stage, hbm, sem, to_hbm):
    base = (tile * N_EXPERTS + e) * 3
    s = pl.multiple_of(runs_ref[base], SUBLANES)
    g = pl.multiple_of(runs_ref[base + 1], SUBLANES)
    n = pl.multiple_of(runs_ref[base + 2], SUBLANES)
    src, dst = stage.at[pl.ds(s, n), :], hbm.at[pl.ds(g, n), :]
    if not to_hbm:
        src, dst = dst, src
    return n, pltpu.make_async_copy(src, dst, sem)


def _for_each_run(runs_ref, tile, stage, hbm, sem, to_hbm, action):
    def one(e, carry):
        n, cp = _run_copy(runs_ref, tile, e, stage, hbm, sem, to_hbm)

        @pl.when(n > 0)
        def _():
            getattr(cp, action)()

        return carry

    lax.fori_loop(0, N_EXPERTS, one, 0)


def _dispatch_kernel(runs_ref, gaps_ref, h_ref, idx_ref, xs_ref, stage, zeros, sems):
    i = pl.program_id(0)

    def gap_copy(e):
        g = pl.multiple_of(gaps_ref[2 * e], SUBLANES)
        n = pl.multiple_of(gaps_ref[2 * e + 1], SUBLANES)
        return n, pltpu.make_async_copy(zeros.at[pl.ds(0, n), :], xs_ref.at[pl.ds(g, n), :], sems.at[2])

    def for_each_gap(action):
        def one(e, carry):
            n, cp = gap_copy(e)

            @pl.when(n > 0)
            def _():
                getattr(cp, action)()

            return carry

        lax.fori_loop(0, N_EXPERTS, one, 0)

        def tail(b, carry):
            rows = pl.ds(pl.multiple_of(b * EXPERT_TILE, EXPERT_TILE), EXPERT_TILE)
            getattr(pltpu.make_async_copy(zeros.at[pl.ds(0, EXPERT_TILE), :], xs_ref.at[rows, :], sems.at[2]),
                    action)()
            return carry

        lax.fori_loop(gaps_ref[2 * N_EXPERTS], xs_ref.shape[0] // EXPERT_TILE, tail, 0)

    @pl.when(i == 0)
    def _():
        zeros[...] = jnp.zeros(zeros.shape, zeros.dtype)
        for_each_gap("start")

    pos_t = idx_ref[...].astype(F32).T.astype(jnp.int32)
    h = h_ref[...]
    chunk = EXPERT_TILE
    row_id = lax.broadcasted_iota(jnp.int32, (chunk, ROW_TILE), 0)
    slot = i % 2
    for c in range(STAGE_ROWS // chunk):
        onehot = jnp.zeros((chunk, ROW_TILE), F32)
        for kk in range(TOP_K):
            onehot = jnp.where(row_id == (pos_t[TOP_K + kk:TOP_K + kk + 1] - c * chunk), 1.0, onehot)
        stage[slot, c * chunk:(c + 1) * chunk, :] = _pack_rows(
            jnp.dot(onehot.astype(BF16), h, preferred_element_type=F32))

    _for_each_run(runs_ref, i, stage.at[slot], xs_ref, sems.at[slot], True, "start")

    @pl.when(i > 0)
    def _():
        _for_each_run(runs_ref, i - 1, stage.at[1 - slot], xs_ref, sems.at[1 - slot], True, "wait")

    @pl.when(i == pl.num_programs(0) - 1)
    def _():
        _for_each_run(runs_ref, i, stage.at[slot], xs_ref, sems.at[slot], True, "wait")

    @pl.when(i == 0)
    def _():
        for_each_gap("wait")


def _dispatch(runs, gaps, hb, idx, n_rows):
    T = hb.shape[0]
    grid_spec = pltpu.PrefetchScalarGridSpec(
        num_scalar_prefetch=2,
        grid=(T // ROW_TILE,),
        in_specs=[pl.BlockSpec((ROW_TILE, D_MODEL), lambda i, r, g: (i, 0)),
                  pl.BlockSpec((ROW_TILE, LANES), lambda i, r, g: (i, 0))],
        out_specs=pl.BlockSpec(memory_space=pl.ANY),
        scratch_shapes=[pltpu.VMEM((2, STAGE_ROWS, PACKED_W), jnp.uint32),
                        pltpu.VMEM((EXPERT_STEP, PACKED_W), jnp.uint32),
                        pltpu.SemaphoreType.DMA((3,))],
    )
    return pl.pallas_call(
        _dispatch_kernel,
        out_shape=jax.ShapeDtypeStruct((n_rows, PACKED_W), jnp.uint32),
        grid_spec=grid_spec,
        compiler_params=_cparams("arbitrary"),
        name="dispatch_rows",
    )(runs, gaps, hb, idx)


def _expert_kernel(blk_e_ref, nact_ref, next_e_ref, nsub_ref, x_ref, bu_ref, bd_ref, wu_hbm, wd_hbm, y_ref,
                   wuf_scr, wdf_scr, wub_scr, wdb_scr, sems):
    i = pl.program_id(0)
    e = blk_e_ref[i]

    def weight_copies(expert):
        return (pltpu.make_async_copy(wu_hbm.at[expert], wuf_scr, sems.at[0]),
                pltpu.make_async_copy(wd_hbm.at[expert], wdf_scr, sems.at[1]))

    @pl.when(i == 0)
    def _():
        for cp in weight_copies(e):
            cp.start()

    first_of_group = jnp.logical_or(i == 0, e != blk_e_ref[jnp.maximum(i - 1, 0)])

    @pl.when(jnp.logical_and(i < nact_ref[0], first_of_group))
    def _():
        for cp in weight_copies(e):
            cp.wait()
        wub_scr[...] = wuf_scr[...].astype(BF16)
        wdb_scr[...] = wdf_scr[...].astype(BF16)

        @pl.when(next_e_ref[i] >= 0)
        def _():
            for cp in weight_copies(next_e_ref[i]):
                cp.start()

    for sub in range(EXPERT_STEP // EXPERT_TILE):
        rows = slice(sub * EXPERT_TILE, (sub + 1) * EXPERT_TILE)

        @pl.when(sub < nsub_ref[i])
        def _():
            lo, hi = _unpack_rows(x_ref[rows, :])
            x = jnp.concatenate([lo.astype(BF16), hi.astype(BF16)], axis=-1)
            glu = jnp.dot(x, wub_scr[:, :D_EXPERT], preferred_element_type=F32)
            lin = jnp.dot(x, wub_scr[:, D_EXPERT:], preferred_element_type=F32)
            glu = jnp.minimum(glu + bu_ref[0, :, :D_EXPERT], SWIGLU_LIMIT)
            lin = jnp.clip(lin + bu_ref[0, :, D_EXPERT:], -SWIGLU_LIMIT, SWIGLU_LIMIT)
            a = glu * _sigmoid(SWIGLU_ALPHA * glu) * (lin + 1.0)
            y = jnp.dot(a.astype(BF16), wdb_scr[...], preferred_element_type=F32)
            y_ref[rows, :] = _pack_rows(y + bd_ref[0])

        @pl.when(sub >= nsub_ref[i])
        def _():
            y_ref[rows, :] = jnp.zeros((EXPERT_TILE, PACKED_W), y_ref.dtype)


def _experts(blk_e, nact, next_e, nsub, xs, w_up, b_up, w_down, b_down):
    P = xs.shape[0]
    n_blocks = P // EXPERT_STEP
    act = lambda i, na: jnp.minimum(i, na[0] - 1)
    grid_spec = pltpu.PrefetchScalarGridSpec(
        num_scalar_prefetch=4,
        grid=(n_blocks,),
        in_specs=[pl.BlockSpec((EXPERT_STEP, PACKED_W), lambda i, be, na, ne, ns: (act(i, na), 0)),
                  pl.BlockSpec((1, 1, 2 * D_EXPERT), lambda i, be, na, ne, ns: (be[act(i, na)], 0, 0)),
                  pl.BlockSpec((1, 1, D_MODEL), lambda i, be, na, ne, ns: (be[act(i, na)], 0, 0)),
                  pl.BlockSpec(memory_space=pl.ANY),
                  pl.BlockSpec(memory_space=pl.ANY)],
        out_specs=pl.BlockSpec((EXPERT_STEP, PACKED_W), lambda i, be, na, ne, ns: (i, 0)),
        scratch_shapes=[pltpu.VMEM((D_MODEL, 2 * D_EXPERT), F32),
                        pltpu.VMEM((D_EXPERT, D_MODEL), F32),
                        pltpu.VMEM((D_MODEL, 2 * D_EXPERT), BF16),
                        pltpu.VMEM((D_EXPERT, D_MODEL), BF16),
                        pltpu.SemaphoreType.DMA((2,))],
    )
    return pl.pallas_call(
        _expert_kernel,
        out_shape=jax.ShapeDtypeStruct((P, PACKED_W), jnp.uint32),
        grid_spec=grid_spec,
        compiler_params=_cparams("arbitrary"),
        name="routed_experts",
    )(blk_e, nact, next_e, nsub, xs, b_up.reshape(N_EXPERTS, 1, 2 * D_EXPERT),
      b_down.reshape(N_EXPERTS, 1, D_MODEL), w_up, w_down)


def _combine_kernel(runs_ref, h_ref, idx_ref, gate_ref, g_ref, b_ref, ys_ref, o_ref, ybuf, sems):
    i = pl.program_id(0)
    n = pl.num_programs(0)

    def gather(tile, slot, action):
        _for_each_run(runs_ref, tile, ybuf.at[slot], ys_ref, sems.at[slot], False, action)

    @pl.when(i == 0)
    def _():
        ybuf[...] = jnp.zeros(ybuf.shape, ybuf.dtype)
        gather(0, 0, "start")

    @pl.when(i + 1 < n)
    def _():
        gather(i + 1, (i + 1) % 2, "start")

    slot = i % 2
    gather(i, slot, "wait")

    idx = idx_ref[...]
    gate = gate_ref[...]
    chunk = EXPERT_TILE
    col_id = lax.broadcasted_iota(jnp.int32, (ROW_TILE, LANES), 1)
    pos_b = [jnp.broadcast_to(idx[:, TOP_K + kk:TOP_K + kk + 1], (ROW_TILE, LANES)) for kk in range(TOP_K)]
    gate_b = [jnp.broadcast_to(gate[:, kk:kk + 1], (ROW_TILE, LANES)) for kk in range(TOP_K)]
    f_lo = jnp.zeros((ROW_TILE, PACKED_W), F32)
    f_hi = jnp.zeros((ROW_TILE, PACKED_W), F32)
    for c in range(STAGE_ROWS // chunk):
        parts = []
        for off in range(c * chunk, (c + 1) * chunk, LANES):
            part = jnp.zeros((ROW_TILE, LANES), F32)
            for kk in range(TOP_K):
                part = jnp.where(col_id + off == pos_b[kk], gate_b[kk], part)
            parts.append(part)
        wb = jnp.concatenate(parts, axis=-1).astype(BF16)
        lo, hi = _unpack_rows(ybuf[slot, c * chunk:(c + 1) * chunk, :])
        f_lo = f_lo + jnp.dot(wb, lo.astype(BF16), preferred_element_type=F32)
        f_hi = f_hi + jnp.dot(wb, hi.astype(BF16), preferred_element_type=F32)
    y_lo = DN_ALPHA * h_ref[:, :PACKED_W] + f_lo
    y_hi = DN_ALPHA * h_ref[:, PACKED_W:] + f_hi
    mu = (jnp.sum(y_lo, axis=-1, keepdims=True) + jnp.sum(y_hi, axis=-1, keepdims=True)) / D_MODEL
    d_lo = y_lo - mu
    d_hi = y_hi - mu
    var = (jnp.sum(d_lo * d_lo, axis=-1, keepdims=True) + jnp.sum(d_hi * d_hi, axis=-1, keepdims=True)) / D_MODEL
    inv = lax.rsqrt(var + LN_EPS)
    o_ref[:, :PACKED_W] = d_lo * inv * g_ref[:, :PACKED_W] + b_ref[:, :PACKED_W]
    o_ref[:, PACKED_W:] = d_hi * inv * g_ref[:, PACKED_W:] + b_ref[:, PACKED_W:]


def _combine(runs, h, idx, gate, ys, g, b):
    T = h.shape[0]
    grid_spec = pltpu.PrefetchScalarGridSpec(
        num_scalar_prefetch=1,
        grid=(T // ROW_TILE,),
        in_specs=[pl.BlockSpec((ROW_TILE, D_MODEL), lambda i, r: (i, 0)),
                  pl.BlockSpec((ROW_TILE, LANES), lambda i, r: (i, 0)),
                  pl.BlockSpec((ROW_TILE, LANES), lambda i, r: (i, 0)),
                  pl.BlockSpec(g.shape, lambda i, r: (0, 0)),
                  pl.BlockSpec(b.shape, lambda i, r: (0, 0)),
                  pl.BlockSpec(memory_space=pl.ANY)],
        out_specs=pl.BlockSpec((ROW_TILE, D_MODEL), lambda i, r: (i, 0)),
        scratch_shapes=[pltpu.VMEM((2, STAGE_ROWS, PACKED_W), jnp.uint32),
                        pltpu.SemaphoreType.DMA((2,))],
    )
    return pl.pallas_call(
        _combine_kernel,
        out_shape=jax.ShapeDtypeStruct((T, D_MODEL), F32),
        grid_spec=grid_spec,
        compiler_params=_cparams("arbitrary"),
        name="combine_ln",
    )(runs, h, idx, gate, g, b, ys)


def _rot_cols(w):
    half = w.shape[-1] // 2
    return jnp.concatenate([-w[:, half:], w[:, :half]], axis=-1)


def _pack_w_in(w_in):
    lat = Q_LORA + KV_LORA
    w_kr = w_in[:, lat:lat + MLA_ROPE]
    w_lat = jnp.concatenate([w_in[:, :lat + MLA_ROPE], _rot_cols(w_kr)], axis=-1).astype(BF16)
    return w_in[:, lat + MLA_ROPE:].astype(BF16), w_lat


def _pack_w_uq(w_uq):
    w = w_uq.reshape(Q_LORA, MLA_HEADS, MLA_NOPE + MLA_ROPE)
    rope = w[:, :, MLA_NOPE:]
    rot = jnp.concatenate([-rope[:, :, MLA_ROPE // 2:], rope[:, :, :MLA_ROPE // 2]], axis=-1)
    return jnp.concatenate([w, rot], axis=-1).reshape(Q_LORA, MLA_HEADS * MLA_QK).astype(BF16)


def _rope_freqs():
    ret = 1.0 / (10000.0 ** jnp.linspace(0.0, 1.0, RET_DK // 2, dtype=F32))
    mla = 1.0 / (ROPE_BASE ** (jnp.arange(0, MLA_ROPE, 2, dtype=F32) / MLA_ROPE))
    return jnp.concatenate([ret, mla, jnp.zeros((LANES - ret.shape[0] - mla.shape[0],), F32)])[None, :]


def _routing_plan(cnt, n_tokens):
    counts = cnt[::SUBLANES, :N_EXPERTS]
    n_tiles = counts.shape[0]
    run_len = ((counts + SUBLANES - 1) // SUBLANES) * SUBLANES
    stage_off = jnp.cumsum(run_len, axis=1) - run_len
    tile_off = jnp.cumsum(run_len, axis=0) - run_len
    rows_used = jnp.sum(run_len, axis=0)
    padded = ((rows_used + EXPERT_STEP - 1) // EXPERT_STEP) * EXPERT_STEP
    pad_end = jnp.cumsum(padded)
    pad_start = pad_end - padded
    runs = jnp.stack([stage_off, pad_start[None, :] + tile_off, run_len], axis=-1).reshape(-1).astype(jnp.int32)
    nact = (pad_end[-1:] // EXPERT_STEP).astype(jnp.int32)
    gaps = jnp.concatenate([jnp.stack([pad_start + rows_used, padded - rows_used], axis=-1).reshape(-1),
                            pad_end[-1:] // EXPERT_TILE]).astype(jnp.int32)
    n_rows = n_tokens * TOP_K + (SUBLANES - 1) * n_tiles * N_EXPERTS + N_EXPERTS * EXPERT_STEP
    n_rows = -(-n_rows // EXPERT_STEP) * EXPERT_STEP
    n_blocks = n_rows // EXPERT_STEP
    experts = jnp.arange(N_EXPERTS, dtype=jnp.int32)
    blk_row = jnp.arange(n_blocks, dtype=jnp.int32) * EXPERT_STEP
    blk_e = jnp.minimum(jnp.sum((pad_end[None, :] <= blk_row[:, None]).astype(jnp.int32), axis=-1),
                        N_EXPERTS - 1)
    used_end = jnp.sum(jnp.where(blk_e[:, None] == experts[None, :], (pad_start + rows_used)[None, :], 0), axis=-1)
    nsub = jnp.clip((used_end - blk_row + EXPERT_TILE - 1) // EXPERT_TILE, 0,
                    EXPERT_STEP // EXPERT_TILE).astype(jnp.int32)
    later = jnp.where((rows_used > 0)[None, :] & (experts[None, :] > experts[:, None]), experts[None, :], N_EXPERTS)
    nxt = jnp.min(later, axis=-1)
    nxt = jnp.where(nxt == N_EXPERTS, -1, nxt)
    next_e = jnp.sum(jnp.where(blk_e[:, None] == experts[None, :], nxt[None, :], 0), axis=-1).astype(jnp.int32)
    return runs, gaps, blk_e, nact, next_e, nsub, n_rows


def kernel(x, positions, w_in, q_norm_g, w_uq, kv_norm_g, w_ukv, w_o, ln1_g, ln1_b,
           w_router, b_router, w_up, b_up, w_down, b_down, ln2_g, ln2_b):
    B, S, _ = x.shape
    T = B * S
    assert S % ATTN_Q_TILE == 0 and T % ROW_TILE == 0, "sequence / token count must be whole tiles"
    rc, rs, mc, ms = _rope_tables(positions.reshape(T // LANES, LANES).astype(jnp.int32), _rope_freqs())
    dec, qw, kw, cd = _retention_constants()
    h = x.reshape(T, D_MODEL)
    for l in range(DEPTH):
        proj = _input_projection(h, *_pack_w_in(w_in[l]))
        w_kv = w_ukv[l].reshape(KV_LORA, MLA_HEADS, MLA_NOPE + MLA_V)
        w_k = w_kv[:, :, :MLA_NOPE].reshape(KV_LORA, MLA_HEADS * MLA_NOPE).astype(BF16)
        w_vt = w_kv[:, :, MLA_NOPE:].reshape(KV_LORA, MLA_HEADS * MLA_V).T.astype(BF16)
        q_cat, kn, vt, kr = _mla_prep(proj, mc, ms, _pack_w_uq(w_uq[l]), w_k, w_vt,
                                      q_norm_g[l][None, :], kv_norm_g[l][None, :])
        o_mla = _attention(q_cat, kn, kr, vt, B, S)
        o_ret = _retention(proj, rc, rs, dec, qw, kw, cd, B, S)
        h1, h1b, idx_out, gate, cnt = _merge_and_route(
            h, o_mla, o_ret, proj, w_o[l].astype(BF16), ln1_g[l][None, :], ln1_b[l][None, :],
            w_router[l], b_router[l])
        runs, gaps, blk_e, nact, next_e, nsub, n_rows = _routing_plan(cnt, T)
        xs = _dispatch(runs, gaps, h1b, idx_out, n_rows)
        ys = _experts(blk_e, nact, next_e, nsub, xs, w_up[l], b_up[l], w_down[l], b_down[l])
        h = _combine(runs, h1, idx_out, gate, ys, ln2_g[l][None, :], ln2_b[l][None, :])
    return h.reshape(B, S, D_MODEL)
```

```python
import functools

import numpy as np
import jax
import jax.numpy as jnp
from jax import lax
from jax.experimental import pallas as pl
from jax.experimental.pallas import tpu as pltpu

F32 = jnp.float32
BF16 = jnp.bfloat16

D_MODEL = 1024
DEPTH = 1
MLA_HEADS = 8
MLA_NOPE = 128
MLA_ROPE = 64
MLA_V = 128
Q_LORA = 384
KV_LORA = 256
ROPE_BASE = 10000.0
RET_HEADS = 8
RET_DK = 128
RET_DV = 128
RET_CHUNK = 128
N_EXPERTS = 32
TOP_K = 4
D_EXPERT = 1024
SWIGLU_LIMIT = 7.0
SWIGLU_ALPHA = 1.702
DN_ALPHA = (2.0 * DEPTH) ** 0.25
LN_EPS = 1e-5
RMS_EPS = 1e-6
GN_EPS = 1e-6
NEG_INF = -1e30

LANES = 128
SUBLANES = 8
MLA_QK = 2 * LANES
PACKED_W = D_MODEL // 2
LATENT_W = 768
N_WIDE = 6 * D_MODEL
N_PROJ = N_WIDE + LATENT_W
VMEM_LIMIT = 56 * 1024 * 1024

ROW_TILE = 512
ATTN_TILE = ROW_TILE
ATTN_Q_TILE = 2 * ATTN_TILE
ATTN_Q_SPLIT = 4
VT_ROWS = MLA_V + 16
LOG2_E = 1.4426950408889634
RET_GROUP = 16
EXPERT_TILE = 256
EXPERT_STEP = 2 * EXPERT_TILE
STAGE_ROWS = -(-(ROW_TILE * TOP_K + (SUBLANES - 1) * N_EXPERTS) // EXPERT_TILE) * EXPERT_TILE


def _cparams(*sem):
    return pltpu.CompilerParams(dimension_semantics=sem, vmem_limit_bytes=VMEM_LIMIT)


def _const_spec(shape):
    nd = len(shape)
    return pl.BlockSpec(shape, lambda *_: (0,) * nd, pipeline_mode=pl.Buffered(1))


def _sigmoid(x):
    return 1.0 / (1.0 + jnp.exp(-x))


def _pack_rows(x):
    n = x.shape[-1] // 2
    xb = x.astype(BF16).astype(F32)
    lo = pltpu.bitcast(xb[:, :n], jnp.uint32) >> 16
    hi = pltpu.bitcast(xb[:, n:], jnp.uint32) & jnp.uint32(0xFFFF0000)
    return lo | hi


def _unpack_rows(w):
    lo = pltpu.bitcast(w << 16, F32)
    hi = pltpu.bitcast(w & jnp.uint32(0xFFFF0000), F32)
    return lo, hi


def _swap_halves(x):
    return pltpu.roll(x, LANES // 2, 1)


def _proj_kernel(x_ref, wt_wide_ref, wt_lat_ref, o_ref):
    xb = x_ref[...].astype(BF16)
    nt = (((1,), (1,)), ((), ()))
    for c in range(N_WIDE // LATENT_W):
        sl = slice(c * LATENT_W, (c + 1) * LATENT_W)
        o_ref[:, sl] = lax.dot_general(xb, wt_wide_ref[sl, :], nt, preferred_element_type=F32).astype(o_ref.dtype)
    o_ref[:, N_WIDE:] = lax.dot_general(xb, wt_lat_ref[...], nt, preferred_element_type=F32).astype(o_ref.dtype)


def _input_projection(x2, wt_wide, wt_lat):
    T = x2.shape[0]
    return pl.pallas_call(
        _proj_kernel,
        out_shape=jax.ShapeDtypeStruct((T, N_PROJ), BF16),
        grid=(T // ROW_TILE,),
        in_specs=[pl.BlockSpec((ROW_TILE, D_MODEL), lambda i: (i, 0)),
                  _const_spec((N_WIDE, D_MODEL)), _const_spec((LATENT_W, D_MODEL))],
        out_specs=pl.BlockSpec((ROW_TILE, N_PROJ), lambda i: (i, 0)),
        compiler_params=_cparams("parallel"),
        name="input_projection",
    )(x2, wt_wide, wt_lat)


def _rope_table_kernel(pos_ref, freq_ref, rc_ref, rs_ref, mc_ref, ms_ref):
    pos = pos_ref[...].astype(F32)
    pos_t = jnp.concatenate([pos, jnp.zeros((LANES - pos.shape[0], LANES), F32)], axis=0).T
    freq = freq_ref[...]
    ang = jnp.concatenate([pos_t[:, j:j + 1] * freq for j in range(pos.shape[0])], axis=0)
    c = jnp.cos(ang)
    s = jnp.sin(ang)
    lane = lax.broadcasted_iota(jnp.int32, ang.shape, 1)
    c64 = pltpu.roll(c, 64, 1)
    s64 = pltpu.roll(s, 64, 1)
    rc_ref[...] = jnp.where(lane < 64, c, c64)
    rs_ref[...] = jnp.where(lane < 64, -s, s64)
    c96 = pltpu.roll(c, 96, 1)
    s96 = pltpu.roll(s, 96, 1)
    mc_ref[...] = jnp.where(lane < 32, c64, jnp.where(lane < 64, c96, 0.0))
    ms_ref[...] = jnp.where(lane < 32, s64, jnp.where(lane < 64, s96, 0.0))


def _rope_tables(pos2d, freq_row):
    T = pos2d.shape[0] * LANES
    rows = SUBLANES * LANES
    tab = jax.ShapeDtypeStruct((T, LANES), F32)
    spec = pl.BlockSpec((rows, LANES), lambda i: (i, 0))
    return pl.pallas_call(
        _rope_table_kernel,
        out_shape=(tab, tab, tab, tab),
        grid=(T // rows,),
        in_specs=[pl.BlockSpec((SUBLANES, LANES), lambda i: (i, 0)), _const_spec((1, LANES))],
        out_specs=(spec, spec, spec, spec),
        compiler_params=_cparams("parallel"),
        name="rope_tables",
    )(pos2d, freq_row)


def _mla_prep_kernel(a_ref, mc_ref, ms_ref, wq_ref, wk_ref, wvt_ref, qg_ref, kvg_ref,
                     q_ref, kn_ref, vt_ref, kr_ref):
    a = a_ref[...].astype(F32)
    mc = mc_ref[...]
    ms = ms_ref[...]
    cq = a[:, :Q_LORA]
    qn = cq * lax.rsqrt(jnp.mean(cq * cq, axis=-1, keepdims=True) + RMS_EPS) * qg_ref[...]
    q = jnp.dot(qn.astype(BF16), wq_ref[...], preferred_element_type=F32)
    scale = (MLA_NOPE + MLA_ROPE) ** -0.5 * LOG2_E
    for h in range(MLA_HEADS):
        nope = q[:, h * MLA_QK:h * MLA_QK + LANES]
        blk = q[:, h * MLA_QK + LANES:(h + 1) * MLA_QK]
        rope = blk * mc + _swap_halves(blk) * ms
        q_ref[:, h * MLA_QK:h * MLA_QK + LANES] = (nope * scale).astype(BF16)
        q_ref[:, h * MLA_QK + LANES:(h + 1) * MLA_QK] = (rope * scale).astype(BF16)
    ckv = a[:, Q_LORA:Q_LORA + KV_LORA]
    kvn = ckv * lax.rsqrt(jnp.mean(ckv * ckv, axis=-1, keepdims=True) + RMS_EPS) * kvg_ref[...]
    kvb = kvn.astype(BF16)
    kn_ref[...] = jnp.dot(kvb, wk_ref[...], preferred_element_type=F32).astype(BF16)
    vt = lax.dot_general(wvt_ref[...], kvb, (((1,), (1,)), ((), ())),
                         preferred_element_type=F32).astype(BF16)
    ones = jnp.ones((VT_ROWS - MLA_V, vt.shape[1]), BF16)
    for h in range(MLA_HEADS):
        vt_ref[0, h * VT_ROWS:h * VT_ROWS + MLA_V, :] = vt[h * MLA_V:(h + 1) * MLA_V, :]
        vt_ref[0, h * VT_ROWS + MLA_V:(h + 1) * VT_ROWS, :] = ones
    krb =a[:, Q_LORA + KV_LORA:]
    kr_ref[...] = (krb * mc + _swap_halves(krb) * ms).astype(BF16)


def _mla_prep(proj, mc, ms, wq, wk, wvt, qg, kvg):
    T = proj.shape[0]
    row = lambda w: pl.BlockSpec((ROW_TILE, w), lambda i: (i, 0))
    return pl.pallas_call(
        _mla_prep_kernel,
        out_shape=(jax.ShapeDtypeStruct((T, MLA_HEADS * MLA_QK), BF16),
                   jax.ShapeDtypeStruct((T, MLA_HEADS * MLA_NOPE), BF16),
                   jax.ShapeDtypeStruct((T // ROW_TILE, MLA_HEADS * VT_ROWS, ROW_TILE), BF16),
                   jax.ShapeDtypeStruct((T, LANES), BF16)),
        grid=(T // ROW_TILE,),
        in_specs=[pl.BlockSpec((ROW_TILE, LATENT_W), lambda i: (i, N_WIDE // LATENT_W)),
                  row(LANES), row(LANES),
                  _const_spec(wq.shape), _const_spec(wk.shape), _const_spec(wvt.shape),
                  _const_spec(qg.shape), _const_spec(kvg.shape)],
        out_specs=(row(MLA_HEADS * MLA_QK), row(MLA_HEADS * MLA_NOPE),
                   pl.BlockSpec((1, MLA_HEADS * VT_ROWS, ROW_TILE), lambda i: (i, 0, 0)), row(LANES)),
        compiler_params=_cparams("parallel"),
        name="mla_prep",
    )(proj, mc, ms, wq, wk, wvt, qg, kvg)


def _attn_kernel(q_ref, kn_ref, kr_ref, vt_ref, o_ref, acc_scr):
    W = ATTN_Q_TILE // ATTN_Q_SPLIT
    n_diag = ATTN_Q_TILE // ATTN_TILE
    groups = tuple(range(ATTN_Q_SPLIT))

    def visible(d, c):
        return min(max((c + 1) * W - d * ATTN_TILE, 0), ATTN_TILE)

    for qi in range(q_ref.shape[0] // ATTN_Q_TILE):
        q0 = qi * ATTN_Q_TILE
        qs = [q_ref[q0 + c * W:q0 + (c + 1) * W, :] for c in groups]
        acc_scr[...] = jnp.zeros(acc_scr.shape, F32)

        def scores(j, which=groups):
            rows = slice(j * ATTN_TILE, (j + 1) * ATTN_TILE)
            k = jnp.concatenate([kn_ref[rows, :], kr_ref[rows, :]], axis=-1)
            return {c: lax.dot_general(k, qs[c], (((1,), (1,)), ((), ())), preferred_element_type=F32)
                    for c in which}

        def softmax_pv(j, s_all, m_all, diag=None):
            vt = vt_ref[j]
            out = list(m_all)
            for c in s_all:
                s = s_all[c]
                nk = ATTN_TILE if diag is None else visible(diag, c)
                if diag is not None:
                    s = s[:nk]
                    if diag * ATTN_TILE + nk - 1 > c * W:
                        kp = lax.broadcasted_iota(jnp.int32, s.shape, 0) + diag * ATTN_TILE
                        qp = lax.broadcasted_iota(jnp.int32, s.shape, 1) + c * W
                        s = jnp.where(kp <= qp, s, NEG_INF)
                m_new = jnp.maximum(m_all[c], jnp.max(s, axis=0, keepdims=True))
                alpha = jnp.exp2(m_all[c] - m_new)
                p = jnp.exp2(s - m_new).astype(BF16)
                acc_scr[c] = alpha * acc_scr[c] + jnp.dot(vt[:, :nk], p, preferred_element_type=F32)
                out[c] = m_new
            return tuple(out)

        first = n_diag * qi
        work = [(j, None, groups) for j in range(first)]
        work += [(first + d, d, tuple(c for c in groups if visible(d, c) > 0)) for d in range(n_diag)]
        m_all = tuple(jnp.full((1, W), NEG_INF, F32) for _ in groups)
        pending = [scores(j, which) for j, _, which in work[:2]]
        for n, (j, diag, _) in enumerate(work):
            m_all = softmax_pv(j, pending.pop(0), m_all, diag)
            if n + 2 < len(work):
                pending.append(scores(work[n + 2][0], work[n + 2][2]))
        for c in groups:
            acc = acc_scr[c]
            o_ref[q0 + c * W:q0 + (c + 1) * W, :] = (acc[:MLA_V] / acc[MLA_V:MLA_V + 1]).T.astype(o_ref.dtype)


def _attention(q_cat, kn, kr, vt, B, S):
    T = B * S
    return pl.pallas_call(
        _attn_kernel,
        out_shape=jax.ShapeDtypeStruct((T, MLA_HEADS * MLA_V), BF16),
        grid=(B, MLA_HEADS),
        in_specs=[pl.BlockSpec((S, MLA_QK), lambda b, h: (b, h)),
                  pl.BlockSpec((S, MLA_NOPE), lambda b, h: (b, h)),
                  pl.BlockSpec((S, LANES), lambda b, h: (b, 0)),
                  pl.BlockSpec((S // ATTN_TILE, VT_ROWS, ATTN_TILE), lambda b, h: (b, h, 0))],
        out_specs=pl.BlockSpec((S, MLA_V), lambda b, h: (b, h)),
        scratch_shapes=[pltpu.VMEM((ATTN_Q_SPLIT, VT_ROWS, ATTN_Q_TILE // ATTN_Q_SPLIT), F32)],
        compiler_params=_cparams("parallel", "parallel"),
        name="mla_attention",
    )(q_cat, kn, kr, vt)


def _ret_kernel(q_ref, k_ref, v_ref, g_ref, rc_ref, rs_ref, dec_ref, qw_ref, kw_ref, cd_ref,
                o_ref, qb_scr, qwb_scr, kb_scr, kwb_scr):
    S = q_ref.shape[0]
    C = RET_CHUNK
    n_chunks = S // C
    rc = rc_ref[...]
    rs = rs_ref[...]
    q = q_ref[...].astype(F32)
    k = k_ref[...].astype(F32)
    q = q * rc + _swap_halves(q) * rs
    k = (k * rc + _swap_halves(k) * rs) * (RET_DK ** -0.5)
    qw = qw_ref[0]
    kw = kw_ref[0]
    q3 = q.reshape(n_chunks, C, RET_DK)
    k3 = k.reshape(n_chunks, C, RET_DK)
    qb_scr[...] = q.astype(BF16)
    kb_scr[...] = k.astype(BF16)
    qwb_scr[...] = (q3 * qw[None]).reshape(S, RET_DK).astype(BF16)
    kwb_scr[...] = (k3 * kw[None]).reshape(S, RET_DK)
    dec = dec_ref[0]
    cd = cd_ref[0]

    def group(gi, R):
        rows_of = [pl.ds(pl.multiple_of((gi * RET_GROUP + u) * C, C), C) for u in range(RET_GROUP)]
        scores, updates = [], []
        for rows in rows_of:
            scores.append(lax.dot_general(qb_scr[rows, :], kb_scr[rows, :], (((1,), (1,)), ((), ())),
                                          preferred_element_type=F32))
            kwt = kwb_scr[rows, :].T.astype(BF16)
            updates.append(jnp.dot(kwt, v_ref[rows, :], preferred_element_type=F32))
        states = []
        for u in range(RET_GROUP):
            states.append(R.astype(BF16))
            R = cd * R + updates[u]
        for u, rows in enumerate(rows_of):
            o = jnp.dot((scores[u] * dec).astype(BF16), v_ref[rows, :], preferred_element_type=F32)
            o = o + jnp.dot(qwb_scr[rows, :], states[u], preferred_element_type=F32)
            mu = jnp.mean(o, axis=-1, keepdims=True)
            d = o - mu
            var = jnp.mean(d * d, axis=-1, keepdims=True)
            gate = g_ref[rows, :].astype(F32)
            o_ref[rows, :] = (d * lax.rsqrt(var + GN_EPS) * (gate * _sigmoid(gate))).astype(o_ref.dtype)
        return R

    lax.fori_loop(0, n_chunks // RET_GROUP, group, jnp.zeros((RET_DK, RET_DV), F32))


def _retention(proj, rc, rs, dec, qw, kw, cd, B, S):
    assert S % (RET_CHUNK * RET_GROUP) == 0, "sequence must hold whole groups of retention chunks"
    T = B * S
    H = RET_HEADS
    col = lambda off: pl.BlockSpec((S, LANES), lambda b, h: (b, off + h))
    tab = pl.BlockSpec((S, LANES), lambda b, h: (b, 0))
    per_head = lambda r: pl.BlockSpec((1, r, LANES), lambda b, h: (h, 0, 0))
    return pl.pallas_call(
        _ret_kernel,
        out_shape=jax.ShapeDtypeStruct((T, H * RET_DV), BF16),
        grid=(B, H),
        in_specs=[col(0), col(H), col(2 * H), col(3 * H), tab, tab,
                  per_head(RET_CHUNK), per_head(RET_CHUNK), per_head(RET_CHUNK), per_head(1)],
        out_specs=pl.BlockSpec((S, RET_DV), lambda b, h: (b, h)),
        scratch_shapes=[pltpu.VMEM((S, RET_DK), BF16), pltpu.VMEM((S, RET_DK), BF16),
                        pltpu.VMEM((S, RET_DK), BF16), pltpu.VMEM((S, RET_DK), F32)],
        compiler_params=_cparams("parallel", "parallel"),
        name="retention",
    )(proj, proj, proj, proj, rc, rs, dec, qw, kw, cd)


def _retention_constants():
    C = RET_CHUNK
    h = np.arange(RET_HEADS, dtype=np.float64)
    log_gamma = np.log(1.0 - 2.0 ** (-5.0 - h))
    idx = np.arange(C, dtype=np.float64)
    rel = idx[:, None] - idx[None, :]
    dec = np.where(rel[None] >= 0, np.exp(np.maximum(rel, 0.0)[None] * log_gamma[:, None, None]), 0.0)
    qw = np.exp((idx + 1.0)[None, :] * log_gamma[:, None])
    kw = np.exp((C - 1.0 - idx)[None, :] * log_gamma[:, None])
    cd = np.exp(C * log_gamma)
    bc = lambda a: np.ascontiguousarray(np.broadcast_to(a[..., None], a.shape + (LANES,))).astype(np.float32)
    return dec.astype(np.float32), bc(qw), bc(kw), bc(cd[:, None])


def _layernorm(y, g, b):
    mu = jnp.mean(y, axis=-1, keepdims=True)
    d = y - mu
    var = jnp.mean(d * d, axis=-1, keepdims=True)
    return d * lax.rsqrt(var + LN_EPS) * g + b


def _merge_kernel(x_ref, om_ref, or_ref, gm_ref, gr_ref, wo_ref, g_ref, b_ref, w01_ref, rb_ref,
                  h_ref, hb_ref, idx_ref, gate_ref, cnt_ref):
    mixed = (_sigmoid(gm_ref[...].astype(F32)) * om_ref[...].astype(F32)
             + _sigmoid(gr_ref[...].astype(F32)) * or_ref[...].astype(F32))
    mix = jnp.dot(mixed.astype(BF16), wo_ref[...], preferred_element_type=F32)
    h = _layernorm(DN_ALPHA * x_ref[...] + mix, g_ref[...], b_ref[...])
    h_ref[...] = h
    hb_ref[...] = h.astype(BF16)
    _route_tile(h, w01_ref, rb_ref, idx_ref, gate_ref, cnt_ref)


def _merge_and_route(x2, o_mla, o_ret, proj, wo, g, b, w_router, b_router):
    T = x2.shape[0]
    wpad = jnp.zeros((D_MODEL, LANES), F32).at[:, :N_EXPERTS].set(w_router)
    w0 = wpad.astype(BF16)
    w01 = jnp.concatenate([w0, (wpad - w0.astype(F32)).astype(BF16)], axis=-1)
    bpad = jnp.zeros((1, LANES), F32).at[0, :N_EXPERTS].set(b_router)
    row = lambda c: pl.BlockSpec((ROW_TILE, D_MODEL), lambda i: (i, c))
    lanes = pl.BlockSpec((ROW_TILE, LANES), lambda i: (i, 0))
    return pl.pallas_call(
        _merge_kernel,
        out_shape=(jax.ShapeDtypeStruct((T, D_MODEL), F32), jax.ShapeDtypeStruct((T, D_MODEL), BF16),
                   jax.ShapeDtypeStruct((T, LANES), jnp.int32), jax.ShapeDtypeStruct((T, LANES), F32),
                   jax.ShapeDtypeStruct((T // ROW_TILE * SUBLANES, LANES), jnp.int32)),
        grid=(T // ROW_TILE,),
        in_specs=[row(0), row(0), row(0), row(4), row(5),
                  _const_spec(wo.shape), _const_spec(g.shape), _const_spec(b.shape),
                  _const_spec(w01.shape), _const_spec(bpad.shape)],
        out_specs=(row(0), row(0), lanes, lanes, pl.BlockSpec((SUBLANES, LANES), lambda i: (i, 0))),
        compiler_params=_cparams("parallel"),
        name="merge_out_proj_ln_route",
    )(x2, o_mla, o_ret, proj, proj, wo, g, b, w01, bpad)


def _split2(a):
    a0 = a.astype(BF16)
    return a0, (a - a0.astype(F32)).astype(BF16)


def _route_tile(h, w01_ref, b_ref, idx_ref, gate_ref, cnt_ref):
    h0, h1 = _split2(h)
    p0 = jnp.dot(h0, w01_ref[...], preferred_element_type=F32)
    p1 = jnp.dot(h1, w01_ref[:, :LANES], preferred_element_type=F32)
    logits = p0[:, :LANES] + (p0[:, LANES:] + p1) + b_ref[...]
    rows = logits.shape[0]
    lane = lax.broadcasted_iota(jnp.int32, (rows, LANES), 1)
    lane_f = lane.astype(F32)
    work = jnp.where(lane < N_EXPERTS, logits, -jnp.inf)
    vals, ids, sels = [], [], []
    for _ in range(TOP_K):
        m = jnp.max(work, axis=-1, keepdims=True)
        first = jnp.min(jnp.where(work == m, lane_f, float(LANES)), axis=-1, keepdims=True)
        sel = lane_f == first
        work = jnp.where(sel, -jnp.inf, work)
        vals.append(m)
        ids.append(first)
        sels.append(sel)
    exps = [jnp.exp(v - vals[0]) for v in vals]
    denom = exps[0] + exps[1] + exps[2] + exps[3]
    chosen = jnp.zeros((rows, LANES), F32)
    for sel in sels:
        chosen = jnp.where(sel, 1.0, chosen)
    r = lax.broadcasted_iota(jnp.int32, (rows, rows), 0)
    c = lax.broadcasted_iota(jnp.int32, (rows, rows), 1)
    lower = jnp.where(r > c, 1.0, 0.0).astype(BF16)
    before = jnp.dot(lower, chosen.astype(BF16), preferred_element_type=F32)
    counts = jnp.sum(chosen, axis=0, keepdims=True)
    run_len = jnp.floor((counts + (SUBLANES - 1)) * (1.0 / SUBLANES)) * SUBLANES
    er = lax.broadcasted_iota(jnp.int32, (LANES, LANES), 0)
    ec = lax.broadcasted_iota(jnp.int32, (LANES, LANES), 1)
    earlier = jnp.where(er < ec, 1.0, 0.0).astype(BF16)
    run_start = jnp.dot(jnp.broadcast_to(run_len, (SUBLANES, LANES)).astype(BF16), earlier,
                        preferred_element_type=F32)[:1]
    slot = before + run_start
    idx_out = jnp.zeros((rows, LANES), jnp.int32)
    gate_out = jnp.zeros((rows, LANES), F32)
    for kk in range(TOP_K):
        pos = jnp.sum(jnp.where(sels[kk], slot, 0.0), axis=-1, keepdims=True).astype(jnp.int32)
        idx_out = jnp.where(lane == kk, ids[kk].astype(jnp.int32), idx_out)
        idx_out = jnp.where(lane == TOP_K + kk, pos, idx_out)
        gate_out = jnp.where(lane == kk, exps[kk] / denom, gate_out)
    idx_ref[...] = idx_out
    gate_ref[...] = gate_out
    cnt_ref[...] = jnp.broadcast_to(counts, cnt_ref.shape).astype(jnp.int32)


def _run_copy(runs_ref, tile, e, stage, hbm, sem, to_hbm):
    base = (tile * N_EXPERTS + e) * 3
    s = pl.multiple_of(runs_ref[base], SUBLANES)
    g = pl.multiple_of(runs_ref[base + 1], SUBLANES)
    n = pl.multiple_of(runs_ref[base + 2], SUBLANES)
    src, dst = stage.at[pl.ds(s, n), :], hbm.at[pl.ds(g, n), :]
    if not to_hbm:
        src, dst = dst, src
    return n, pltpu.make_async_copy(src, dst, sem)


def _for_each_run(runs_ref, tile, stage, hbm, sem, to_hbm, action):
    def one(e, carry):
        n, cp = _run_copy(runs_ref, tile, e, stage, hbm, sem, to_hbm)

        @pl.when(n > 0)
        def _():
            getattr(cp, action)()

        return carry

    lax.fori_loop(0, N_EXPERTS, one, 0)


def _dispatch_kernel(runs_ref, gaps_ref, h_ref, idx_ref, xs_ref, stage, zeros, sems):
    i = pl.program_id(0)

    def gap_copy(e):
        g = pl.multiple_of(gaps_ref[2 * e], SUBLANES)
        n = pl.multiple_of(gaps_ref[2 * e + 1], SUBLANES)
        return n, pltpu.make_async_copy(zeros.at[pl.ds(0, n), :], xs_ref.at[pl.ds(g, n), :], sems.at[2])

    def for_each_gap(action):
        def one(e, carry):
            n, cp = gap_copy(e)

            @pl.when(n > 0)
            def _():
                getattr(cp, action)()

            return carry

        lax.fori_loop(0, N_EXPERTS, one, 0)

        def tail(b, carry):
            rows = pl.ds(pl.multiple_of(b * EXPERT_TILE, EXPERT_TILE), EXPERT_TILE)
            getattr(pltpu.make_async_copy(zeros.at[pl.ds(0, EXPERT_TILE), :], xs_ref.at[rows, :], sems.at[2]),
                    action)()
            return carry

        lax.fori_loop(gaps_ref[2 * N_EXPERTS], xs_ref.shape[0] // EXPERT_TILE, tail, 0)

    @pl.when(i == 0)
    def _():
        zeros[...] = jnp.zeros(zeros.shape, zeros.dtype)
        for_each_gap("start")

    pos_t = idx_ref[...].astype(F32).T.astype(jnp.int32)
    h = h_ref[...]
    chunk = EXPERT_TILE
    row_id = lax.broadcasted_iota(jnp.int32, (chunk, ROW_TILE), 0)
    slot = i % 2
    for c in range(STAGE_ROWS // chunk):
        onehot = jnp.zeros((chunk, ROW_TILE), F32)
        for kk in range(TOP_K):
            onehot = jnp.where(row_id == (pos_t[TOP_K + kk:TOP_K + kk + 1] - c * chunk), 1.0, onehot)
        stage[slot, c * chunk:(c + 1) * chunk, :] = _pack_rows(
            jnp.dot(onehot.astype(BF16), h, preferred_element_type=F32))

    _for_each_run(runs_ref, i, stage.at[slot], xs_ref, sems.at[slot], True, "start")

    @pl.when(i > 0)
    def _():
        _for_each_run(runs_ref, i - 1, stage.at[1 - slot], xs_ref, sems.at[1 - slot], True, "wait")

    @pl.when(i == pl.num_programs(0) - 1)
    def _():
        _for_each_run(runs_ref, i, stage.at[slot], xs_ref, sems.at[slot], True, "wait")

    @pl.when(i == 0)
    def _():
        for_each_gap("wait")


def _dispatch(runs, gaps, hb, idx, n_rows):
    T = hb.shape[0]
    grid_spec = pltpu.PrefetchScalarGridSpec(
        num_scalar_prefetch=2,
        grid=(T // ROW_TILE,),
        in_specs=[pl.BlockSpec((ROW_TILE, D_MODEL), lambda i, r, g: (i, 0)),
                  pl.BlockSpec((ROW_TILE, LANES), lambda i, r, g: (i, 0))],
        out_specs=pl.BlockSpec(memory_space=pl.ANY),
        scratch_shapes=[pltpu.VMEM((2, STAGE_ROWS, PACKED_W), jnp.uint32),
                        pltpu.VMEM((EXPERT_STEP, PACKED_W), jnp.uint32),
                        pltpu.SemaphoreType.DMA((3,))],
    )
    return pl.pallas_call(
        _dispatch_kernel,
        out_shape=jax.ShapeDtypeStruct((n_rows, PACKED_W), jnp.uint32),
        grid_spec=grid_spec,
        compiler_params=_cparams("arbitrary"),
        name="dispatch_rows",
    )(runs, gaps, hb, idx)


def _expert_kernel(blk_e_ref, nact_ref, next_e_ref, nsub_ref, x_ref, bu_ref, bd_ref, wu_hbm, wd_hbm, y_ref,
                   wuf_scr, wdf_scr, wub_scr, wdb_scr, sems):
    i = pl.program_id(0)
    e = blk_e_ref[i]

    def weight_copies(expert):
        return (pltpu.make_async_copy(wu_hbm.at[expert], wuf_scr, sems.at[0]),
                pltpu.make_async_copy(wd_hbm.at[expert], wdf_scr, sems.at[1]))

    @pl.when(i == 0)
    def _():
        for cp in weight_copies(e):
            cp.start()

    first_of_group = jnp.logical_or(i == 0, e != blk_e_ref[jnp.maximum(i - 1, 0)])

    @pl.when(jnp.logical_and(i < nact_ref[0], first_of_group))
    def _():
        for cp in weight_copies(e):
            cp.wait()
        wub_scr[...] = wuf_scr[...].astype(BF16)
        wdb_scr[...] = wdf_scr[...].astype(BF16)

        @pl.when(next_e_ref[i] >= 0)
        def _():
            for cp in weight_copies(next_e_ref[i]):
                cp.start()

    def ffn(rows):
        lo, hi = _unpack_rows(x_ref[rows, :])
        x = jnp.concatenate([lo.astype(BF16), hi.astype(BF16)], axis=-1)
        glu = jnp.dot(x, wub_scr[:, :D_EXPERT], preferred_element_type=F32)
        lin = jnp.dot(x, wub_scr[:, D_EXPERT:], preferred_element_type=F32)
        glu = jnp.minimum(glu + bu_ref[0, :, :D_EXPERT], SWIGLU_LIMIT)
        lin = jnp.clip(lin + bu_ref[0, :, D_EXPERT:], -SWIGLU_LIMIT, SWIGLU_LIMIT)
        a = glu * _sigmoid(SWIGLU_ALPHA * glu) * (lin + 1.0)
        y = jnp.dot(a.astype(BF16), wdb_scr[...], preferred_element_type=F32)
        y_ref[rows, :] = _pack_rows(y + bd_ref[0])

    n_sub = EXPERT_STEP // EXPERT_TILE
    for used in range(n_sub + 1):
        @pl.when(nsub_ref[i] == used)
        def _():
            if used:
                ffn(slice(0, used * EXPERT_TILE))
            if used < n_sub:
                y_ref[used * EXPERT_TILE:, :] = jnp.zeros(((n_sub - used) * EXPERT_TILE, PACKED_W), y_ref.dtype)


def _experts(blk_e, nact, next_e, nsub, xs, w_up, b_up, w_down, b_down):
    P = xs.shape[0]
    n_blocks = P // EXPERT_STEP
    act = lambda i, na: jnp.minimum(i, na[0] - 1)
    grid_spec = pltpu.PrefetchScalarGridSpec(
        num_scalar_prefetch=4,
        grid=(n_blocks,),
        in_specs=[pl.BlockSpec((EXPERT_STEP, PACKED_W), lambda i, be, na, ne, ns: (act(i, na), 0)),
                  pl.BlockSpec((1, 1, 2 * D_EXPERT), lambda i, be, na, ne, ns: (be[act(i, na)], 0, 0)),
                  pl.BlockSpec((1, 1, D_MODEL), lambda i, be, na, ne, ns: (be[act(i, na)], 0, 0)),
                  pl.BlockSpec(memory_space=pl.ANY),
                  pl.BlockSpec(memory_space=pl.ANY)],
        out_specs=pl.BlockSpec((EXPERT_STEP, PACKED_W), lambda i, be, na, ne, ns: (i, 0)),
        scratch_shapes=[pltpu.VMEM((D_MODEL, 2 * D_EXPERT), F32),
                        pltpu.VMEM((D_EXPERT, D_MODEL), F32),
                        pltpu.VMEM((D_MODEL, 2 * D_EXPERT), BF16),
                        pltpu.VMEM((D_EXPERT, D_MODEL), BF16),
                        pltpu.SemaphoreType.DMA((2,))],
    )
    return pl.pallas_call(
        _expert_kernel,
        out_shape=jax.ShapeDtypeStruct((P, PACKED_W), jnp.uint32),
        grid_spec=grid_spec,
        compiler_params=_cparams("arbitrary"),
        name="routed_experts",
    )(blk_e, nact, next_e, nsub, xs, b_up.reshape(N_EXPERTS, 1, 2 * D_EXPERT),
      b_down.reshape(N_EXPERTS, 1, D_MODEL), w_up, w_down)


def _combine_kernel(runs_ref, h_ref, idx_ref, gate_ref, g_ref, b_ref, ys_ref, o_ref, ybuf, sems):
    i = pl.program_id(0)
    n = pl.num_programs(0)

    def gather(tile, slot, action):
        _for_each_run(runs_ref, tile, ybuf.at[slot], ys_ref, sems.at[slot], False, action)

    @pl.when(i == 0)
    def _():
        ybuf[...] = jnp.zeros(ybuf.shape, ybuf.dtype)
        gather(0, 0, "start")

    @pl.when(i + 1 < n)
    def _():
        gather(i + 1, (i + 1) % 2, "start")

    slot = i % 2
    gather(i, slot, "wait")

    idx = idx_ref[...]
    gate = gate_ref[...]
    chunk = EXPERT_TILE
    col_id = lax.broadcasted_iota(jnp.int32, (ROW_TILE, LANES), 1)
    pos_b = [jnp.broadcast_to(idx[:, TOP_K + kk:TOP_K + kk + 1], (ROW_TILE, LANES)) for kk in range(TOP_K)]
    gate_b = [jnp.broadcast_to(gate[:, kk:kk + 1], (ROW_TILE, LANES)) for kk in range(TOP_K)]
    f_lo = jnp.zeros((ROW_TILE, PACKED_W), F32)
    f_hi = jnp.zeros((ROW_TILE, PACKED_W), F32)
    for c in range(STAGE_ROWS // chunk):
        parts = []
        for off in range(c * chunk, (c + 1) * chunk, LANES):
            part = jnp.zeros((ROW_TILE, LANES), F32)
            for kk in range(TOP_K):
                part = jnp.where(col_id + off == pos_b[kk], gate_b[kk], part)
            parts.append(part)
        wb = jnp.concatenate(parts, axis=-1).astype(BF16)
        lo, hi = _unpack_rows(ybuf[slot, c * chunk:(c + 1) * chunk, :])
        f_lo = f_lo + jnp.dot(wb, lo.astype(BF16), preferred_element_type=F32)
        f_hi = f_hi + jnp.dot(wb, hi.astype(BF16), preferred_element_type=F32)
    y_lo = DN_ALPHA * h_ref[:, :PACKED_W] + f_lo
    y_hi = DN_ALPHA * h_ref[:, PACKED_W:] + f_hi
    mu = (jnp.sum(y_lo, axis=-1, keepdims=True) + jnp.sum(y_hi, axis=-1, keepdims=True)) / D_MODEL
    d_lo = y_lo - mu
    d_hi = y_hi - mu
    var = (jnp.sum(d_lo * d_lo, axis=-1, keepdims=True) + jnp.sum(d_hi * d_hi, axis=-1, keepdims=True)) / D_MODEL
    inv = lax.rsqrt(var + LN_EPS)
    o_ref[:, :PACKED_W] = d_lo * inv * g_ref[:, :PACKED_W] + b_ref[:, :PACKED_W]
    o_ref[:, PACKED_W:] = d_hi * inv * g_ref[:, PACKED_W:] + b_ref[:, PACKED_W:]


def _combine(runs, h, idx, gate, ys, g, b):
    T = h.shape[0]
    grid_spec = pltpu.PrefetchScalarGridSpec(
        num_scalar_prefetch=1,
        grid=(T // ROW_TILE,),
        in_specs=[pl.BlockSpec((ROW_TILE, D_MODEL), lambda i, r: (i, 0)),
                  pl.BlockSpec((ROW_TILE, LANES), lambda i, r: (i, 0)),
                  pl.BlockSpec((ROW_TILE, LANES), lambda i, r: (i, 0)),
                  pl.BlockSpec(g.shape, lambda i, r: (0, 0)),
                  pl.BlockSpec(b.shape, lambda i, r: (0, 0)),
                  pl.BlockSpec(memory_space=pl.ANY)],
        out_specs=pl.BlockSpec((ROW_TILE, D_MODEL), lambda i, r: (i, 0)),
        scratch_shapes=[pltpu.VMEM((2, STAGE_ROWS, PACKED_W), jnp.uint32),
                        pltpu.SemaphoreType.DMA((2,))],
    )
    return pl.pallas_call(
        _combine_kernel,
        out_shape=jax.ShapeDtypeStruct((T, D_MODEL), F32),
        grid_spec=grid_spec,
        compiler_params=_cparams("arbitrary"),
        name="combine_ln",
    )(runs, h, idx, gate, g, b, ys)


def _pack_w_in(w_in):
    wt = w_in.T
    lat = Q_LORA + KV_LORA + MLA_ROPE
    half = MLA_ROPE // 2
    wt_lat = jnp.concatenate([wt[:lat], -wt[lat - half:lat], wt[lat - MLA_ROPE:lat - half]], axis=0)
    return wt[lat:].astype(BF16), wt_lat.astype(BF16)


def _pack_w_uq(w_uq):
    w = w_uq.reshape(Q_LORA, MLA_HEADS, MLA_NOPE + MLA_ROPE)
    rope = w[:, :, MLA_NOPE:]
    rot = jnp.concatenate([-rope[:, :, MLA_ROPE // 2:], rope[:, :, :MLA_ROPE // 2]], axis=-1)
    return jnp.concatenate([w, rot], axis=-1).reshape(Q_LORA, MLA_HEADS * MLA_QK).astype(BF16)


def _rope_freqs():
    ret = 1.0 / (10000.0 ** jnp.linspace(0.0, 1.0, RET_DK // 2, dtype=F32))
    mla = 1.0 / (ROPE_BASE ** (jnp.arange(0, MLA_ROPE, 2, dtype=F32) / MLA_ROPE))
    return jnp.concatenate([ret, mla, jnp.zeros((LANES - ret.shape[0] - mla.shape[0],), F32)])[None, :]


def _routing_plan(cnt, n_tokens):
    counts = cnt[::SUBLANES, :N_EXPERTS]
    n_tiles = counts.shape[0]
    run_len = ((counts + SUBLANES - 1) // SUBLANES) * SUBLANES
    stage_off = jnp.cumsum(run_len, axis=1) - run_len
    tile_off = jnp.cumsum(run_len, axis=0) - run_len
    rows_used = jnp.sum(run_len, axis=0)
    padded = ((rows_used + EXPERT_STEP - 1) // EXPERT_STEP) * EXPERT_STEP
    pad_end = jnp.cumsum(padded)
    pad_start = pad_end - padded
    runs = jnp.stack([stage_off, pad_start[None, :] + tile_off, run_len], axis=-1).reshape(-1).astype(jnp.int32)
    nact = (pad_end[-1:] // EXPERT_STEP).astype(jnp.int32)
    gaps = jnp.concatenate([jnp.stack([pad_start + rows_used, padded - rows_used], axis=-1).reshape(-1),
                            pad_end[-1:] // EXPERT_TILE]).astype(jnp.int32)
    n_rows = n_tokens * TOP_K + (SUBLANES - 1) * n_tiles * N_EXPERTS + N_EXPERTS * EXPERT_STEP
    n_rows = -(-n_rows // EXPERT_STEP) * EXPERT_STEP
    n_blocks = n_rows // EXPERT_STEP
    experts = jnp.arange(N_EXPERTS, dtype=jnp.int32)
    blk_row = jnp.arange(n_blocks, dtype=jnp.int32) * EXPERT_STEP
    blk_e = jnp.minimum(jnp.sum((pad_end[None, :] <= blk_row[:, None]).astype(jnp.int32), axis=-1),
                        N_EXPERTS - 1)
    used_end = jnp.sum(jnp.where(blk_e[:, None] == experts[None, :], (pad_start + rows_used)[None, :], 0), axis=-1)
    nsub = jnp.clip((used_end - blk_row + EXPERT_TILE - 1) // EXPERT_TILE, 0,
                    EXPERT_STEP // EXPERT_TILE).astype(jnp.int32)
    later = jnp.where((rows_used > 0)[None, :] & (experts[None, :] > experts[:, None]), experts[None, :], N_EXPERTS)
    nxt = jnp.min(later, axis=-1)
    nxt = jnp.where(nxt == N_EXPERTS, -1, nxt)
    next_e = jnp.sum(jnp.where(blk_e[:, None] == experts[None, :], nxt[None, :], 0), axis=-1).astype(jnp.int32)
    return runs, gaps, blk_e, nact, next_e, nsub, n_rows


def kernel(x, positions, w_in, q_norm_g, w_uq, kv_norm_g, w_ukv, w_o, ln1_g, ln1_b,
           w_router, b_router, w_up, b_up, w_down, b_down, ln2_g, ln2_b):
    B, S, _ = x.shape
    T = B * S
    assert S % ATTN_Q_TILE == 0 and T % ROW_TILE == 0, "sequence / token count must be whole tiles"
    rc, rs, mc, ms = _rope_tables(positions.reshape(T // LANES, LANES).astype(jnp.int32), _rope_freqs())
    dec, qw, kw, cd = _retention_constants()
    h = x.reshape(T, D_MODEL)
    for l in range(DEPTH):
        proj = _input_projection(h, *_pack_w_in(w_in[l]))
        w_kv = w_ukv[l].reshape(KV_LORA, MLA_HEADS, MLA_NOPE + MLA_V)
        w_k = w_kv[:, :, :MLA_NOPE].reshape(KV_LORA, MLA_HEADS * MLA_NOPE).astype(BF16)
        w_vt = w_kv[:, :, MLA_NOPE:].reshape(KV_LORA, MLA_HEADS * MLA_V).T.astype(BF16)
        q_cat, kn, vt, kr = _mla_prep(proj, mc, ms, _pack_w_uq(w_uq[l]), w_k, w_vt,
                                      q_norm_g[l][None, :], kv_norm_g[l][None, :])
        o_mla = _attention(q_cat, kn, kr, vt, B, S)
        o_ret = _retention(proj, rc, rs, dec, qw, kw, cd, B, S)
        h1, h1b, idx_out, gate, cnt = _merge_and_route(
            h, o_mla, o_ret, proj, w_o[l].astype(BF16), ln1_g[l][None, :], ln1_b[l][None, :],
            w_router[l], b_router[l])
        runs, gaps, blk_e, nact, next_e, nsub, n_rows = _routing_plan(cnt, T)
        xs = _dispatch(runs, gaps, h1b, idx_out, n_rows)
        ys = _experts(blk_e, nact, next_e, nsub, xs, w_up[l], b_up[l], w_down[l], b_down[l])
        h = _combine(runs, h1, idx_out, gate, ys, ln2_g[l][None, :], ln2_b[l][None, :])
    return h.reshape(B, S, D_MODEL)
```

```python
import functools

import numpy as np
import jax
import jax.numpy as jnp
from jax import lax
from jax.experimental import pallas as pl
from jax.experimental.pallas import tpu as pltpu

F32 = jnp.float32
BF16 = jnp.bfloat16

D_MODEL = 1024
DEPTH = 1
MLA_HEADS = 8
MLA_NOPE = 128
MLA_ROPE = 64
MLA_V = 128
Q_LORA = 384
KV_LORA = 256
ROPE_BASE = 10000.0
RET_HEADS = 8
RET_DK = 128
RET_DV = 128
RET_CHUNK = 128
N_EXPERTS = 32
TOP_K = 4
D_EXPERT = 1024
SWIGLU_LIMIT = 7.0
SWIGLU_ALPHA = 1.702
DN_ALPHA = (2.0 * DEPTH) ** 0.25
LN_EPS = 1e-5
RMS_EPS = 1e-6
GN_EPS = 1e-6
NEG_INF = -1e30

LANES = 128
SUBLANES = 8
MLA_QK = 2 * LANES
PACKED_W = D_MODEL // 2
LATENT_W = 768
N_WIDE = 6 * D_MODEL
N_PROJ = N_WIDE + LATENT_W
VMEM_LIMIT = 56 * 1024 * 1024

ROW_TILE = 512
ATTN_TILE = ROW_TILE
ATTN_Q_TILE = 2 * ATTN_TILE
ATTN_Q_SPLIT = 4
VT_ROWS = MLA_V + 16
LOG2_E = 1.4426950408889634
RET_GROUP = 16
EXPERT_TILE = 256
EXPERT_STEP = 4 * EXPERT_TILE
STAGE_ROWS = -(-(ROW_TILE * TOP_K + (SUBLANES - 1) * N_EXPERTS) // EXPERT_TILE) * EXPERT_TILE


def _cparams(*sem):
    return pltpu.CompilerParams(dimension_semantics=sem, vmem_limit_bytes=VMEM_LIMIT)


def _const_spec(shape):
    nd = len(shape)
    return pl.BlockSpec(shape, lambda *_: (0,) * nd, pipeline_mode=pl.Buffered(1))


def _sigmoid(x):
    return 0.5 * jnp.tanh(0.5 * x) + 0.5


def _pack_rows(x):
    n = x.shape[-1] // 2
    xb = x.astype(BF16).astype(F32)
    lo = pltpu.bitcast(xb[:, :n], jnp.uint32) >> 16
    hi = pltpu.bitcast(xb[:, n:], jnp.uint32) & jnp.uint32(0xFFFF0000)
    return lo | hi


def _unpack_rows(w):
    lo = pltpu.bitcast(w << 16, F32)
    hi = pltpu.bitcast(w & jnp.uint32(0xFFFF0000), F32)
    return lo, hi


def _swap_halves(x):
    return pltpu.roll(x, LANES // 2, 1)


def _proj_kernel(x_ref, wt_wide_ref, wt_lat_ref, o_ref):
    xb = x_ref[...].astype(BF16)
    nt = (((1,), (1,)), ((), ()))
    for c in range(N_WIDE // LATENT_W):
        sl = slice(c * LATENT_W, (c + 1) * LATENT_W)
        o_ref[:, sl] = lax.dot_general(xb, wt_wide_ref[sl, :], nt, preferred_element_type=F32).astype(o_ref.dtype)
    o_ref[:, N_WIDE:] = lax.dot_general(xb, wt_lat_ref[...], nt, preferred_element_type=F32).astype(o_ref.dtype)


def _input_projection(x2, wt_wide, wt_lat):
    T = x2.shape[0]
    return pl.pallas_call(
        _proj_kernel,
        out_shape=jax.ShapeDtypeStruct((T, N_PROJ), BF16),
        grid=(T // ROW_TILE,),
        in_specs=[pl.BlockSpec((ROW_TILE, D_MODEL), lambda i: (i, 0)),
                  _const_spec((N_WIDE, D_MODEL)), _const_spec((LATENT_W, D_MODEL))],
        out_specs=pl.BlockSpec((ROW_TILE, N_PROJ), lambda i: (i, 0)),
        compiler_params=_cparams("parallel"),
        name="input_projection",
    )(x2, wt_wide, wt_lat)


def _rope_table_kernel(pos_ref, freq_ref, rc_ref, rs_ref, mc_ref, ms_ref):
    pos = pos_ref[...].astype(F32)
    pos_t = jnp.concatenate([pos, jnp.zeros((LANES - pos.shape[0], LANES), F32)], axis=0).T
    freq = freq_ref[...]
    ang = jnp.concatenate([pos_t[:, j:j + 1] * freq for j in range(pos.shape[0])], axis=0)
    c = jnp.cos(ang)
    s = jnp.sin(ang)
    lane = lax.broadcasted_iota(jnp.int32, ang.shape, 1)
    c64 = pltpu.roll(c, 64, 1)
    s64 = pltpu.roll(s, 64, 1)
    rc_ref[...] = jnp.where(lane < 64, c, c64)
    rs_ref[...] = jnp.where(lane < 64, -s, s64)
    c96 = pltpu.roll(c, 96, 1)
    s96 = pltpu.roll(s, 96, 1)
    mc_ref[...] = jnp.where(lane < 32, c64, jnp.where(lane < 64, c96, 0.0))
    ms_ref[...] = jnp.where(lane < 32, s64, jnp.where(lane < 64, s96, 0.0))


def _rope_tables(pos2d, freq_row):
    T = pos2d.shape[0] * LANES
    rows = SUBLANES * LANES
    tab = jax.ShapeDtypeStruct((T, LANES), F32)
    spec = pl.BlockSpec((rows, LANES), lambda i: (i, 0))
    return pl.pallas_call(
        _rope_table_kernel,
        out_shape=(tab, tab, tab, tab),
        grid=(T // rows,),
        in_specs=[pl.BlockSpec((SUBLANES, LANES), lambda i: (i, 0)), _const_spec((1, LANES))],
        out_specs=(spec, spec, spec, spec),
        compiler_params=_cparams("parallel"),
        name="rope_tables",
    )(pos2d, freq_row)


def _mla_prep_kernel(a_ref, mc_ref, ms_ref, wq_ref, wk_ref, wvt_ref, qg_ref, kvg_ref,
                     q_ref, kn_ref, vt_ref, kr_ref):
    a = a_ref[...].astype(F32)
    mc = mc_ref[...]
    ms = ms_ref[...]
    cq = a[:, :Q_LORA]
    qn = cq * lax.rsqrt(jnp.mean(cq * cq, axis=-1, keepdims=True) + RMS_EPS) * qg_ref[...]
    q = jnp.dot(qn.astype(BF16), wq_ref[...], preferred_element_type=F32)
    scale = (MLA_NOPE + MLA_ROPE) ** -0.5 * LOG2_E
    for h in range(MLA_HEADS):
        nope = q[:, h * MLA_QK:h * MLA_QK + LANES]
        blk = q[:, h * MLA_QK + LANES:(h + 1) * MLA_QK]
        rope = blk * mc + _swap_halves(blk) * ms
        q_ref[:, h * MLA_QK:h * MLA_QK + LANES] = (nope * scale).astype(BF16)
        q_ref[:, h * MLA_QK + LANES:(h + 1) * MLA_QK] = (rope * scale).astype(BF16)
    ckv = a[:, Q_LORA:Q_LORA + KV_LORA]
    kvn = ckv * lax.rsqrt(jnp.mean(ckv * ckv, axis=-1, keepdims=True) + RMS_EPS) * kvg_ref[...]
    kvb = kvn.astype(BF16)
    kn_ref[...] = jnp.dot(kvb, wk_ref[...], preferred_element_type=F32).astype(BF16)
    vt = lax.dot_general(wvt_ref[...], kvb, (((1,), (1,)), ((), ())),
                         preferred_element_type=F32).astype(BF16)
    ones = jnp.ones((VT_ROWS - MLA_V, vt.shape[1]), BF16)
    for h in range(MLA_HEADS):
        vt_ref[0, h * VT_ROWS:h * VT_ROWS + MLA_V, :] = vt[h * MLA_V:(h + 1) * MLA_V, :]
        vt_ref[0, h * VT_ROWS + MLA_V:(h + 1) * VT_ROWS, :] = ones
    krb =a[:, Q_LORA + KV_LORA:]
    kr_ref[...] = (krb * mc + _swap_halves(krb) * ms).astype(BF16)


def _mla_prep(proj, mc, ms, wq, wk, wvt, qg, kvg):
    T = proj.shape[0]
    row = lambda w: pl.BlockSpec((ROW_TILE, w), lambda i: (i, 0))
    return pl.pallas_call(
        _mla_prep_kernel,
        out_shape=(jax.ShapeDtypeStruct((T, MLA_HEADS * MLA_QK), BF16),
                   jax.ShapeDtypeStruct((T, MLA_HEADS * MLA_NOPE), BF16),
                   jax.ShapeDtypeStruct((T // ROW_TILE, MLA_HEADS * VT_ROWS, ROW_TILE), BF16),
                   jax.ShapeDtypeStruct((T, LANES), BF16)),
        grid=(T // ROW_TILE,),
        in_specs=[pl.BlockSpec((ROW_TILE, LATENT_W), lambda i: (i, N_WIDE // LATENT_W)),
                  row(LANES), row(LANES),
                  _const_spec(wq.shape), _const_spec(wk.shape), _const_spec(wvt.shape),
                  _const_spec(qg.shape), _const_spec(kvg.shape)],
        out_specs=(row(MLA_HEADS * MLA_QK), row(MLA_HEADS * MLA_NOPE),
                   pl.BlockSpec((1, MLA_HEADS * VT_ROWS, ROW_TILE), lambda i: (i, 0, 0)), row(LANES)),
        compiler_params=_cparams("parallel"),
        name="mla_prep",
    )(proj, mc, ms, wq, wk, wvt, qg, kvg)


def _attn_kernel(q_ref, kn_ref, kr_ref, vt_ref, o_ref, acc_scr):
    W = ATTN_Q_TILE // ATTN_Q_SPLIT
    n_diag = ATTN_Q_TILE // ATTN_TILE
    groups = tuple(range(ATTN_Q_SPLIT))

    def visible(d, c):
        return min(max((c + 1) * W - d * ATTN_TILE, 0), ATTN_TILE)

    for qi in range(q_ref.shape[0] // ATTN_Q_TILE):
        q0 = qi * ATTN_Q_TILE
        qs = [q_ref[q0 + c * W:q0 + (c + 1) * W, :] for c in groups]
        acc_scr[...] = jnp.zeros(acc_scr.shape, F32)

        def scores(j, which=groups):
            rows = slice(j * ATTN_TILE, (j + 1) * ATTN_TILE)
            k = jnp.concatenate([kn_ref[rows, :], kr_ref[rows, :]], axis=-1)
            return {c: lax.dot_general(k, qs[c], (((1,), (1,)), ((), ())), preferred_element_type=F32)
                    for c in which}

        def softmax_pv(j, s_all, m_all, diag=None):
            vt = vt_ref[j]
            out = list(m_all)
            for c in s_all:
                s = s_all[c]
                nk = ATTN_TILE if diag is None else visible(diag, c)
                if diag is not None:
                    s = s[:nk]
                    if diag * ATTN_TILE + nk - 1 > c * W:
                        kp = lax.broadcasted_iota(jnp.int32, s.shape, 0) + diag * ATTN_TILE
                        qp = lax.broadcasted_iota(jnp.int32, s.shape, 1) + c * W
                        s = jnp.where(kp <= qp, s, NEG_INF)
                m_new = jnp.maximum(m_all[c], jnp.max(s, axis=0, keepdims=True))
                alpha = jnp.exp2(m_all[c] - m_new)
                p = jnp.exp2(s - m_new).astype(BF16)
                acc_scr[c] = alpha * acc_scr[c] + jnp.dot(vt[:, :nk], p, preferred_element_type=F32)
                out[c] = m_new
            return tuple(out)

        first = n_diag * qi
        work = [(j, None, groups) for j in range(first)]
        work += [(first + d, d, tuple(c for c in groups if visible(d, c) > 0)) for d in range(n_diag)]
        m_all = tuple(jnp.full((1, W), NEG_INF, F32) for _ in groups)
        pending = [scores(j, which) for j, _, which in work[:2]]
        for n, (j, diag, _) in enumerate(work):
            m_all = softmax_pv(j, pending.pop(0), m_all, diag)
            if n + 2 < len(work):
                pending.append(scores(work[n + 2][0], work[n + 2][2]))
        for c in groups:
            acc = acc_scr[c]
            o_ref[q0 + c * W:q0 + (c + 1) * W, :] = (acc[:MLA_V] / acc[MLA_V:MLA_V + 1]).T.astype(o_ref.dtype)


def _attention(q_cat, kn, kr, vt, B, S):
    T = B * S
    return pl.pallas_call(
        _attn_kernel,
        out_shape=jax.ShapeDtypeStruct((T, MLA_HEADS * MLA_V), BF16),
        grid=(B, MLA_HEADS),
        in_specs=[pl.BlockSpec((S, MLA_QK), lambda b, h: (b, h)),
                  pl.BlockSpec((S, MLA_NOPE), lambda b, h: (b, h)),
                  pl.BlockSpec((S, LANES), lambda b, h: (b, 0)),
                  pl.BlockSpec((S // ATTN_TILE, VT_ROWS, ATTN_TILE), lambda b, h: (b, h, 0))],
        out_specs=pl.BlockSpec((S, MLA_V), lambda b, h: (b, h)),
        scratch_shapes=[pltpu.VMEM((ATTN_Q_SPLIT, VT_ROWS, ATTN_Q_TILE // ATTN_Q_SPLIT), F32)],
        compiler_params=_cparams("parallel", "parallel"),
        name="mla_attention",
    )(q_cat, kn, kr, vt)


def _ret_kernel(q_ref, k_ref, v_ref, g_ref, rc_ref, rs_ref, dec_ref, qw_ref, kw_ref, cd_ref,
                o_ref, qb_scr, qwb_scr, kb_scr, kwb_scr):
    S = q_ref.shape[0]
    C = RET_CHUNK
    n_chunks = S // C
    rc = rc_ref[...]
    rs = rs_ref[...]
    q = q_ref[...].astype(F32)
    k = k_ref[...].astype(F32)
    q = q * rc + _swap_halves(q) * rs
    k = (k * rc + _swap_halves(k) * rs) * (RET_DK ** -0.5)
    qw = qw_ref[0]
    kw = kw_ref[0]
    q3 = q.reshape(n_chunks, C, RET_DK)
    k3 = k.reshape(n_chunks, C, RET_DK)
    qb_scr[...] = q.astype(BF16)
    kb_scr[...] = k.astype(BF16)
    qwb_scr[...] = (q3 * qw[None]).reshape(S, RET_DK).astype(BF16)
    kwb_scr[...] = (k3 * kw[None]).reshape(S, RET_DK)
    dec = dec_ref[0]
    cd = cd_ref[0]

    def group(gi, R):
        rows_of = [pl.ds(pl.multiple_of((gi * RET_GROUP + u) * C, C), C) for u in range(RET_GROUP)]
        scores, updates = [], []
        for rows in rows_of:
            scores.append(lax.dot_general(qb_scr[rows, :], kb_scr[rows, :], (((1,), (1,)), ((), ())),
                                          preferred_element_type=F32))
            kwt = kwb_scr[rows, :].T.astype(BF16)
            updates.append(jnp.dot(kwt, v_ref[rows, :], preferred_element_type=F32))
        states = []
        for u in range(RET_GROUP):
            states.append(R.astype(BF16))
            R = cd * R + updates[u]
        for u, rows in enumerate(rows_of):
            o = jnp.dot((scores[u] * dec).astype(BF16), v_ref[rows, :], preferred_element_type=F32)
            o = o + jnp.dot(qwb_scr[rows, :], states[u], preferred_element_type=F32)
            mu = jnp.mean(o, axis=-1, keepdims=True)
            d = o - mu
            var = jnp.mean(d * d, axis=-1, keepdims=True)
            gate = g_ref[rows, :].astype(F32)
            o_ref[rows, :] = (d * lax.rsqrt(var + GN_EPS) * (gate * _sigmoid(gate))).astype(o_ref.dtype)
        return R

    lax.fori_loop(0, n_chunks // RET_GROUP, group, jnp.zeros((RET_DK, RET_DV), F32))


def _retention(proj, rc, rs, dec, qw, kw, cd, B, S):
    assert S % (RET_CHUNK * RET_GROUP) == 0, "sequence must hold whole groups of retention chunks"
    T = B * S
    H = RET_HEADS
    col = lambda off: pl.BlockSpec((S, LANES), lambda b, h: (b, off + h))
    tab = pl.BlockSpec((S, LANES), lambda b, h: (b, 0))
    per_head = lambda r: pl.BlockSpec((1, r, LANES), lambda b, h: (h, 0, 0))
    return pl.pallas_call(
        _ret_kernel,
        out_shape=jax.ShapeDtypeStruct((T, H * RET_DV), BF16),
        grid=(B, H),
        in_specs=[col(0), col(H), col(2 * H), col(3 * H), tab, tab,
                  per_head(RET_CHUNK), per_head(RET_CHUNK), per_head(RET_CHUNK), per_head(1)],
        out_specs=pl.BlockSpec((S, RET_DV), lambda b, h: (b, h)),
        scratch_shapes=[pltpu.VMEM((S, RET_DK), BF16), pltpu.VMEM((S, RET_DK), BF16),
                        pltpu.VMEM((S, RET_DK), BF16), pltpu.VMEM((S, RET_DK), F32)],
        compiler_params=_cparams("parallel", "parallel"),
        name="retention",
    )(proj, proj, proj, proj, rc, rs, dec, qw, kw, cd)


def _retention_constants():
    C = RET_CHUNK
    h = np.arange(RET_HEADS, dtype=np.float64)
    log_gamma = np.log(1.0 - 2.0 ** (-5.0 - h))
    idx = np.arange(C, dtype=np.float64)
    rel = idx[:, None] - idx[None, :]
    dec = np.where(rel[None] >= 0, np.exp(np.maximum(rel, 0.0)[None] * log_gamma[:, None, None]), 0.0)
    qw = np.exp((idx + 1.0)[None, :] * log_gamma[:, None])
    kw = np.exp((C - 1.0 - idx)[None, :] * log_gamma[:, None])
    cd = np.exp(C * log_gamma)
    bc = lambda a: np.ascontiguousarray(np.broadcast_to(a[..., None], a.shape + (LANES,))).astype(np.float32)
    return dec.astype(np.float32), bc(qw), bc(kw), bc(cd[:, None])


def _layernorm(y, g, b):
    mu = jnp.mean(y, axis=-1, keepdims=True)
    d = y - mu
    var = jnp.mean(d * d, axis=-1, keepdims=True)
    return d * lax.rsqrt(var + LN_EPS) * g + b


def _merge_kernel(x_ref, om_ref, or_ref, gm_ref, gr_ref, wo_ref, g_ref, b_ref, w01_ref, rb_ref,
                  h_ref, hb_ref, idx_ref, gate_ref, cnt_ref):
    mixed = (_sigmoid(gm_ref[...].astype(F32)) * om_ref[...].astype(F32)
             + _sigmoid(gr_ref[...].astype(F32)) * or_ref[...].astype(F32))
    mix = jnp.dot(mixed.astype(BF16), wo_ref[...], preferred_element_type=F32)
    h = _layernorm(DN_ALPHA * x_ref[...] + mix, g_ref[...], b_ref[...])
    h_ref[...] = h
    hb_ref[...] = h.astype(BF16)
    _route_tile(h, w01_ref, rb_ref, idx_ref, gate_ref, cnt_ref)


def _merge_and_route(x2, o_mla, o_ret, proj, wo, g, b, w_router, b_router):
    T = x2.shape[0]
    wpad = jnp.zeros((D_MODEL, LANES), F32).at[:, :N_EXPERTS].set(w_router)
    w0 = wpad.astype(BF16)
    w01 = jnp.concatenate([w0, (wpad - w0.astype(F32)).astype(BF16)], axis=-1)
    bpad = jnp.zeros((1, LANES), F32).at[0, :N_EXPERTS].set(b_router)
    row = lambda c: pl.BlockSpec((ROW_TILE, D_MODEL), lambda i: (i, c))
    lanes = pl.BlockSpec((ROW_TILE, LANES), lambda i: (i, 0))
    return pl.pallas_call(
        _merge_kernel,
        out_shape=(jax.ShapeDtypeStruct((T, D_MODEL), F32), jax.ShapeDtypeStruct((T, D_MODEL), BF16),
                   jax.ShapeDtypeStruct((T, LANES), jnp.int32), jax.ShapeDtypeStruct((T, LANES), F32),
                   jax.ShapeDtypeStruct((T // ROW_TILE * SUBLANES, LANES), jnp.int32)),
        grid=(T // ROW_TILE,),
        in_specs=[row(0), row(0), row(0), row(4), row(5),
                  _const_spec(wo.shape), _const_spec(g.shape), _const_spec(b.shape),
                  _const_spec(w01.shape), _const_spec(bpad.shape)],
        out_specs=(row(0), row(0), lanes, lanes, pl.BlockSpec((SUBLANES, LANES), lambda i: (i, 0))),
        compiler_params=_cparams("parallel"),
        name="merge_out_proj_ln_route",
    )(x2, o_mla, o_ret, proj, proj, wo, g, b, w01, bpad)


def _split2(a):
    a0 = a.astype(BF16)
    return a0, (a - a0.astype(F32)).astype(BF16)


def _route_tile(h, w01_ref, b_ref, idx_ref, gate_ref, cnt_ref):
    h0, h1 = _split2(h)
    p0 = jnp.dot(h0, w01_ref[...], preferred_element_type=F32)
    p1 = jnp.dot(h1, w01_ref[:, :LANES], preferred_element_type=F32)
    logits = p0[:, :LANES] + (p0[:, LANES:] + p1) + b_ref[...]
    rows = logits.shape[0]
    lane = lax.broadcasted_iota(jnp.int32, (rows, LANES), 1)
    lane_f = lane.astype(F32)
    work = jnp.where(lane < N_EXPERTS, logits, -jnp.inf)
    vals, ids, sels = [], [], []
    for _ in range(TOP_K):
        m = jnp.max(work, axis=-1, keepdims=True)
        first = jnp.min(jnp.where(work == m, lane_f, float(LANES)), axis=-1, keepdims=True)
        sel = lane_f == first
        work = jnp.where(sel, -jnp.inf, work)
        vals.append(m)
        ids.append(first)
        sels.append(sel)
    exps = [jnp.exp(v - vals[0]) for v in vals]
    denom = exps[0] + exps[1] + exps[2] + exps[3]
    chosen = jnp.zeros((rows, LANES), F32)
    for sel in sels:
        chosen = jnp.where(sel, 1.0, chosen)
    r = lax.broadcasted_iota(jnp.int32, (rows, rows), 0)
    c = lax.broadcasted_iota(jnp.int32, (rows, rows), 1)
    lower = jnp.where(r > c, 1.0, 0.0).astype(BF16)
    before = jnp.dot(lower, chosen.astype(BF16), preferred_element_type=F32)
    counts = jnp.sum(chosen, axis=0, keepdims=True)
    run_len = jnp.floor((counts + (SUBLANES - 1)) * (1.0 / SUBLANES)) * SUBLANES
    er = lax.broadcasted_iota(jnp.int32, (LANES, LANES), 0)
    ec = lax.broadcasted_iota(jnp.int32, (LANES, LANES), 1)
    earlier = jnp.where(er < ec, 1.0, 0.0).astype(BF16)
    run_start = jnp.dot(jnp.broadcast_to(run_len, (SUBLANES, LANES)).astype(BF16), earlier,
                        preferred_element_type=F32)[:1]
    slot = before + run_start
    idx_out = jnp.zeros((rows, LANES), jnp.int32)
    gate_out = jnp.zeros((rows, LANES), F32)
    for kk in range(TOP_K):
        pos = jnp.sum(jnp.where(sels[kk], slot, 0.0), axis=-1, keepdims=True).astype(jnp.int32)
        idx_out = jnp.where(lane == kk, ids[kk].astype(jnp.int32), idx_out)
        idx_out = jnp.where(lane == TOP_K + kk, pos, idx_out)
        gate_out = jnp.where(lane == kk, exps[kk] / denom, gate_out)
    idx_ref[...] = idx_out
    gate_ref[...] = gate_out
    cnt_ref[...] = jnp.broadcast_to(counts, cnt_ref.shape).astype(jnp.int32)


def _run_copy(runs_ref, tile, e, stage, hbm, sem, to_hbm):
    base = (tile * N_EXPERTS + e) * 3
    s = pl.multiple_of(runs_ref[base], SUBLANES)
    g = pl.multiple_of(runs_ref[base + 1], SUBLANES)
    n = pl.multiple_of(runs_ref[base + 2], SUBLANES)
    src, dst = stage.at[pl.ds(s, n), :], hbm.at[pl.ds(g, n), :]
    if not to_hbm:
        src, dst = dst, src
    return n, pltpu.make_async_copy(src, dst, sem)


def _for_each_run(runs_ref, tile, stage, hbm, sem, to_hbm, action):
    def one(e, carry):
        n, cp = _run_copy(runs_ref, tile, e, stage, hbm, sem, to_hbm)

        @pl.when(n > 0)
        def _():
            getattr(cp, action)()

        return carry

    lax.fori_loop(0, N_EXPERTS, one, 0)


def _dispatch_kernel(runs_ref, gaps_ref, h_ref, idx_ref, xs_ref, stage, zeros, sems):
    i = pl.program_id(0)

    def gap_copy(e):
        g = pl.multiple_of(gaps_ref[2 * e], SUBLANES)
        n = pl.multiple_of(gaps_ref[2 * e + 1], SUBLANES)
        return n, pltpu.make_async_copy(zeros.at[pl.ds(0, n), :], xs_ref.at[pl.ds(g, n), :], sems.at[2])

    def for_each_gap(action):
        def one(e, carry):
            n, cp = gap_copy(e)

            @pl.when(n > 0)
            def _():
                getattr(cp, action)()

            return carry

        lax.fori_loop(0, N_EXPERTS, one, 0)

        def tail(b, carry):
            rows = pl.ds(pl.multiple_of(b * EXPERT_TILE, EXPERT_TILE), EXPERT_TILE)
            getattr(pltpu.make_async_copy(zeros.at[pl.ds(0, EXPERT_TILE), :], xs_ref.at[rows, :], sems.at[2]),
                    action)()
            return carry

        lax.fori_loop(gaps_ref[2 * N_EXPERTS], xs_ref.shape[0] // EXPERT_TILE, tail, 0)

    @pl.when(i == 0)
    def _():
        zeros[...] = jnp.zeros(zeros.shape, zeros.dtype)
        for_each_gap("start")

    pos_t = idx_ref[...].astype(F32).T.astype(jnp.int32)
    h = h_ref[...]
    chunk = EXPERT_TILE
    row_id = lax.broadcasted_iota(jnp.int32, (chunk, ROW_TILE), 0)
    slot = i % 2
    for c in range(STAGE_ROWS // chunk):
        onehot = jnp.zeros((chunk, ROW_TILE), F32)
        for kk in range(TOP_K):
            onehot = jnp.where(row_id == (pos_t[TOP_K + kk:TOP_K + kk + 1] - c * chunk), 1.0, onehot)
        stage[slot, c * chunk:(c + 1) * chunk, :] = _pack_rows(
            jnp.dot(onehot.astype(BF16), h, preferred_element_type=F32))

    _for_each_run(runs_ref, i, stage.at[slot], xs_ref, sems.at[slot], True, "start")

    @pl.when(i > 0)
    def _():
        _for_each_run(runs_ref, i - 1, stage.at[1 - slot], xs_ref, sems.at[1 - slot], True, "wait")

    @pl.when(i == pl.num_programs(0) - 1)
    def _():
        _for_each_run(runs_ref, i, stage.at[slot], xs_ref, sems.at[slot], True, "wait")
        for_each_gap("wait")


def _dispatch(runs, gaps, hb, idx, n_rows):
    T = hb.shape[0]
    grid_spec = pltpu.PrefetchScalarGridSpec(
        num_scalar_prefetch=2,
        grid=(T // ROW_TILE,),
        in_specs=[pl.BlockSpec((ROW_TILE, D_MODEL), lambda i, r, g: (i, 0)),
                  pl.BlockSpec((ROW_TILE, LANES), lambda i, r, g: (i, 0))],
        out_specs=pl.BlockSpec(memory_space=pl.ANY),
        scratch_shapes=[pltpu.VMEM((2, STAGE_ROWS, PACKED_W), jnp.uint32),
                        pltpu.VMEM((EXPERT_STEP, PACKED_W), jnp.uint32),
                        pltpu.SemaphoreType.DMA((3,))],
    )
    return pl.pallas_call(
        _dispatch_kernel,
        out_shape=jax.ShapeDtypeStruct((n_rows, PACKED_W), jnp.uint32),
        grid_spec=grid_spec,
        compiler_params=_cparams("arbitrary"),
        name="dispatch_rows",
    )(runs, gaps, hb, idx)


def _expert_kernel(blk_e_ref, nact_ref, next_e_ref, nsub_ref, x_ref, bu_ref, bd_ref, wu_hbm, wd_hbm, y_ref,
                   wuf_scr, wdf_scr, wub_scr, wdb_scr, sems):
    i = pl.program_id(0)
    e = blk_e_ref[i]

    def weight_copies(expert):
        return (pltpu.make_async_copy(wu_hbm.at[expert], wuf_scr, sems.at[0]),
                pltpu.make_async_copy(wd_hbm.at[expert], wdf_scr, sems.at[1]))

    @pl.when(i == 0)
    def _():
        for cp in weight_copies(e):
            cp.start()

    first_of_group = jnp.logical_or(i == 0, e != blk_e_ref[jnp.maximum(i - 1, 0)])

    @pl.when(jnp.logical_and(i < nact_ref[0], first_of_group))
    def _():
        for cp in weight_copies(e):
            cp.wait()
        wub_scr[...] = wuf_scr[...].astype(BF16)
        wdb_scr[...] = wdf_scr[...].astype(BF16)

        @pl.when(next_e_ref[i] >= 0)
        def _():
            for cp in weight_copies(next_e_ref[i]):
                cp.start()

    def ffn(rows):
        lo, hi = _unpack_rows(x_ref[rows, :])
        x = jnp.concatenate([lo.astype(BF16), hi.astype(BF16)], axis=-1)
        glu = jnp.dot(x, wub_scr[:, :D_EXPERT], preferred_element_type=F32)
        lin = jnp.dot(x, wub_scr[:, D_EXPERT:], preferred_element_type=F32)
        glu = jnp.minimum(glu + bu_ref[0, :, :D_EXPERT], SWIGLU_LIMIT)
        lin = jnp.clip(lin + bu_ref[0, :, D_EXPERT:], -SWIGLU_LIMIT, SWIGLU_LIMIT)
        a = glu * _sigmoid(SWIGLU_ALPHA * glu) * (lin + 1.0)
        y = jnp.dot(a.astype(BF16), wdb_scr[...], preferred_element_type=F32)
        y_ref[rows, :] = _pack_rows(y + bd_ref[0])

    n_sub = EXPERT_STEP // EXPERT_TILE
    for used in range(n_sub + 1):
        @pl.when(nsub_ref[i] == used)
        def _():
            if used:
                ffn(slice(0, used * EXPERT_TILE))
            if used < n_sub:
                y_ref[used * EXPERT_TILE:, :] = jnp.zeros(((n_sub - used) * EXPERT_TILE, PACKED_W), y_ref.dtype)


def _experts(blk_e, nact, next_e, nsub, xs, w_up, b_up, w_down, b_down):
    P = xs.shape[0]
    n_blocks = P // EXPERT_STEP
    act = lambda i, na: jnp.minimum(i, na[0] - 1)
    grid_spec = pltpu.PrefetchScalarGridSpec(
        num_scalar_prefetch=4,
        grid=(n_blocks,),
        in_specs=[pl.BlockSpec((EXPERT_STEP, PACKED_W), lambda i, be, na, ne, ns: (act(i, na), 0)),
                  pl.BlockSpec((1, 1, 2 * D_EXPERT), lambda i, be, na, ne, ns: (be[act(i, na)], 0, 0)),
                  pl.BlockSpec((1, 1, D_MODEL), lambda i, be, na, ne, ns: (be[act(i, na)], 0, 0)),
                  pl.BlockSpec(memory_space=pl.ANY),
                  pl.BlockSpec(memory_space=pl.ANY)],
        out_specs=pl.BlockSpec((EXPERT_STEP, PACKED_W), lambda i, be, na, ne, ns: (i, 0)),
        scratch_shapes=[pltpu.VMEM((D_MODEL, 2 * D_EXPERT), F32),
                        pltpu.VMEM((D_EXPERT, D_MODEL), F32),
                        pltpu.VMEM((D_MODEL, 2 * D_EXPERT), BF16),
                        pltpu.VMEM((D_EXPERT, D_MODEL), BF16),
                        pltpu.SemaphoreType.DMA((2,))],
    )
    return pl.pallas_call(
        _expert_kernel,
        out_shape=jax.ShapeDtypeStruct((P, PACKED_W), jnp.uint32),
        grid_spec=grid_spec,
        compiler_params=_cparams("arbitrary"),
        name="routed_experts",
    )(blk_e, nact, next_e, nsub, xs, b_up.reshape(N_EXPERTS, 1, 2 * D_EXPERT),
      b_down.reshape(N_EXPERTS, 1, D_MODEL), w_up, w_down)


def _combine_kernel(runs_ref, h_ref, idx_ref, gate_ref, g_ref, b_ref, ys_ref, o_ref, ybuf, sems):
    i = pl.program_id(0)
    n = pl.num_programs(0)

    def gather(tile, slot, action):
        _for_each_run(runs_ref, tile, ybuf.at[slot], ys_ref, sems.at[slot], False, action)

    @pl.when(i == 0)
    def _():
        ybuf[...] = jnp.zeros(ybuf.shape, ybuf.dtype)
        gather(0, 0, "start")

    @pl.when(i + 1 < n)
    def _():
        gather(i + 1, (i + 1) % 2, "start")

    slot = i % 2
    gather(i, slot, "wait")

    idx = idx_ref[...]
    gate = gate_ref[...]
    chunk = EXPERT_TILE
    col_id = lax.broadcasted_iota(jnp.int32, (ROW_TILE, LANES), 1)
    pos_b = [jnp.broadcast_to(idx[:, TOP_K + kk:TOP_K + kk + 1], (ROW_TILE, LANES)) for kk in range(TOP_K)]
    gate_b = [jnp.broadcast_to(gate[:, kk:kk + 1], (ROW_TILE, LANES)) for kk in range(TOP_K)]
    f_lo = jnp.zeros((ROW_TILE, PACKED_W), F32)
    f_hi = jnp.zeros((ROW_TILE, PACKED_W), F32)
    def operands(c):
        parts = []
        for off in range(c * chunk, (c + 1) * chunk, LANES):
            part = jnp.zeros((ROW_TILE, LANES), F32)
            for kk in range(TOP_K):
                part = jnp.where(col_id + off == pos_b[kk], gate_b[kk], part)
            parts.append(part)
        lo, hi = _unpack_rows(ybuf[slot, c * chunk:(c + 1) * chunk, :])
        return jnp.concatenate(parts, axis=-1).astype(BF16), lo.astype(BF16), hi.astype(BF16)

    for c in range(STAGE_ROWS // chunk):
        wb, lo, hi = operands(c)
        f_lo = f_lo + jnp.dot(wb, lo, preferred_element_type=F32)
        f_hi = f_hi + jnp.dot(wb, hi, preferred_element_type=F32)
    y_lo = DN_ALPHA * h_ref[:, :PACKED_W] + f_lo
    y_hi = DN_ALPHA * h_ref[:, PACKED_W:] + f_hi
    mu = (jnp.sum(y_lo, axis=-1, keepdims=True) + jnp.sum(y_hi, axis=-1, keepdims=True)) / D_MODEL
    d_lo = y_lo - mu
    d_hi = y_hi - mu
    var = (jnp.sum(d_lo * d_lo, axis=-1, keepdims=True) + jnp.sum(d_hi * d_hi, axis=-1, keepdims=True)) / D_MODEL
    inv = lax.rsqrt(var + LN_EPS)
    o_ref[:, :PACKED_W] = d_lo * inv * g_ref[:, :PACKED_W] + b_ref[:, :PACKED_W]
    o_ref[:, PACKED_W:] = d_hi * inv * g_ref[:, PACKED_W:] + b_ref[:, PACKED_W:]


def _combine(runs, h, idx, gate, ys, g, b):
    T = h.shape[0]
    grid_spec = pltpu.PrefetchScalarGridSpec(
        num_scalar_prefetch=1,
        grid=(T // ROW_TILE,),
        in_specs=[pl.BlockSpec((ROW_TILE, D_MODEL), lambda i, r: (i, 0)),
                  pl.BlockSpec((ROW_TILE, LANES), lambda i, r: (i, 0)),
                  pl.BlockSpec((ROW_TILE, LANES), lambda i, r: (i, 0)),
                  pl.BlockSpec(g.shape, lambda i, r: (0, 0)),
                  pl.BlockSpec(b.shape, lambda i, r: (0, 0)),
                  pl.BlockSpec(memory_space=pl.ANY)],
        out_specs=pl.BlockSpec((ROW_TILE, D_MODEL), lambda i, r: (i, 0)),
        scratch_shapes=[pltpu.VMEM((2, STAGE_ROWS, PACKED_W), jnp.uint32),
                        pltpu.SemaphoreType.DMA((2,))],
    )
    return pl.pallas_call(
        _combine_kernel,
        out_shape=jax.ShapeDtypeStruct((T, D_MODEL), F32),
        grid_spec=grid_spec,
        compiler_params=_cparams("arbitrary"),
        name="combine_ln",
    )(runs, h, idx, gate, g, b, ys)


def _pack_w_in(w_in):
    wt = w_in.T
    lat = Q_LORA + KV_LORA + MLA_ROPE
    half = MLA_ROPE // 2
    wt_lat = jnp.concatenate([wt[:lat], -wt[lat - half:lat], wt[lat - MLA_ROPE:lat - half]], axis=0)
    return wt[lat:].astype(BF16), wt_lat.astype(BF16)


def _pack_w_uq(w_uq):
    w = w_uq.reshape(Q_LORA, MLA_HEADS, MLA_NOPE + MLA_ROPE)
    rope = w[:, :, MLA_NOPE:]
    rot = jnp.concatenate([-rope[:, :, MLA_ROPE // 2:], rope[:, :, :MLA_ROPE // 2]], axis=-1)
    return jnp.concatenate([w, rot], axis=-1).reshape(Q_LORA, MLA_HEADS * MLA_QK).astype(BF16)


def _rope_freqs():
    ret = 1.0 / (10000.0 ** jnp.linspace(0.0, 1.0, RET_DK // 2, dtype=F32))
    mla = 1.0 / (ROPE_BASE ** (jnp.arange(0, MLA_ROPE, 2, dtype=F32) / MLA_ROPE))
    return jnp.concatenate([ret, mla, jnp.zeros((LANES - ret.shape[0] - mla.shape[0],), F32)])[None, :]


def _routing_plan(cnt, n_tokens):
    counts = cnt[::SUBLANES, :N_EXPERTS]
    n_tiles = counts.shape[0]
    run_len = ((counts + SUBLANES - 1) // SUBLANES) * SUBLANES
    stage_off = jnp.cumsum(run_len, axis=1) - run_len
    tile_off = jnp.cumsum(run_len, axis=0) - run_len
    rows_used = jnp.sum(run_len, axis=0)
    padded = ((rows_used + EXPERT_STEP - 1) // EXPERT_STEP) * EXPERT_STEP
    pad_end = jnp.cumsum(padded)
    pad_start = pad_end - padded
    runs = jnp.stack([stage_off, pad_start[None, :] + tile_off, run_len], axis=-1).reshape(-1).astype(jnp.int32)
    nact = (pad_end[-1:] // EXPERT_STEP).astype(jnp.int32)
    gaps = jnp.concatenate([jnp.stack([pad_start + rows_used, padded - rows_used], axis=-1).reshape(-1),
                            pad_end[-1:] // EXPERT_TILE]).astype(jnp.int32)
    n_rows = n_tokens * TOP_K + (SUBLANES - 1) * n_tiles * N_EXPERTS + N_EXPERTS * EXPERT_STEP
    n_rows = -(-n_rows // EXPERT_STEP) * EXPERT_STEP
    n_blocks = n_rows // EXPERT_STEP
    experts = jnp.arange(N_EXPERTS, dtype=jnp.int32)
    blk_row = jnp.arange(n_blocks, dtype=jnp.int32) * EXPERT_STEP
    blk_e = jnp.minimum(jnp.sum((pad_end[None, :] <= blk_row[:, None]).astype(jnp.int32), axis=-1),
                        N_EXPERTS - 1)
    used_end = jnp.sum(jnp.where(blk_e[:, None] == experts[None, :], (pad_start + rows_used)[None, :], 0), axis=-1)
    nsub = jnp.clip((used_end - blk_row + EXPERT_TILE - 1) // EXPERT_TILE, 0,
                    EXPERT_STEP // EXPERT_TILE).astype(jnp.int32)
    later = jnp.where((rows_used > 0)[None, :] & (experts[None, :] > experts[:, None]), experts[None, :], N_EXPERTS)
    nxt = jnp.min(later, axis=-1)
    nxt = jnp.where(nxt == N_EXPERTS, -1, nxt)
    next_e = jnp.sum(jnp.where(blk_e[:, None] == experts[None, :], nxt[None, :], 0), axis=-1).astype(jnp.int32)
    return runs, gaps, blk_e, nact, next_e, nsub, n_rows


def kernel(x, positions, w_in, q_norm_g, w_uq, kv_norm_g, w_ukv, w_o, ln1_g, ln1_b,
           w_router, b_router, w_up, b_up, w_down, b_down, ln2_g, ln2_b):
    B, S, _ = x.shape
    T = B * S
    assert S % ATTN_Q_TILE == 0 and T % ROW_TILE == 0, "sequence / token count must be whole tiles"
    rc, rs, mc, ms = _rope_tables(positions.reshape(T // LANES, LANES).astype(jnp.int32), _rope_freqs())
    dec, qw, kw, cd = _retention_constants()
    h = x.reshape(T, D_MODEL)
    for l in range(DEPTH):
        proj = _input_projection(h, *_pack_w_in(w_in[l]))
        w_kv = w_ukv[l].reshape(KV_LORA, MLA_HEADS, MLA_NOPE + MLA_V)
        w_k = w_kv[:, :, :MLA_NOPE].reshape(KV_LORA, MLA_HEADS * MLA_NOPE).astype(BF16)
        w_vt = w_kv[:, :, MLA_NOPE:].reshape(KV_LORA, MLA_HEADS * MLA_V).T.astype(BF16)
        q_cat, kn, vt, kr = _mla_prep(proj, mc, ms, _pack_w_uq(w_uq[l]), w_k, w_vt,
                                      q_norm_g[l][None, :], kv_norm_g[l][None, :])
        o_mla = _attention(q_cat, kn, kr, vt, B, S)
        o_ret = _retention(proj, rc, rs, dec, qw, kw, cd, B, S)
        h1, h1b, idx_out, gate, cnt = _merge_and_route(
            h, o_mla, o_ret, proj, w_o[l].astype(BF16), ln1_g[l][None, :], ln1_b[l][None, :],
            w_router[l], b_router[l])
        runs, gaps, blk_e, nact, next_e, nsub, n_rows = _routing_plan(cnt, T)
        xs = _dispatch(runs, gaps, h1b, idx_out, n_rows)
        ys = _experts(blk_e, nact, next_e, nsub, xs, w_up[l], b_up[l], w_down[l], b_down[l])
        h = _combine(runs, h1, idx_out, gate, ys, ln2_g[l][None, :], ln2_b[l][None, :])
    return h.reshape(B, S, D_MODEL)
```

```python
import numpy as np
import jax
import jax.numpy as jnp
from jax import lax
from jax.experimental import pallas as pl
from jax.experimental.pallas import tpu as pltpu

F32 = jnp.float32
BF16 = jnp.bfloat16

D_MODEL = 1024
DEPTH = 1
MLA_HEADS = 8
MLA_NOPE = 128
MLA_ROPE = 64
MLA_V = 128
Q_LORA = 384
KV_LORA = 256
ROPE_BASE = 10000.0
RET_HEADS = 8
RET_DK = 128
RET_DV = 128
RET_CHUNK = 128
N_EXPERTS = 32
TOP_K = 4
D_EXPERT = 1024
SWIGLU_LIMIT = 7.0
SWIGLU_ALPHA = 1.702
DN_ALPHA = (2.0 * DEPTH) ** 0.25
LN_EPS = 1e-5
RMS_EPS = 1e-6
GN_EPS = 1e-6
NEG_INF = -1e30

LANES = 128
SUBLANES = 8
MLA_QK = 2 * LANES
PACKED_W = D_MODEL // 2
LATENT_W = 768
N_WIDE = 6 * D_MODEL
N_PROJ = N_WIDE + LATENT_W
VMEM_LIMIT = 56 * 1024 * 1024

ROW_TILE = 512
ATTN_TILE = ROW_TILE
ATTN_Q_TILE = 2 * ATTN_TILE
ATTN_Q_SPLIT = 4
BF16_SUBLANES = 2 * SUBLANES
VT_ROWS = MLA_V + BF16_SUBLANES
LOG2_E = 1.4426950408889634
RET_GROUP = 16
EXPERT_TILE = 256
EXPERT_STEP = 4 * EXPERT_TILE
STAGE_ROWS = -(-(ROW_TILE * TOP_K + (SUBLANES - 1) * N_EXPERTS) // EXPERT_TILE) * EXPERT_TILE


def _cparams(*sem):
    return pltpu.CompilerParams(dimension_semantics=sem, vmem_limit_bytes=VMEM_LIMIT)


def _const_spec(shape):
    nd = len(shape)
    return pl.BlockSpec(shape, lambda *_: (0,) * nd, pipeline_mode=pl.Buffered(1))


def _sigmoid(x):
    return 0.5 * jnp.tanh(0.5 * x) + 0.5


def _pack_rows(x):
    n = x.shape[-1] // 2
    xb = x.astype(BF16).astype(F32)
    lo = pltpu.bitcast(xb[:, :n], jnp.uint32) >> 16
    hi = pltpu.bitcast(xb[:, n:], jnp.uint32) & jnp.uint32(0xFFFF0000)
    return lo | hi


def _unpack_rows(w):
    lo = pltpu.bitcast(w << 16, F32)
    hi = pltpu.bitcast(w & jnp.uint32(0xFFFF0000), F32)
    return lo, hi


def _swap_halves(x):
    return pltpu.roll(x, LANES // 2, 1)


def _proj_kernel(x_ref, wt_wide_ref, wt_lat_ref, o_ref):
    xb = x_ref[...].astype(BF16)
    nt = (((1,), (1,)), ((), ()))
    for c in range(N_WIDE // LATENT_W):
        sl = slice(c * LATENT_W, (c + 1) * LATENT_W)
        o_ref[:, sl] = lax.dot_general(xb, wt_wide_ref[sl, :], nt, preferred_element_type=F32).astype(o_ref.dtype)
    o_ref[:, N_WIDE:] = lax.dot_general(xb, wt_lat_ref[...], nt, preferred_element_type=F32).astype(o_ref.dtype)


def _input_projection(x2, wt_wide, wt_lat):
    T = x2.shape[0]
    return pl.pallas_call(
        _proj_kernel,
        out_shape=jax.ShapeDtypeStruct((T, N_PROJ), BF16),
        grid=(T // ROW_TILE,),
        in_specs=[pl.BlockSpec((ROW_TILE, D_MODEL), lambda i: (i, 0)),
                  _const_spec((N_WIDE, D_MODEL)), _const_spec((LATENT_W, D_MODEL))],
        out_specs=pl.BlockSpec((ROW_TILE, N_PROJ), lambda i: (i, 0)),
        compiler_params=_cparams("parallel"),
        name="input_projection",
    )(x2, wt_wide, wt_lat)


def _rope_table_kernel(pos_ref, freq_ref, rc_ref, rs_ref, mc_ref, ms_ref):
    pos = pos_ref[...].astype(F32)
    pos_t = jnp.concatenate([pos, jnp.zeros((LANES - pos.shape[0], LANES), F32)], axis=0).T
    freq = freq_ref[...]
    ang = jnp.concatenate([pos_t[:, j:j + 1] * freq for j in range(pos.shape[0])], axis=0)
    c = jnp.cos(ang)
    s = jnp.sin(ang)
    lane = lax.broadcasted_iota(jnp.int32, ang.shape, 1)
    c64 = pltpu.roll(c, 64, 1)
    s64 = pltpu.roll(s, 64, 1)
    rc_ref[...] = jnp.where(lane < 64, c, c64)
    rs_ref[...] = jnp.where(lane < 64, -s, s64)
    c96 = pltpu.roll(c, 96, 1)
    s96 = pltpu.roll(s, 96, 1)
    mc_ref[...] = jnp.where(lane < 32, c64, jnp.where(lane < 64, c96, 0.0))
    ms_ref[...] = jnp.where(lane < 32, s64, jnp.where(lane < 64, s96, 0.0))


def _rope_tables(pos2d, freq_row):
    T = pos2d.shape[0] * LANES
    rows = SUBLANES * LANES
    tab = jax.ShapeDtypeStruct((T, LANES), F32)
    spec = pl.BlockSpec((rows, LANES), lambda i: (i, 0))
    return pl.pallas_call(
        _rope_table_kernel,
        out_shape=(tab, tab, tab, tab),
        grid=(T // rows,),
        in_specs=[pl.BlockSpec((SUBLANES, LANES), lambda i: (i, 0)), _const_spec((1, LANES))],
        out_specs=(spec, spec, spec, spec),
        compiler_params=_cparams("parallel"),
        name="rope_tables",
    )(pos2d, freq_row)


def _mla_prep_kernel(a_ref, mc_ref, ms_ref, wq_ref, wk_ref, wvt_ref, qg_ref, kvg_ref,
                     q_ref, kn_ref, vt_ref, kr_ref):
    a = a_ref[...].astype(F32)
    mc = mc_ref[...]
    ms = ms_ref[...]
    cq = a[:, :Q_LORA]
    qn = cq * lax.rsqrt(jnp.mean(cq * cq, axis=-1, keepdims=True) + RMS_EPS) * qg_ref[...]
    q = jnp.dot(qn.astype(BF16), wq_ref[...], preferred_element_type=F32)
    scale = (MLA_NOPE + MLA_ROPE) ** -0.5 * LOG2_E
    for h in range(MLA_HEADS):
        nope = q[:, h * MLA_QK:h * MLA_QK + LANES]
        blk = q[:, h * MLA_QK + LANES:(h + 1) * MLA_QK]
        rope = blk * mc + _swap_halves(blk) * ms
        q_ref[:, h * MLA_QK:h * MLA_QK + LANES] = (nope * scale).astype(BF16)
        q_ref[:, h * MLA_QK + LANES:(h + 1) * MLA_QK] = (rope * scale).astype(BF16)
    ckv = a[:, Q_LORA:Q_LORA + KV_LORA]
    kvn = ckv * lax.rsqrt(jnp.mean(ckv * ckv, axis=-1, keepdims=True) + RMS_EPS) * kvg_ref[...]
    kvb = kvn.astype(BF16)
    kn_ref[...] = jnp.dot(kvb, wk_ref[...], preferred_element_type=F32).astype(BF16)
    vt = lax.dot_general(wvt_ref[...], kvb, (((1,), (1,)), ((), ())),
                         preferred_element_type=F32).astype(BF16)
    ones = jnp.ones((VT_ROWS - MLA_V, vt.shape[1]), BF16)
    for h in range(MLA_HEADS):
        vt_ref[0, h * VT_ROWS:h * VT_ROWS + MLA_V, :] = vt[h * MLA_V:(h + 1) * MLA_V, :]
        vt_ref[0, h * VT_ROWS + MLA_V:(h + 1) * VT_ROWS, :] = ones
    krb = a[:, Q_LORA + KV_LORA:]
    kr_ref[...] = (krb * mc + _swap_halves(krb) * ms).astype(BF16)


def _mla_prep(proj, mc, ms, wq, wk, wvt, qg, kvg):
    T = proj.shape[0]
    row = lambda w: pl.BlockSpec((ROW_TILE, w), lambda i: (i, 0))
    return pl.pallas_call(
        _mla_prep_kernel,
        out_shape=(jax.ShapeDtypeStruct((T, MLA_HEADS * MLA_QK), BF16),
                   jax.ShapeDtypeStruct((T, MLA_HEADS * MLA_NOPE), BF16),
                   jax.ShapeDtypeStruct((T // ROW_TILE, MLA_HEADS * VT_ROWS, ROW_TILE), BF16),
                   jax.ShapeDtypeStruct((T, LANES), BF16)),
        grid=(T // ROW_TILE,),
        in_specs=[pl.BlockSpec((ROW_TILE, LATENT_W), lambda i: (i, N_WIDE // LATENT_W)),
                  row(LANES), row(LANES),
                  _const_spec(wq.shape), _const_spec(wk.shape), _const_spec(wvt.shape),
                  _const_spec(qg.shape), _const_spec(kvg.shape)],
        out_specs=(row(MLA_HEADS * MLA_QK), row(MLA_HEADS * MLA_NOPE),
                   pl.BlockSpec((1, MLA_HEADS * VT_ROWS, ROW_TILE), lambda i: (i, 0, 0)), row(LANES)),
        compiler_params=_cparams("parallel"),
        name="mla_prep",
    )(proj, mc, ms, wq, wk, wvt, qg, kvg)


def _attn_kernel(q_ref, kn_ref, kr_ref, vt_ref, o_ref, acc_scr):
    W = ATTN_Q_TILE // ATTN_Q_SPLIT
    n_diag = ATTN_Q_TILE // ATTN_TILE
    groups = tuple(range(ATTN_Q_SPLIT))

    def visible(d, c):
        return min(max((c + 1) * W - d * ATTN_TILE, 0), ATTN_TILE)

    for qi in range(q_ref.shape[0] // ATTN_Q_TILE):
        q0 = qi * ATTN_Q_TILE
        qs = [q_ref[q0 + c * W:q0 + (c + 1) * W, :] for c in groups]
        acc_scr[...] = jnp.zeros(acc_scr.shape, F32)

        def scores(j, diag=None):
            rows = slice(j * ATTN_TILE, (j + 1) * ATTN_TILE)
            k = jnp.concatenate([kn_ref[rows, :], kr_ref[rows, :]], axis=-1)
            keys = {c: ATTN_TILE if diag is None else visible(diag, c) for c in groups}
            return {c: lax.dot_general(k[:nk], qs[c], (((1,), (1,)), ((), ())), preferred_element_type=F32)
                    for c, nk in keys.items() if nk > 0}

        def softmax_pv(j, s_all, m_all, diag=None):
            vt = vt_ref[j]
            out = list(m_all)
            for c in s_all:
                s = s_all[c]
                nk = s.shape[0]
                if diag is not None and diag * ATTN_TILE + nk - 1 > c * W:
                    kp = lax.broadcasted_iota(jnp.int32, s.shape, 0) + diag * ATTN_TILE
                    qp = lax.broadcasted_iota(jnp.int32, s.shape, 1) + c * W
                    s = jnp.where(kp <= qp, s, NEG_INF)
                m_new = jnp.maximum(m_all[c], jnp.max(s, axis=0, keepdims=True))
                alpha = jnp.exp2(m_all[c] - m_new)
                p = jnp.exp2(s - m_new).astype(BF16)
                acc_scr[c] = alpha * acc_scr[c] + jnp.dot(vt[:, :nk], p, preferred_element_type=F32)
                out[c] = m_new
            return tuple(out)

        first = n_diag * qi
        work = [(j, None) for j in range(first)] + [(first + d, d) for d in range(n_diag)]
        m_all = tuple(jnp.full((1, W), NEG_INF, F32) for _ in groups)
        pending = [scores(*item) for item in work[:2]]
        for n, (j, diag) in enumerate(work):
            m_all = softmax_pv(j, pending.pop(0), m_all, diag)
            if n + 2 < len(work):
                pending.append(scores(*work[n + 2]))
        for c in groups:
            acc = acc_scr[c]
            o_ref[q0 + c * W:q0 + (c + 1) * W, :] = (acc[:MLA_V] / acc[MLA_V:MLA_V + 1]).T.astype(o_ref.dtype)


def _attention(q_cat, kn, kr, vt, B, S):
    T = B * S
    return pl.pallas_call(
        _attn_kernel,
        out_shape=jax.ShapeDtypeStruct((T, MLA_HEADS * MLA_V), BF16),
        grid=(B, MLA_HEADS),
        in_specs=[pl.BlockSpec((S, MLA_QK), lambda b, h: (b, h)),
                  pl.BlockSpec((S, MLA_NOPE), lambda b, h: (b, h)),
                  pl.BlockSpec((S, LANES), lambda b, h: (b, 0)),
                  pl.BlockSpec((S // ATTN_TILE, VT_ROWS, ATTN_TILE), lambda b, h: (b, h, 0))],
        out_specs=pl.BlockSpec((S, MLA_V), lambda b, h: (b, h)),
        scratch_shapes=[pltpu.VMEM((ATTN_Q_SPLIT, VT_ROWS, ATTN_Q_TILE // ATTN_Q_SPLIT), F32)],
        compiler_params=_cparams("parallel", "parallel"),
        name="mla_attention",
    )(q_cat, kn, kr, vt)


def _ret_kernel(q_ref, k_ref, v_ref, g_ref, rc_ref, rs_ref, dec_ref, qw_ref, kw_ref, cd_ref,
                o_ref, qb_scr, qwb_scr, kb_scr, kwb_scr):
    S = q_ref.shape[0]
    C = RET_CHUNK
    n_chunks = S // C
    rc = rc_ref[...]
    rs = rs_ref[...]
    q = q_ref[...].astype(F32)
    k = k_ref[...].astype(F32)
    q = q * rc + _swap_halves(q) * rs
    k = (k * rc + _swap_halves(k) * rs) * (RET_DK ** -0.5)
    qw = qw_ref[0]
    kw = kw_ref[0]
    q3 = q.reshape(n_chunks, C, RET_DK)
    k3 = k.reshape(n_chunks, C, RET_DK)
    qb_scr[...] = q.astype(BF16)
    kb_scr[...] = k.astype(BF16)
    qwb_scr[...] = (q3 * qw[None]).reshape(S, RET_DK).astype(BF16)
    kwb_scr[...] = (k3 * kw[None]).reshape(S, RET_DK)
    dec = dec_ref[0]
    cd = cd_ref[0]

    def group(gi, R):
        rows_of = [pl.ds(pl.multiple_of((gi * RET_GROUP + u) * C, C), C) for u in range(RET_GROUP)]
        scores, updates = [], []
        for rows in rows_of:
            scores.append(lax.dot_general(qb_scr[rows, :], kb_scr[rows, :], (((1,), (1,)), ((), ())),
                                          preferred_element_type=F32))
            kwt = kwb_scr[rows, :].T.astype(BF16)
            updates.append(jnp.dot(kwt, v_ref[rows, :], preferred_element_type=F32))
        states = []
        for u in range(RET_GROUP):
            states.append(R.astype(BF16))
            R = cd * R + updates[u]
        for u, rows in enumerate(rows_of):
            o = jnp.dot((scores[u] * dec).astype(BF16), v_ref[rows, :], preferred_element_type=F32)
            o = o + jnp.dot(qwb_scr[rows, :], states[u], preferred_element_type=F32)
            mu = jnp.mean(o, axis=-1, keepdims=True)
            d = o - mu
            var = jnp.mean(d * d, axis=-1, keepdims=True)
            gate = g_ref[rows, :].astype(F32)
            o_ref[rows, :] = (d * lax.rsqrt(var + GN_EPS) * (gate * _sigmoid(gate))).astype(o_ref.dtype)
        return R

    lax.fori_loop(0, n_chunks // RET_GROUP, group, jnp.zeros((RET_DK, RET_DV), F32))


def _retention(proj, rc, rs, dec, qw, kw, cd, B, S):
    assert S % (RET_CHUNK * RET_GROUP) == 0, "sequence must hold whole groups of retention chunks"
    T = B * S
    H = RET_HEADS
    col = lambda off: pl.BlockSpec((S, LANES), lambda b, h: (b, off + h))
    tab = pl.BlockSpec((S, LANES), lambda b, h: (b, 0))
    per_head = lambda r: pl.BlockSpec((1, r, LANES), lambda b, h: (h, 0, 0))
    return pl.pallas_call(
        _ret_kernel,
        out_shape=jax.ShapeDtypeStruct((T, H * RET_DV), BF16),
        grid=(B, H),
        in_specs=[col(0), col(H), col(2 * H), col(3 * H), tab, tab,
                  per_head(RET_CHUNK), per_head(RET_CHUNK), per_head(RET_CHUNK), per_head(1)],
        out_specs=pl.BlockSpec((S, RET_DV), lambda b, h: (b, h)),
        scratch_shapes=[pltpu.VMEM((S, RET_DK), BF16), pltpu.VMEM((S, RET_DK), BF16),
                        pltpu.VMEM((S, RET_DK), BF16), pltpu.VMEM((S, RET_DK), F32)],
        compiler_params=_cparams("parallel", "parallel"),
        name="retention",
    )(proj, proj, proj, proj, rc, rs, dec, qw, kw, cd)


def _retention_constants():
    C = RET_CHUNK
    h = np.arange(RET_HEADS, dtype=np.float64)
    log_gamma = np.log(1.0 - 2.0 ** (-5.0 - h))
    idx = np.arange(C, dtype=np.float64)
    rel = idx[:, None] - idx[None, :]
    dec = np.where(rel[None] >= 0, np.exp(np.maximum(rel, 0.0)[None] * log_gamma[:, None, None]), 0.0)
    qw = np.exp((idx + 1.0)[None, :] * log_gamma[:, None])
    kw = np.exp((C - 1.0 - idx)[None, :] * log_gamma[:, None])
    cd = np.exp(C * log_gamma)
    bc = lambda a: np.ascontiguousarray(np.broadcast_to(a[..., None], a.shape + (LANES,))).astype(np.float32)
    return dec.astype(np.float32), bc(qw), bc(kw), bc(cd[:, None])


def _layernorm(y, g, b):
    mu = jnp.mean(y, axis=-1, keepdims=True)
    d = y - mu
    var = jnp.mean(d * d, axis=-1, keepdims=True)
    return d * lax.rsqrt(var + LN_EPS) * g + b


def _merge_kernel(x_ref, om_ref, or_ref, gm_ref, gr_ref, wo_ref, g_ref, b_ref, w01_ref, rb_ref,
                  h_ref, hb_ref, idx_ref, gate_ref, cnt_ref):
    mixed = (_sigmoid(gm_ref[...].astype(F32)) * om_ref[...].astype(F32)
             + _sigmoid(gr_ref[...].astype(F32)) * or_ref[...].astype(F32))
    mix = jnp.dot(mixed.astype(BF16), wo_ref[...], preferred_element_type=F32)
    h = _layernorm(DN_ALPHA * x_ref[...] + mix, g_ref[...], b_ref[...])
    h_ref[...] = h
    hb_ref[...] = h.astype(BF16)
    _route_tile(h, w01_ref, rb_ref, idx_ref, gate_ref, cnt_ref)


def _merge_and_route(x2, o_mla, o_ret, proj, wo, g, b, w_router, b_router):
    T = x2.shape[0]
    wpad = jnp.zeros((D_MODEL, LANES), F32).at[:, :N_EXPERTS].set(w_router)
    w0 = wpad.astype(BF16)
    w01 = jnp.concatenate([w0, (wpad - w0.astype(F32)).astype(BF16)], axis=-1)
    bpad = jnp.zeros((1, LANES), F32).at[0, :N_EXPERTS].set(b_router)
    row = lambda c: pl.BlockSpec((ROW_TILE, D_MODEL), lambda i: (i, c))
    lanes = pl.BlockSpec((ROW_TILE, LANES), lambda i: (i, 0))
    return pl.pallas_call(
        _merge_kernel,
        out_shape=(jax.ShapeDtypeStruct((T, D_MODEL), F32), jax.ShapeDtypeStruct((T, D_MODEL), BF16),
                   jax.ShapeDtypeStruct((T, LANES), jnp.int32), jax.ShapeDtypeStruct((T, LANES), F32),
                   jax.ShapeDtypeStruct((T // ROW_TILE * SUBLANES, LANES), jnp.int32)),
        grid=(T // ROW_TILE,),
        in_specs=[row(0), row(0), row(0), row(4), row(5),
                  _const_spec(wo.shape), _const_spec(g.shape), _const_spec(b.shape),
                  _const_spec(w01.shape), _const_spec(bpad.shape)],
        out_specs=(row(0), row(0), lanes, lanes, pl.BlockSpec((SUBLANES, LANES), lambda i: (i, 0))),
        compiler_params=_cparams("parallel"),
        name="merge_out_proj_ln_route",
    )(x2, o_mla, o_ret, proj, proj, wo, g, b, w01, bpad)


def _split2(a):
    a0 = a.astype(BF16)
    return a0, (a - a0.astype(F32)).astype(BF16)


def _route_tile(h, w01_ref, b_ref, idx_ref, gate_ref, cnt_ref):
    h0, h1 = _split2(h)
    p0 = jnp.dot(h0, w01_ref[...], preferred_element_type=F32)
    p1 = jnp.dot(h1, w01_ref[:, :LANES], preferred_element_type=F32)
    logits = p0[:, :LANES] + (p0[:, LANES:] + p1) + b_ref[...]
    rows = logits.shape[0]
    lane = lax.broadcasted_iota(jnp.int32, (rows, LANES), 1)
    lane_f = lane.astype(F32)
    work = jnp.where(lane < N_EXPERTS, logits, -jnp.inf)
    vals, ids, sels = [], [], []
    for _ in range(TOP_K):
        m = jnp.max(work, axis=-1, keepdims=True)
        first = jnp.min(jnp.where(work == m, lane_f, float(LANES)), axis=-1, keepdims=True)
        sel = lane_f == first
        work = jnp.where(sel, -jnp.inf, work)
        vals.append(m)
        ids.append(first)
        sels.append(sel)
    exps = [jnp.exp(v - vals[0]) for v in vals]
    denom = exps[0] + exps[1] + exps[2] + exps[3]
    chosen = jnp.zeros((rows, LANES), F32)
    for sel in sels:
        chosen = jnp.where(sel, 1.0, chosen)
    r = lax.broadcasted_iota(jnp.int32, (rows, rows), 0)
    c = lax.broadcasted_iota(jnp.int32, (rows, rows), 1)
    lower = jnp.where(r > c, 1.0, 0.0).astype(BF16)
    before = jnp.dot(lower, chosen.astype(BF16), preferred_element_type=F32)
    counts = jnp.sum(chosen, axis=0, keepdims=True)
    run_len = jnp.floor((counts + (SUBLANES - 1)) * (1.0 / SUBLANES)) * SUBLANES
    er = lax.broadcasted_iota(jnp.int32, (LANES, LANES), 0)
    ec = lax.broadcasted_iota(jnp.int32, (LANES, LANES), 1)
    earlier = jnp.where(er < ec, 1.0, 0.0).astype(BF16)
    run_start = jnp.dot(jnp.broadcast_to(run_len, (SUBLANES, LANES)).astype(BF16), earlier,
                        preferred_element_type=F32)[:1]
    slot = before + run_start
    idx_out = jnp.zeros((rows, LANES), jnp.int32)
    gate_out = jnp.zeros((rows, LANES), F32)
    for kk in range(TOP_K):
        pos = jnp.sum(jnp.where(sels[kk], slot, 0.0), axis=-1, keepdims=True).astype(jnp.int32)
        idx_out = jnp.where(lane == kk, ids[kk].astype(jnp.int32), idx_out)
        idx_out = jnp.where(lane == TOP_K + kk, pos, idx_out)
        gate_out = jnp.where(lane == kk, exps[kk] / denom, gate_out)
    idx_ref[...] = idx_out
    gate_ref[...] = gate_out
    cnt_ref[...] = jnp.broadcast_to(counts, cnt_ref.shape).astype(jnp.int32)


def _run_copy(runs_ref, tile, e, stage, hbm, sem, to_hbm):
    base = (tile * N_EXPERTS + e) * 3
    s = pl.multiple_of(runs_ref[base], SUBLANES)
    g = pl.multiple_of(runs_ref[base + 1], SUBLANES)
    n = pl.multiple_of(runs_ref[base + 2], SUBLANES)
    src, dst = stage.at[pl.ds(s, n), :], hbm.at[pl.ds(g, n), :]
    if not to_hbm:
        src, dst = dst, src
    return n, pltpu.make_async_copy(src, dst, sem)


def _for_each_run(runs_ref, tile, stage, hbm, sem, to_hbm, action):
    def one(e, carry):
        n, cp = _run_copy(runs_ref, tile, e, stage, hbm, sem, to_hbm)

        @pl.when(n > 0)
        def _():
            getattr(cp, action)()

        return carry

    lax.fori_loop(0, N_EXPERTS, one, 0)


def _dispatch_kernel(runs_ref, gaps_ref, h_ref, idx_ref, xs_ref, stage, zeros, sems):
    i = pl.program_id(0)

    def gap_copy(e):
        g = pl.multiple_of(gaps_ref[2 * e], SUBLANES)
        n = pl.multiple_of(gaps_ref[2 * e + 1], SUBLANES)
        return n, pltpu.make_async_copy(zeros.at[pl.ds(0, n), :], xs_ref.at[pl.ds(g, n), :], sems.at[2])

    def for_each_gap(action):
        def one(e, carry):
            n, cp = gap_copy(e)

            @pl.when(n > 0)
            def _():
                getattr(cp, action)()

            return carry

        lax.fori_loop(0, N_EXPERTS, one, 0)

        def tail(b, carry):
            rows = pl.ds(pl.multiple_of(b * EXPERT_TILE, EXPERT_TILE), EXPERT_TILE)
            getattr(pltpu.make_async_copy(zeros.at[pl.ds(0, EXPERT_TILE), :], xs_ref.at[rows, :], sems.at[2]),
                    action)()
            return carry

        lax.fori_loop(gaps_ref[2 * N_EXPERTS], xs_ref.shape[0] // EXPERT_TILE, tail, 0)

    @pl.when(i == 0)
    def _():
        zeros[...] = jnp.zeros(zeros.shape, zeros.dtype)
        for_each_gap("start")

    pos_t = idx_ref[...].astype(F32).T.astype(jnp.int32)
    h = h_ref[...]
    chunk = EXPERT_TILE
    row_id = lax.broadcasted_iota(jnp.int32, (chunk, ROW_TILE), 0)
    slot = i % 2
    for c in range(STAGE_ROWS // chunk):
        onehot = jnp.zeros((chunk, ROW_TILE), F32)
        for kk in range(TOP_K):
            onehot = jnp.where(row_id == (pos_t[TOP_K + kk:TOP_K + kk + 1] - c * chunk), 1.0, onehot)
        stage[slot, c * chunk:(c + 1) * chunk, :] = _pack_rows(
            jnp.dot(onehot.astype(BF16), h, preferred_element_type=F32))

    _for_each_run(runs_ref, i, stage.at[slot], xs_ref, sems.at[slot], True, "start")

    @pl.when(i > 0)
    def _():
        _for_each_run(runs_ref, i - 1, stage.at[1 - slot], xs_ref, sems.at[1 - slot], True, "wait")

    @pl.when(i == pl.num_programs(0) - 1)
    def _():
        _for_each_run(runs_ref, i, stage.at[slot], xs_ref, sems.at[slot], True, "wait")
        for_each_gap("wait")


def _dispatch(runs, gaps, hb, idx, n_rows):
    T = hb.shape[0]
    grid_spec = pltpu.PrefetchScalarGridSpec(
        num_scalar_prefetch=2,
        grid=(T // ROW_TILE,),
        in_specs=[pl.BlockSpec((ROW_TILE, D_MODEL), lambda i, r, g: (i, 0)),
                  pl.BlockSpec((ROW_TILE, LANES), lambda i, r, g: (i, 0))],
        out_specs=pl.BlockSpec(memory_space=pl.ANY),
        scratch_shapes=[pltpu.VMEM((2, STAGE_ROWS, PACKED_W), jnp.uint32),
                        pltpu.VMEM((EXPERT_STEP, PACKED_W), jnp.uint32),
                        pltpu.SemaphoreType.DMA((3,))],
    )
    return pl.pallas_call(
        _dispatch_kernel,
        out_shape=jax.ShapeDtypeStruct((n_rows, PACKED_W), jnp.uint32),
        grid_spec=grid_spec,
        compiler_params=_cparams("arbitrary"),
        name="dispatch_rows",
    )(runs, gaps, hb, idx)


def _expert_kernel(blk_e_ref, nact_ref, next_e_ref, nsub_ref, x_ref, bu_ref, bd_ref, wu_hbm, wd_hbm, y_ref,
                   wuf_scr, wdf_scr, wub_scr, wdb_scr, sems):
    i = pl.program_id(0)
    e = blk_e_ref[i]

    def weight_copies(expert):
        return (pltpu.make_async_copy(wu_hbm.at[expert], wuf_scr, sems.at[0]),
                pltpu.make_async_copy(wd_hbm.at[expert], wdf_scr, sems.at[1]))

    @pl.when(i == 0)
    def _():
        for cp in weight_copies(e):
            cp.start()

    first_of_group = jnp.logical_or(i == 0, e != blk_e_ref[jnp.maximum(i - 1, 0)])

    @pl.when(jnp.logical_and(i < nact_ref[0], first_of_group))
    def _():
        for cp in weight_copies(e):
            cp.wait()
        wub_scr[...] = wuf_scr[...].astype(BF16)
        wdb_scr[...] = wdf_scr[...].astype(BF16)

        @pl.when(next_e_ref[i] >= 0)
        def _():
            for cp in weight_copies(next_e_ref[i]):
                cp.start()

    def ffn(rows):
        lo, hi = _unpack_rows(x_ref[rows, :])
        x = jnp.concatenate([lo.astype(BF16), hi.astype(BF16)], axis=-1)
        glu = jnp.dot(x, wub_scr[:, :D_EXPERT], preferred_element_type=F32)
        lin = jnp.dot(x, wub_scr[:, D_EXPERT:], preferred_element_type=F32)
        glu = jnp.minimum(glu + bu_ref[0, :, :D_EXPERT], SWIGLU_LIMIT)
        lin = jnp.clip(lin + bu_ref[0, :, D_EXPERT:], -SWIGLU_LIMIT, SWIGLU_LIMIT)
        a = glu * _sigmoid(SWIGLU_ALPHA * glu) * (lin + 1.0)
        y = jnp.dot(a.astype(BF16), wdb_scr[...], preferred_element_type=F32)
        y_ref[rows, :] = _pack_rows(y + bd_ref[0])

    n_sub = EXPERT_STEP // EXPERT_TILE
    for used in range(n_sub + 1):
        @pl.when(nsub_ref[i] == used)
        def _():
            if used:
                ffn(slice(0, used * EXPERT_TILE))
            if used < n_sub:
                y_ref[used * EXPERT_TILE:, :] = jnp.zeros(((n_sub - used) * EXPERT_TILE, PACKED_W), y_ref.dtype)


def _experts(blk_e, nact, next_e, nsub, xs, w_up, b_up, w_down, b_down):
    P = xs.shape[0]
    n_blocks = P // EXPERT_STEP
    act = lambda i, na: jnp.minimum(i, na[0] - 1)
    grid_spec = pltpu.PrefetchScalarGridSpec(
        num_scalar_prefetch=4,
        grid=(n_blocks,),
        in_specs=[pl.BlockSpec((EXPERT_STEP, PACKED_W), lambda i, be, na, ne, ns: (act(i, na), 0)),
                  pl.BlockSpec((1, 1, 2 * D_EXPERT), lambda i, be, na, ne, ns: (be[act(i, na)], 0, 0)),
                  pl.BlockSpec((1, 1, D_MODEL), lambda i, be, na, ne, ns: (be[act(i, na)], 0, 0)),
                  pl.BlockSpec(memory_space=pl.ANY),
                  pl.BlockSpec(memory_space=pl.ANY)],
        out_specs=pl.BlockSpec((EXPERT_STEP, PACKED_W), lambda i, be, na, ne, ns: (i, 0)),
        scratch_shapes=[pltpu.VMEM((D_MODEL, 2 * D_EXPERT), F32),
                        pltpu.VMEM((D_EXPERT, D_MODEL), F32),
                        pltpu.VMEM((D_MODEL, 2 * D_EXPERT), BF16),
                        pltpu.VMEM((D_EXPERT, D_MODEL), BF16),
                        pltpu.SemaphoreType.DMA((2,))],
    )
    return pl.pallas_call(
        _expert_kernel,
        out_shape=jax.ShapeDtypeStruct((P, PACKED_W), jnp.uint32),
        grid_spec=grid_spec,
        compiler_params=_cparams("arbitrary"),
        name="routed_experts",
    )(blk_e, nact, next_e, nsub, xs, b_up.reshape(N_EXPERTS, 1, 2 * D_EXPERT),
      b_down.reshape(N_EXPERTS, 1, D_MODEL), w_up, w_down)


def _combine_kernel(runs_ref, h_ref, idx_ref, gate_ref, g_ref, b_ref, ys_ref, o_ref, ybuf, sems):
    i = pl.program_id(0)
    n = pl.num_programs(0)

    def gather(tile, slot, action):
        _for_each_run(runs_ref, tile, ybuf.at[slot], ys_ref, sems.at[slot], False, action)

    @pl.when(i == 0)
    def _():
        ybuf[...] = jnp.zeros(ybuf.shape, ybuf.dtype)
        gather(0, 0, "start")

    @pl.when(i + 1 < n)
    def _():
        gather(i + 1, (i + 1) % 2, "start")

    slot = i % 2
    gather(i, slot, "wait")

    idx = idx_ref[...]
    gate = gate_ref[...]
    chunk = EXPERT_TILE
    col_id = lax.broadcasted_iota(jnp.int32, (ROW_TILE, LANES), 1)
    pos_b = [jnp.broadcast_to(idx[:, TOP_K + kk:TOP_K + kk + 1], (ROW_TILE, LANES)) for kk in range(TOP_K)]
    gate_b = [jnp.broadcast_to(gate[:, kk:kk + 1], (ROW_TILE, LANES)) for kk in range(TOP_K)]
    f_lo = jnp.zeros((ROW_TILE, PACKED_W), F32)
    f_hi = jnp.zeros((ROW_TILE, PACKED_W), F32)
    def operands(c):
        parts = []
        for off in range(c * chunk, (c + 1) * chunk, LANES):
            part = jnp.zeros((ROW_TILE, LANES), F32)
            for kk in range(TOP_K):
                part = jnp.where(col_id + off == pos_b[kk], gate_b[kk], part)
            parts.append(part)
        lo, hi = _unpack_rows(ybuf[slot, c * chunk:(c + 1) * chunk, :])
        return jnp.concatenate(parts, axis=-1).astype(BF16), lo.astype(BF16), hi.astype(BF16)

    for c in range(STAGE_ROWS // chunk):
        wb, lo, hi = operands(c)
        f_lo = f_lo + jnp.dot(wb, lo, preferred_element_type=F32)
        f_hi = f_hi + jnp.dot(wb, hi, preferred_element_type=F32)
    y_lo = DN_ALPHA * h_ref[:, :PACKED_W] + f_lo
    y_hi = DN_ALPHA * h_ref[:, PACKED_W:] + f_hi
    mu = (jnp.sum(y_lo, axis=-1, keepdims=True) + jnp.sum(y_hi, axis=-1, keepdims=True)) / D_MODEL
    d_lo = y_lo - mu
    d_hi = y_hi - mu
    var = (jnp.sum(d_lo * d_lo, axis=-1, keepdims=True) + jnp.sum(d_hi * d_hi, axis=-1, keepdims=True)) / D_MODEL
    inv = lax.rsqrt(var + LN_EPS)
    o_ref[:, :PACKED_W] = d_lo * inv * g_ref[:, :PACKED_W] + b_ref[:, :PACKED_W]
    o_ref[:, PACKED_W:] = d_hi * inv * g_ref[:, PACKED_W:] + b_ref[:, PACKED_W:]


def _combine(runs, h, idx, gate, ys, g, b):
    T = h.shape[0]
    grid_spec = pltpu.PrefetchScalarGridSpec(
        num_scalar_prefetch=1,
        grid=(T // ROW_TILE,),
        in_specs=[pl.BlockSpec((ROW_TILE, D_MODEL), lambda i, r: (i, 0)),
                  pl.BlockSpec((ROW_TILE, LANES), lambda i, r: (i, 0)),
                  pl.BlockSpec((ROW_TILE, LANES), lambda i, r: (i, 0)),
                  pl.BlockSpec(g.shape, lambda i, r: (0, 0)),
                  pl.BlockSpec(b.shape, lambda i, r: (0, 0)),
                  pl.BlockSpec(memory_space=pl.ANY)],
        out_specs=pl.BlockSpec((ROW_TILE, D_MODEL), lambda i, r: (i, 0)),
        scratch_shapes=[pltpu.VMEM((2, STAGE_ROWS, PACKED_W), jnp.uint32),
                        pltpu.SemaphoreType.DMA((2,))],
    )
    return pl.pallas_call(
        _combine_kernel,
        out_shape=jax.ShapeDtypeStruct((T, D_MODEL), F32),
        grid_spec=grid_spec,
        compiler_params=_cparams("arbitrary"),
        name="combine_ln",
    )(runs, h, idx, gate, g, b, ys)


def _pack_w_in(w_in):
    wt = w_in.T
    lat = Q_LORA + KV_LORA + MLA_ROPE
    half = MLA_ROPE // 2
    wt_lat = jnp.concatenate([wt[:lat], -wt[lat - half:lat], wt[lat - MLA_ROPE:lat - half]], axis=0)
    return wt[lat:].astype(BF16), wt_lat.astype(BF16)


def _pack_w_uq(w_uq):
    w = w_uq.reshape(Q_LORA, MLA_HEADS, MLA_NOPE + MLA_ROPE)
    rope = w[:, :, MLA_NOPE:]
    rot = jnp.concatenate([-rope[:, :, MLA_ROPE // 2:], rope[:, :, :MLA_ROPE // 2]], axis=-1)
    return jnp.concatenate([w, rot], axis=-1).reshape(Q_LORA, MLA_HEADS * MLA_QK).astype(BF16)


def _rope_freqs():
    ret = 1.0 / (10000.0 ** jnp.linspace(0.0, 1.0, RET_DK // 2, dtype=F32))
    mla = 1.0 / (ROPE_BASE ** (jnp.arange(0, MLA_ROPE, 2, dtype=F32) / MLA_ROPE))
    return jnp.concatenate([ret, mla, jnp.zeros((LANES - ret.shape[0] - mla.shape[0],), F32)])[None, :]


def _routing_plan(cnt, n_tokens):
    counts = cnt[::SUBLANES, :N_EXPERTS]
    n_tiles = counts.shape[0]
    run_len = ((counts + SUBLANES - 1) // SUBLANES) * SUBLANES
    stage_off = jnp.cumsum(run_len, axis=1) - run_len
    tile_off = jnp.cumsum(run_len, axis=0) - run_len
    rows_used = jnp.sum(run_len, axis=0)
    padded = ((rows_used + EXPERT_STEP - 1) // EXPERT_STEP) * EXPERT_STEP
    pad_end = jnp.cumsum(padded)
    pad_start = pad_end - padded
    runs = jnp.stack([stage_off, pad_start[None, :] + tile_off, run_len], axis=-1).reshape(-1).astype(jnp.int32)
    nact = (pad_end[-1:] // EXPERT_STEP).astype(jnp.int32)
    gaps = jnp.concatenate([jnp.stack([pad_start + rows_used, padded - rows_used], axis=-1).reshape(-1),
                            pad_end[-1:] // EXPERT_TILE]).astype(jnp.int32)
    n_rows = n_tokens * TOP_K + (SUBLANES - 1) * n_tiles * N_EXPERTS + N_EXPERTS * EXPERT_STEP
    n_rows = -(-n_rows // EXPERT_STEP) * EXPERT_STEP
    n_blocks = n_rows // EXPERT_STEP
    experts = jnp.arange(N_EXPERTS, dtype=jnp.int32)
    blk_row = jnp.arange(n_blocks, dtype=jnp.int32) * EXPERT_STEP
    blk_e = jnp.minimum(jnp.sum((pad_end[None, :] <= blk_row[:, None]).astype(jnp.int32), axis=-1),
                        N_EXPERTS - 1)
    used_end = jnp.sum(jnp.where(blk_e[:, None] == experts[None, :], (pad_start + rows_used)[None, :], 0), axis=-1)
    nsub = jnp.clip((used_end - blk_row + EXPERT_TILE - 1) // EXPERT_TILE, 0,
                    EXPERT_STEP // EXPERT_TILE).astype(jnp.int32)
    later = jnp.where((rows_used > 0)[None, :] & (experts[None, :] > experts[:, None]), experts[None, :], N_EXPERTS)
    nxt = jnp.min(later, axis=-1)
    nxt = jnp.where(nxt == N_EXPERTS, -1, nxt)
    next_e = jnp.sum(jnp.where(blk_e[:, None] == experts[None, :], nxt[None, :], 0), axis=-1).astype(jnp.int32)
    return runs, gaps, blk_e, nact, next_e, nsub, n_rows


def kernel(x, positions, w_in, q_norm_g, w_uq, kv_norm_g, w_ukv, w_o, ln1_g, ln1_b,
           w_router, b_router, w_up, b_up, w_down, b_down, ln2_g, ln2_b):
    B, S, _ = x.shape
    T = B * S
    assert S % ATTN_Q_TILE == 0 and T % ROW_TILE == 0, "sequence / token count must be whole tiles"
    rc, rs, mc, ms = _rope_tables(positions.reshape(T // LANES, LANES).astype(jnp.int32), _rope_freqs())
    dec, qw, kw, cd = _retention_constants()
    h = x.reshape(T, D_MODEL)
    for l in range(DEPTH):
        proj = _input_projection(h, *_pack_w_in(w_in[l]))
        w_kv = w_ukv[l].reshape(KV_LORA, MLA_HEADS, MLA_NOPE + MLA_V)
        w_k = w_kv[:, :, :MLA_NOPE].reshape(KV_LORA, MLA_HEADS * MLA_NOPE).astype(BF16)
        w_vt = w_kv[:, :, MLA_NOPE:].reshape(KV_LORA, MLA_HEADS * MLA_V).T.astype(BF16)
        q_cat, kn, vt, kr = _mla_prep(proj, mc, ms, _pack_w_uq(w_uq[l]), w_k, w_vt,
                                      q_norm_g[l][None, :], kv_norm_g[l][None, :])
        o_mla = _attention(q_cat, kn, kr, vt, B, S)
        o_ret = _retention(proj, rc, rs, dec, qw, kw, cd, B, S)
        h1, h1b, idx_out, gate, cnt = _merge_and_route(
            h, o_mla, o_ret, proj, w_o[l].astype(BF16), ln1_g[l][None, :], ln1_b[l][None, :],
            w_router[l], b_router[l])
        runs, gaps, blk_e, nact, next_e, nsub, n_rows = _routing_plan(cnt, T)
        xs = _dispatch(runs, gaps, h1b, idx_out, n_rows)
        ys = _experts(blk_e, nact, next_e, nsub, xs, w_up[l], b_up[l], w_down[l], b_down[l])
        h = _combine(runs, h1, idx_out, gate, ys, ln2_g[l][None, :], ln2_b[l][None, :])
    return h.reshape(B, S, D_MODEL)
```

```python
import numpy as np
import jax
import jax.numpy as jnp
from jax import lax
from jax.experimental import pallas as pl
from jax.experimental.pallas import tpu as pltpu

F32 = jnp.float32
BF16 = jnp.bfloat16

D_MODEL = 1024
DEPTH = 1
MLA_HEADS = 8
MLA_NOPE = 128
MLA_ROPE = 64
MLA_V = 128
Q_LORA = 384
KV_LORA = 256
ROPE_BASE = 10000.0
RET_HEADS = 8
RET_DK = 128
RET_DV = 128
RET_CHUNK = 128
N_EXPERTS = 32
TOP_K = 4
D_EXPERT = 1024
SWIGLU_LIMIT = 7.0
SWIGLU_ALPHA = 1.702
DN_ALPHA = (2.0 * DEPTH) ** 0.25
LN_EPS = 1e-5
RMS_EPS = 1e-6
GN_EPS = 1e-6
NEG_INF = -1e30

LANES = 128
SUBLANES = 8
MLA_QK = 2 * LANES
PACKED_W = D_MODEL // 2
LATENT_W = 768
N_WIDE = 6 * D_MODEL
N_PROJ = N_WIDE + LATENT_W
VMEM_LIMIT = 56 * 1024 * 1024

ROW_TILE = 512
ATTN_TILE = ROW_TILE
ATTN_Q_TILE = 2 * ATTN_TILE
ATTN_Q_SPLIT = 4
BF16_SUBLANES = 2 * SUBLANES
VT_ROWS = MLA_V + BF16_SUBLANES
LOG2_E = 1.4426950408889634
RET_GROUP = 16
EXPERT_TILE = 256
EXPERT_STEP = 4 * EXPERT_TILE
STAGE_ROWS = -(-(ROW_TILE * TOP_K + (SUBLANES - 1) * N_EXPERTS) // EXPERT_TILE) * EXPERT_TILE


def _cparams(*sem):
    return pltpu.CompilerParams(dimension_semantics=sem, vmem_limit_bytes=VMEM_LIMIT)


def _const_spec(shape):
    nd = len(shape)
    return pl.BlockSpec(shape, lambda *_: (0,) * nd, pipeline_mode=pl.Buffered(1))


def _sigmoid(x):
    return 0.5 * jnp.tanh(0.5 * x) + 0.5


def _pack_rows(x):
    n = x.shape[-1] // 2
    xb = x.astype(BF16).astype(F32)
    lo = pltpu.bitcast(xb[:, :n], jnp.uint32) >> 16
    hi = pltpu.bitcast(xb[:, n:], jnp.uint32) & jnp.uint32(0xFFFF0000)
    return lo | hi


def _unpack_rows(w):
    lo = pltpu.bitcast(w << 16, F32)
    hi = pltpu.bitcast(w & jnp.uint32(0xFFFF0000), F32)
    return lo, hi


def _swap_halves(x):
    return pltpu.roll(x, LANES // 2, 1)


def _proj_kernel(x_ref, wt_wide_ref, wt_lat_ref, o_ref):
    xb = x_ref[...].astype(BF16)
    nt = (((1,), (1,)), ((), ()))
    for c in range(N_WIDE // LATENT_W):
        sl = slice(c * LATENT_W, (c + 1) * LATENT_W)
        o_ref[:, sl] = lax.dot_general(xb, wt_wide_ref[sl, :], nt, preferred_element_type=F32).astype(o_ref.dtype)
    o_ref[:, N_WIDE:] = lax.dot_general(xb, wt_lat_ref[...], nt, preferred_element_type=F32).astype(o_ref.dtype)


def _input_projection(x2, wt_wide, wt_lat):
    T = x2.shape[0]
    return pl.pallas_call(
        _proj_kernel,
        out_shape=jax.ShapeDtypeStruct((T, N_PROJ), BF16),
        grid=(T // ROW_TILE,),
        in_specs=[pl.BlockSpec((ROW_TILE, D_MODEL), lambda i: (i, 0)),
                  _const_spec((N_WIDE, D_MODEL)), _const_spec((LATENT_W, D_MODEL))],
        out_specs=pl.BlockSpec((ROW_TILE, N_PROJ), lambda i: (i, 0)),
        compiler_params=_cparams("parallel"),
        name="input_projection",
    )(x2, wt_wide, wt_lat)


def _rope_table_kernel(pos_ref, freq_ref, rc_ref, rs_ref, mc_ref, ms_ref):
    pos = pos_ref[...].astype(F32)
    pos_t = jnp.concatenate([pos, jnp.zeros((LANES - pos.shape[0], LANES), F32)], axis=0).T
    freq = freq_ref[...]
    ang = jnp.concatenate([pos_t[:, j:j + 1] * freq for j in range(pos.shape[0])], axis=0)
    c = jnp.cos(ang)
    s = jnp.sin(ang)
    lane = lax.broadcasted_iota(jnp.int32, ang.shape, 1)
    c64 = pltpu.roll(c, 64, 1)
    s64 = pltpu.roll(s, 64, 1)
    rc_ref[...] = jnp.where(lane < 64, c, c64)
    rs_ref[...] = jnp.where(lane < 64, -s, s64)
    c96 = pltpu.roll(c, 96, 1)
    s96 = pltpu.roll(s, 96, 1)
    mc_ref[...] = jnp.where(lane < 32, c64, jnp.where(lane < 64, c96, 0.0))
    ms_ref[...] = jnp.where(lane < 32, s64, jnp.where(lane < 64, s96, 0.0))


def _rope_tables(pos2d, freq_row):
    T = pos2d.shape[0] * LANES
    rows = SUBLANES * LANES
    tab = jax.ShapeDtypeStruct((T, LANES), F32)
    spec = pl.BlockSpec((rows, LANES), lambda i: (i, 0))
    return pl.pallas_call(
        _rope_table_kernel,
        out_shape=(tab, tab, tab, tab),
        grid=(T // rows,),
        in_specs=[pl.BlockSpec((SUBLANES, LANES), lambda i: (i, 0)), _const_spec((1, LANES))],
        out_specs=(spec, spec, spec, spec),
        compiler_params=_cparams("parallel"),
        name="rope_tables",
    )(pos2d, freq_row)


def _mla_prep_kernel(a_ref, mc_ref, ms_ref, wq_ref, wk_ref, wvt_ref, qg_ref, kvg_ref,
                     q_ref, kn_ref, vt_ref, kr_ref):
    a = a_ref[...].astype(F32)
    mc = mc_ref[...]
    ms = ms_ref[...]
    cq = a[:, :Q_LORA]
    qn = cq * lax.rsqrt(jnp.mean(cq * cq, axis=-1, keepdims=True) + RMS_EPS) * qg_ref[...]
    q = jnp.dot(qn.astype(BF16), wq_ref[...], preferred_element_type=F32)
    scale = (MLA_NOPE + MLA_ROPE) ** -0.5 * LOG2_E
    for h in range(MLA_HEADS):
        nope = q[:, h * MLA_QK:h * MLA_QK + LANES]
        blk = q[:, h * MLA_QK + LANES:(h + 1) * MLA_QK]
        rope = blk * mc + _swap_halves(blk) * ms
        q_ref[:, h * MLA_QK:h * MLA_QK + LANES] = (nope * scale).astype(BF16)
        q_ref[:, h * MLA_QK + LANES:(h + 1) * MLA_QK] = (rope * scale).astype(BF16)
    ckv = a[:, Q_LORA:Q_LORA + KV_LORA]
    kvn = ckv * lax.rsqrt(jnp.mean(ckv * ckv, axis=-1, keepdims=True) + RMS_EPS) * kvg_ref[...]
    kvb = kvn.astype(BF16)
    kn_ref[...] = jnp.dot(kvb, wk_ref[...], preferred_element_type=F32).astype(BF16)
    vt = lax.dot_general(wvt_ref[...], kvb, (((1,), (1,)), ((), ())),
                         preferred_element_type=F32).astype(BF16)
    ones = jnp.ones((VT_ROWS - MLA_V, vt.shape[1]), BF16)
    for h in range(MLA_HEADS):
        vt_ref[0, h * VT_ROWS:h * VT_ROWS + MLA_V, :] = vt[h * MLA_V:(h + 1) * MLA_V, :]
        vt_ref[0, h * VT_ROWS + MLA_V:(h + 1) * VT_ROWS, :] = ones
    krb = a[:, Q_LORA + KV_LORA:]
    kr_ref[...] = (krb * mc + _swap_halves(krb) * ms).astype(BF16)


def _mla_prep(proj, mc, ms, wq, wk, wvt, qg, kvg):
    T = proj.shape[0]
    row = lambda w: pl.BlockSpec((ROW_TILE, w), lambda i: (i, 0))
    return pl.pallas_call(
        _mla_prep_kernel,
        out_shape=(jax.ShapeDtypeStruct((T, MLA_HEADS * MLA_QK), BF16),
                   jax.ShapeDtypeStruct((T, MLA_HEADS * MLA_NOPE), BF16),
                   jax.ShapeDtypeStruct((T // ROW_TILE, MLA_HEADS * VT_ROWS, ROW_TILE), BF16),
                   jax.ShapeDtypeStruct((T, LANES), BF16)),
        grid=(T // ROW_TILE,),
        in_specs=[pl.BlockSpec((ROW_TILE, LATENT_W), lambda i: (i, N_WIDE // LATENT_W)),
                  row(LANES), row(LANES),
                  _const_spec(wq.shape), _const_spec(wk.shape), _const_spec(wvt.shape),
                  _const_spec(qg.shape), _const_spec(kvg.shape)],
        out_specs=(row(MLA_HEADS * MLA_QK), row(MLA_HEADS * MLA_NOPE),
                   pl.BlockSpec((1, MLA_HEADS * VT_ROWS, ROW_TILE), lambda i: (i, 0, 0)), row(LANES)),
        compiler_params=_cparams("parallel"),
        name="mla_prep",
    )(proj, mc, ms, wq, wk, wvt, qg, kvg)


def _attn_kernel(q_ref, kn_ref, kr_ref, vt_ref, o_ref, acc_scr):
    W = ATTN_Q_TILE // ATTN_Q_SPLIT
    n_diag = ATTN_Q_TILE // ATTN_TILE
    groups = tuple(range(ATTN_Q_SPLIT))

    def visible(d, c):
        return min(max((c + 1) * W - d * ATTN_TILE, 0), ATTN_TILE)

    for qi in range(q_ref.shape[0] // ATTN_Q_TILE):
        q0 = qi * ATTN_Q_TILE
        qs = [q_ref[q0 + c * W:q0 + (c + 1) * W, :] for c in groups]
        acc_scr[...] = jnp.zeros(acc_scr.shape, F32)

        def scores(j, diag=None):
            rows = slice(j * ATTN_TILE, (j + 1) * ATTN_TILE)
            k = jnp.concatenate([kn_ref[rows, :], kr_ref[rows, :]], axis=-1)
            keys = {c: ATTN_TILE if diag is None else visible(diag, c) for c in groups}
            return {c: lax.dot_general(k[:nk], qs[c], (((1,), (1,)), ((), ())), preferred_element_type=F32)
                    for c, nk in keys.items() if nk > 0}

        def softmax_pv(j, s_all, m_all, diag=None):
            vt = vt_ref[j]
            out = list(m_all)
            for c in s_all:
                s = s_all[c]
                nk = s.shape[0]
                if diag is not None and diag * ATTN_TILE + nk - 1 > c * W:
                    kp = lax.broadcasted_iota(jnp.int32, s.shape, 0) + diag * ATTN_TILE
                    qp = lax.broadcasted_iota(jnp.int32, s.shape, 1) + c * W
                    s = jnp.where(kp <= qp, s, NEG_INF)
                m_new = jnp.maximum(m_all[c], jnp.max(s, axis=0, keepdims=True))
                alpha = jnp.exp2(m_all[c] - m_new)
                p = jnp.exp2(s - m_new).astype(BF16)
                acc_scr[c] = alpha * acc_scr[c] + jnp.dot(vt[:, :nk], p, preferred_element_type=F32)
                out[c] = m_new
            return tuple(out)

        first = n_diag * qi
        work = [(j, None) for j in range(first)] + [(first + d, d) for d in range(n_diag)]
        m_all = tuple(jnp.full((1, W), NEG_INF, F32) for _ in groups)
        pending = [scores(*item) for item in work[:2]]
        for n, (j, diag) in enumerate(work):
            m_all = softmax_pv(j, pending.pop(0), m_all, diag)
            if n + 2 < len(work):
                pending.append(scores(*work[n + 2]))
        for c in groups:
            acc = acc_scr[c]
            o_ref[q0 + c * W:q0 + (c + 1) * W, :] = (acc[:MLA_V] / acc[MLA_V:MLA_V + 1]).T.astype(o_ref.dtype)


def _attention(q_cat, kn, kr, vt, B, S):
    T = B * S
    return pl.pallas_call(
        _attn_kernel,
        out_shape=jax.ShapeDtypeStruct((T, MLA_HEADS * MLA_V), BF16),
        grid=(B, MLA_HEADS),
        in_specs=[pl.BlockSpec((S, MLA_QK), lambda b, h: (b, h)),
                  pl.BlockSpec((S, MLA_NOPE), lambda b, h: (b, h)),
                  pl.BlockSpec((S, LANES), lambda b, h: (b, 0)),
                  pl.BlockSpec((S // ATTN_TILE, VT_ROWS, ATTN_TILE), lambda b, h: (b, h, 0))],
        out_specs=pl.BlockSpec((S, MLA_V), lambda b, h: (b, h)),
        scratch_shapes=[pltpu.VMEM((ATTN_Q_SPLIT, VT_ROWS, ATTN_Q_TILE // ATTN_Q_SPLIT), F32)],
        compiler_params=_cparams("parallel", "parallel"),
        name="mla_attention",
    )(q_cat, kn, kr, vt)


def _ret_kernel(q_ref, k_ref, v_ref, g_ref, rc_ref, rs_ref, dec_ref, qw_ref, kw_ref, cd_ref,
                o_ref, qb_scr, qwb_scr, kb_scr, kwb_scr):
    S = q_ref.shape[0]
    C = RET_CHUNK
    n_chunks = S // C
    rc = rc_ref[...]
    rs = rs_ref[...]
    q = q_ref[...].astype(F32)
    k = k_ref[...].astype(F32)
    q = q * rc + _swap_halves(q) * rs
    k = (k * rc + _swap_halves(k) * rs) * (RET_DK ** -0.5)
    qw = qw_ref[0]
    kw = kw_ref[0]
    q3 = q.reshape(n_chunks, C, RET_DK)
    k3 = k.reshape(n_chunks, C, RET_DK)
    qb_scr[...] = q.astype(BF16)
    kb_scr[...] = k.astype(BF16)
    qwb_scr[...] = (q3 * qw[None]).reshape(S, RET_DK).astype(BF16)
    kwb_scr[...] = (k3 * kw[None]).reshape(S, RET_DK)
    dec = dec_ref[0]
    cd = cd_ref[0]

    def group(gi, R):
        rows_of = [pl.ds(pl.multiple_of((gi * RET_GROUP + u) * C, C), C) for u in range(RET_GROUP)]
        scores, updates = [], []
        for rows in rows_of:
            scores.append(lax.dot_general(qb_scr[rows, :], kb_scr[rows, :], (((1,), (1,)), ((), ())),
                                          preferred_element_type=F32))
            kwt = kwb_scr[rows, :].T.astype(BF16)
            updates.append(jnp.dot(kwt, v_ref[rows, :], preferred_element_type=F32))
        states = []
        for u in range(RET_GROUP):
            states.append(R.astype(BF16))
            R = cd * R + updates[u]
        for u, rows in enumerate(rows_of):
            o = jnp.dot((scores[u] * dec).astype(BF16), v_ref[rows, :], preferred_element_type=F32)
            o = o + jnp.dot(qwb_scr[rows, :], states[u], preferred_element_type=F32)
            mu = jnp.mean(o, axis=-1, keepdims=True)
            d = o - mu
            var = jnp.mean(d * d, axis=-1, keepdims=True)
            gate = g_ref[rows, :].astype(F32)
            o_ref[rows, :] = (d * lax.rsqrt(var + GN_EPS) * (gate * _sigmoid(gate))).astype(o_ref.dtype)
        return R

    lax.fori_loop(0, n_chunks // RET_GROUP, group, jnp.zeros((RET_DK, RET_DV), F32))


def _retention(proj, rc, rs, dec, qw, kw, cd, B, S):
    assert S % (RET_CHUNK * RET_GROUP) == 0, "sequence must hold whole groups of retention chunks"
    T = B * S
    H = RET_HEADS
    col = lambda off: pl.BlockSpec((S, LANES), lambda b, h: (b, off + h))
    tab = pl.BlockSpec((S, LANES), lambda b, h: (b, 0))
    per_head = lambda r: pl.BlockSpec((1, r, LANES), lambda b, h: (h, 0, 0))
    return pl.pallas_call(
        _ret_kernel,
        out_shape=jax.ShapeDtypeStruct((T, H * RET_DV), BF16),
        grid=(B, H),
        in_specs=[col(0), col(H), col(2 * H), col(3 * H), tab, tab,
                  per_head(RET_CHUNK), per_head(RET_CHUNK), per_head(RET_CHUNK), per_head(1)],
        out_specs=pl.BlockSpec((S, RET_DV), lambda b, h: (b, h)),
        scratch_shapes=[pltpu.VMEM((S, RET_DK), BF16), pltpu.VMEM((S, RET_DK), BF16),
                        pltpu.VMEM((S, RET_DK), BF16), pltpu.VMEM((S, RET_DK), F32)],
        compiler_params=_cparams("parallel", "parallel"),
        name="retention",
    )(proj, proj, proj, proj, rc, rs, dec, qw, kw, cd)


def _retention_constants():
    C = RET_CHUNK
    h = np.arange(RET_HEADS, dtype=np.float64)
    log_gamma = np.log(1.0 - 2.0 ** (-5.0 - h))
    idx = np.arange(C, dtype=np.float64)
    rel = idx[:, None] - idx[None, :]
    dec = np.where(rel[None] >= 0, np.exp(np.maximum(rel, 0.0)[None] * log_gamma[:, None, None]), 0.0)
    qw = np.exp((idx + 1.0)[None, :] * log_gamma[:, None])
    kw = np.exp((C - 1.0 - idx)[None, :] * log_gamma[:, None])
    cd = np.exp(C * log_gamma)
    bc = lambda a: np.ascontiguousarray(np.broadcast_to(a[..., None], a.shape + (LANES,))).astype(np.float32)
    return dec.astype(np.float32), bc(qw), bc(kw), bc(cd[:, None])


def _layernorm(y, g, b):
    mu = jnp.mean(y, axis=-1, keepdims=True)
    d = y - mu
    var = jnp.mean(d * d, axis=-1, keepdims=True)
    return d * lax.rsqrt(var + LN_EPS) * g + b


def _merge_kernel(x_ref, om_ref, or_ref, gm_ref, gr_ref, wo_ref, g_ref, b_ref, w01_ref, rb_ref,
                  h_ref, hb_ref, idx_ref, gate_ref, cnt_ref):
    mixed = (_sigmoid(gm_ref[...].astype(F32)) * om_ref[...].astype(F32)
             + _sigmoid(gr_ref[...].astype(F32)) * or_ref[...].astype(F32))
    mix = jnp.dot(mixed.astype(BF16), wo_ref[...], preferred_element_type=F32)
    h = _layernorm(DN_ALPHA * x_ref[...] + mix, g_ref[...], b_ref[...])
    h_ref[...] = h
    hb_ref[...] = h.astype(BF16)
    _route_tile(h, w01_ref, rb_ref, idx_ref, gate_ref, cnt_ref)


def _merge_and_route(x2, o_mla, o_ret, proj, wo, g, b, w_router, b_router):
    T = x2.shape[0]
    wpad = jnp.zeros((D_MODEL, LANES), F32).at[:, :N_EXPERTS].set(w_router)
    w0 = wpad.astype(BF16)
    w01 = jnp.concatenate([w0, (wpad - w0.astype(F32)).astype(BF16)], axis=-1)
    bpad = jnp.zeros((1, LANES), F32).at[0, :N_EXPERTS].set(b_router)
    row = lambda c: pl.BlockSpec((ROW_TILE, D_MODEL), lambda i: (i, c))
    lanes = pl.BlockSpec((ROW_TILE, LANES), lambda i: (i, 0))
    return pl.pallas_call(
        _merge_kernel,
        out_shape=(jax.ShapeDtypeStruct((T, D_MODEL), F32), jax.ShapeDtypeStruct((T, D_MODEL), BF16),
                   jax.ShapeDtypeStruct((T, LANES), jnp.int32), jax.ShapeDtypeStruct((T, LANES), F32),
                   jax.ShapeDtypeStruct((T // ROW_TILE * SUBLANES, LANES), jnp.int32)),
        grid=(T // ROW_TILE,),
        in_specs=[row(0), row(0), row(0), row(4), row(5),
                  _const_spec(wo.shape), _const_spec(g.shape), _const_spec(b.shape),
                  _const_spec(w01.shape), _const_spec(bpad.shape)],
        out_specs=(row(0), row(0), lanes, lanes, pl.BlockSpec((SUBLANES, LANES), lambda i: (i, 0))),
        compiler_params=_cparams("parallel"),
        name="merge_out_proj_ln_route",
    )(x2, o_mla, o_ret, proj, proj, wo, g, b, w01, bpad)


def _split2(a):
    a0 = a.astype(BF16)
    return a0, (a - a0.astype(F32)).astype(BF16)


def _route_tile(h, w01_ref, b_ref, idx_ref, gate_ref, cnt_ref):
    h0, h1 = _split2(h)
    p0 = jnp.dot(h0, w01_ref[...], preferred_element_type=F32)
    p1 = jnp.dot(h1, w01_ref[:, :LANES], preferred_element_type=F32)
    logits = p0[:, :LANES] + (p0[:, LANES:] + p1) + b_ref[...]
    rows = logits.shape[0]
    lane = lax.broadcasted_iota(jnp.int32, (rows, LANES), 1)
    lane_f = lane.astype(F32)
    work = jnp.where(lane < N_EXPERTS, logits, -jnp.inf)
    vals, ids, sels = [], [], []
    for _ in range(TOP_K):
        m = jnp.max(work, axis=-1, keepdims=True)
        first = jnp.min(jnp.where(work == m, lane_f, float(LANES)), axis=-1, keepdims=True)
        sel = lane_f == first
        work = jnp.where(sel, -jnp.inf, work)
        vals.append(m)
        ids.append(first)
        sels.append(sel)
    exps = [jnp.exp(v - vals[0]) for v in vals]
    denom = exps[0] + exps[1] + exps[2] + exps[3]
    chosen = jnp.zeros((rows, LANES), F32)
    for sel in sels:
        chosen = jnp.where(sel, 1.0, chosen)
    r = lax.broadcasted_iota(jnp.int32, (rows, rows), 0)
    c = lax.broadcasted_iota(jnp.int32, (rows, rows), 1)
    lower = jnp.where(r > c, 1.0, 0.0).astype(BF16)
    before = jnp.dot(lower, chosen.astype(BF16), preferred_element_type=F32)
    counts = jnp.sum(chosen, axis=0, keepdims=True)
    run_len = jnp.floor((counts + (SUBLANES - 1)) * (1.0 / SUBLANES)) * SUBLANES
    er = lax.broadcasted_iota(jnp.int32, (LANES, LANES), 0)
    ec = lax.broadcasted_iota(jnp.int32, (LANES, LANES), 1)
    earlier = jnp.where(er < ec, 1.0, 0.0).astype(BF16)
    run_start = jnp.dot(jnp.broadcast_to(run_len, (SUBLANES, LANES)).astype(BF16), earlier,
                        preferred_element_type=F32)[:1]
    slot = before + run_start
    idx_out = jnp.zeros((rows, LANES), jnp.int32)
    gate_out = jnp.zeros((rows, LANES), F32)
    for kk in range(TOP_K):
        pos = jnp.sum(jnp.where(sels[kk], slot, 0.0), axis=-1, keepdims=True).astype(jnp.int32)
        idx_out = jnp.where(lane == kk, ids[kk].astype(jnp.int32), idx_out)
        idx_out = jnp.where(lane == TOP_K + kk, pos, idx_out)
        gate_out = jnp.where(lane == kk, exps[kk] / denom, gate_out)
    idx_ref[...] = idx_out
    gate_ref[...] = gate_out
    cnt_ref[...] = jnp.broadcast_to(counts, cnt_ref.shape).astype(jnp.int32)


def _run_copy(runs_ref, tile, e, stage, hbm, sem, to_hbm):
    base = (tile * N_EXPERTS + e) * 3
    s = pl.multiple_of(runs_ref[base], SUBLANES)
    g = pl.multiple_of(runs_ref[base + 1], SUBLANES)
    n = pl.multiple_of(runs_ref[base + 2], SUBLANES)
    src, dst = stage.at[pl.ds(s, n), :], hbm.at[pl.ds(g, n), :]
    if not to_hbm:
        src, dst = dst, src
    return n, pltpu.make_async_copy(src, dst, sem)


def _for_each_run(runs_ref, tile, stage, hbm, sem, to_hbm, action):
    def one(e, carry):
        n, cp = _run_copy(runs_ref, tile, e, stage, hbm, sem, to_hbm)

        @pl.when(n > 0)
        def _():
            getattr(cp, action)()

        return carry

    lax.fori_loop(0, N_EXPERTS, one, 0)


def _dispatch_kernel(runs_ref, gaps_ref, h_ref, idx_ref, xs_ref, stage, zeros, sems):
    i = pl.program_id(0)

    def gap_copy(e):
        g = pl.multiple_of(gaps_ref[2 * e], SUBLANES)
        n = pl.multiple_of(gaps_ref[2 * e + 1], SUBLANES)
        return n, pltpu.make_async_copy(zeros.at[pl.ds(0, n), :], xs_ref.at[pl.ds(g, n), :], sems.at[2])

    def for_each_gap(action):
        def one(e, carry):
            n, cp = gap_copy(e)

            @pl.when(n > 0)
            def _():
                getattr(cp, action)()

            return carry

        lax.fori_loop(0, N_EXPERTS, one, 0)

        def tail(b, carry):
            rows = pl.ds(pl.multiple_of(b * EXPERT_TILE, EXPERT_TILE), EXPERT_TILE)
            getattr(pltpu.make_async_copy(zeros.at[pl.ds(0, EXPERT_TILE), :], xs_ref.at[rows, :], sems.at[2]),
                    action)()
            return carry

        lax.fori_loop(gaps_ref[2 * N_EXPERTS], xs_ref.shape[0] // EXPERT_TILE, tail, 0)

    @pl.when(i == 0)
    def _():
        zeros[...] = jnp.zeros(zeros.shape, zeros.dtype)
        for_each_gap("start")

    pos_t = idx_ref[...].astype(F32).T.astype(jnp.int32)
    h = h_ref[...]
    chunk = EXPERT_TILE
    row_id = lax.broadcasted_iota(jnp.int32, (chunk, ROW_TILE), 0)
    slot = i % 2
    for c in range(STAGE_ROWS // chunk):
        onehot = jnp.zeros((chunk, ROW_TILE), F32)
        for kk in range(TOP_K):
            onehot = jnp.where(row_id == (pos_t[TOP_K + kk:TOP_K + kk + 1] - c * chunk), 1.0, onehot)
        stage[slot, c * chunk:(c + 1) * chunk, :] = _pack_rows(
            jnp.dot(onehot.astype(BF16), h, preferred_element_type=F32))

    _for_each_run(runs_ref, i, stage.at[slot], xs_ref, sems.at[slot], True, "start")

    @pl.when(i > 0)
    def _():
        _for_each_run(runs_ref, i - 1, stage.at[1 - slot], xs_ref, sems.at[1 - slot], True, "wait")

    @pl.when(i == pl.num_programs(0) - 1)
    def _():
        _for_each_run(runs_ref, i, stage.at[slot], xs_ref, sems.at[slot], True, "wait")
        for_each_gap("wait")


def _dispatch(runs, gaps, hb, idx, n_rows):
    T = hb.shape[0]
    grid_spec = pltpu.PrefetchScalarGridSpec(
        num_scalar_prefetch=2,
        grid=(T // ROW_TILE,),
        in_specs=[pl.BlockSpec((ROW_TILE, D_MODEL), lambda i, r, g: (i, 0)),
                  pl.BlockSpec((ROW_TILE, LANES), lambda i, r, g: (i, 0))],
        out_specs=pl.BlockSpec(memory_space=pl.ANY),
        scratch_shapes=[pltpu.VMEM((2, STAGE_ROWS, PACKED_W), jnp.uint32),
                        pltpu.VMEM((EXPERT_STEP, PACKED_W), jnp.uint32),
                        pltpu.SemaphoreType.DMA((3,))],
    )
    return pl.pallas_call(
        _dispatch_kernel,
        out_shape=jax.ShapeDtypeStruct((n_rows, PACKED_W), jnp.uint32),
        grid_spec=grid_spec,
        compiler_params=_cparams("arbitrary"),
        name="dispatch_rows",
    )(runs, gaps, hb, idx)


def _expert_kernel(blk_e_ref, nact_ref, next_e_ref, nsub_ref, x_ref, bu_ref, bd_ref, wu_hbm, wd_hbm, y_ref,
                   wuf_scr, wdf_scr, wub_scr, wdb_scr, sems):
    i = pl.program_id(0)
    e = blk_e_ref[i]

    def weight_copies(expert):
        return (pltpu.make_async_copy(wu_hbm.at[expert], wuf_scr, sems.at[0]),
                pltpu.make_async_copy(wd_hbm.at[expert], wdf_scr, sems.at[1]))

    @pl.when(i == 0)
    def _():
        for cp in weight_copies(e):
            cp.start()

    first_of_group = jnp.logical_or(i == 0, e != blk_e_ref[jnp.maximum(i - 1, 0)])

    @pl.when(jnp.logical_and(i < nact_ref[0], first_of_group))
    def _():
        for cp in weight_copies(e):
            cp.wait()
        wub_scr[...] = wuf_scr[...].astype(BF16)
        wdb_scr[...] = wdf_scr[...].astype(BF16)

        @pl.when(next_e_ref[i] >= 0)
        def _():
            for cp in weight_copies(next_e_ref[i]):
                cp.start()

    def ffn(rows):
        lo, hi = _unpack_rows(x_ref[rows, :])
        x = jnp.concatenate([lo.astype(BF16), hi.astype(BF16)], axis=-1)
        glu = jnp.dot(x, wub_scr[:, :D_EXPERT], preferred_element_type=F32)
        lin = jnp.dot(x, wub_scr[:, D_EXPERT:], preferred_element_type=F32)
        glu = jnp.minimum(glu + bu_ref[0, :, :D_EXPERT], SWIGLU_LIMIT)
        lin = jnp.clip(lin + bu_ref[0, :, D_EXPERT:], -SWIGLU_LIMIT, SWIGLU_LIMIT)
        a = glu * _sigmoid(SWIGLU_ALPHA * glu) * (lin + 1.0)
        y = jnp.dot(a.astype(BF16), wdb_scr[...], preferred_element_type=F32)
        y_ref[rows, :] = _pack_rows(y + bd_ref[0])

    n_sub = EXPERT_STEP // EXPERT_TILE
    for used in range(n_sub + 1):
        @pl.when(nsub_ref[i] == used)
        def _():
            if used:
                ffn(slice(0, used * EXPERT_TILE))
            if used < n_sub:
                y_ref[used * EXPERT_TILE:, :] = jnp.zeros(((n_sub - used) * EXPERT_TILE, PACKED_W), y_ref.dtype)


def _experts(blk_e, nact, next_e, nsub, xs, w_up, b_up, w_down, b_down):
    P = xs.shape[0]
    n_blocks = P // EXPERT_STEP
    act = lambda i, na: jnp.minimum(i, na[0] - 1)
    grid_spec = pltpu.PrefetchScalarGridSpec(
        num_scalar_prefetch=4,
        grid=(n_blocks,),
        in_specs=[pl.BlockSpec((EXPERT_STEP, PACKED_W), lambda i, be, na, ne, ns: (act(i, na), 0)),
                  pl.BlockSpec((1, 1, 2 * D_EXPERT), lambda i, be, na, ne, ns: (be[act(i, na)], 0, 0)),
                  pl.BlockSpec((1, 1, D_MODEL), lambda i, be, na, ne, ns: (be[act(i, na)], 0, 0)),
                  pl.BlockSpec(memory_space=pl.ANY),
                  pl.BlockSpec(memory_space=pl.ANY)],
        out_specs=pl.BlockSpec((EXPERT_STEP, PACKED_W), lambda i, be, na, ne, ns: (i, 0)),
        scratch_shapes=[pltpu.VMEM((D_MODEL, 2 * D_EXPERT), F32),
                        pltpu.VMEM((D_EXPERT, D_MODEL), F32),
                        pltpu.VMEM((D_MODEL, 2 * D_EXPERT), BF16),
                        pltpu.VMEM((D_EXPERT, D_MODEL), BF16),
                        pltpu.SemaphoreType.DMA((2,))],
    )
    return pl.pallas_call(
        _expert_kernel,
        out_shape=jax.ShapeDtypeStruct((P, PACKED_W), jnp.uint32),
        grid_spec=grid_spec,
        compiler_params=_cparams("arbitrary"),
        name="routed_experts",
    )(blk_e, nact, next_e, nsub, xs, b_up.reshape(N_EXPERTS, 1, 2 * D_EXPERT),
      b_down.reshape(N_EXPERTS, 1, D_MODEL), w_up, w_down)


def _combine_kernel(runs_ref, h_ref, idx_ref, gate_ref, g_ref, b_ref, ys_ref, o_ref, ybuf, sems):
    i = pl.program_id(0)
    n = pl.num_programs(0)

    def gather(tile, slot, action):
        _for_each_run(runs_ref, tile, ybuf.at[slot], ys_ref, sems.at[slot], False, action)

    @pl.when(i == 0)
    def _():
        ybuf[...] = jnp.zeros(ybuf.shape, ybuf.dtype)
        gather(0, 0, "start")

    @pl.when(i + 1 < n)
    def _():
        gather(i + 1, (i + 1) % 2, "start")

    slot = i % 2
    gather(i, slot, "wait")

    idx = idx_ref[...]
    gate = gate_ref[...]
    chunk = EXPERT_TILE
    n_chunks = STAGE_ROWS // chunk
    half = ROW_TILE // 2
    col_id = lax.broadcasted_iota(jnp.int32, (half, LANES), 1)
    rows_lo, rows_hi = [], []
    for c in range(n_chunks):
        lo, hi = _unpack_rows(ybuf[slot, c * chunk:(c + 1) * chunk, :])
        rows_lo.append(lo.astype(BF16))
        rows_hi.append(hi.astype(BF16))
    for t0 in range(0, ROW_TILE, half):
        tok = slice(t0, t0 + half)
        pos_b = [jnp.broadcast_to(idx[tok, TOP_K + kk:TOP_K + kk + 1], (half, LANES)) for kk in range(TOP_K)]
        gate_b = [jnp.broadcast_to(gate[tok, kk:kk + 1], (half, LANES)) for kk in range(TOP_K)]
        f_lo = jnp.zeros((half, PACKED_W), F32)
        f_hi = jnp.zeros((half, PACKED_W), F32)
        for c in range(n_chunks):
            parts = []
            for off in range(c * chunk, (c + 1) * chunk, LANES):
                part = jnp.zeros((half, LANES), F32)
                for kk in range(TOP_K):
                    part = jnp.where(col_id + off == pos_b[kk], gate_b[kk], part)
                parts.append(part)
            wb = jnp.concatenate(parts, axis=-1).astype(BF16)
            f_lo = f_lo + jnp.dot(wb, rows_lo[c], preferred_element_type=F32)
            f_hi = f_hi + jnp.dot(wb, rows_hi[c], preferred_element_type=F32)
        y_lo = DN_ALPHA * h_ref[tok, :PACKED_W] + f_lo
        y_hi = DN_ALPHA * h_ref[tok, PACKED_W:] + f_hi
        mu = (jnp.sum(y_lo, axis=-1, keepdims=True) + jnp.sum(y_hi, axis=-1, keepdims=True)) / D_MODEL
        d_lo = y_lo - mu
        d_hi = y_hi - mu
        var = (jnp.sum(d_lo * d_lo, axis=-1, keepdims=True)
               + jnp.sum(d_hi * d_hi, axis=-1, keepdims=True)) / D_MODEL
        inv = lax.rsqrt(var + LN_EPS)
        o_ref[tok, :PACKED_W] = d_lo * inv * g_ref[:, :PACKED_W] + b_ref[:, :PACKED_W]
        o_ref[tok, PACKED_W:] = d_hi * inv * g_ref[:, PACKED_W:] + b_ref[:, PACKED_W:]


def _combine(runs, h, idx, gate, ys, g, b):
    T = h.shape[0]
    grid_spec = pltpu.PrefetchScalarGridSpec(
        num_scalar_prefetch=1,
        grid=(T // ROW_TILE,),
        in_specs=[pl.BlockSpec((ROW_TILE, D_MODEL), lambda i, r: (i, 0)),
                  pl.BlockSpec((ROW_TILE, LANES), lambda i, r: (i, 0)),
                  pl.BlockSpec((ROW_TILE, LANES), lambda i, r: (i, 0)),
                  pl.BlockSpec(g.shape, lambda i, r: (0, 0)),
                  pl.BlockSpec(b.shape, lambda i, r: (0, 0)),
                  pl.BlockSpec(memory_space=pl.ANY)],
        out_specs=pl.BlockSpec((ROW_TILE, D_MODEL), lambda i, r: (i, 0)),
        scratch_shapes=[pltpu.VMEM((2, STAGE_ROWS, PACKED_W), jnp.uint32),
                        pltpu.SemaphoreType.DMA((2,))],
    )
    return pl.pallas_call(
        _combine_kernel,
        out_shape=jax.ShapeDtypeStruct((T, D_MODEL), F32),
        grid_spec=grid_spec,
        compiler_params=_cparams("arbitrary"),
        name="combine_ln",
    )(runs, h, idx, gate, g, b, ys)


def _pack_w_in(w_in):
    wt = w_in.T
    lat = Q_LORA + KV_LORA + MLA_ROPE
    half = MLA_ROPE // 2
    wt_lat = jnp.concatenate([wt[:lat], -wt[lat - half:lat], wt[lat - MLA_ROPE:lat - half]], axis=0)
    return wt[lat:].astype(BF16), wt_lat.astype(BF16)


def _pack_w_uq(w_uq):
    w = w_uq.reshape(Q_LORA, MLA_HEADS, MLA_NOPE + MLA_ROPE)
    rope = w[:, :, MLA_NOPE:]
    rot = jnp.concatenate([-rope[:, :, MLA_ROPE // 2:], rope[:, :, :MLA_ROPE // 2]], axis=-1)
    return jnp.concatenate([w, rot], axis=-1).reshape(Q_LORA, MLA_HEADS * MLA_QK).astype(BF16)


def _rope_freqs():
    ret = 1.0 / (10000.0 ** jnp.linspace(0.0, 1.0, RET_DK // 2, dtype=F32))
    mla = 1.0 / (ROPE_BASE ** (jnp.arange(0, MLA_ROPE, 2, dtype=F32) / MLA_ROPE))
    return jnp.concatenate([ret, mla, jnp.zeros((LANES - ret.shape[0] - mla.shape[0],), F32)])[None, :]


def _routing_plan(cnt, n_tokens):
    counts = cnt[::SUBLANES, :N_EXPERTS]
    n_tiles = counts.shape[0]
    run_len = ((counts + SUBLANES - 1) // SUBLANES) * SUBLANES
    stage_off = jnp.cumsum(run_len, axis=1) - run_len
    tile_off = jnp.cumsum(run_len, axis=0) - run_len
    rows_used = jnp.sum(run_len, axis=0)
    padded = ((rows_used + EXPERT_STEP - 1) // EXPERT_STEP) * EXPERT_STEP
    pad_end = jnp.cumsum(padded)
    pad_start = pad_end - padded
    runs = jnp.stack([stage_off, pad_start[None, :] + tile_off, run_len], axis=-1).reshape(-1).astype(jnp.int32)
    nact = (pad_end[-1:] // EXPERT_STEP).astype(jnp.int32)
    gaps = jnp.concatenate([jnp.stack([pad_start + rows_used, padded - rows_used], axis=-1).reshape(-1),
                            pad_end[-1:] // EXPERT_TILE]).astype(jnp.int32)
    n_rows = n_tokens * TOP_K + (SUBLANES - 1) * n_tiles * N_EXPERTS + N_EXPERTS * EXPERT_STEP
    n_rows = -(-n_rows // EXPERT_STEP) * EXPERT_STEP
    n_blocks = n_rows // EXPERT_STEP
    experts = jnp.arange(N_EXPERTS, dtype=jnp.int32)
    blk_row = jnp.arange(n_blocks, dtype=jnp.int32) * EXPERT_STEP
    blk_e = jnp.minimum(jnp.sum((pad_end[None, :] <= blk_row[:, None]).astype(jnp.int32), axis=-1),
                        N_EXPERTS - 1)
    used_end = jnp.sum(jnp.where(blk_e[:, None] == experts[None, :], (pad_start + rows_used)[None, :], 0), axis=-1)
    nsub = jnp.clip((used_end - blk_row + EXPERT_TILE - 1) // EXPERT_TILE, 0,
                    EXPERT_STEP // EXPERT_TILE).astype(jnp.int32)
    later = jnp.where((rows_used > 0)[None, :] & (experts[None, :] > experts[:, None]), experts[None, :], N_EXPERTS)
    nxt = jnp.min(later, axis=-1)
    nxt = jnp.where(nxt == N_EXPERTS, -1, nxt)
    next_e = jnp.sum(jnp.where(blk_e[:, None] == experts[None, :], nxt[None, :], 0), axis=-1).astype(jnp.int32)
    return runs, gaps, blk_e, nact, next_e, nsub, n_rows


def kernel(x, positions, w_in, q_norm_g, w_uq, kv_norm_g, w_ukv, w_o, ln1_g, ln1_b,
           w_router, b_router, w_up, b_up, w_down, b_down, ln2_g, ln2_b):
    B, S, _ = x.shape
    T = B * S
    assert S % ATTN_Q_TILE == 0 and T % ROW_TILE == 0, "sequence / token count must be whole tiles"
    rc, rs, mc, ms = _rope_tables(positions.reshape(T // LANES, LANES).astype(jnp.int32), _rope_freqs())
    dec, qw, kw, cd = _retention_constants()
    h = x.reshape(T, D_MODEL)
    for l in range(DEPTH):
        proj = _input_projection(h, *_pack_w_in(w_in[l]))
        w_kv = w_ukv[l].reshape(KV_LORA, MLA_HEADS, MLA_NOPE + MLA_V)
        w_k = w_kv[:, :, :MLA_NOPE].reshape(KV_LORA, MLA_HEADS * MLA_NOPE).astype(BF16)
        w_vt = w_kv[:, :, MLA_NOPE:].reshape(KV_LORA, MLA_HEADS * MLA_V).T.astype(BF16)
        q_cat, kn, vt, kr = _mla_prep(proj, mc, ms, _pack_w_uq(w_uq[l]), w_k, w_vt,
                                      q_norm_g[l][None, :], kv_norm_g[l][None, :])
        o_mla = _attention(q_cat, kn, kr, vt, B, S)
        o_ret = _retention(proj, rc, rs, dec, qw, kw, cd, B, S)
        h1, h1b, idx_out, gate, cnt = _merge_and_route(
            h, o_mla, o_ret, proj, w_o[l].astype(BF16), ln1_g[l][None, :], ln1_b[l][None, :],
            w_router[l], b_router[l])
        runs, gaps, blk_e, nact, next_e, nsub, n_rows = _routing_plan(cnt, T)
        xs = _dispatch(runs, gaps, h1b, idx_out, n_rows)
        ys = _experts(blk_e, nact, next_e, nsub, xs, w_up[l], b_up[l], w_down[l], b_down[l])
        h = _combine(runs, h1, idx_out, gate, ys, ln2_g[l][None, :], ln2_b[l][None, :])
    return h.reshape(B, S, D_MODEL)
```
